```python
import math
import jax, jax.numpy as jnp
from jax import lax
import numpy as np

D_MODEL = 1024
BATCH = 16
SEQ = 2048
DEPTH = 2
DEC_BATCH = 8
DEC_SEQ = 2048
PAST_LEN = 128

N_META = 16
RET_HEADS = 8
RET_DK = 128
RET_DV = 256
RET_CHUNK = 128
META_PAD = RET_CHUNK - N_META
ROPE_THETA = 10000.0
HY_WIDTH = 1024
HY_ORDER = 2
HY_SHORT = 3
HY_EMB = 33
HY_BANDS = (HY_EMB - 1) // 2
HY_FILT_HIDDEN = 64
HY_DECAY_TARGET = 1e-2
HY_FAST_PCT = 0.3
HY_SLOW_PCT = 1.5
HY_MIN_DECAY = math.log(HY_DECAY_TARGET) / HY_SLOW_PCT
HY_MAX_DECAY = math.log(HY_DECAY_TARGET) / HY_FAST_PCT
N_GROUPS = 4
EXP_PER_GROUP = 8
N_EXPERTS = N_GROUPS * EXP_PER_GROUP
TOP_K_IN_GROUP = 2
EXPERT_FF = 256
RMS_EPS = 1e-6
Q_W = RET_HEADS * RET_DK
V_W = RET_HEADS * RET_DV
IN_COLS = 2 * Q_W + 2 * V_W + 3 * HY_WIDTH + 2 * D_MODEL
IN_SPLITS = [Q_W, 2 * Q_W, 2 * Q_W + V_W, 2 * Q_W + 2 * V_W, 2 * Q_W + 2 * V_W + 3 * HY_WIDTH]

kernel_name = "hybrid_retention_hyena_hmoe_encoder"


def rmsnorm(x, gain):
    xf = x.astype(jnp.float32)
    y = xf * lax.rsqrt(jnp.mean(xf * xf, axis=-1, keepdims=True) + RMS_EPS)
    return (y * gain.astype(jnp.float32)).astype(x.dtype)


def rotary(x, L):
    half = x.shape[-1] // 2
    inv = ROPE_THETA ** (-jnp.arange(half, dtype=jnp.float32) / half)
    ang = jnp.arange(L, dtype=jnp.float32)[:, None] * inv[None, :]
    cos = jnp.cos(ang)[None, :, None, :]
    sin = jnp.sin(ang)[None, :, None, :]
    xf = x.astype(jnp.float32)
    x1, x2 = xf[..., :half], xf[..., half:]
    return jnp.concatenate([x1 * cos - x2 * sin, x2 * cos + x1 * sin], axis=-1)


def retention_direction(q, k, v, log_gamma, strict):
    B, H, Lp, dk = q.shape
    dv = v.shape[-1]
    C = RET_CHUNK
    n = Lp // C
    q = q.reshape(B, H, n, C, dk)
    k = k.reshape(B, H, n, C, dk)
    v = v.reshape(B, H, n, C, dv)
    idx = jnp.arange(C, dtype=jnp.float32)
    diff = idx[:, None] - idx[None, :]
    mask = (diff > 0) if strict else (diff >= 0)
    lg = log_gamma[:, None, None]
    intra_decay = jnp.where(mask, jnp.exp(jnp.where(mask, diff, 0.0) * lg), 0.0)
    scores = jnp.einsum('bhncd,bhnsd->bhncs', q, k) * intra_decay[None, :, None]
    intra = jnp.einsum('bhncs,bhnse->bhnce', scores, v)
    k_dec = jnp.exp((C - 1 - idx)[None, :] * log_gamma[:, None])
    kv = jnp.einsum('bhnsd,bhnse->nbhde', k * k_dec[None, :, None, :, None], v)
    chunk_decay = jnp.exp(C * log_gamma)[None, :, None, None]

    def step(state, kv_n):
        return chunk_decay * state + kv_n, state

    _, s_prev = lax.scan(step, jnp.zeros((B, H, dk, dv), jnp.float32), kv)
    q_dec = jnp.exp((idx + 1.0)[None, :] * log_gamma[:, None])
    cross = jnp.einsum('bhncd,nbhde->bhnce', q * q_dec[None, :, None, :, None], s_prev)
    return (intra + cross).reshape(B, H, Lp, dv)


def retention_branch(q, k, v, g, decay_fwd, decay_bwd):
    B, L, _ = q.shape
    qh = rotary(q.reshape(B, L, RET_HEADS, RET_DK), L) * (RET_DK ** -0.5)
    kh = rotary(k.reshape(B, L, RET_HEADS, RET_DK), L)
    vh = v.reshape(B, L, RET_HEADS, RET_DV).astype(jnp.float32)
    pad = ((0, 0), (0, 0), (META_PAD, 0), (0, 0))
    qp = jnp.pad(jnp.transpose(qh, (0, 2, 1, 3)), pad)
    kp = jnp.pad(jnp.transpose(kh, (0, 2, 1, 3)), pad)
    vp = jnp.pad(jnp.transpose(vh, (0, 2, 1, 3)), pad)
    lg_f = jax.nn.log_sigmoid(decay_fwd.astype(jnp.float32))
    lg_b = jax.nn.log_sigmoid(decay_bwd.astype(jnp.float32))
    fwd = retention_direction(qp, kp, vp, lg_f, strict=False)
    bwd = retention_direction(qp[:, :, ::-1], kp[:, :, ::-1], vp[:, :, ::-1], lg_b, strict=True)[:, :, ::-1]
    ret = jnp.transpose((fwd + bwd)[:, :, META_PAD:], (0, 2, 1, 3))
    ret = ret * lax.rsqrt(jnp.mean(ret * ret, axis=-1, keepdims=True) + RMS_EPS)
    gh = g.reshape(B, L, RET_HEADS, RET_DV).astype(jnp.float32)
    return (jax.nn.silu(gh) * ret).reshape(B, L, V_W)


def hyena_filter_spectrum(L, w1, b1, w2, b2, w3, freq):
    f32 = jnp.float32
    t = jnp.linspace(0.0, 1.0, L, dtype=f32)
    w = (2.0 * math.pi / L) * jnp.arange(L, dtype=f32)
    bands = jnp.linspace(1e-4, HY_BANDS - 1, HY_BANDS, dtype=f32)
    fw = w[:, None] * bands[None, :]
    z = jnp.concatenate([t[:, None], jnp.cos(fw), -jnp.sin(fw)], axis=-1)
    fr = freq.astype(f32)
    h = jnp.sin(fr * (z @ w1.astype(f32) + b1.astype(f32)))
    h = jnp.sin(fr * (h @ w2.astype(f32) + b2.astype(f32)))
    h = (h @ w3.astype(f32)).reshape(L, HY_ORDER, 2, HY_WIDTH)
    deltas = jnp.abs(jnp.linspace(HY_MIN_DECAY, HY_MAX_DECAY, HY_WIDTH, dtype=f32))
    h = h * jnp.exp(-t[:, None] * deltas[None, :])[:, None, None, :]
    kern = jnp.concatenate([h[:, :, 0], jnp.zeros((1, HY_ORDER, HY_WIDTH), f32), h[:0:-1, :, 1]], axis=0)
    kern = kern * lax.rsqrt(jnp.sum(kern * kern, axis=0, keepdims=True) + 1e-6)
    return jnp.fft.rfft(kern, n=2 * L, axis=0)


def fftconv(u, spec):
    L = u.shape[1]
    U = jnp.fft.rfft(u, n=2 * L, axis=1)
    return jnp.fft.irfft(U * spec[None], n=2 * L, axis=1)[:, :L]


def hyena_branch(u3, short_w, short_b, spec, skip):
    L = u3.shape[1]
    r = HY_SHORT // 2
    up = jnp.pad(u3.astype(jnp.float32), ((0, 0), (r, r), (0, 0)))
    sw = short_w.astype(jnp.float32)
    u = short_b.astype(jnp.float32) + sum(sw[j] * up[:, j:j + L] for j in range(HY_SHORT))
    v, x1, x2 = jnp.split(u, 3, axis=-1)
    gates = (x1, x2)
    z = v
    for o in range(HY_ORDER):
        z = gates[o] * (fftconv(z, spec[:, o]) + skip[o].astype(jnp.float32) * z)
    return z


def token_mixer(xn, w_in, decay_fwd, decay_bwd, short_w, short_b, spec, skip, w_ret_o, w_hy_o, w_out):
    proj = xn @ w_in
    q, k, v, g, hy, gates = jnp.split(proj, IN_SPLITS + [IN_SPLITS[-1] + 0], axis=-1)[:5] + [proj[..., IN_SPLITS[-1]:]] if False else _split_proj(proj)
    ret = retention_branch(q, k, v, g, decay_fwd, decay_bwd) @ w_ret_o.astype(jnp.float32)
    hyo = hyena_branch(hy, short_w, short_b, spec, skip) @ w_hy_o.astype(jnp.float32)
    gf = gates.astype(jnp.float32)
    merged = jax.nn.sigmoid(gf[..., :D_MODEL]) * ret + jax.nn.sigmoid(gf[..., D_MODEL:]) * hyo
    return (merged @ w_out.astype(jnp.float32)).astype(xn.dtype)


def _split_proj(proj):
    q, k, v, g, hy, gates = jnp.split(proj, IN_SPLITS, axis=-1)
    return q, k, v, g, hy, gates


def hier_moe(xn, router_group, router_expert, w_gate, w_up, w_down):
    def per_sequence(tokens):
        xf = tokens.astype(jnp.float32)
        pg = jax.nn.softmax(xf @ router_group.astype(jnp.float32), axis=-1)
        p_top, g_idx = lax.top_k(pg, 1)
        p_top, g_idx = p_top[:, 0], g_idx[:, 0]
        el = (xf @ router_expert.astype(jnp.float32)).reshape(-1, N_GROUPS, EXP_PER_GROUP)
        el = jnp.take_along_axis(el, g_idx[:, None, None], axis=1)[:, 0]
        pe = jax.nn.softmax(el, axis=-1)
        top_v, top_i = lax.top_k(pe, TOP_K_IN_GROUP)
        top_v = top_v / jnp.sum(top_v, axis=-1, keepdims=True)
        eid = g_idx[:, None] * EXP_PER_GROUP + top_i
        comb = p_top[:, None] * jnp.einsum('tk,tke->te', top_v, jax.nn.one_hot(eid, N_EXPERTS, dtype=jnp.float32))
        hg = jnp.einsum('td,edf->tef', tokens, w_gate)
        hu = jnp.einsum('td,edf->tef', tokens, w_up)
        h = jax.nn.silu(hg) * hu * comb.astype(hg.dtype)[:, :, None]
        return jnp.einsum('tef,efd->td', h, w_down)

    return lax.map(per_sequence, xn)


def encoder_trunk(x, meta_tokens, norm_mix, w_in, ret_decay_fwd, ret_decay_bwd, hy_short_w, hy_short_b,
                  hy_filt_w1, hy_filt_b1, hy_filt_w2, hy_filt_b2, hy_filt_w3, hy_sin_freq, hy_skip,
                  w_ret_o, w_hy_o, w_out, norm_ffn, router_group, router_expert, moe_w_gate, moe_w_up,
                  moe_w_down, norm_final):
    B = x.shape[0]
    meta = jnp.broadcast_to(meta_tokens[None].astype(x.dtype), (B, N_META, D_MODEL))
    h = jnp.concatenate([meta, x], axis=1)
    L = h.shape[1]
    for i in range(DEPTH):
        spec = hyena_filter_spectrum(L, hy_filt_w1[i], hy_filt_b1[i], hy_filt_w2[i], hy_filt_b2[i],
                                     hy_filt_w3[i], hy_sin_freq[i])
        h = h + token_mixer(rmsnorm(h, norm_mix[i]), w_in[i], ret_decay_fwd[i], ret_decay_bwd[i],
                            hy_short_w[i], hy_short_b[i], spec, hy_skip[i], w_ret_o[i], w_hy_o[i], w_out[i])
        h = h + hier_moe(rmsnorm(h, norm_ffn[i]), router_group[i], router_expert[i],
                         moe_w_gate[i], moe_w_up[i], moe_w_down[i]).astype(h.dtype)
    h = rmsnorm(h, norm_final)
    return h[:, N_META:]


def setup_inputs(seed: int = 0) -> dict:
    key = jax.random.key(seed)
    ks = jax.random.split(key, 32)
    f32 = jnp.float32

    def nrm(k, shape, scale):
        return jax.random.normal(k, shape, f32) * scale

    gam = 1.0 - 2.0 ** (-5.0 - jnp.arange(RET_HEADS, dtype=f32))
    logit = jnp.log(gam) - jnp.log1p(-gam)
    return {
        "x_prompt": nrm(ks[0], (BATCH, SEQ, D_MODEL), 1.0),
        "x_sample": nrm(ks[1], (DEC_BATCH, DEC_SEQ, D_MODEL), 1.0),
        "meta_tokens": nrm(ks[2], (N_META, D_MODEL), 1.0),
        "norm_mix": 1.0 + nrm(ks[3], (DEPTH, D_MODEL), 0.02),
        "w_in": nrm(ks[4], (DEPTH, D_MODEL, IN_COLS), D_MODEL ** -0.5),
        "ret_decay_fwd": logit[None, :] + nrm(ks[5], (DEPTH, RET_HEADS), 0.05),
        "ret_decay_bwd": logit[None, :] + nrm(ks[6], (DEPTH, RET_HEADS), 0.05),
        "hy_short_w": nrm(ks[7], (DEPTH, HY_SHORT, 3 * HY_WIDTH), HY_SHORT ** -0.5),
        "hy_short_b": nrm(ks[8], (DEPTH, 3 * HY_WIDTH), 0.02),
        "hy_filt_w1": nrm(ks[9], (DEPTH, HY_EMB, HY_FILT_HIDDEN), HY_EMB ** -0.5),
        "hy_filt_b1": nrm(ks[10], (DEPTH, HY_FILT_HIDDEN), 0.02),
        "hy_filt_w2": nrm(ks[11], (DEPTH, HY_FILT_HIDDEN, HY_FILT_HIDDEN), HY_FILT_HIDDEN ** -0.5),
        "hy_filt_b2": nrm(ks[12], (DEPTH, HY_FILT_HIDDEN), 0.02),
        "hy_filt_w3": nrm(ks[13], (DEPTH, HY_FILT_HIDDEN, HY_ORDER * 2 * HY_WIDTH), HY_FILT_HIDDEN ** -0.5),
        "hy_sin_freq": 1.0 + nrm(ks[14], (DEPTH, HY_FILT_HIDDEN), 0.05),
        "hy_skip": nrm(ks[15], (DEPTH, HY_ORDER, HY_WIDTH), 1.0),
        "w_ret_o": nrm(ks[16], (DEPTH, V_W, D_MODEL), V_W ** -0.5),
        "w_hy_o": nrm(ks[17], (DEPTH, HY_WIDTH, D_MODEL), HY_WIDTH ** -0.5),
        "w_out": nrm(ks[18], (DEPTH, D_MODEL, D_MODEL), D_MODEL ** -0.5),
        "norm_ffn": 1.0 + nrm(ks[19], (DEPTH, D_MODEL), 0.02),
        "router_group": nrm(ks[20], (DEPTH, D_MODEL, N_GROUPS), D_MODEL ** -0.5),
        "router_expert": nrm(ks[21], (DEPTH, D_MODEL, N_EXPERTS), D_MODEL ** -0.5),
        "moe_w_gate": nrm(ks[22], (DEPTH, N_EXPERTS, D_MODEL, EXPERT_FF), D_MODEL ** -0.5),
        "moe_w_up": nrm(ks[23], (DEPTH, N_EXPERTS, D_MODEL, EXPERT_FF), D_MODEL ** -0.5),
        "moe_w_down": nrm(ks[24], (DEPTH, N_EXPERTS, EXPERT_FF, D_MODEL), EXPERT_FF ** -0.5),
        "norm_final": 1.0 + nrm(ks[25], (D_MODEL,), 0.02),
    }


def reference(x_prompt, x_sample, meta_tokens, norm_mix, w_in, ret_decay_fwd, ret_decay_bwd, hy_short_w,
              hy_short_b, hy_filt_w1, hy_filt_b1, hy_filt_w2, hy_filt_b2, hy_filt_w3, hy_sin_freq, hy_skip,
              w_ret_o, w_hy_o, w_out, norm_ffn, router_group, router_expert, moe_w_gate, moe_w_up,
              moe_w_down, norm_final):
    y_prompt = encoder_trunk(x_prompt, meta_tokens, norm_mix, w_in, ret_decay_fwd, ret_decay_bwd, hy_short_w,
                             hy_short_b, hy_filt_w1, hy_filt_b1, hy_filt_w2, hy_filt_b2, hy_filt_w3, hy_sin_freq,
                             hy_skip, w_ret_o, w_hy_o, w_out, norm_ffn, router_group, router_expert, moe_w_gate,
                             moe_w_up, moe_w_down, norm_final)
    y_sample = encoder_trunk(x_sample, meta_tokens, norm_mix, w_in, ret_decay_fwd, ret_decay_bwd, hy_short_w,
                             hy_short_b, hy_filt_w1, hy_filt_b1, hy_filt_w2, hy_filt_b2, hy_filt_w3, hy_sin_freq,
                             hy_skip, w_ret_o, w_hy_o, w_out, norm_ffn, router_group, router_expert, moe_w_gate,
                             moe_w_up, moe_w_down, norm_final)
    return (y_prompt, y_sample)
```

```python
import functools
import math

import jax
import jax.numpy as jnp
from jax import lax
from jax.experimental import pallas as pl
from jax.experimental.pallas import tpu as pltpu

N_META = 16
RET_HEADS = 8
RET_DK = 128
RET_DV = 256
ROPE_THETA = 10000.0
HY_ORDER = 2
HY_SHORT = 3
HY_EMB = 33
HY_BANDS = (HY_EMB - 1) // 2
HY_DECAY_TARGET = 1e-2
HY_MIN_DECAY = math.log(HY_DECAY_TARGET) / 1.5
HY_MAX_DECAY = math.log(HY_DECAY_TARGET) / 0.3
N_GROUPS = 4
EXP_PER_GROUP = 8
N_EXPERTS = N_GROUPS * EXP_PER_GROUP
RMS_EPS = 1e-6

LANES = 128
BF16_SUBLANES = 16
MXU_DIM = 256
RET_CHUNK = 256
VMEM_LIMIT = 56 * 1024 * 1024

F32 = jnp.float32
BF16 = jnp.bfloat16


def _round_up(n, m):
    return (n + m - 1) // m * m


def _pick_tile(n, target, mult):
    best = None
    for t in range(mult, min(n, target) + 1, mult):
        if n % t == 0:
            best = t
    assert best is not None, (n, target, mult)
    return best


def _params(sem):
    return pltpu.CompilerParams(dimension_semantics=sem, vmem_limit_bytes=VMEM_LIMIT)


def _rms(x, gain):
    ms = jnp.mean(x * x, axis=-1, keepdims=True)
    return x * lax.rsqrt(ms + RMS_EPS) * gain


def _inproj_body(h_ref, g_ref, w_ref, o_ref, xn_ref):
    @pl.when(pl.program_id(1) == 0)
    def _():
        xn_ref[...] = _rms(h_ref[...], g_ref[...]).astype(BF16)

    o_ref[...] = jnp.dot(xn_ref[...], w_ref[...], preferred_element_type=F32).astype(o_ref.dtype)


def _inproj(h2, gain, w):
    T, D = h2.shape
    nc = w.shape[1]
    tm = _pick_tile(T, 1152, BF16_SUBLANES)
    tn = _pick_tile(nc, 1024, LANES)
    return pl.pallas_call(
        _inproj_body,
        grid=(T // tm, nc // tn),
        in_specs=[
            pl.BlockSpec((tm, D), lambda i, j: (i, 0)),
            pl.BlockSpec((1, D), lambda i, j: (0, 0)),
            pl.BlockSpec((D, tn), lambda i, j: (0, j)),
        ],
        out_specs=pl.BlockSpec((tm, tn), lambda i, j: (i, j)),
        out_shape=jax.ShapeDtypeStruct((T, nc), BF16),
        scratch_shapes=[pltpu.VMEM((tm, D), BF16)],
        compiler_params=_params(("parallel", "arbitrary")),
        name="norm_inproj",
    )(h2, gain.reshape(1, D), w)


def _dot_t(a, b):
    return lax.dot_general(a, b, (((0,), (0,)), ((), ())), preferred_element_type=F32)


def _dot_nt(a, b):
    return lax.dot_general(a, b, (((1,), (1,)), ((), ())), preferred_element_type=F32)


def _ret_body(lg_ref, q_ref, k_ref, v_ref, g_ref, cos_ref, sin_ref, o_ref, rb_ref, *, seq_len):
    C = RET_CHUNK
    L = seq_len
    n_chunks = pl.cdiv(L, C)
    head = pl.program_id(1)
    lgf = lg_ref[0, head]
    lgb = lg_ref[1, head]

    def chunk(ref, n):
        lo, hi = n * C, min((n + 1) * C, L)
        x = ref[0, lo:hi, :]
        if hi - lo < C:
            x = jnp.concatenate([x, jnp.zeros((C - (hi - lo), x.shape[1]), x.dtype)], axis=0)
        return x

    def rotary(ref, n):
        x = chunk(ref, n).astype(F32)
        sl = slice(n * C, (n + 1) * C)
        return x * cos_ref[sl, :] + pltpu.roll(x, RET_DK // 2, 1) * sin_ref[sl, :]

    row = lax.broadcasted_iota(jnp.int32, (C, LANES), 0).astype(F32)
    ri = lax.broadcasted_iota(jnp.int32, (C, C), 0).astype(F32)
    ci = lax.broadcasted_iota(jnp.int32, (C, C), 1).astype(F32)
    decay = jnp.exp(jnp.where(ci <= ri, (ri - ci) * lgf, (ci - ri) * lgb))
    qf_dec = jnp.exp((row + 1.0) * lgf)
    qb_dec = jnp.exp((C - row) * lgb)
    kf_dec = jnp.exp((C - 1.0 - row) * lgf)
    kb_dec = jnp.exp(row * lgb)
    cf = jnp.exp(C * lgf)
    cb = jnp.exp(C * lgb)

    state = jnp.zeros((RET_DK, RET_DV), F32)
    for n in reversed(range(n_chunks)):
        rb_ref[n] = state.astype(BF16)
        if n > 0:
            kb = (rotary(k_ref, n) * kb_dec).astype(BF16)
            state = cb * state + _dot_t(kb, chunk(v_ref, n))

    state = jnp.zeros((RET_DK, RET_DV), F32)
    for n in range(n_chunks):
        q = rotary(q_ref, n) * (RET_DK ** -0.5)
        k = rotary(k_ref, n)
        v = chunk(v_ref, n)
        scores = _dot_nt(q.astype(BF16), k.astype(BF16)) * decay
        o = jnp.dot(scores.astype(BF16), v, preferred_element_type=F32)
        o += jnp.dot((q * qf_dec).astype(BF16), state.astype(BF16), preferred_element_type=F32)
        o += jnp.dot((q * qb_dec).astype(BF16), rb_ref[n], preferred_element_type=F32)
        if n + 1 < n_chunks:
            state = cf * state + _dot_t((k * kf_dec).astype(BF16), v)
        o = o * lax.rsqrt(jnp.mean(o * o, axis=-1, keepdims=True) + RMS_EPS)
        lo, hi = n * C, min((n + 1) * C, L)
        g = g_ref[0, lo:hi, :].astype(F32)
        o_ref[0, lo:hi, :] = (g * jax.nn.sigmoid(g) * o[: hi - lo]).astype(o_ref.dtype)


def _retention(proj3, lg, cos_t, sin_t):
    B, L, _ = proj3.shape
    n_chunks = pl.cdiv(L, RET_CHUNK)
    lp = n_chunks * RET_CHUNK
    k_blk = RET_HEADS
    v_blk = 2 * RET_HEADS * RET_DK // RET_DV
    g_blk = v_blk + RET_HEADS
    return pl.pallas_call(
        functools.partial(_ret_body, seq_len=L),
        grid=(B, RET_HEADS),
        in_specs=[
            pl.BlockSpec(memory_space=pltpu.SMEM),
            pl.BlockSpec((1, L, RET_DK), lambda b, h: (b, 0, h)),
            pl.BlockSpec((1, L, RET_DK), lambda b, h: (b, 0, k_blk + h)),
            pl.BlockSpec((1, L, RET_DV), lambda b, h: (b, 0, v_blk + h)),
            pl.BlockSpec((1, L, RET_DV), lambda b, h: (b, 0, g_blk + h)),
            pl.BlockSpec((lp, RET_DK), lambda b, h: (0, 0)),
            pl.BlockSpec((lp, RET_DK), lambda b, h: (0, 0)),
        ],
        out_specs=pl.BlockSpec((1, L, RET_DV), lambda b, h: (b, 0, h)),
        out_shape=jax.ShapeDtypeStruct((B, L, RET_HEADS * RET_DV), BF16),
        scratch_shapes=[pltpu.VMEM((n_chunks, RET_DK, RET_DV), BF16)],
        compiler_params=_params(("parallel", "arbitrary")),
        name="retention",
    )(lg, proj3, proj3, proj3, proj3, cos_t, sin_t)


def _rotary_tables(L):
    half = RET_DK // 2
    lp = _round_up(L, RET_CHUNK)
    inv = ROPE_THETA ** (-jnp.arange(half, dtype=F32) / half)
    ang = jnp.arange(lp, dtype=F32)[:, None] * inv[None, :]
    cos, sin = jnp.cos(ang), jnp.sin(ang)
    return jnp.concatenate([cos, cos], axis=1), jnp.concatenate([-sin, sin], axis=1)


def _short_conv_body(u_ref, w_ref, b_ref, o_ref):
    u = u_ref[0].astype(F32)
    L = u.shape[0]
    t = lax.broadcasted_iota(jnp.int32, u.shape, 0)
    prev = jnp.where(t == 0, 0.0, pltpu.roll(u, 1, 0))
    nxt = jnp.where(t == L - 1, 0.0, pltpu.roll(u, L - 1, 0))
    w = w_ref[...]
    o_ref[0] = (b_ref[...] + w[0:1] * prev + w[1:2] * u + w[2:3] * nxt).astype(o_ref.dtype)


def _short_conv(proj3, col0, short_w, short_b):
    B, L, _ = proj3.shape
    width = short_w.shape[1]
    cw = 512
    blk0 = col0 // cw
    return pl.pallas_call(
        _short_conv_body,
        grid=(B, width // cw),
        in_specs=[
            pl.BlockSpec((1, L, cw), lambda b, j: (b, 0, blk0 + j)),
            pl.BlockSpec((HY_SHORT, cw), lambda b, j: (0, j)),
            pl.BlockSpec((1, cw), lambda b, j: (0, j)),
        ],
        out_specs=pl.BlockSpec((1, L, cw), lambda b, j: (b, 0, j)),
        out_shape=jax.ShapeDtypeStruct((B, L, width), BF16),
        compiler_params=_params(("parallel", "parallel")),
        name="hyena_short_conv",
    )(proj3, short_w, short_b.reshape(1, width))


def _row_tiles(n, tile):
    return [(s, min(tile, n - s)) for s in range(0, n, tile)]


def _dft_tables(L):
    fp = _round_up(L + 1, MXU_DIM)
    n = 2 * L
    a = jnp.arange(fp, dtype=jnp.int32)
    m = (a[:, None] * a[None, :]) % n
    ang = m.astype(F32) * (2.0 * math.pi / n)
    return jnp.cos(ang).astype(BF16), jnp.sin(ang).astype(BF16)


def _spec_body(c_ref, s_ref, hs_ref, hd_ref, k1_ref, k2_ref, pad_ref, *, seq_len):
    L = seq_len
    fp = c_ref.shape[0]
    f = lax.broadcasted_iota(jnp.int32, (fp, 1), 0)
    wgt = jnp.where((f == 0) | (f == L), 1.0, 2.0)
    wgt = jnp.where(f <= L, wgt, 0.0) / (2.0 * L)
    pad_ref[L:, :] = jnp.zeros((fp - L, pad_ref.shape[1]), BF16)
    pad_ref[:L, :] = hs_ref[...].astype(BF16)
    k1_ref[...] = wgt * jnp.dot(c_ref[...], pad_ref[...], preferred_element_type=F32)
    pad_ref[:L, :] = hd_ref[...].astype(BF16)
    k2_ref[...] = -wgt * jnp.dot(s_ref[...], pad_ref[...], preferred_element_type=F32)


def _filter_spectrum(cmat, smat, hs, hd):
    L, width = hs.shape
    fp = cmat.shape[0]
    cw = 256
    resident = pl.BlockSpec((fp, fp), lambda j: (0, 0), pipeline_mode=pl.Buffered(1))
    return pl.pallas_call(
        functools.partial(_spec_body, seq_len=L),
        grid=(width // cw,),
        in_specs=[resident, resident,
                  pl.BlockSpec((L, cw), lambda j: (0, j)),
                  pl.BlockSpec((L, cw), lambda j: (0, j))],
        out_specs=[pl.BlockSpec((fp, cw), lambda j: (0, j))] * 2,
        out_shape=[jax.ShapeDtypeStruct((fp, width), F32)] * 2,
        scratch_shapes=[pltpu.VMEM((fp, cw), BF16)],
        compiler_params=_params(("arbitrary",)),
        name="hyena_filter_spectrum",
    )(cmat, smat, hs, hd)


def _conv_body(c_ref, s_ref, u_ref, x_ref, skip_ref, k1_ref, k2_ref, o_ref, pad_ref, e1_ref, e2_ref, *, seq_len):
    L = seq_len
    fp = c_ref.shape[0]
    cw = pad_ref.shape[1]
    nf = _round_up(L + 1, BF16_SUBLANES)
    pad_ref[L:, :] = jnp.zeros((fp - L, cw), BF16)
    pad_ref[:L, :] = u_ref[0]
    if nf < fp:
        e1_ref[nf:, :] = jnp.zeros((fp - nf, cw), BF16)
        e2_ref[nf:, :] = jnp.zeros((fp - nf, cw), BF16)
    for lo, sz in _row_tiles(nf, MXU_DIM):
        p = jnp.dot(c_ref[lo:lo + sz, :], pad_ref[...], preferred_element_type=F32)
        a = jnp.dot(s_ref[lo:lo + sz, :], pad_ref[...], preferred_element_type=F32)
        k1 = k1_ref[lo:lo + sz, :]
        k2 = k2_ref[lo:lo + sz, :]
        e1_ref[lo:lo + sz, :] = (p * k1 + a * k2).astype(BF16)
        e2_ref[lo:lo + sz, :] = (a * k1 - p * k2).astype(BF16)
    skip = skip_ref[...]
    for lo, sz in _row_tiles(L, MXU_DIM):
        y = jnp.dot(c_ref[lo:lo + sz, :], e1_ref[...], preferred_element_type=F32)
        y += jnp.dot(s_ref[lo:lo + sz, :], e2_ref[...], preferred_element_type=F32)
        u = u_ref[0, lo:lo + sz, :].astype(F32)
        x = x_ref[0, lo:lo + sz, :].astype(F32)
        o_ref[0, lo:lo + sz, :] = (x * (y + skip * u)).astype(o_ref.dtype)


def _long_conv(cmat, smat, u_arr, u_col0, x_arr, x_col0, skip, k1, k2, k_col0, width):
    B, L, _ = u_arr.shape
    fp = cmat.shape[0]
    cw = MXU_DIM
    ub, xb, kb = u_col0 // cw, x_col0 // cw, k_col0 // cw
    resident = pl.BlockSpec((fp, fp), lambda c, b: (0, 0), pipeline_mode=pl.Buffered(1))
    return pl.pallas_call(
        functools.partial(_conv_body, seq_len=L),
        grid=(width // cw, B),
        in_specs=[
            resident, resident,
            pl.BlockSpec((1, L, cw), lambda c, b: (b, 0, ub + c)),
            pl.BlockSpec((1, L, cw), lambda c, b: (b, 0, xb + c)),
            pl.BlockSpec((1, cw), lambda c, b: (0, c)),
            pl.BlockSpec((fp, cw), lambda c, b: (0, kb + c)),
            pl.BlockSpec((fp, cw), lambda c, b: (0, kb + c)),
        ],
        out_specs=pl.BlockSpec((1, L, cw), lambda c, b: (b, 0, c)),
        out_shape=jax.ShapeDtypeStruct((B, L, width), BF16),
        scratch_shapes=[pltpu.VMEM((fp, cw), BF16)] * 3,
        compiler_params=_params(("parallel", "arbitrary")),
        name="hyena_long_conv",
    )(cmat, smat, u_arr, x_arr, skip.reshape(1, width), k1, k2)


def _hyena_filters(L, w1, b1, w2, b2, w3, freq):
    hp = lax.Precision.HIGHEST
    width = w3.shape[1] // (2 * HY_ORDER)
    t = jnp.linspace(0.0, 1.0, L, dtype=F32)
    w = (2.0 * math.pi / L) * jnp.arange(L, dtype=F32)
    bands = jnp.linspace(1e-4, HY_BANDS - 1, HY_BANDS, dtype=F32)
    fw = w[:, None] * bands[None, :]
    z = jnp.concatenate([t[:, None], jnp.cos(fw), -jnp.sin(fw)], axis=-1)
    h = jnp.sin(freq * (jnp.dot(z, w1, precision=hp) + b1))
    h = jnp.sin(freq * (jnp.dot(h, w2, precision=hp) + b2))
    h = jnp.dot(h, w3, precision=hp).reshape(L, HY_ORDER, 2, width)
    deltas = jnp.abs(jnp.linspace(HY_MIN_DECAY, HY_MAX_DECAY, width, dtype=F32))
    h = h * jnp.exp(-t[:, None] * deltas[None, :])[:, None, None, :]
    hf = h[:, :, 0]
    hb = h[:, :, 1].at[0].set(0.0)
    scale = lax.rsqrt(jnp.sum(hf * hf, axis=0) + jnp.sum(hb * hb, axis=0) + 1e-6)
    hf, hb = hf * scale, hb * scale
    return (hf + hb).reshape(L, HY_ORDER * width), (hf - hb).reshape(L, HY_ORDER * width)


def _mix_body(ret_ref, hy_ref, gr_ref, gh_ref, h_ref, wr_ref, wh_ref, wo_ref, o_ref):
    ret = jnp.dot(ret_ref[...], wr_ref[...], preferred_element_type=F32)
    hyo = jnp.dot(hy_ref[...], wh_ref[...], preferred_element_type=F32)
    merged = (jax.nn.sigmoid(gr_ref[...].astype(F32)) * ret
              + jax.nn.sigmoid(gh_ref[...].astype(F32)) * hyo)
    o_ref[...] = h_ref[...] + jnp.dot(merged.astype(BF16), wo_ref[...], preferred_element_type=F32)


def _mix_out(ret2, hy2, proj2, gate_col0, h2, w_ret_o, w_hy_o, w_out):
    T, D = h2.shape
    tm = _pick_tile(T, 384, BF16_SUBLANES)
    gb = gate_col0 // D
    row = lambda w: pl.BlockSpec((tm, w), lambda i: (i, 0))
    full = lambda a: pl.BlockSpec(a.shape, lambda i: (0, 0))
    return pl.pallas_call(
        _mix_body,
        grid=(T // tm,),
        in_specs=[
            row(ret2.shape[1]), row(hy2.shape[1]),
            pl.BlockSpec((tm, D), lambda i: (i, gb)),
            pl.BlockSpec((tm, D), lambda i: (i, gb + 1)),
            row(D), full(w_ret_o), full(w_hy_o), full(w_out),
        ],
        out_specs=row(D),
        out_shape=jax.ShapeDtypeStruct((T, D), F32),
        compiler_params=_params(("parallel",)),
        name="merge_outproj",
    )(ret2, hy2, proj2, proj2, h2, w_ret_o, w_hy_o, w_out)


def _route(logits):
    lane = lax.broadcasted_iota(jnp.int32, logits.shape, 1)
    neg = -jnp.inf
    big = jnp.int32(1 << 20)
    gmask = lane < N_GROUPS
    gl = jnp.where(gmask, logits, neg)
    gmax = jnp.max(gl, axis=1, keepdims=True)
    p_top = 1.0 / jnp.sum(jnp.exp(gl - gmax), axis=1, keepdims=True)
    g_idx = jnp.min(jnp.where(gl == gmax, lane, big), axis=1, keepdims=True)
    lo = N_GROUPS + g_idx * EXP_PER_GROUP
    emask = (lane >= lo) & (lane < lo + EXP_PER_GROUP)
    el = jnp.where(emask, logits, neg)
    m1 = jnp.max(el, axis=1, keepdims=True)
    i1 = jnp.min(jnp.where(el == m1, lane, big), axis=1, keepdims=True)
    el2 = jnp.where(lane == i1, neg, el)
    m2 = jnp.max(el2, axis=1, keepdims=True)
    i2 = jnp.min(jnp.where(el2 == m2, lane, big), axis=1, keepdims=True)
    r = jnp.exp(m2 - m1)
    w1 = 1.0 / (1.0 + r)
    w2 = r / (1.0 + r)
    return p_top * (jnp.where(lane == i1, w1, 0.0) + jnp.where(lane == i2, w2, 0.0))


def _moe_body(h_ref, g_ref, wr_ref, wg_ref, wu_ref, wd_ref, o_ref, xn_ref, comb_ref, acc_ref):
    e = pl.program_id(1)

    @pl.when(e == 0)
    def _():
        xn = _rms(h_ref[...], g_ref[...])
        xn_ref[...] = xn.astype(BF16)
        logits = jnp.dot(xn, wr_ref[...], preferred_element_type=F32, precision=lax.Precision.HIGHEST)
        comb_ref[...] = _route(logits)
        acc_ref[...] = jnp.zeros_like(acc_ref)

    lane = lax.broadcasted_iota(jnp.int32, comb_ref.shape, 1)
    scale = jnp.sum(jnp.where(lane == N_GROUPS + e, comb_ref[...], 0.0), axis=1, keepdims=True)
    xn = xn_ref[...]
    hg = jnp.dot(xn, wg_ref[0], preferred_element_type=F32)
    hu = jnp.dot(xn, wu_ref[0], preferred_element_type=F32)
    act = hg * jax.nn.sigmoid(hg) * hu * scale
    acc_ref[...] += jnp.dot(act.astype(BF16), wd_ref[0], preferred_element_type=F32)

    @pl.when(e == pl.num_programs(1) - 1)
    def _():
        o_ref[...] = h_ref[...] + acc_ref[...]


def _moe(h2, gain, w_router, w_gate, w_up, w_down):
    T, D = h2.shape
    E, _, FF = w_gate.shape
    tm = _pick_tile(T, 1152, BF16_SUBLANES)
    return pl.pallas_call(
        _moe_body,
        grid=(T // tm, E),
        in_specs=[
            pl.BlockSpec((tm, D), lambda i, e: (i, 0)),
            pl.BlockSpec((1, D), lambda i, e: (0, 0)),
            pl.BlockSpec((D, LANES), lambda i, e: (0, 0)),
            pl.BlockSpec((1, D, FF), lambda i, e: (e, 0, 0)),
            pl.BlockSpec((1, D, FF), lambda i, e: (e, 0, 0)),
            pl.BlockSpec((1, FF, D), lambda i, e: (e, 0, 0)),
        ],
        out_specs=pl.BlockSpec((tm, D), lambda i, e: (i, 0)),
        out_shape=jax.ShapeDtypeStruct((T, D), F32),
        scratch_shapes=[pltpu.VMEM((tm, D), BF16), pltpu.VMEM((tm, LANES), F32), pltpu.VMEM((tm, D), F32)],
        compiler_params=_params(("parallel", "arbitrary")),
        name="moe_ffn",
    )(h2, gain.reshape(1, D), w_router, w_gate, w_up, w_down)


def _final_body(h_ref, g_ref, o_ref):
    o_ref[0] = _rms(h_ref[0, N_META:, :], g_ref[...])


def _final_norm(h3, gain, b0, nb):
    _, L, D = h3.shape
    return pl.pallas_call(
        _final_body,
        grid=(nb,),
        in_specs=[pl.BlockSpec((1, L, D), lambda b: (b0 + b, 0, 0)),
                  pl.BlockSpec((1, D), lambda b: (0, 0))],
        out_specs=pl.BlockSpec((1, L - N_META, D), lambda b: (b, 0, 0)),
        out_shape=jax.ShapeDtypeStruct((nb, L - N_META, D), F32),
        compiler_params=_params(("parallel",)),
        name="final_norm",
    )(h3, gain.reshape(1, D))


def kernel(x_prompt, x_sample, meta_tokens, norm_mix, w_in, ret_decay_fwd, ret_decay_bwd, hy_short_w, hy_short_b, hy_filt_w1, hy_filt_b1, hy_filt_w2, hy_filt_b2, hy_filt_w3, hy_sin_freq, hy_skip, w_ret_o, w_hy_o, w_out, norm_ffn, router_group, router_expert, moe_w_gate, moe_w_up, moe_w_down, norm_final):
    assert x_prompt.shape[1:] == x_sample.shape[1:]
    nbp, nbs = x_prompt.shape[0], x_sample.shape[0]
    B = nbp + nbs
    D = x_prompt.shape[2]
    L = N_META + x_prompt.shape[1]
    T = B * L
    depth = w_in.shape[0]
    q_w = RET_HEADS * RET_DK
    v_w = RET_HEADS * RET_DV
    hy_w = hy_skip.shape[2]
    hy_col0 = 2 * q_w + 2 * v_w
    gate_col0 = hy_col0 + 3 * hy_w
    assert D == q_w and w_in.shape[2] == gate_col0 + 2 * D

    x = jnp.concatenate([x_prompt, x_sample], axis=0)
    meta = jnp.broadcast_to(meta_tokens[None].astype(x.dtype), (B, N_META, D))
    h = jnp.concatenate([meta, x], axis=1).reshape(T, D)

    cos_t, sin_t = _rotary_tables(L)
    cmat, smat = _dft_tables(L)
    router = jnp.concatenate([router_group, router_expert], axis=2).astype(F32)
    router = jnp.pad(router, ((0, 0), (0, 0), (0, LANES - router.shape[2])))

    for i in range(depth):
        proj = _inproj(h, norm_mix[i], w_in[i].astype(BF16))
        proj3 = proj.reshape(B, L, -1)

        lg = jnp.stack([jax.nn.log_sigmoid(ret_decay_fwd[i].astype(F32)),
                        jax.nn.log_sigmoid(ret_decay_bwd[i].astype(F32))])
        ret = _retention(proj3, lg, cos_t, sin_t)

        hs, hd = _hyena_filters(L, hy_filt_w1[i], hy_filt_b1[i], hy_filt_w2[i], hy_filt_b2[i],
                                hy_filt_w3[i], hy_sin_freq[i])
        k1, k2 = _filter_spectrum(cmat, smat, hs, hd)
        vx = _short_conv(proj3, hy_col0, hy_short_w[i].astype(F32), hy_short_b[i].astype(F32))
        z = _long_conv(cmat, smat, vx, 0, vx, hy_w, hy_skip[i, 0].astype(F32), k1, k2, 0, hy_w)
        z = _long_conv(cmat, smat, z, 0, vx, 2 * hy_w, hy_skip[i, 1].astype(F32), k1, k2, hy_w, hy_w)

        h = _mix_out(ret.reshape(T, v_w), z.reshape(T, hy_w), proj, gate_col0, h,
                     w_ret_o[i].astype(BF16), w_hy_o[i].astype(BF16), w_out[i].astype(BF16))
        h = _moe(h, norm_ffn[i], router[i], moe_w_gate[i].astype(BF16), moe_w_up[i].astype(BF16),
                 moe_w_down[i].astype(BF16))

    h3 = h.reshape(B, L, D)
    return _final_norm(h3, norm_final, 0, nbp), _final_norm(h3, norm_final, nbp, nbs)
```

```python
import functools
import math

import jax
import jax.numpy as jnp
from jax import lax
from jax.experimental import pallas as pl
from jax.experimental.pallas import tpu as pltpu
from jax.experimental.pallas import tpu_sc as plsc

N_META = 16
RET_HEADS = 8
RET_DK = 128
RET_DV = 256
ROPE_THETA = 10000.0
HY_ORDER = 2
HY_SHORT = 3
HY_EMB = 33
HY_BANDS = (HY_EMB - 1) // 2
HY_DECAY_TARGET = 1e-2
HY_MIN_DECAY = math.log(HY_DECAY_TARGET) / 1.5
HY_MAX_DECAY = math.log(HY_DECAY_TARGET) / 0.3
N_GROUPS = 4
EXP_PER_GROUP = 8
N_EXPERTS = N_GROUPS * EXP_PER_GROUP
RMS_EPS = 1e-6

LANES = 128
BF16_SUBLANES = 16
MXU_DIM = 256
RET_CHUNK = 256
VMEM_LIMIT = 56 * 1024 * 1024

F32 = jnp.float32
BF16 = jnp.bfloat16


def _round_up(n, m):
    return (n + m - 1) // m * m


def _pick_tile(n, target, mult):
    best = None
    for t in range(mult, min(n, target) + 1, mult):
        if n % t == 0:
            best = t
    assert best is not None, (n, target, mult)
    return best


def _params(sem):
    return pltpu.CompilerParams(dimension_semantics=sem, vmem_limit_bytes=VMEM_LIMIT)


def _rms(x, gain):
    ms = jnp.mean(x * x, axis=-1, keepdims=True)
    return x * lax.rsqrt(ms + RMS_EPS) * gain


def _inproj_body(h_ref, g_ref, w_ref, o_ref, xn_ref):
    @pl.when(pl.program_id(1) == 0)
    def _():
        xn_ref[...] = _rms(h_ref[...], g_ref[...]).astype(BF16)

    o_ref[...] = jnp.dot(xn_ref[...], w_ref[...], preferred_element_type=F32).astype(o_ref.dtype)


def _inproj(h2, gain, w):
    T, D = h2.shape
    nc = w.shape[1]
    tm = _pick_tile(T, 1152, BF16_SUBLANES)
    tn = _pick_tile(nc, 1024, LANES)
    return pl.pallas_call(
        _inproj_body,
        grid=(T // tm, nc // tn),
        in_specs=[
            pl.BlockSpec((tm, D), lambda i, j: (i, 0)),
            pl.BlockSpec((1, D), lambda i, j: (0, 0)),
            pl.BlockSpec((D, tn), lambda i, j: (0, j)),
        ],
        out_specs=pl.BlockSpec((tm, tn), lambda i, j: (i, j)),
        out_shape=jax.ShapeDtypeStruct((T, nc), BF16),
        scratch_shapes=[pltpu.VMEM((tm, D), BF16)],
        compiler_params=_params(("parallel", "arbitrary")),
        name="norm_inproj",
    )(h2, gain.reshape(1, D), w)


def _dot_t(a, b):
    return lax.dot_general(a, b, (((0,), (0,)), ((), ())), preferred_element_type=F32)


def _dot_nt(a, b):
    return lax.dot_general(a, b, (((1,), (1,)), ((), ())), preferred_element_type=F32)


def _ret_body(lg_ref, q_ref, k_ref, v_ref, g_ref, cos_ref, sin_ref, o_ref, rb_ref, *, seq_len):
    C = RET_CHUNK
    L = seq_len
    n_chunks = pl.cdiv(L, C)
    head = pl.program_id(1)
    lgf = lg_ref[0, head]
    lgb = lg_ref[1, head]

    def chunk(ref, n):
        lo, hi = n * C, min((n + 1) * C, L)
        x = ref[0, lo:hi, :]
        if hi - lo < C:
            x = jnp.concatenate([x, jnp.zeros((C - (hi - lo), x.shape[1]), x.dtype)], axis=0)
        return x

    def rotary(ref, n):
        x = chunk(ref, n).astype(F32)
        sl = slice(n * C, (n + 1) * C)
        return x * cos_ref[sl, :] + pltpu.roll(x, RET_DK // 2, 1) * sin_ref[sl, :]

    row = lax.broadcasted_iota(jnp.int32, (C, LANES), 0).astype(F32)
    ri = lax.broadcasted_iota(jnp.int32, (C, C), 0).astype(F32)
    ci = lax.broadcasted_iota(jnp.int32, (C, C), 1).astype(F32)
    decay = jnp.exp(jnp.where(ci <= ri, (ri - ci) * lgf, (ci - ri) * lgb))
    qf_dec = jnp.exp((row + 1.0) * lgf)
    qb_dec = jnp.exp((C - row) * lgb)
    kf_dec = jnp.exp((C - 1.0 - row) * lgf)
    kb_dec = jnp.exp(row * lgb)
    cf = jnp.exp(C * lgf)
    cb = jnp.exp(C * lgb)

    state = jnp.zeros((RET_DK, RET_DV), F32)
    for n in reversed(range(n_chunks)):
        rb_ref[n] = state.astype(BF16)
        if n > 0:
            kb = (rotary(k_ref, n) * kb_dec).astype(BF16)
            state = cb * state + _dot_t(kb, chunk(v_ref, n))

    state = jnp.zeros((RET_DK, RET_DV), F32)
    for n in range(n_chunks):
        q = rotary(q_ref, n) * (RET_DK ** -0.5)
        k = rotary(k_ref, n)
        v = chunk(v_ref, n)
        scores = _dot_nt(q.astype(BF16), k.astype(BF16)) * decay
        o = jnp.dot(scores.astype(BF16), v, preferred_element_type=F32)
        o += jnp.dot((q * qf_dec).astype(BF16), state.astype(BF16), preferred_element_type=F32)
        o += jnp.dot((q * qb_dec).astype(BF16), rb_ref[n], preferred_element_type=F32)
        if n + 1 < n_chunks:
            state = cf * state + _dot_t((k * kf_dec).astype(BF16), v)
        o = o * lax.rsqrt(jnp.mean(o * o, axis=-1, keepdims=True) + RMS_EPS)
        lo, hi = n * C, min((n + 1) * C, L)
        g = g_ref[0, lo:hi, :].astype(F32)
        o_ref[0, lo:hi, :] = (g * jax.nn.sigmoid(g) * o[: hi - lo]).astype(o_ref.dtype)


def _retention(proj3, lg, cos_t, sin_t):
    B, L, _ = proj3.shape
    n_chunks = pl.cdiv(L, RET_CHUNK)
    lp = n_chunks * RET_CHUNK
    k_blk = RET_HEADS
    v_blk = 2 * RET_HEADS * RET_DK // RET_DV
    g_blk = v_blk + RET_HEADS
    return pl.pallas_call(
        functools.partial(_ret_body, seq_len=L),
        grid=(B, RET_HEADS),
        in_specs=[
            pl.BlockSpec(memory_space=pltpu.SMEM),
            pl.BlockSpec((1, L, RET_DK), lambda b, h: (b, 0, h)),
            pl.BlockSpec((1, L, RET_DK), lambda b, h: (b, 0, k_blk + h)),
            pl.BlockSpec((1, L, RET_DV), lambda b, h: (b, 0, v_blk + h)),
            pl.BlockSpec((1, L, RET_DV), lambda b, h: (b, 0, g_blk + h)),
            pl.BlockSpec((lp, RET_DK), lambda b, h: (0, 0)),
            pl.BlockSpec((lp, RET_DK), lambda b, h: (0, 0)),
        ],
        out_specs=pl.BlockSpec((1, L, RET_DV), lambda b, h: (b, 0, h)),
        out_shape=jax.ShapeDtypeStruct((B, L, RET_HEADS * RET_DV), BF16),
        scratch_shapes=[pltpu.VMEM((n_chunks, RET_DK, RET_DV), BF16)],
        compiler_params=_params(("parallel", "arbitrary")),
        name="retention",
    )(lg, proj3, proj3, proj3, proj3, cos_t, sin_t)


def _rotary_tables(L):
    half = RET_DK // 2
    lp = _round_up(L, RET_CHUNK)
    inv = ROPE_THETA ** (-jnp.arange(half, dtype=F32) / half)
    ang = jnp.arange(lp, dtype=F32)[:, None] * inv[None, :]
    cos, sin = jnp.cos(ang), jnp.sin(ang)
    return jnp.concatenate([cos, cos], axis=1), jnp.concatenate([-sin, sin], axis=1)


def _short_conv_body(u_ref, w_ref, b_ref, o_ref):
    u = u_ref[0].astype(F32)
    L = u.shape[0]
    t = lax.broadcasted_iota(jnp.int32, u.shape, 0)
    prev = jnp.where(t == 0, 0.0, pltpu.roll(u, 1, 0))
    nxt = jnp.where(t == L - 1, 0.0, pltpu.roll(u, L - 1, 0))
    w = w_ref[...]
    o_ref[0] = (b_ref[...] + w[0:1] * prev + w[1:2] * u + w[2:3] * nxt).astype(o_ref.dtype)


def _short_conv(proj3, col0, short_w, short_b):
    B, L, _ = proj3.shape
    width = short_w.shape[1]
    cw = 512
    blk0 = col0 // cw
    return pl.pallas_call(
        _short_conv_body,
        grid=(B, width // cw),
        in_specs=[
            pl.BlockSpec((1, L, cw), lambda b, j: (b, 0, blk0 + j)),
            pl.BlockSpec((HY_SHORT, cw), lambda b, j: (0, j)),
            pl.BlockSpec((1, cw), lambda b, j: (0, j)),
        ],
        out_specs=pl.BlockSpec((1, L, cw), lambda b, j: (b, 0, j)),
        out_shape=jax.ShapeDtypeStruct((B, L, width), BF16),
        compiler_params=_params(("parallel", "parallel")),
        name="hyena_short_conv",
    )(proj3, short_w, short_b.reshape(1, width))


def _row_tiles(n, tile):
    return [(s, min(tile, n - s)) for s in range(0, n, tile)]


def _dft_tables(L):
    fp = _round_up(L + 1, MXU_DIM)
    n = 2 * L
    a = jnp.arange(fp, dtype=jnp.int32)
    m = (a[:, None] * a[None, :]) % n
    ang = m.astype(F32) * (2.0 * math.pi / n)
    return jnp.cos(ang).astype(BF16), jnp.sin(ang).astype(BF16)


def _spec_body(c_ref, s_ref, hs_ref, hd_ref, k1_ref, k2_ref, pad_ref, *, seq_len):
    L = seq_len
    fp = c_ref.shape[0]
    f = lax.broadcasted_iota(jnp.int32, (fp, 1), 0)
    wgt = jnp.where((f == 0) | (f == L), 1.0, 2.0)
    wgt = jnp.where(f <= L, wgt, 0.0) / (2.0 * L)
    pad_ref[L:, :] = jnp.zeros((fp - L, pad_ref.shape[1]), BF16)
    pad_ref[:L, :] = hs_ref[...].astype(BF16)
    k1_ref[...] = wgt * jnp.dot(c_ref[...], pad_ref[...], preferred_element_type=F32)
    pad_ref[:L, :] = hd_ref[...].astype(BF16)
    k2_ref[...] = -wgt * jnp.dot(s_ref[...], pad_ref[...], preferred_element_type=F32)


def _filter_spectrum(cmat, smat, hs, hd):
    L, width = hs.shape
    fp = cmat.shape[0]
    cw = 256
    resident = pl.BlockSpec((fp, fp), lambda j: (0, 0), pipeline_mode=pl.Buffered(1))
    return pl.pallas_call(
        functools.partial(_spec_body, seq_len=L),
        grid=(width // cw,),
        in_specs=[resident, resident,
                  pl.BlockSpec((L, cw), lambda j: (0, j)),
                  pl.BlockSpec((L, cw), lambda j: (0, j))],
        out_specs=[pl.BlockSpec((fp, cw), lambda j: (0, j))] * 2,
        out_shape=[jax.ShapeDtypeStruct((fp, width), F32)] * 2,
        scratch_shapes=[pltpu.VMEM((fp, cw), BF16)],
        compiler_params=_params(("arbitrary",)),
        name="hyena_filter_spectrum",
    )(cmat, smat, hs, hd)


def _conv_body(c_ref, s_ref, u_ref, x_ref, skip_ref, k1_ref, k2_ref, o_ref, pad_ref, e1_ref, e2_ref, *, seq_len):
    L = seq_len
    fp = c_ref.shape[0]
    cw = pad_ref.shape[1]
    nf = _round_up(L + 1, BF16_SUBLANES)
    pad_ref[L:, :] = jnp.zeros((fp - L, cw), BF16)
    pad_ref[:L, :] = u_ref[0]
    if nf < fp:
        e1_ref[nf:, :] = jnp.zeros((fp - nf, cw), BF16)
        e2_ref[nf:, :] = jnp.zeros((fp - nf, cw), BF16)
    for lo, sz in _row_tiles(nf, MXU_DIM):
        p = jnp.dot(c_ref[lo:lo + sz, :], pad_ref[...], preferred_element_type=F32)
        a = jnp.dot(s_ref[lo:lo + sz, :], pad_ref[...], preferred_element_type=F32)
        k1 = k1_ref[lo:lo + sz, :]
        k2 = k2_ref[lo:lo + sz, :]
        e1_ref[lo:lo + sz, :] = (p * k1 + a * k2).astype(BF16)
        e2_ref[lo:lo + sz, :] = (a * k1 - p * k2).astype(BF16)
    skip = skip_ref[...]
    for lo, sz in _row_tiles(L, MXU_DIM):
        y = jnp.dot(c_ref[lo:lo + sz, :], e1_ref[...], preferred_element_type=F32)
        y += jnp.dot(s_ref[lo:lo + sz, :], e2_ref[...], preferred_element_type=F32)
        u = u_ref[0, lo:lo + sz, :].astype(F32)
        x = x_ref[0, lo:lo + sz, :].astype(F32)
        o_ref[0, lo:lo + sz, :] = (x * (y + skip * u)).astype(o_ref.dtype)


def _long_conv(cmat, smat, u_arr, u_col0, x_arr, x_col0, skip, k1, k2, k_col0, width):
    B, L, _ = u_arr.shape
    fp = cmat.shape[0]
    cw = MXU_DIM
    ub, xb, kb = u_col0 // cw, x_col0 // cw, k_col0 // cw
    resident = pl.BlockSpec((fp, fp), lambda c, b: (0, 0), pipeline_mode=pl.Buffered(1))
    return pl.pallas_call(
        functools.partial(_conv_body, seq_len=L),
        grid=(width // cw, B),
        in_specs=[
            resident, resident,
            pl.BlockSpec((1, L, cw), lambda c, b: (b, 0, ub + c)),
            pl.BlockSpec((1, L, cw), lambda c, b: (b, 0, xb + c)),
            pl.BlockSpec((1, cw), lambda c, b: (0, c)),
            pl.BlockSpec((fp, cw), lambda c, b: (0, kb + c)),
            pl.BlockSpec((fp, cw), lambda c, b: (0, kb + c)),
        ],
        out_specs=pl.BlockSpec((1, L, cw), lambda c, b: (b, 0, c)),
        out_shape=jax.ShapeDtypeStruct((B, L, width), BF16),
        scratch_shapes=[pltpu.VMEM((fp, cw), BF16)] * 3,
        compiler_params=_params(("parallel", "arbitrary")),
        name="hyena_long_conv",
    )(cmat, smat, u_arr, x_arr, skip.reshape(1, width), k1, k2)


def _hyena_filters(L, w1, b1, w2, b2, w3, freq):
    hp = lax.Precision.HIGHEST
    width = w3.shape[1] // (2 * HY_ORDER)
    t = jnp.linspace(0.0, 1.0, L, dtype=F32)
    w = (2.0 * math.pi / L) * jnp.arange(L, dtype=F32)
    bands = jnp.linspace(1e-4, HY_BANDS - 1, HY_BANDS, dtype=F32)
    fw = w[:, None] * bands[None, :]
    z = jnp.concatenate([t[:, None], jnp.cos(fw), -jnp.sin(fw)], axis=-1)
    h = jnp.sin(freq * (jnp.dot(z, w1, precision=hp) + b1))
    h = jnp.sin(freq * (jnp.dot(h, w2, precision=hp) + b2))
    h = jnp.dot(h, w3, precision=hp).reshape(L, HY_ORDER, 2, width)
    deltas = jnp.abs(jnp.linspace(HY_MIN_DECAY, HY_MAX_DECAY, width, dtype=F32))
    h = h * jnp.exp(-t[:, None] * deltas[None, :])[:, None, None, :]
    hf = h[:, :, 0]
    hb = h[:, :, 1].at[0].set(0.0)
    scale = lax.rsqrt(jnp.sum(hf * hf, axis=0) + jnp.sum(hb * hb, axis=0) + 1e-6)
    hf, hb = hf * scale, hb * scale
    return (hf + hb).reshape(L, HY_ORDER * width), (hf - hb).reshape(L, HY_ORDER * width)


def _mix_body(ret_ref, hy_ref, gr_ref, gh_ref, h_ref, wr_ref, wh_ref, wo_ref, o_ref):
    ret = jnp.dot(ret_ref[...], wr_ref[...], preferred_element_type=F32)
    hyo = jnp.dot(hy_ref[...], wh_ref[...], preferred_element_type=F32)
    merged = (jax.nn.sigmoid(gr_ref[...].astype(F32)) * ret
              + jax.nn.sigmoid(gh_ref[...].astype(F32)) * hyo)
    o_ref[...] = h_ref[...] + jnp.dot(merged.astype(BF16), wo_ref[...], preferred_element_type=F32)


def _mix_out(ret2, hy2, proj2, gate_col0, h2, w_ret_o, w_hy_o, w_out):
    T, D = h2.shape
    tm = _pick_tile(T, 384, BF16_SUBLANES)
    gb = gate_col0 // D
    row = lambda w: pl.BlockSpec((tm, w), lambda i: (i, 0))
    full = lambda a: pl.BlockSpec(a.shape, lambda i: (0, 0))
    return pl.pallas_call(
        _mix_body,
        grid=(T // tm,),
        in_specs=[
            row(ret2.shape[1]), row(hy2.shape[1]),
            pl.BlockSpec((tm, D), lambda i: (i, gb)),
            pl.BlockSpec((tm, D), lambda i: (i, gb + 1)),
            row(D), full(w_ret_o), full(w_hy_o), full(w_out),
        ],
        out_specs=row(D),
        out_shape=jax.ShapeDtypeStruct((T, D), F32),
        compiler_params=_params(("parallel",)),
        name="merge_outproj",
    )(ret2, hy2, proj2, proj2, h2, w_ret_o, w_hy_o, w_out)


EXPERT_TILE = 256
SC_WINDOW = 64
SC_WORKERS = 32
SC_CHUNK = 2 * SC_WINDOW * SC_WORKERS


def _pack_bf16_pairs(x):
    n = x.shape[1] // 2
    xb = x.astype(BF16).astype(F32)
    hi = lax.bitcast_convert_type(xb[:, :n], jnp.uint32)
    lo = lax.bitcast_convert_type(xb[:, n:], jnp.uint32)
    return lax.bitcast_convert_type(hi | (lo >> 16), jnp.int32)


def _unpack_bf16_pairs(w):
    u = lax.bitcast_convert_type(w, jnp.uint32)
    hi = lax.bitcast_convert_type(u & jnp.uint32(0xFFFF0000), F32)
    lo = lax.bitcast_convert_type(u << 16, F32)
    return hi, lo


def _route(logits):
    lane = lax.broadcasted_iota(jnp.int32, logits.shape, 1)
    neg = -jnp.inf
    big = jnp.int32(1 << 20)
    gl = jnp.where(lane < N_GROUPS, logits, neg)
    gmax = jnp.max(gl, axis=1, keepdims=True)
    p_top = 1.0 / jnp.sum(jnp.exp(gl - gmax), axis=1, keepdims=True)
    g_idx = jnp.min(jnp.where(gl == gmax, lane, big), axis=1, keepdims=True)
    lo = N_GROUPS + g_idx * EXP_PER_GROUP
    el = jnp.where((lane >= lo) & (lane < lo + EXP_PER_GROUP), logits, neg)
    m1 = jnp.max(el, axis=1, keepdims=True)
    i1 = jnp.min(jnp.where(el == m1, lane, big), axis=1, keepdims=True)
    el2 = jnp.where(lane == i1, neg, el)
    m2 = jnp.max(el2, axis=1, keepdims=True)
    i2 = jnp.min(jnp.where(el2 == m2, lane, big), axis=1, keepdims=True)
    r = jnp.exp(m2 - m1)
    return i1 - N_GROUPS, i2 - N_GROUPS, p_top / (1.0 + r), p_top * r / (1.0 + r)


def _route_body(h_ref, g_ref, wr_ref, xpk_ref, meta_ref, cnt_ref, carry_ref):
    i = pl.program_id(0)

    @pl.when(i == 0)
    def _():
        carry_ref[...] = jnp.zeros_like(carry_ref)

    xn = _rms(h_ref[...], g_ref[...])
    xpk_ref[...] = _pack_bf16_pairs(xn)
    logits = jnp.dot(xn, wr_ref[...], preferred_element_type=F32, precision=lax.Precision.HIGHEST)
    e0, e1, w0, w1 = _route(logits)
    tm = logits.shape[0]
    lane = lax.broadcasted_iota(jnp.int32, (tm, LANES), 1)
    onehot = jnp.where((lane == e0) | (lane == e1), 1.0, 0.0)
    ri = lax.broadcasted_iota(jnp.int32, (tm, tm), 0)
    ci = lax.broadcasted_iota(jnp.int32, (tm, tm), 1)
    earlier = jnp.where(ci < ri, 1.0, 0.0).astype(BF16)
    prefix = carry_ref[...] + jnp.dot(earlier, onehot.astype(BF16), preferred_element_type=F32)
    r0 = jnp.sum(jnp.where(lane == e0, prefix, 0.0), axis=1, keepdims=True)
    r1 = jnp.sum(jnp.where(lane == e1, prefix, 0.0), axis=1, keepdims=True)
    cols = (e0.astype(F32), e1.astype(F32), r0, r1, w0, w1)
    meta = jnp.zeros((tm, LANES), F32)
    for c, val in enumerate(cols):
        meta = jnp.where(lane == c, val, meta)
    meta_ref[...] = meta
    carry_ref[...] += jnp.sum(onehot, axis=0, keepdims=True)
    cnt_ref[...] = carry_ref[...]


def _moe_route(h2, gain, w_router):
    T, D = h2.shape
    tm = _pick_tile(T, 1152, BF16_SUBLANES)
    return pl.pallas_call(
        _route_body,
        grid=(T // tm,),
        in_specs=[
            pl.BlockSpec((tm, D), lambda i: (i, 0)),
            pl.BlockSpec((1, D), lambda i: (0, 0)),
            pl.BlockSpec((D, LANES), lambda i: (0, 0)),
        ],
        out_specs=[
            pl.BlockSpec((tm, D // 2), lambda i: (i, 0)),
            pl.BlockSpec((tm, LANES), lambda i: (i, 0)),
            pl.BlockSpec((1, LANES), lambda i: (0, 0)),
        ],
        out_shape=[
            jax.ShapeDtypeStruct((T, D // 2), jnp.int32),
            jax.ShapeDtypeStruct((T, LANES), F32),
            jax.ShapeDtypeStruct((1, LANES), F32),
        ],
        scratch_shapes=[pltpu.VMEM((1, LANES), F32)],
        compiler_params=_params(("arbitrary",)),
        name="moe_route",
    )(h2, gain.reshape(1, D), w_router)


def _sc_gather(table, idx):
    n = idx.shape[0]
    width = table.shape[1]
    win = SC_WINDOW
    assert n % SC_CHUNK == 0
    per_worker = n // SC_WORKERS
    mesh = plsc.VectorSubcoreMesh(core_axis_name="c", subcore_axis_name="s")

    @functools.partial(
        pl.kernel, out_type=jax.ShapeDtypeStruct((n, width), table.dtype), mesh=mesh,
        scratch_types=[pltpu.VMEM((per_worker,), jnp.int32), pltpu.VMEM((2, win, width), table.dtype),
                       pltpu.SemaphoreType.DMA((2,)), pltpu.SemaphoreType.DMA((2,))],
        name="sc_row_gather")
    def gather(table_hbm, idx_hbm, out_hbm, idx_v, rows_v, gsem, osem):
        worker = lax.axis_index("s") * mesh.num_cores + lax.axis_index("c")
        base = worker * per_worker
        pltpu.sync_copy(idx_hbm.at[pl.ds(base, per_worker)], idx_v)

        @pl.loop(0, per_worker, step=2 * win)
        def _(off):
            fetch = [pltpu.async_copy(table_hbm.at[idx_v.at[pl.ds(off + s * win, win)]], rows_v.at[s], gsem.at[s])
                     for s in range(2)]
            store = []
            for s in range(2):
                fetch[s].wait()
                store.append(pltpu.async_copy(rows_v.at[s], out_hbm.at[pl.ds(base + off + s * win, win)],
                                              osem.at[s]))
            for s in range(2):
                store[s].wait()

    return gather(table, idx)


def _expert_body(te_ref, tv_ref, x_ref, wg_ref, wu_ref, wd_ref, y_ref):
    t = pl.program_id(0)
    half = x_ref.shape[1]

    @pl.when(tv_ref[t] > 0)
    def _():
        hi, lo = _unpack_bf16_pairs(x_ref[...])
        hi, lo = hi.astype(BF16), lo.astype(BF16)
        hg = jnp.dot(hi, wg_ref[0, :half, :], preferred_element_type=F32)
        hg += jnp.dot(lo, wg_ref[0, half:, :], preferred_element_type=F32)
        hu = jnp.dot(hi, wu_ref[0, :half, :], preferred_element_type=F32)
        hu += jnp.dot(lo, wu_ref[0, half:, :], preferred_element_type=F32)
        act = (hg * jax.nn.sigmoid(hg) * hu).astype(BF16)
        y_ref[...] = _pack_bf16_pairs(jnp.dot(act, wd_ref[0], preferred_element_type=F32))

    @pl.when(tv_ref[t] == 0)
    def _():
        y_ref[...] = jnp.zeros_like(y_ref)


def _moe_experts(xs, tile_expert, tile_valid, w_gate, w_up, w_down):
    NP, half = xs.shape
    E, D, FF = w_gate.shape
    tr = EXPERT_TILE
    grid_spec = pltpu.PrefetchScalarGridSpec(
        num_scalar_prefetch=2,
        grid=(NP // tr,),
        in_specs=[
            pl.BlockSpec((tr, half), lambda t, te, tv: (t, 0)),
            pl.BlockSpec((1, D, FF), lambda t, te, tv: (te[t], 0, 0)),
            pl.BlockSpec((1, D, FF), lambda t, te, tv: (te[t], 0, 0)),
            pl.BlockSpec((1, FF, D), lambda t, te, tv: (te[t], 0, 0)),
        ],
        out_specs=pl.BlockSpec((tr, half), lambda t, te, tv: (t, 0)),
    )
    return pl.pallas_call(
        _expert_body,
        grid_spec=grid_spec,
        out_shape=jax.ShapeDtypeStruct((NP, half), jnp.int32),
        compiler_params=_params(("arbitrary",)),
        name="moe_experts",
    )(tile_expert, tile_valid, xs, w_gate, w_up, w_down)


def _combine_body(h_ref, meta_ref, y0_ref, y1_ref, o_ref):
    half = y0_ref.shape[2]
    w0 = meta_ref[:, 4:5]
    w1 = meta_ref[:, 5:6]
    hi0, lo0 = _unpack_bf16_pairs(y0_ref[0])
    hi1, lo1 = _unpack_bf16_pairs(y1_ref[0])
    o_ref[:, :half] = h_ref[:, :half] + w0 * hi0 + w1 * hi1
    o_ref[:, half:] = h_ref[:, half:] + w0 * lo0 + w1 * lo1


def _moe_combine(h2, meta, yg):
    T, D = h2.shape
    tm = _pick_tile(T, 1152, BF16_SUBLANES)
    half = D // 2
    return pl.pallas_call(
        _combine_body,
        grid=(T // tm,),
        in_specs=[
            pl.BlockSpec((tm, D), lambda i: (i, 0)),
            pl.BlockSpec((tm, LANES), lambda i: (i, 0)),
            pl.BlockSpec((1, tm, half), lambda i: (0, i, 0)),
            pl.BlockSpec((1, tm, half), lambda i: (1, i, 0)),
        ],
        out_specs=pl.BlockSpec((tm, D), lambda i: (i, 0)),
        out_shape=jax.ShapeDtypeStruct((T, D), F32),
        compiler_params=_params(("parallel",)),
        name="moe_combine",
    )(h2, meta, yg, yg)


def _moe(h2, gain, w_router, w_gate, w_up, w_down):
    T, D = h2.shape
    E = w_gate.shape[0]
    tr = EXPERT_TILE
    n_sorted = _round_up(2 * T + E * (tr - 1), math.lcm(SC_CHUNK, tr))
    t_pad = _round_up(T, SC_CHUNK // 2)

    xpk, meta, counts = _moe_route(h2, gain, w_router)

    cnt = counts[0, :E].astype(jnp.int32)
    padded = (cnt + tr - 1) // tr * tr
    ends = jnp.cumsum(padded)
    starts = ends - padded
    eid = meta[:, 0:2].astype(jnp.int32)
    pos = starts[eid] + meta[:, 2:4].astype(jnp.int32)
    tok = jnp.broadcast_to(jnp.arange(T, dtype=jnp.int32)[:, None], (T, 2))
    src_tok = jnp.zeros((n_sorted,), jnp.int32).at[pos.reshape(-1)].set(tok.reshape(-1))
    tile_start = jnp.arange(n_sorted // tr, dtype=jnp.int32) * tr
    tile_expert = jnp.minimum(jnp.searchsorted(ends, tile_start, side="right"), E - 1).astype(jnp.int32)
    tile_valid = jnp.clip(cnt[tile_expert] - (tile_start - starts[tile_expert]), 0, tr).astype(jnp.int32)

    xs = _sc_gather(xpk, src_tok)
    ys = _moe_experts(xs, tile_expert, tile_valid, w_gate, w_up, w_down)
    back = jnp.zeros((2, t_pad), jnp.int32).at[:, :T].set(pos.T)
    yg = _sc_gather(ys, back.reshape(-1)).reshape(2, t_pad, D // 2)
    return _moe_combine(h2, meta, yg)


def _final_body(h_ref, g_ref, o_ref):
    o_ref[0] = _rms(h_ref[0, N_META:, :], g_ref[...])


def _final_norm(h3, gain, b0, nb):
    _, L, D = h3.shape
    return pl.pallas_call(
        _final_body,
        grid=(nb,),
        in_specs=[pl.BlockSpec((1, L, D), lambda b: (b0 + b, 0, 0)),
                  pl.BlockSpec((1, D), lambda b: (0, 0))],
        out_specs=pl.BlockSpec((1, L - N_META, D), lambda b: (b, 0, 0)),
        out_shape=jax.ShapeDtypeStruct((nb, L - N_META, D), F32),
        compiler_params=_params(("parallel",)),
        name="final_norm",
    )(h3, gain.reshape(1, D))


def kernel(x_prompt, x_sample, meta_tokens, norm_mix, w_in, ret_decay_fwd, ret_decay_bwd, hy_short_w, hy_short_b, hy_filt_w1, hy_filt_b1, hy_filt_w2, hy_filt_b2, hy_filt_w3, hy_sin_freq, hy_skip, w_ret_o, w_hy_o, w_out, norm_ffn, router_group, router_expert, moe_w_gate, moe_w_up, moe_w_down, norm_final):
    assert x_prompt.shape[1:] == x_sample.shape[1:]
    nbp, nbs = x_prompt.shape[0], x_sample.shape[0]
    B = nbp + nbs
    D = x_prompt.shape[2]
    L = N_META + x_prompt.shape[1]
    T = B * L
    depth = w_in.shape[0]
    q_w = RET_HEADS * RET_DK
    v_w = RET_HEADS * RET_DV
    hy_w = hy_skip.shape[2]
    hy_col0 = 2 * q_w + 2 * v_w
    gate_col0 = hy_col0 + 3 * hy_w
    assert D == q_w and w_in.shape[2] == gate_col0 + 2 * D

    x = jnp.concatenate([x_prompt, x_sample], axis=0)
    meta = jnp.broadcast_to(meta_tokens[None].astype(x.dtype), (B, N_META, D))
    h = jnp.concatenate([meta, x], axis=1).reshape(T, D)

    cos_t, sin_t = _rotary_tables(L)
    cmat, smat = _dft_tables(L)
    router = jnp.concatenate([router_group, router_expert], axis=2).astype(F32)
    router = jnp.pad(router, ((0, 0), (0, 0), (0, LANES - router.shape[2])))

    for i in range(depth):
        proj = _inproj(h, norm_mix[i], w_in[i].astype(BF16))
        proj3 = proj.reshape(B, L, -1)

        lg = jnp.stack([jax.nn.log_sigmoid(ret_decay_fwd[i].astype(F32)),
                        jax.nn.log_sigmoid(ret_decay_bwd[i].astype(F32))])
        ret = _retention(proj3, lg, cos_t, sin_t)

        hs, hd = _hyena_filters(L, hy_filt_w1[i], hy_filt_b1[i], hy_filt_w2[i], hy_filt_b2[i],
                                hy_filt_w3[i], hy_sin_freq[i])
        k1, k2 = _filter_spectrum(cmat, smat, hs, hd)
        vx = _short_conv(proj3, hy_col0, hy_short_w[i].astype(F32), hy_short_b[i].astype(F32))
        z = _long_conv(cmat, smat, vx, 0, vx, hy_w, hy_skip[i, 0].astype(F32), k1, k2, 0, hy_w)
        z = _long_conv(cmat, smat, z, 0, vx, 2 * hy_w, hy_skip[i, 1].astype(F32), k1, k2, hy_w, hy_w)

        h = _mix_out(ret.reshape(T, v_w), z.reshape(T, hy_w), proj, gate_col0, h,
                     w_ret_o[i].astype(BF16), w_hy_o[i].astype(BF16), w_out[i].astype(BF16))
        h = _moe(h, norm_ffn[i], router[i], moe_w_gate[i].astype(BF16), moe_w_up[i].astype(BF16),
                 moe_w_down[i].astype(BF16))

    h3 = h.reshape(B, L, D)
    return _final_norm(h3, norm_final, 0, nbp), _final_norm(h3, norm_final, nbp, nbs)
```

```python
import functools
import math

import jax
import jax.numpy as jnp
from jax import lax
from jax.experimental import pallas as pl
from jax.experimental.pallas import tpu as pltpu
from jax.experimental.pallas import tpu_sc as plsc

N_META = 16
RET_HEADS = 8
RET_DK = 128
RET_DV = 256
ROPE_THETA = 10000.0
HY_ORDER = 2
HY_SHORT = 3
HY_EMB = 33
HY_BANDS = (HY_EMB - 1) // 2
HY_DECAY_TARGET = 1e-2
HY_MIN_DECAY = math.log(HY_DECAY_TARGET) / 1.5
HY_MAX_DECAY = math.log(HY_DECAY_TARGET) / 0.3
N_GROUPS = 4
EXP_PER_GROUP = 8
N_EXPERTS = N_GROUPS * EXP_PER_GROUP
RMS_EPS = 1e-6

LANES = 128
BF16_SUBLANES = 16
MXU_DIM = 256
RET_CHUNK = 256
VMEM_LIMIT = 56 * 1024 * 1024

F32 = jnp.float32
BF16 = jnp.bfloat16


def _round_up(n, m):
    return (n + m - 1) // m * m


def _pick_tile(n, target, mult):
    best = None
    for t in range(mult, min(n, target) + 1, mult):
        if n % t == 0:
            best = t
    assert best is not None, (n, target, mult)
    return best


def _params(sem):
    return pltpu.CompilerParams(dimension_semantics=sem, vmem_limit_bytes=VMEM_LIMIT)


def _rms(x, gain):
    ms = jnp.mean(x * x, axis=-1, keepdims=True)
    return x * lax.rsqrt(ms + RMS_EPS) * gain


def _inproj_body(h_ref, g_ref, w_ref, o_ref, xn_ref):
    @pl.when(pl.program_id(1) == 0)
    def _():
        xn_ref[...] = _rms(h_ref[...], g_ref[...]).astype(BF16)

    o_ref[...] = jnp.dot(xn_ref[...], w_ref[...], preferred_element_type=F32).astype(o_ref.dtype)


def _inproj_moe_body(h_ref, meta_ref, y0_ref, y1_ref, g_ref, w_ref, o_ref, hn_ref, xn_ref):
    @pl.when(pl.program_id(1) == 0)
    def _():
        _moe_combine_into(hn_ref, h_ref[...], meta_ref[...], y0_ref[0], y1_ref[0])
        xn_ref[...] = _rms(hn_ref[...], g_ref[...]).astype(BF16)

    o_ref[...] = jnp.dot(xn_ref[...], w_ref[...], preferred_element_type=F32).astype(o_ref.dtype)


def _inproj(h2, gain, w, moe=None):
    T, D = h2.shape
    nc = w.shape[1]
    tm = _pick_tile(T, 1152, BF16_SUBLANES)
    tn = _pick_tile(nc, 1024, LANES)
    common = dict(
        grid=(T // tm, nc // tn),
        scratch_shapes=[pltpu.VMEM((tm, D), BF16)],
        compiler_params=_params(("parallel", "arbitrary")),
    )
    h_spec = pl.BlockSpec((tm, D), lambda i, j: (i, 0))
    g_spec = pl.BlockSpec((1, D), lambda i, j: (0, 0))
    w_spec = pl.BlockSpec((D, tn), lambda i, j: (0, j))
    o_spec = pl.BlockSpec((tm, tn), lambda i, j: (i, j))
    o_shape = jax.ShapeDtypeStruct((T, nc), BF16)
    if moe is None:
        proj = pl.pallas_call(_inproj_body, in_specs=[h_spec, g_spec, w_spec], out_specs=o_spec,
                              out_shape=o_shape, name="norm_inproj", **common)(h2, gain.reshape(1, D), w)
        return proj, h2
    meta, yg = moe
    return pl.pallas_call(
        _inproj_moe_body,
        in_specs=[h_spec, pl.BlockSpec((tm, meta.shape[1]), lambda i, j: (i, 0)),
                  pl.BlockSpec((1, tm, D // 2), lambda i, j: (0, i, 0)),
                  pl.BlockSpec((1, tm, D // 2), lambda i, j: (1, i, 0)),
                  g_spec, w_spec],
        out_specs=[o_spec, h_spec],
        out_shape=[o_shape, jax.ShapeDtypeStruct((T, D), F32)],
        name="combine_norm_inproj", **common,
    )(h2, meta, yg, yg, gain.reshape(1, D), w)


def _dot_t(a, b):
    return lax.dot_general(a, b, (((0,), (0,)), ((), ())), preferred_element_type=F32)


def _dot_nt(a, b):
    return lax.dot_general(a, b, (((1,), (1,)), ((), ())), preferred_element_type=F32)


def _ret_body(lg_ref, q_ref, k_ref, v_ref, g_ref, cos_ref, sin_ref, o_ref, rb_ref, *, seq_len):
    C = RET_CHUNK
    L = seq_len
    n_chunks = pl.cdiv(L, C)
    head = pl.program_id(1)
    lgf = lg_ref[0, head]
    lgb = lg_ref[1, head]

    def chunk(ref, n):
        lo, hi = n * C, min((n + 1) * C, L)
        x = ref[0, lo:hi, :]
        if hi - lo < C:
            x = jnp.concatenate([x, jnp.zeros((C - (hi - lo), x.shape[1]), x.dtype)], axis=0)
        return x

    def rotary(ref, n):
        x = chunk(ref, n).astype(F32)
        sl = slice(n * C, (n + 1) * C)
        return x * cos_ref[sl, :] + pltpu.roll(x, RET_DK // 2, 1) * sin_ref[sl, :]

    row = lax.broadcasted_iota(jnp.int32, (C, LANES), 0).astype(F32)
    ri = lax.broadcasted_iota(jnp.int32, (C, C), 0).astype(F32)
    ci = lax.broadcasted_iota(jnp.int32, (C, C), 1).astype(F32)
    decay = jnp.exp(jnp.where(ci <= ri, (ri - ci) * lgf, (ci - ri) * lgb))
    qf_dec = jnp.exp((row + 1.0) * lgf)
    qb_dec = jnp.exp((C - row) * lgb)
    kf_dec = jnp.exp((C - 1.0 - row) * lgf)
    kb_dec = jnp.exp(row * lgb)
    cf = jnp.exp(C * lgf)
    cb = jnp.exp(C * lgb)

    state = jnp.zeros((RET_DK, RET_DV), F32)
    for n in reversed(range(n_chunks)):
        rb_ref[n] = state.astype(BF16)
        if n > 0:
            kb = (rotary(k_ref, n) * kb_dec).astype(BF16)
            state = cb * state + _dot_t(kb, chunk(v_ref, n))

    state = jnp.zeros((RET_DK, RET_DV), F32)
    for n in range(n_chunks):
        q = rotary(q_ref, n) * (RET_DK ** -0.5)
        k = rotary(k_ref, n)
        v = chunk(v_ref, n)
        scores = _dot_nt(q.astype(BF16), k.astype(BF16)) * decay
        o = jnp.dot(scores.astype(BF16), v, preferred_element_type=F32)
        o += jnp.dot((q * qf_dec).astype(BF16), state.astype(BF16), preferred_element_type=F32)
        o += jnp.dot((q * qb_dec).astype(BF16), rb_ref[n], preferred_element_type=F32)
        if n + 1 < n_chunks:
            state = cf * state + _dot_t((k * kf_dec).astype(BF16), v)
        o = o * lax.rsqrt(jnp.mean(o * o, axis=-1, keepdims=True) + RMS_EPS)
        lo, hi = n * C, min((n + 1) * C, L)
        g = g_ref[0, lo:hi, :].astype(F32)
        o_ref[0, lo:hi, :] = (g * jax.nn.sigmoid(g) * o[: hi - lo]).astype(o_ref.dtype)


def _retention(proj3, lg, cos_t, sin_t):
    B, L, _ = proj3.shape
    n_chunks = pl.cdiv(L, RET_CHUNK)
    lp = n_chunks * RET_CHUNK
    k_blk = RET_HEADS
    v_blk = 2 * RET_HEADS * RET_DK // RET_DV
    g_blk = v_blk + RET_HEADS
    return pl.pallas_call(
        functools.partial(_ret_body, seq_len=L),
        grid=(B, RET_HEADS),
        in_specs=[
            pl.BlockSpec(memory_space=pltpu.SMEM),
            pl.BlockSpec((1, L, RET_DK), lambda b, h: (b, 0, h)),
            pl.BlockSpec((1, L, RET_DK), lambda b, h: (b, 0, k_blk + h)),
            pl.BlockSpec((1, L, RET_DV), lambda b, h: (b, 0, v_blk + h)),
            pl.BlockSpec((1, L, RET_DV), lambda b, h: (b, 0, g_blk + h)),
            pl.BlockSpec((lp, RET_DK), lambda b, h: (0, 0)),
            pl.BlockSpec((lp, RET_DK), lambda b, h: (0, 0)),
        ],
        out_specs=pl.BlockSpec((1, L, RET_DV), lambda b, h: (b, 0, h)),
        out_shape=jax.ShapeDtypeStruct((B, L, RET_HEADS * RET_DV), BF16),
        scratch_shapes=[pltpu.VMEM((n_chunks, RET_DK, RET_DV), BF16)],
        compiler_params=_params(("parallel", "arbitrary")),
        name="retention",
    )(lg, proj3, proj3, proj3, proj3, cos_t, sin_t)


def _rotary_tables(L):
    half = RET_DK // 2
    lp = _round_up(L, RET_CHUNK)
    inv = ROPE_THETA ** (-jnp.arange(half, dtype=F32) / half)
    ang = jnp.arange(lp, dtype=F32)[:, None] * inv[None, :]
    cos, sin = jnp.cos(ang), jnp.sin(ang)
    return jnp.concatenate([cos, cos], axis=1), jnp.concatenate([-sin, sin], axis=1)


def _short_conv_body(u_ref, w_ref, b_ref, o_ref):
    u = u_ref[0].astype(F32)
    L = u.shape[0]
    t = lax.broadcasted_iota(jnp.int32, u.shape, 0)
    prev = jnp.where(t == 0, 0.0, pltpu.roll(u, 1, 0))
    nxt = jnp.where(t == L - 1, 0.0, pltpu.roll(u, L - 1, 0))
    w = w_ref[...]
    o_ref[0] = (b_ref[...] + w[0:1] * prev + w[1:2] * u + w[2:3] * nxt).astype(o_ref.dtype)


def _short_conv(proj3, col0, short_w, short_b):
    B, L, _ = proj3.shape
    width = short_w.shape[1]
    cw = 512
    blk0 = col0 // cw
    return pl.pallas_call(
        _short_conv_body,
        grid=(B, width // cw),
        in_specs=[
            pl.BlockSpec((1, L, cw), lambda b, j: (b, 0, blk0 + j)),
            pl.BlockSpec((HY_SHORT, cw), lambda b, j: (0, j)),
            pl.BlockSpec((1, cw), lambda b, j: (0, j)),
        ],
        out_specs=pl.BlockSpec((1, L, cw), lambda b, j: (b, 0, j)),
        out_shape=jax.ShapeDtypeStruct((B, L, width), BF16),
        compiler_params=_params(("parallel", "parallel")),
        name="hyena_short_conv",
    )(proj3, short_w, short_b.reshape(1, width))


def _row_tiles(n, tile):
    return [(s, min(tile, n - s)) for s in range(0, n, tile)]


def _dft_tables(L):
    fp = _round_up(L + 1, MXU_DIM)
    n = 2 * L
    a = jnp.arange(fp, dtype=jnp.int32)
    m = (a[:, None] * a[None, :]) % n
    ang = m.astype(F32) * (2.0 * math.pi / n)
    return jnp.cos(ang).astype(BF16), jnp.sin(ang).astype(BF16)


def _spec_body(c_ref, s_ref, hs_ref, hd_ref, k1_ref, k2_ref, pad_ref, *, seq_len):
    L = seq_len
    fp = c_ref.shape[0]
    f = lax.broadcasted_iota(jnp.int32, (fp, 1), 0)
    wgt = jnp.where((f == 0) | (f == L), 1.0, 2.0)
    wgt = jnp.where(f <= L, wgt, 0.0) / (2.0 * L)
    pad_ref[L:, :] = jnp.zeros((fp - L, pad_ref.shape[1]), BF16)
    pad_ref[:L, :] = hs_ref[...].astype(BF16)
    k1_ref[...] = wgt * jnp.dot(c_ref[...], pad_ref[...], preferred_element_type=F32)
    pad_ref[:L, :] = hd_ref[...].astype(BF16)
    k2_ref[...] = -wgt * jnp.dot(s_ref[...], pad_ref[...], preferred_element_type=F32)


def _filter_spectrum(cmat, smat, hs, hd):
    L, width = hs.shape
    fp = cmat.shape[0]
    cw = 256
    resident = pl.BlockSpec((fp, fp), lambda j: (0, 0), pipeline_mode=pl.Buffered(1))
    return pl.pallas_call(
        functools.partial(_spec_body, seq_len=L),
        grid=(width // cw,),
        in_specs=[resident, resident,
                  pl.BlockSpec((L, cw), lambda j: (0, j)),
                  pl.BlockSpec((L, cw), lambda j: (0, j))],
        out_specs=[pl.BlockSpec((fp, cw), lambda j: (0, j))] * 2,
        out_shape=[jax.ShapeDtypeStruct((fp, width), F32)] * 2,
        scratch_shapes=[pltpu.VMEM((fp, cw), BF16)],
        compiler_params=_params(("arbitrary",)),
        name="hyena_filter_spectrum",
    )(cmat, smat, hs, hd)


def _conv_body(c_ref, s_ref, u_ref, x_ref, skip_ref, k1_ref, k2_ref, o_ref, pad_ref, e1_ref, e2_ref, *, seq_len):
    L = seq_len
    fp = c_ref.shape[0]
    cw = pad_ref.shape[1]
    nf = _round_up(L + 1, BF16_SUBLANES)
    pad_ref[L:, :] = jnp.zeros((fp - L, cw), BF16)
    pad_ref[:L, :] = u_ref[0]
    if nf < fp:
        e1_ref[nf:, :] = jnp.zeros((fp - nf, cw), BF16)
        e2_ref[nf:, :] = jnp.zeros((fp - nf, cw), BF16)
    for lo, sz in _row_tiles(nf, MXU_DIM):
        p = jnp.dot(c_ref[lo:lo + sz, :], pad_ref[...], preferred_element_type=F32)
        a = jnp.dot(s_ref[lo:lo + sz, :], pad_ref[...], preferred_element_type=F32)
        k1 = k1_ref[lo:lo + sz, :]
        k2 = k2_ref[lo:lo + sz, :]
        e1_ref[lo:lo + sz, :] = (p * k1 + a * k2).astype(BF16)
        e2_ref[lo:lo + sz, :] = (a * k1 - p * k2).astype(BF16)
    skip = skip_ref[...]
    for lo, sz in _row_tiles(L, MXU_DIM):
        y = jnp.dot(c_ref[lo:lo + sz, :], e1_ref[...], preferred_element_type=F32)
        y += jnp.dot(s_ref[lo:lo + sz, :], e2_ref[...], preferred_element_type=F32)
        u = u_ref[0, lo:lo + sz, :].astype(F32)
        x = x_ref[0, lo:lo + sz, :].astype(F32)
        o_ref[0, lo:lo + sz, :] = (x * (y + skip * u)).astype(o_ref.dtype)


def _long_conv(cmat, smat, u_arr, u_col0, x_arr, x_col0, skip, k1, k2, k_col0, width):
    B, L, _ = u_arr.shape
    fp = cmat.shape[0]
    cw = MXU_DIM
    ub, xb, kb = u_col0 // cw, x_col0 // cw, k_col0 // cw
    resident = pl.BlockSpec((fp, fp), lambda c, b: (0, 0), pipeline_mode=pl.Buffered(1))
    return pl.pallas_call(
        functools.partial(_conv_body, seq_len=L),
        grid=(width // cw, B),
        in_specs=[
            resident, resident,
            pl.BlockSpec((1, L, cw), lambda c, b: (b, 0, ub + c)),
            pl.BlockSpec((1, L, cw), lambda c, b: (b, 0, xb + c)),
            pl.BlockSpec((1, cw), lambda c, b: (0, c)),
            pl.BlockSpec((fp, cw), lambda c, b: (0, kb + c)),
            pl.BlockSpec((fp, cw), lambda c, b: (0, kb + c)),
        ],
        out_specs=pl.BlockSpec((1, L, cw), lambda c, b: (b, 0, c)),
        out_shape=jax.ShapeDtypeStruct((B, L, width), BF16),
        scratch_shapes=[pltpu.VMEM((fp, cw), BF16)] * 3,
        compiler_params=_params(("parallel", "arbitrary")),
        name="hyena_long_conv",
    )(cmat, smat, u_arr, x_arr, skip.reshape(1, width), k1, k2)


def _hyena_filters(L, w1, b1, w2, b2, w3, freq):
    hp = lax.Precision.HIGHEST
    width = w3.shape[1] // (2 * HY_ORDER)
    t = jnp.linspace(0.0, 1.0, L, dtype=F32)
    w = (2.0 * math.pi / L) * jnp.arange(L, dtype=F32)
    bands = jnp.linspace(1e-4, HY_BANDS - 1, HY_BANDS, dtype=F32)
    fw = w[:, None] * bands[None, :]
    z = jnp.concatenate([t[:, None], jnp.cos(fw), -jnp.sin(fw)], axis=-1)
    h = jnp.sin(freq * (jnp.dot(z, w1, precision=hp) + b1))
    h = jnp.sin(freq * (jnp.dot(h, w2, precision=hp) + b2))
    h = jnp.dot(h, w3, precision=hp).reshape(L, HY_ORDER, 2, width)
    deltas = jnp.abs(jnp.linspace(HY_MIN_DECAY, HY_MAX_DECAY, width, dtype=F32))
    h = h * jnp.exp(-t[:, None] * deltas[None, :])[:, None, None, :]
    hf = h[:, :, 0]
    hb = h[:, :, 1].at[0].set(0.0)
    scale = lax.rsqrt(jnp.sum(hf * hf, axis=0) + jnp.sum(hb * hb, axis=0) + 1e-6)
    hf, hb = hf * scale, hb * scale
    return (hf + hb).reshape(L, HY_ORDER * width), (hf - hb).reshape(L, HY_ORDER * width)


def _mix_body(ret_ref, hy_ref, gr_ref, gh_ref, h_ref, wr_ref, wh_ref, wo_ref, o_ref):
    ret = jnp.dot(ret_ref[...], wr_ref[...], preferred_element_type=F32)
    hyo = jnp.dot(hy_ref[...], wh_ref[...], preferred_element_type=F32)
    merged = (jax.nn.sigmoid(gr_ref[...].astype(F32)) * ret
              + jax.nn.sigmoid(gh_ref[...].astype(F32)) * hyo)
    o_ref[...] = h_ref[...] + jnp.dot(merged.astype(BF16), wo_ref[...], preferred_element_type=F32)


def _mix_out(ret2, hy2, proj2, gate_col0, h2, w_ret_o, w_hy_o, w_out):
    T, D = h2.shape
    tm = _pick_tile(T, 384, BF16_SUBLANES)
    gb = gate_col0 // D
    row = lambda w: pl.BlockSpec((tm, w), lambda i: (i, 0))
    full = lambda a: pl.BlockSpec(a.shape, lambda i: (0, 0))
    return pl.pallas_call(
        _mix_body,
        grid=(T // tm,),
        in_specs=[
            row(ret2.shape[1]), row(hy2.shape[1]),
            pl.BlockSpec((tm, D), lambda i: (i, gb)),
            pl.BlockSpec((tm, D), lambda i: (i, gb + 1)),
            row(D), full(w_ret_o), full(w_hy_o), full(w_out),
        ],
        out_specs=row(D),
        out_shape=jax.ShapeDtypeStruct((T, D), F32),
        compiler_params=_params(("parallel",)),
        name="merge_outproj",
    )(ret2, hy2, proj2, proj2, h2, w_ret_o, w_hy_o, w_out)


EXPERT_TILE = 512
META_COLS = 8
SC_WINDOW = 64
SC_WORKERS = 32
SC_CHUNK = 2 * SC_WINDOW * SC_WORKERS


def _pack_bf16_pairs(x):
    n = x.shape[1] // 2
    xb = x.astype(BF16).astype(F32)
    hi = lax.bitcast_convert_type(xb[:, :n], jnp.uint32)
    lo = lax.bitcast_convert_type(xb[:, n:], jnp.uint32)
    return lax.bitcast_convert_type(hi | (lo >> 16), jnp.int32)


def _unpack_bf16_pairs(w):
    u = lax.bitcast_convert_type(w, jnp.uint32)
    hi = lax.bitcast_convert_type(u & jnp.uint32(0xFFFF0000), F32)
    lo = lax.bitcast_convert_type(u << 16, F32)
    return hi, lo


def _route(logits):
    lane = lax.broadcasted_iota(jnp.int32, logits.shape, 1)
    neg = -jnp.inf
    big = jnp.int32(1 << 20)
    gl = jnp.where(lane < N_GROUPS, logits, neg)
    gmax = jnp.max(gl, axis=1, keepdims=True)
    p_top = 1.0 / jnp.sum(jnp.exp(gl - gmax), axis=1, keepdims=True)
    g_idx = jnp.min(jnp.where(gl == gmax, lane, big), axis=1, keepdims=True)
    lo = N_GROUPS + g_idx * EXP_PER_GROUP
    el = jnp.where((lane >= lo) & (lane < lo + EXP_PER_GROUP), logits, neg)
    m1 = jnp.max(el, axis=1, keepdims=True)
    i1 = jnp.min(jnp.where(el == m1, lane, big), axis=1, keepdims=True)
    el2 = jnp.where(lane == i1, neg, el)
    m2 = jnp.max(el2, axis=1, keepdims=True)
    i2 = jnp.min(jnp.where(el2 == m2, lane, big), axis=1, keepdims=True)
    r = jnp.exp(m2 - m1)
    return i1 - N_GROUPS, i2 - N_GROUPS, p_top / (1.0 + r), p_top * r / (1.0 + r)


def _route_body(h_ref, g_ref, wr_ref, xpk_ref, meta_ref, cnt_ref, carry_ref):
    i = pl.program_id(0)

    @pl.when(i == 0)
    def _():
        carry_ref[...] = jnp.zeros_like(carry_ref)

    xn = _rms(h_ref[...], g_ref[...])
    xpk_ref[...] = _pack_bf16_pairs(xn)
    logits = jnp.dot(xn, wr_ref[...], preferred_element_type=F32, precision=lax.Precision.HIGHEST)
    e0, e1, w0, w1 = _route(logits)
    tm = logits.shape[0]
    lane = lax.broadcasted_iota(jnp.int32, (tm, LANES), 1)
    onehot = jnp.where((lane == e0) | (lane == e1), 1.0, 0.0)
    ri = lax.broadcasted_iota(jnp.int32, (tm, tm), 0)
    ci = lax.broadcasted_iota(jnp.int32, (tm, tm), 1)
    earlier = jnp.where(ci < ri, 1.0, 0.0).astype(BF16)
    prefix = carry_ref[...] + jnp.dot(earlier, onehot.astype(BF16), preferred_element_type=F32)
    r0 = jnp.sum(jnp.where(lane == e0, prefix, 0.0), axis=1, keepdims=True)
    r1 = jnp.sum(jnp.where(lane == e1, prefix, 0.0), axis=1, keepdims=True)
    cols = (e0.astype(F32), e1.astype(F32), r0, r1, w0, w1)
    meta = jnp.zeros((tm, LANES), F32)
    for c, val in enumerate(cols):
        meta = jnp.where(lane == c, val, meta)
    meta_ref[...] = meta[:, :META_COLS]
    carry_ref[...] += jnp.sum(onehot, axis=0, keepdims=True)
    cnt_ref[...] = carry_ref[...]


def _moe_route(h2, gain, w_router):
    T, D = h2.shape
    tm = _pick_tile(T, 1152, BF16_SUBLANES)
    return pl.pallas_call(
        _route_body,
        grid=(T // tm,),
        in_specs=[
            pl.BlockSpec((tm, D), lambda i: (i, 0)),
            pl.BlockSpec((1, D), lambda i: (0, 0)),
            pl.BlockSpec((D, LANES), lambda i: (0, 0)),
        ],
        out_specs=[
            pl.BlockSpec((tm, D // 2), lambda i: (i, 0)),
            pl.BlockSpec((tm, META_COLS), lambda i: (i, 0)),
            pl.BlockSpec((1, LANES), lambda i: (0, 0)),
        ],
        out_shape=[
            jax.ShapeDtypeStruct((T, D // 2), jnp.int32),
            jax.ShapeDtypeStruct((T, META_COLS), F32),
            jax.ShapeDtypeStruct((1, LANES), F32),
        ],
        scratch_shapes=[pltpu.VMEM((1, LANES), F32)],
        compiler_params=_params(("arbitrary",)),
        name="moe_route",
    )(h2, gain.reshape(1, D), w_router)


def _sc_gather(table, idx):
    n = idx.shape[0]
    width = table.shape[1]
    win = SC_WINDOW
    assert n % SC_CHUNK == 0
    per_worker = n // SC_WORKERS
    mesh = plsc.VectorSubcoreMesh(core_axis_name="c", subcore_axis_name="s")

    @functools.partial(
        pl.kernel, out_type=jax.ShapeDtypeStruct((n, width), table.dtype), mesh=mesh,
        scratch_types=[pltpu.VMEM((per_worker,), jnp.int32), pltpu.VMEM((2, win, width), table.dtype),
                       pltpu.SemaphoreType.DMA((2,)), pltpu.SemaphoreType.DMA((2,))],
        name="sc_row_gather")
    def gather(table_hbm, idx_hbm, out_hbm, idx_v, rows_v, gsem, osem):
        worker = lax.axis_index("s") * mesh.num_cores + lax.axis_index("c")
        base = worker * per_worker
        pltpu.sync_copy(idx_hbm.at[pl.ds(base, per_worker)], idx_v)

        @pl.loop(0, per_worker, step=2 * win)
        def _(off):
            fetch = [pltpu.async_copy(table_hbm.at[idx_v.at[pl.ds(off + s * win, win)]], rows_v.at[s], gsem.at[s])
                     for s in range(2)]
            store = []
            for s in range(2):
                fetch[s].wait()
                store.append(pltpu.async_copy(rows_v.at[s], out_hbm.at[pl.ds(base + off + s * win, win)],
                                              osem.at[s]))
            for s in range(2):
                store[s].wait()

    return gather(table, idx)


def _expert_body(te_ref, tv_ref, x_ref, wg_ref, wu_ref, wd_ref, y_ref, wg_s, wu_s, wd_s):
    t = pl.program_id(0)
    half = x_ref.shape[1]

    @pl.when((t == 0) | (te_ref[t] != te_ref[jnp.maximum(t - 1, 0)]))
    def _():
        wg_s[...] = wg_ref[0].astype(BF16)
        wu_s[...] = wu_ref[0].astype(BF16)
        wd_s[...] = wd_ref[0].astype(BF16)

    @pl.when(tv_ref[t] > 0)
    def _():
        hi, lo = _unpack_bf16_pairs(x_ref[...])
        hi, lo = hi.astype(BF16), lo.astype(BF16)
        hg = jnp.dot(hi, wg_s[:half, :], preferred_element_type=F32)
        hg += jnp.dot(lo, wg_s[half:, :], preferred_element_type=F32)
        hu = jnp.dot(hi, wu_s[:half, :], preferred_element_type=F32)
        hu += jnp.dot(lo, wu_s[half:, :], preferred_element_type=F32)
        act = (hg * jax.nn.sigmoid(hg) * hu).astype(BF16)
        y_ref[...] = _pack_bf16_pairs(jnp.dot(act, wd_s[...], preferred_element_type=F32))

    @pl.when(tv_ref[t] == 0)
    def _():
        y_ref[...] = jnp.zeros_like(y_ref)


def _moe_experts(xs, tile_expert, tile_valid, w_gate, w_up, w_down):
    NP, half = xs.shape
    E, D, FF = w_gate.shape
    tr = EXPERT_TILE
    grid_spec = pltpu.PrefetchScalarGridSpec(
        num_scalar_prefetch=2,
        grid=(NP // tr,),
        in_specs=[
            pl.BlockSpec((tr, half), lambda t, te, tv: (t, 0)),
            pl.BlockSpec((1, D, FF), lambda t, te, tv: (te[t], 0, 0)),
            pl.BlockSpec((1, D, FF), lambda t, te, tv: (te[t], 0, 0)),
            pl.BlockSpec((1, FF, D), lambda t, te, tv: (te[t], 0, 0)),
        ],
        out_specs=pl.BlockSpec((tr, half), lambda t, te, tv: (t, 0)),
        scratch_shapes=[pltpu.VMEM((D, FF), BF16), pltpu.VMEM((D, FF), BF16), pltpu.VMEM((FF, D), BF16)],
    )
    return pl.pallas_call(
        _expert_body,
        grid_spec=grid_spec,
        out_shape=jax.ShapeDtypeStruct((NP, half), jnp.int32),
        compiler_params=_params(("arbitrary",)),
        name="moe_experts",
    )(tile_expert, tile_valid, xs, w_gate, w_up, w_down)


def _moe_combine_into(o_ref, h, meta, y0, y1):
    half = y0.shape[1]
    w0 = meta[:, 4:5]
    w1 = meta[:, 5:6]
    hi0, lo0 = _unpack_bf16_pairs(y0)
    hi1, lo1 = _unpack_bf16_pairs(y1)
    o_ref[:, :half] = h[:, :half] + w0 * hi0 + w1 * hi1
    o_ref[:, half:] = h[:, half:] + w0 * lo0 + w1 * lo1


def _moe(h2, gain, w_router, w_gate, w_up, w_down):
    T, D = h2.shape
    E = w_gate.shape[0]
    tr = EXPERT_TILE
    n_sorted = _round_up(2 * T + E * (tr - 1), math.lcm(SC_CHUNK, tr))
    t_pad = _round_up(T, SC_CHUNK // 2)

    xpk, meta, counts = _moe_route(h2, gain, w_router)

    cnt = counts[0, :E].astype(jnp.int32)
    padded = (cnt + tr - 1) // tr * tr
    ends = jnp.cumsum(padded)
    starts = ends - padded
    eid = meta[:, 0:2].astype(jnp.int32)
    pos = starts[eid] + meta[:, 2:4].astype(jnp.int32)
    tok = jnp.broadcast_to(jnp.arange(T, dtype=jnp.int32)[:, None], (T, 2))
    src_tok = jnp.zeros((n_sorted,), jnp.int32).at[pos.reshape(-1)].set(
        tok.reshape(-1), unique_indices=True, mode="promise_in_bounds")
    tile_start = jnp.arange(n_sorted // tr, dtype=jnp.int32) * tr
    tile_expert = jnp.minimum(jnp.sum(tile_start[:, None] >= ends[None, :], axis=1), E - 1).astype(jnp.int32)
    tile_valid = jnp.clip(cnt[tile_expert] - (tile_start - starts[tile_expert]), 0, tr).astype(jnp.int32)

    xs = _sc_gather(xpk, src_tok)
    ys = _moe_experts(xs, tile_expert, tile_valid, w_gate, w_up, w_down)
    back = jnp.pad(pos.T, ((0, 0), (0, t_pad - T)))
    yg = _sc_gather(ys, back.reshape(-1)).reshape(2, t_pad, D // 2)
    return meta, yg


def _final_body(h_ref, meta_ref, y0_ref, y1_ref, g_ref, o_ref, hn_ref):
    _moe_combine_into(hn_ref, h_ref[0], meta_ref[0], y0_ref[0], y1_ref[0])
    o_ref[0] = _rms(hn_ref[N_META:, :], g_ref[...])


def _final_norm(h3, meta3, yg, gain, b0, nb):
    _, L, D = h3.shape
    return pl.pallas_call(
        _final_body,
        grid=(nb,),
        in_specs=[pl.BlockSpec((1, L, D), lambda b: (b0 + b, 0, 0)),
                  pl.BlockSpec((1, L, meta3.shape[2]), lambda b: (b0 + b, 0, 0)),
                  pl.BlockSpec((1, L, D // 2), lambda b: (0, b0 + b, 0)),
                  pl.BlockSpec((1, L, D // 2), lambda b: (1, b0 + b, 0)),
                  pl.BlockSpec((1, D), lambda b: (0, 0))],
        out_specs=pl.BlockSpec((1, L - N_META, D), lambda b: (b, 0, 0)),
        out_shape=jax.ShapeDtypeStruct((nb, L - N_META, D), F32),
        scratch_shapes=[pltpu.VMEM((L, D), F32)],
        compiler_params=_params(("parallel",)),
        name="combine_final_norm",
    )(h3, meta3, yg, yg, gain.reshape(1, D))


def kernel(x_prompt, x_sample, meta_tokens, norm_mix, w_in, ret_decay_fwd, ret_decay_bwd, hy_short_w, hy_short_b, hy_filt_w1, hy_filt_b1, hy_filt_w2, hy_filt_b2, hy_filt_w3, hy_sin_freq, hy_skip, w_ret_o, w_hy_o, w_out, norm_ffn, router_group, router_expert, moe_w_gate, moe_w_up, moe_w_down, norm_final):
    assert x_prompt.shape[1:] == x_sample.shape[1:]
    nbp, nbs = x_prompt.shape[0], x_sample.shape[0]
    B = nbp + nbs
    D = x_prompt.shape[2]
    L = N_META + x_prompt.shape[1]
    T = B * L
    depth = w_in.shape[0]
    q_w = RET_HEADS * RET_DK
    v_w = RET_HEADS * RET_DV
    hy_w = hy_skip.shape[2]
    hy_col0 = 2 * q_w + 2 * v_w
    gate_col0 = hy_col0 + 3 * hy_w
    assert D == q_w and w_in.shape[2] == gate_col0 + 2 * D

    x = jnp.concatenate([x_prompt, x_sample], axis=0)
    meta = jnp.broadcast_to(meta_tokens[None].astype(x.dtype), (B, N_META, D))
    h = jnp.concatenate([meta, x], axis=1).reshape(T, D)

    cos_t, sin_t = _rotary_tables(L)
    cmat, smat = _dft_tables(L)
    router = jnp.concatenate([router_group, router_expert], axis=2).astype(F32)
    router = jnp.pad(router, ((0, 0), (0, 0), (0, LANES - router.shape[2])))

    moe = None
    for i in range(depth):
        proj, h = _inproj(h, norm_mix[i], w_in[i].astype(BF16), moe)
        proj3 = proj.reshape(B, L, -1)

        lg = jnp.stack([jax.nn.log_sigmoid(ret_decay_fwd[i].astype(F32)),
                        jax.nn.log_sigmoid(ret_decay_bwd[i].astype(F32))])
        ret = _retention(proj3, lg, cos_t, sin_t)

        hs, hd = _hyena_filters(L, hy_filt_w1[i], hy_filt_b1[i], hy_filt_w2[i], hy_filt_b2[i],
                                hy_filt_w3[i], hy_sin_freq[i])
        k1, k2 = _filter_spectrum(cmat, smat, hs, hd)
        vx = _short_conv(proj3, hy_col0, hy_short_w[i].astype(F32), hy_short_b[i].astype(F32))
        z = _long_conv(cmat, smat, vx, 0, vx, hy_w, hy_skip[i, 0].astype(F32), k1, k2, 0, hy_w)
        z = _long_conv(cmat, smat, z, 0, vx, 2 * hy_w, hy_skip[i, 1].astype(F32), k1, k2, hy_w, hy_w)

        h = _mix_out(ret.reshape(T, v_w), z.reshape(T, hy_w), proj, gate_col0, h,
                     w_ret_o[i].astype(BF16), w_hy_o[i].astype(BF16), w_out[i].astype(BF16))
        moe = _moe(h, norm_ffn[i], router[i], moe_w_gate[i], moe_w_up[i], moe_w_down[i])

    h3 = h.reshape(B, L, D)
    meta3 = moe[0].reshape(B, L, -1)
    return (_final_norm(h3, meta3, moe[1], norm_final, 0, nbp),
            _final_norm(h3, meta3, moe[1], norm_final, nbp, nbs))
```

```python
import functools
import math

import jax
import jax.numpy as jnp
from jax import lax
from jax.experimental import pallas as pl
from jax.experimental.pallas import tpu as pltpu
from jax.experimental.pallas import tpu_sc as plsc

N_META = 16
RET_HEADS = 8
RET_DK = 128
RET_DV = 256
ROPE_THETA = 10000.0
HY_ORDER = 2
HY_SHORT = 3
HY_EMB = 33
HY_BANDS = (HY_EMB - 1) // 2
HY_DECAY_TARGET = 1e-2
HY_MIN_DECAY = math.log(HY_DECAY_TARGET) / 1.5
HY_MAX_DECAY = math.log(HY_DECAY_TARGET) / 0.3
N_GROUPS = 4
EXP_PER_GROUP = 8
N_EXPERTS = N_GROUPS * EXP_PER_GROUP
RMS_EPS = 1e-6

LANES = 128
BF16_SUBLANES = 16
MXU_DIM = 256
RET_CHUNK = 256
VMEM_LIMIT = 56 * 1024 * 1024

F32 = jnp.float32
BF16 = jnp.bfloat16


def _round_up(n, m):
    return (n + m - 1) // m * m


def _pick_tile(n, target, mult):
    best = None
    for t in range(mult, min(n, target) + 1, mult):
        if n % t == 0:
            best = t
    assert best is not None, (n, target, mult)
    return best


def _params(sem):
    return pltpu.CompilerParams(dimension_semantics=sem, vmem_limit_bytes=VMEM_LIMIT)


def _rms(x, gain):
    ms = jnp.mean(x * x, axis=-1, keepdims=True)
    return x * lax.rsqrt(ms + RMS_EPS) * gain


def _inproj_body(h_ref, g_ref, w_ref, o_ref, xn_ref):
    @pl.when(pl.program_id(1) == 0)
    def _():
        xn_ref[...] = _rms(h_ref[...], g_ref[...]).astype(BF16)

    o_ref[...] = jnp.dot(xn_ref[...], w_ref[...], preferred_element_type=F32).astype(o_ref.dtype)


def _inproj_moe_body(h_ref, meta_ref, y0_ref, y1_ref, g_ref, w_ref, o_ref, hn_ref, xn_ref):
    @pl.when(pl.program_id(1) == 0)
    def _():
        _moe_combine_into(hn_ref, h_ref[...], meta_ref[...], y0_ref[0], y1_ref[0])
        xn_ref[...] = _rms(hn_ref[...], g_ref[...]).astype(BF16)

    o_ref[...] = jnp.dot(xn_ref[...], w_ref[...], preferred_element_type=F32).astype(o_ref.dtype)


def _inproj(h2, gain, w, moe=None):
    T, D = h2.shape
    nc = w.shape[1]
    tm = _pick_tile(T, 1152, BF16_SUBLANES)
    tn = _pick_tile(nc, 1024, LANES)
    common = dict(
        grid=(T // tm, nc // tn),
        scratch_shapes=[pltpu.VMEM((tm, D), BF16)],
        compiler_params=_params(("parallel", "arbitrary")),
    )
    h_spec = pl.BlockSpec((tm, D), lambda i, j: (i, 0))
    g_spec = pl.BlockSpec((1, D), lambda i, j: (0, 0))
    w_spec = pl.BlockSpec((D, tn), lambda i, j: (0, j))
    o_spec = pl.BlockSpec((tm, tn), lambda i, j: (i, j))
    o_shape = jax.ShapeDtypeStruct((T, nc), BF16)
    if moe is None:
        proj = pl.pallas_call(_inproj_body, in_specs=[h_spec, g_spec, w_spec], out_specs=o_spec,
                              out_shape=o_shape, name="norm_inproj", **common)(h2, gain.reshape(1, D), w)
        return proj, h2
    meta, yg = moe
    return pl.pallas_call(
        _inproj_moe_body,
        in_specs=[h_spec, pl.BlockSpec((tm, meta.shape[1]), lambda i, j: (i, 0)),
                  pl.BlockSpec((1, tm, D // 2), lambda i, j: (0, i, 0)),
                  pl.BlockSpec((1, tm, D // 2), lambda i, j: (1, i, 0)),
                  g_spec, w_spec],
        out_specs=[o_spec, h_spec],
        out_shape=[o_shape, jax.ShapeDtypeStruct((T, D), F32)],
        name="combine_norm_inproj", **common,
    )(h2, meta, yg, yg, gain.reshape(1, D), w)


def _dot_t(a, b):
    return lax.dot_general(a, b, (((0,), (0,)), ((), ())), preferred_element_type=F32)


def _dot_nt(a, b):
    return lax.dot_general(a, b, (((1,), (1,)), ((), ())), preferred_element_type=F32)


def _ret_body(lg_ref, q_ref, k_ref, v_ref, g_ref, cos_ref, sin_ref, o_ref, rb_ref, *, seq_len):
    C = RET_CHUNK
    L = seq_len
    n_chunks = pl.cdiv(L, C)
    head = pl.program_id(1)
    lgf = lg_ref[0, head]
    lgb = lg_ref[1, head]

    def chunk(ref, n):
        lo, hi = n * C, min((n + 1) * C, L)
        x = ref[0, lo:hi, :]
        if hi - lo < C:
            x = jnp.concatenate([x, jnp.zeros((C - (hi - lo), x.shape[1]), x.dtype)], axis=0)
        return x

    def rotary(ref, n):
        x = chunk(ref, n).astype(F32)
        sl = slice(n * C, (n + 1) * C)
        return x * cos_ref[sl, :] + pltpu.roll(x, RET_DK // 2, 1) * sin_ref[sl, :]

    row = lax.broadcasted_iota(jnp.int32, (C, LANES), 0).astype(F32)
    ri = lax.broadcasted_iota(jnp.int32, (C, C), 0).astype(F32)
    ci = lax.broadcasted_iota(jnp.int32, (C, C), 1).astype(F32)
    decay = jnp.exp(jnp.where(ci <= ri, (ri - ci) * lgf, (ci - ri) * lgb))
    qf_dec = jnp.exp((row + 1.0) * lgf)
    qb_dec = jnp.exp((C - row) * lgb)
    kf_dec = jnp.exp((C - 1.0 - row) * lgf)
    kb_dec = jnp.exp(row * lgb)
    cf = jnp.exp(C * lgf)
    cb = jnp.exp(C * lgb)

    state = jnp.zeros((RET_DK, RET_DV), F32)
    for n in reversed(range(n_chunks)):
        rb_ref[n] = state.astype(BF16)
        if n > 0:
            kb = (rotary(k_ref, n) * kb_dec).astype(BF16)
            state = cb * state + _dot_t(kb, chunk(v_ref, n))

    state = jnp.zeros((RET_DK, RET_DV), F32)
    for n in range(n_chunks):
        q = rotary(q_ref, n) * (RET_DK ** -0.5)
        k = rotary(k_ref, n)
        v = chunk(v_ref, n)
        scores = _dot_nt(q.astype(BF16), k.astype(BF16)) * decay
        o = jnp.dot(scores.astype(BF16), v, preferred_element_type=F32)
        o += jnp.dot((q * qf_dec).astype(BF16), state.astype(BF16), preferred_element_type=F32)
        o += jnp.dot((q * qb_dec).astype(BF16), rb_ref[n], preferred_element_type=F32)
        if n + 1 < n_chunks:
            state = cf * state + _dot_t((k * kf_dec).astype(BF16), v)
        o = o * lax.rsqrt(jnp.mean(o * o, axis=-1, keepdims=True) + RMS_EPS)
        lo, hi = n * C, min((n + 1) * C, L)
        g = g_ref[0, lo:hi, :].astype(F32)
        o_ref[0, lo:hi, :] = (g * jax.nn.sigmoid(g) * o[: hi - lo]).astype(o_ref.dtype)


def _retention(proj3, lg, cos_t, sin_t):
    B, L, _ = proj3.shape
    n_chunks = pl.cdiv(L, RET_CHUNK)
    lp = n_chunks * RET_CHUNK
    k_blk = RET_HEADS
    v_blk = 2 * RET_HEADS * RET_DK // RET_DV
    g_blk = v_blk + RET_HEADS
    return pl.pallas_call(
        functools.partial(_ret_body, seq_len=L),
        grid=(B, RET_HEADS),
        in_specs=[
            pl.BlockSpec(memory_space=pltpu.SMEM),
            pl.BlockSpec((1, L, RET_DK), lambda b, h: (b, 0, h)),
            pl.BlockSpec((1, L, RET_DK), lambda b, h: (b, 0, k_blk + h)),
            pl.BlockSpec((1, L, RET_DV), lambda b, h: (b, 0, v_blk + h)),
            pl.BlockSpec((1, L, RET_DV), lambda b, h: (b, 0, g_blk + h)),
            pl.BlockSpec((lp, RET_DK), lambda b, h: (0, 0)),
            pl.BlockSpec((lp, RET_DK), lambda b, h: (0, 0)),
        ],
        out_specs=pl.BlockSpec((1, L, RET_DV), lambda b, h: (b, 0, h)),
        out_shape=jax.ShapeDtypeStruct((B, L, RET_HEADS * RET_DV), BF16),
        scratch_shapes=[pltpu.VMEM((n_chunks, RET_DK, RET_DV), BF16)],
        compiler_params=_params(("parallel", "arbitrary")),
        name="retention",
    )(lg, proj3, proj3, proj3, proj3, cos_t, sin_t)


def _rotary_tables(L):
    half = RET_DK // 2
    lp = _round_up(L, RET_CHUNK)
    inv = ROPE_THETA ** (-jnp.arange(half, dtype=F32) / half)
    ang = jnp.arange(lp, dtype=F32)[:, None] * inv[None, :]
    cos, sin = jnp.cos(ang), jnp.sin(ang)
    return jnp.concatenate([cos, cos], axis=1), jnp.concatenate([-sin, sin], axis=1)


def _short_conv_body(u_ref, w_ref, b_ref, o_ref):
    u = u_ref[0].astype(F32)
    L = u.shape[0]
    t = lax.broadcasted_iota(jnp.int32, u.shape, 0)
    prev = jnp.where(t == 0, 0.0, pltpu.roll(u, 1, 0))
    nxt = jnp.where(t == L - 1, 0.0, pltpu.roll(u, L - 1, 0))
    w = w_ref[...]
    o_ref[0] = (b_ref[...] + w[0:1] * prev + w[1:2] * u + w[2:3] * nxt).astype(o_ref.dtype)


def _short_conv(proj3, col0, short_w, short_b):
    B, L, _ = proj3.shape
    width = short_w.shape[1]
    cw = 512
    blk0 = col0 // cw
    return pl.pallas_call(
        _short_conv_body,
        grid=(B, width // cw),
        in_specs=[
            pl.BlockSpec((1, L, cw), lambda b, j: (b, 0, blk0 + j)),
            pl.BlockSpec((HY_SHORT, cw), lambda b, j: (0, j)),
            pl.BlockSpec((1, cw), lambda b, j: (0, j)),
        ],
        out_specs=pl.BlockSpec((1, L, cw), lambda b, j: (b, 0, j)),
        out_shape=jax.ShapeDtypeStruct((B, L, width), BF16),
        compiler_params=_params(("parallel", "parallel")),
        name="hyena_short_conv",
    )(proj3, short_w, short_b.reshape(1, width))


def _row_tiles(n, tile):
    return [(s, min(tile, n - s)) for s in range(0, n, tile)]


def _dft_tables(L):
    fp = _round_up(L + 1, MXU_DIM)
    n = 2 * L
    a = jnp.arange(fp, dtype=jnp.int32)
    m = (a[:, None] * a[None, :]) % n
    ang = m.astype(F32) * (2.0 * math.pi / n)
    return jnp.cos(ang).astype(BF16), jnp.sin(ang).astype(BF16)


def _spec_body(c_ref, s_ref, hs_ref, hd_ref, k1_ref, k2_ref, pad_ref, *, seq_len):
    L = seq_len
    fp = c_ref.shape[0]
    f = lax.broadcasted_iota(jnp.int32, (fp, 1), 0)
    wgt = jnp.where((f == 0) | (f == L), 1.0, 2.0)
    wgt = jnp.where(f <= L, wgt, 0.0) / (2.0 * L)
    pad_ref[L:, :] = jnp.zeros((fp - L, pad_ref.shape[1]), BF16)
    pad_ref[:L, :] = hs_ref[...].astype(BF16)
    k1_ref[...] = wgt * jnp.dot(c_ref[...], pad_ref[...], preferred_element_type=F32)
    pad_ref[:L, :] = hd_ref[...].astype(BF16)
    k2_ref[...] = -wgt * jnp.dot(s_ref[...], pad_ref[...], preferred_element_type=F32)


def _filter_spectrum(cmat, smat, hs, hd):
    L, width = hs.shape
    fp = cmat.shape[0]
    cw = 256
    resident = pl.BlockSpec((fp, fp), lambda j: (0, 0), pipeline_mode=pl.Buffered(1))
    return pl.pallas_call(
        functools.partial(_spec_body, seq_len=L),
        grid=(width // cw,),
        in_specs=[resident, resident,
                  pl.BlockSpec((L, cw), lambda j: (0, j)),
                  pl.BlockSpec((L, cw), lambda j: (0, j))],
        out_specs=[pl.BlockSpec((fp, cw), lambda j: (0, j))] * 2,
        out_shape=[jax.ShapeDtypeStruct((fp, width), F32)] * 2,
        scratch_shapes=[pltpu.VMEM((fp, cw), BF16)],
        compiler_params=_params(("arbitrary",)),
        name="hyena_filter_spectrum",
    )(cmat, smat, hs, hd)


def _conv_body(c_ref, s_ref, u_ref, x_ref, skip_ref, k1_ref, k2_ref, o_ref, pad_ref, e1_ref, e2_ref, *, seq_len):
    L = seq_len
    fp = c_ref.shape[0]
    cw = pad_ref.shape[1]
    nf = _round_up(L + 1, BF16_SUBLANES)
    pad_ref[L:, :] = jnp.zeros((fp - L, cw), BF16)
    pad_ref[:L, :] = u_ref[0]
    if nf < fp:
        e1_ref[nf:, :] = jnp.zeros((fp - nf, cw), BF16)
        e2_ref[nf:, :] = jnp.zeros((fp - nf, cw), BF16)
    for lo, sz in _row_tiles(nf, MXU_DIM):
        p = jnp.dot(c_ref[lo:lo + sz, :], pad_ref[...], preferred_element_type=F32)
        a = jnp.dot(s_ref[lo:lo + sz, :], pad_ref[...], preferred_element_type=F32)
        k1 = k1_ref[lo:lo + sz, :]
        k2 = k2_ref[lo:lo + sz, :]
        e1_ref[lo:lo + sz, :] = (p * k1 + a * k2).astype(BF16)
        e2_ref[lo:lo + sz, :] = (a * k1 - p * k2).astype(BF16)
    skip = skip_ref[...]
    for lo, sz in _row_tiles(L, MXU_DIM):
        y = jnp.dot(c_ref[lo:lo + sz, :], e1_ref[...], preferred_element_type=F32)
        y += jnp.dot(s_ref[lo:lo + sz, :], e2_ref[...], preferred_element_type=F32)
        u = u_ref[0, lo:lo + sz, :].astype(F32)
        x = x_ref[0, lo:lo + sz, :].astype(F32)
        o_ref[0, lo:lo + sz, :] = (x * (y + skip * u)).astype(o_ref.dtype)


def _long_conv(cmat, smat, u_arr, u_col0, x_arr, x_col0, skip, k1, k2, k_col0, width):
    B, L, _ = u_arr.shape
    fp = cmat.shape[0]
    cw = MXU_DIM
    ub, xb, kb = u_col0 // cw, x_col0 // cw, k_col0 // cw
    resident = pl.BlockSpec((fp, fp), lambda c, b: (0, 0), pipeline_mode=pl.Buffered(1))
    return pl.pallas_call(
        functools.partial(_conv_body, seq_len=L),
        grid=(width // cw, B),
        in_specs=[
            resident, resident,
            pl.BlockSpec((1, L, cw), lambda c, b: (b, 0, ub + c)),
            pl.BlockSpec((1, L, cw), lambda c, b: (b, 0, xb + c)),
            pl.BlockSpec((1, cw), lambda c, b: (0, c)),
            pl.BlockSpec((fp, cw), lambda c, b: (0, kb + c)),
            pl.BlockSpec((fp, cw), lambda c, b: (0, kb + c)),
        ],
        out_specs=pl.BlockSpec((1, L, cw), lambda c, b: (b, 0, c)),
        out_shape=jax.ShapeDtypeStruct((B, L, width), BF16),
        scratch_shapes=[pltpu.VMEM((fp, cw), BF16)] * 3,
        compiler_params=_params(("parallel", "arbitrary")),
        name="hyena_long_conv",
    )(cmat, smat, u_arr, x_arr, skip.reshape(1, width), k1, k2)


def _hyena_filters(L, w1, b1, w2, b2, w3, freq):
    hp = lax.Precision.HIGHEST
    width = w3.shape[1] // (2 * HY_ORDER)
    t = jnp.linspace(0.0, 1.0, L, dtype=F32)
    w = (2.0 * math.pi / L) * jnp.arange(L, dtype=F32)
    bands = jnp.linspace(1e-4, HY_BANDS - 1, HY_BANDS, dtype=F32)
    fw = w[:, None] * bands[None, :]
    z = jnp.concatenate([t[:, None], jnp.cos(fw), -jnp.sin(fw)], axis=-1)
    h = jnp.sin(freq * (jnp.dot(z, w1, precision=hp) + b1))
    h = jnp.sin(freq * (jnp.dot(h, w2, precision=hp) + b2))
    h = jnp.dot(h, w3, precision=hp).reshape(L, HY_ORDER, 2, width)
    deltas = jnp.abs(jnp.linspace(HY_MIN_DECAY, HY_MAX_DECAY, width, dtype=F32))
    h = h * jnp.exp(-t[:, None] * deltas[None, :])[:, None, None, :]
    hf = h[:, :, 0]
    hb = h[:, :, 1].at[0].set(0.0)
    scale = lax.rsqrt(jnp.sum(hf * hf, axis=0) + jnp.sum(hb * hb, axis=0) + 1e-6)
    hf, hb = hf * scale, hb * scale
    return (hf + hb).reshape(L, HY_ORDER * width), (hf - hb).reshape(L, HY_ORDER * width)


def _mix_body(ret_ref, hy_ref, gr_ref, gh_ref, h_ref, wr_ref, wh_ref, wo_ref, o_ref):
    ret = jnp.dot(ret_ref[...], wr_ref[...], preferred_element_type=F32)
    hyo = jnp.dot(hy_ref[...], wh_ref[...], preferred_element_type=F32)
    merged = (jax.nn.sigmoid(gr_ref[...].astype(F32)) * ret
              + jax.nn.sigmoid(gh_ref[...].astype(F32)) * hyo)
    o_ref[...] = h_ref[...] + jnp.dot(merged.astype(BF16), wo_ref[...], preferred_element_type=F32)


def _mix_out(ret2, hy2, proj2, gate_col0, h2, w_ret_o, w_hy_o, w_out):
    T, D = h2.shape
    tm = _pick_tile(T, 384, BF16_SUBLANES)
    gb = gate_col0 // D
    row = lambda w: pl.BlockSpec((tm, w), lambda i: (i, 0))
    full = lambda a: pl.BlockSpec(a.shape, lambda i: (0, 0))
    return pl.pallas_call(
        _mix_body,
        grid=(T // tm,),
        in_specs=[
            row(ret2.shape[1]), row(hy2.shape[1]),
            pl.BlockSpec((tm, D), lambda i: (i, gb)),
            pl.BlockSpec((tm, D), lambda i: (i, gb + 1)),
            row(D), full(w_ret_o), full(w_hy_o), full(w_out),
        ],
        out_specs=row(D),
        out_shape=jax.ShapeDtypeStruct((T, D), F32),
        compiler_params=_params(("parallel",)),
        name="merge_outproj",
    )(ret2, hy2, proj2, proj2, h2, w_ret_o, w_hy_o, w_out)


EXPERT_TILE = 512
META_COLS = 8
SC_WINDOW = 64
SC_WORKERS = 32
SC_CHUNK = 2 * SC_WINDOW * SC_WORKERS
SC_SCATTER_WINDOW = 128


def _pack_bf16_pairs(x):
    n = x.shape[1] // 2
    xb = x.astype(BF16).astype(F32)
    hi = lax.bitcast_convert_type(xb[:, :n], jnp.uint32)
    lo = lax.bitcast_convert_type(xb[:, n:], jnp.uint32)
    return lax.bitcast_convert_type(hi | (lo >> 16), jnp.int32)


def _unpack_bf16_pairs(w):
    u = lax.bitcast_convert_type(w, jnp.uint32)
    hi = lax.bitcast_convert_type(u & jnp.uint32(0xFFFF0000), F32)
    lo = lax.bitcast_convert_type(u << 16, F32)
    return hi, lo


def _route(logits):
    lane = lax.broadcasted_iota(jnp.int32, logits.shape, 1)
    neg = -jnp.inf
    big = jnp.int32(1 << 20)
    gl = jnp.where(lane < N_GROUPS, logits, neg)
    gmax = jnp.max(gl, axis=1, keepdims=True)
    p_top = 1.0 / jnp.sum(jnp.exp(gl - gmax), axis=1, keepdims=True)
    g_idx = jnp.min(jnp.where(gl == gmax, lane, big), axis=1, keepdims=True)
    lo = N_GROUPS + g_idx * EXP_PER_GROUP
    el = jnp.where((lane >= lo) & (lane < lo + EXP_PER_GROUP), logits, neg)
    m1 = jnp.max(el, axis=1, keepdims=True)
    i1 = jnp.min(jnp.where(el == m1, lane, big), axis=1, keepdims=True)
    el2 = jnp.where(lane == i1, neg, el)
    m2 = jnp.max(el2, axis=1, keepdims=True)
    i2 = jnp.min(jnp.where(el2 == m2, lane, big), axis=1, keepdims=True)
    r = jnp.exp(m2 - m1)
    return i1 - N_GROUPS, i2 - N_GROUPS, p_top / (1.0 + r), p_top * r / (1.0 + r)


def _route_body(h_ref, g_ref, wr_ref, xpk_ref, meta_ref, meta_t_ref, cnt_ref, carry_ref):
    i = pl.program_id(0)

    @pl.when(i == 0)
    def _():
        carry_ref[...] = jnp.zeros_like(carry_ref)

    xn = _rms(h_ref[...], g_ref[...])
    xpk_ref[...] = _pack_bf16_pairs(xn)
    logits = jnp.dot(xn, wr_ref[...], preferred_element_type=F32, precision=lax.Precision.HIGHEST)
    e0, e1, w0, w1 = _route(logits)
    tm = logits.shape[0]
    lane = lax.broadcasted_iota(jnp.int32, (tm, LANES), 1)
    onehot = jnp.where((lane == e0) | (lane == e1), 1.0, 0.0)
    ri = lax.broadcasted_iota(jnp.int32, (tm, tm), 0)
    ci = lax.broadcasted_iota(jnp.int32, (tm, tm), 1)
    earlier = jnp.where(ci < ri, 1.0, 0.0).astype(BF16)
    prefix = carry_ref[...] + jnp.dot(earlier, onehot.astype(BF16), preferred_element_type=F32)
    r0 = jnp.sum(jnp.where(lane == e0, prefix, 0.0), axis=1, keepdims=True)
    r1 = jnp.sum(jnp.where(lane == e1, prefix, 0.0), axis=1, keepdims=True)
    cols = (e0.astype(F32), e1.astype(F32), r0, r1, w0, w1)
    meta = jnp.zeros((tm, LANES), F32)
    for c, val in enumerate(cols):
        meta = jnp.where(lane == c, val, meta)
    meta_ref[...] = meta[:, :META_COLS]
    meta_t_ref[...] = meta.T[:META_COLS, :]
    carry_ref[...] += jnp.sum(onehot, axis=0, keepdims=True)
    cnt_ref[...] = carry_ref[...]


def _moe_route(h2, gain, w_router):
    T, D = h2.shape
    tm = _pick_tile(T, 1152, LANES)
    return pl.pallas_call(
        _route_body,
        grid=(T // tm,),
        in_specs=[
            pl.BlockSpec((tm, D), lambda i: (i, 0)),
            pl.BlockSpec((1, D), lambda i: (0, 0)),
            pl.BlockSpec((D, LANES), lambda i: (0, 0)),
        ],
        out_specs=[
            pl.BlockSpec((tm, D // 2), lambda i: (i, 0)),
            pl.BlockSpec((tm, META_COLS), lambda i: (i, 0)),
            pl.BlockSpec((META_COLS, tm), lambda i: (0, i)),
            pl.BlockSpec((1, LANES), lambda i: (0, 0)),
        ],
        out_shape=[
            jax.ShapeDtypeStruct((T, D // 2), jnp.int32),
            jax.ShapeDtypeStruct((T, META_COLS), F32),
            jax.ShapeDtypeStruct((META_COLS, T), F32),
            jax.ShapeDtypeStruct((1, LANES), F32),
        ],
        scratch_shapes=[pltpu.VMEM((1, LANES), F32)],
        compiler_params=_params(("arbitrary",)),
        name="moe_route",
    )(h2, gain.reshape(1, D), w_router)


def _sc_gather(table, idx):
    n = idx.shape[0]
    width = table.shape[1]
    win = SC_WINDOW
    assert n % SC_CHUNK == 0
    per_worker = n // SC_WORKERS
    mesh = plsc.VectorSubcoreMesh(core_axis_name="c", subcore_axis_name="s")

    @functools.partial(
        pl.kernel, out_type=jax.ShapeDtypeStruct((n, width), table.dtype), mesh=mesh,
        scratch_types=[pltpu.VMEM((per_worker,), jnp.int32), pltpu.VMEM((2, win, width), table.dtype),
                       pltpu.SemaphoreType.DMA((2,)), pltpu.SemaphoreType.DMA((2,))],
        name="sc_row_gather")
    def gather(table_hbm, idx_hbm, out_hbm, idx_v, rows_v, gsem, osem):
        worker = lax.axis_index("s") * mesh.num_cores + lax.axis_index("c")
        base = worker * per_worker
        pltpu.sync_copy(idx_hbm.at[pl.ds(base, per_worker)], idx_v)

        @pl.loop(0, per_worker, step=2 * win)
        def _(off):
            fetch = [pltpu.async_copy(table_hbm.at[idx_v.at[pl.ds(off + s * win, win)]], rows_v.at[s], gsem.at[s])
                     for s in range(2)]
            store = []
            for s in range(2):
                fetch[s].wait()
                store.append(pltpu.async_copy(rows_v.at[s], out_hbm.at[pl.ds(base + off + s * win, win)],
                                              osem.at[s]))
            for s in range(2):
                store[s].wait()

    return gather(table, idx)


def _sc_dispatch(table, dest, n_out):
    n_rows, width = table.shape
    win = SC_SCATTER_WINDOW
    assert n_rows % win == 0
    n_win = n_rows // win
    per_worker = pl.cdiv(n_win, SC_WORKERS)
    idx = jnp.pad(dest, ((0, 0), (0, per_worker * SC_WORKERS * win - n_rows)))
    idx = idx.reshape(2, per_worker, SC_WORKERS, win).transpose(0, 2, 1, 3)
    mesh = plsc.VectorSubcoreMesh(core_axis_name="c", subcore_axis_name="s")

    @functools.partial(
        pl.kernel, out_type=jax.ShapeDtypeStruct((n_out, width), table.dtype), mesh=mesh,
        scratch_types=[pltpu.VMEM((per_worker, win), jnp.int32), pltpu.VMEM((per_worker, win), jnp.int32),
                       pltpu.VMEM((win, width), table.dtype),
                       pltpu.SemaphoreType.DMA, pltpu.SemaphoreType.DMA],
        name="sc_row_dispatch")
    def dispatch(table_hbm, idx_hbm, out_hbm, idx0_v, idx1_v, rows_v, sem0, sem1):
        worker = lax.axis_index("s") * mesh.num_cores + lax.axis_index("c")
        pltpu.sync_copy(idx_hbm.at[0, worker], idx0_v)
        pltpu.sync_copy(idx_hbm.at[1, worker], idx1_v)

        @pl.loop(0, per_worker)
        def _(j):
            window = j * SC_WORKERS + worker

            @pl.when(window < n_win)
            def _():
                pltpu.sync_copy(table_hbm.at[pl.ds(window * win, win)], rows_v)
                first = pltpu.async_copy(rows_v, out_hbm.at[idx0_v.at[j]], sem0)
                second = pltpu.async_copy(rows_v, out_hbm.at[idx1_v.at[j]], sem1)
                first.wait()
                second.wait()

    return dispatch(table, idx)


def _expert_body(te_ref, tv_ref, x_ref, wg_ref, wu_ref, wd_ref, y_ref, wg_s, wu_s, wd_s):
    t = pl.program_id(0)
    half = x_ref.shape[1]

    @pl.when((t == 0) | (te_ref[t] != te_ref[jnp.maximum(t - 1, 0)]))
    def _():
        wg_s[...] = wg_ref[0].astype(BF16)
        wu_s[...] = wu_ref[0].astype(BF16)
        wd_s[...] = wd_ref[0].astype(BF16)

    @pl.when(tv_ref[t] > 0)
    def _():
        row = lax.broadcasted_iota(jnp.int32, x_ref.shape, 0)
        hi, lo = _unpack_bf16_pairs(jnp.where(row < tv_ref[t], x_ref[...], 0))
        hi, lo = hi.astype(BF16), lo.astype(BF16)
        hg = jnp.dot(hi, wg_s[:half, :], preferred_element_type=F32)
        hg += jnp.dot(lo, wg_s[half:, :], preferred_element_type=F32)
        hu = jnp.dot(hi, wu_s[:half, :], preferred_element_type=F32)
        hu += jnp.dot(lo, wu_s[half:, :], preferred_element_type=F32)
        act = (hg * jax.nn.sigmoid(hg) * hu).astype(BF16)
        y_ref[...] = _pack_bf16_pairs(jnp.dot(act, wd_s[...], preferred_element_type=F32))

    @pl.when(tv_ref[t] == 0)
    def _():
        y_ref[...] = jnp.zeros_like(y_ref)


def _moe_experts(xs, n_sorted, tile_expert, tile_valid, w_gate, w_up, w_down):
    NP, half = n_sorted, xs.shape[1]
    _, D, FF = w_gate.shape
    tr = EXPERT_TILE
    grid_spec = pltpu.PrefetchScalarGridSpec(
        num_scalar_prefetch=2,
        grid=(NP // tr,),
        in_specs=[
            pl.BlockSpec((tr, half), lambda t, te, tv: (t, 0)),
            pl.BlockSpec((1, D, FF), lambda t, te, tv: (te[t], 0, 0)),
            pl.BlockSpec((1, D, FF), lambda t, te, tv: (te[t], 0, 0)),
            pl.BlockSpec((1, FF, D), lambda t, te, tv: (te[t], 0, 0)),
        ],
        out_specs=pl.BlockSpec((tr, half), lambda t, te, tv: (t, 0)),
        scratch_shapes=[pltpu.VMEM((D, FF), BF16), pltpu.VMEM((D, FF), BF16), pltpu.VMEM((FF, D), BF16)],
    )
    return pl.pallas_call(
        _expert_body,
        grid_spec=grid_spec,
        out_shape=jax.ShapeDtypeStruct((NP, half), jnp.int32),
        compiler_params=_params(("arbitrary",)),
        name="moe_experts",
    )(tile_expert, tile_valid, xs, w_gate, w_up, w_down)


def _moe_combine_into(o_ref, h, meta, y0, y1):
    half = y0.shape[1]
    w0 = meta[:, 4:5]
    w1 = meta[:, 5:6]
    hi0, lo0 = _unpack_bf16_pairs(y0)
    hi1, lo1 = _unpack_bf16_pairs(y1)
    o_ref[:, :half] = h[:, :half] + w0 * hi0 + w1 * hi1
    o_ref[:, half:] = h[:, half:] + w0 * lo0 + w1 * lo1


def _moe(h2, gain, w_router, layer, w_gate, w_up, w_down):
    T, D = h2.shape
    E = N_EXPERTS
    tr = EXPERT_TILE
    n_sorted = _round_up(2 * T + E * (tr - 1), tr)
    t_pad = _round_up(T, SC_CHUNK // 2)

    xpk, meta, meta_t, counts = _moe_route(h2, gain, w_router)

    cnt = counts[0, :E].astype(jnp.int32)
    padded = (cnt + tr - 1) // tr * tr
    ends = jnp.cumsum(padded)
    starts = ends - padded
    eid = meta_t[0:2].astype(jnp.int32)
    pos = starts[eid] + meta_t[2:4].astype(jnp.int32)
    tile_start = jnp.arange(n_sorted // tr, dtype=jnp.int32) * tr
    tile_expert = jnp.minimum(jnp.sum(tile_start[:, None] >= ends[None, :], axis=1), E - 1).astype(jnp.int32)
    tile_valid = jnp.clip(cnt[tile_expert] - (tile_start - starts[tile_expert]), 0, tr).astype(jnp.int32)

    xs = _sc_dispatch(xpk, pos, n_sorted)
    ys = _moe_experts(xs, n_sorted, layer * E + tile_expert, tile_valid, w_gate, w_up, w_down)
    spare = jnp.arange(t_pad - T, dtype=jnp.int32)
    back = jnp.concatenate([pos, jnp.broadcast_to(spare[None], (2, t_pad - T))], axis=1)
    yg = _sc_gather(ys, back.reshape(-1)).reshape(2, t_pad, D // 2)
    return meta, yg


def _final_body(h_ref, meta_ref, y0_ref, y1_ref, g_ref, o_ref, hn_ref):
    _moe_combine_into(hn_ref, h_ref[0], meta_ref[0], y0_ref[0], y1_ref[0])
    o_ref[0] = _rms(hn_ref[N_META:, :], g_ref[...])


def _final_norm(h3, meta3, yg, gain, b0, nb):
    _, L, D = h3.shape
    return pl.pallas_call(
        _final_body,
        grid=(nb,),
        in_specs=[pl.BlockSpec((1, L, D), lambda b: (b0 + b, 0, 0)),
                  pl.BlockSpec((1, L, meta3.shape[2]), lambda b: (b0 + b, 0, 0)),
                  pl.BlockSpec((1, L, D // 2), lambda b: (0, b0 + b, 0)),
                  pl.BlockSpec((1, L, D // 2), lambda b: (1, b0 + b, 0)),
                  pl.BlockSpec((1, D), lambda b: (0, 0))],
        out_specs=pl.BlockSpec((1, L - N_META, D), lambda b: (b, 0, 0)),
        out_shape=jax.ShapeDtypeStruct((nb, L - N_META, D), F32),
        scratch_shapes=[pltpu.VMEM((L, D), F32)],
        compiler_params=_params(("parallel",)),
        name="combine_final_norm",
    )(h3, meta3, yg, yg, gain.reshape(1, D))


def kernel(x_prompt, x_sample, meta_tokens, norm_mix, w_in, ret_decay_fwd, ret_decay_bwd, hy_short_w, hy_short_b, hy_filt_w1, hy_filt_b1, hy_filt_w2, hy_filt_b2, hy_filt_w3, hy_sin_freq, hy_skip, w_ret_o, w_hy_o, w_out, norm_ffn, router_group, router_expert, moe_w_gate, moe_w_up, moe_w_down, norm_final):
    assert x_prompt.shape[1:] == x_sample.shape[1:]
    nbp, nbs = x_prompt.shape[0], x_sample.shape[0]
    B = nbp + nbs
    D = x_prompt.shape[2]
    L = N_META + x_prompt.shape[1]
    T = B * L
    depth = w_in.shape[0]
    q_w = RET_HEADS * RET_DK
    v_w = RET_HEADS * RET_DV
    hy_w = hy_skip.shape[2]
    hy_col0 = 2 * q_w + 2 * v_w
    gate_col0 = hy_col0 + 3 * hy_w
    assert D == q_w and w_in.shape[2] == gate_col0 + 2 * D

    x = jnp.concatenate([x_prompt, x_sample], axis=0)
    meta = jnp.broadcast_to(meta_tokens[None].astype(x.dtype), (B, N_META, D))
    h = jnp.concatenate([meta, x], axis=1).reshape(T, D)

    cos_t, sin_t = _rotary_tables(L)
    cmat, smat = _dft_tables(L)
    router = jnp.concatenate([router_group, router_expert], axis=2).astype(F32)
    router = jnp.pad(router, ((0, 0), (0, 0), (0, LANES - router.shape[2])))
    assert moe_w_gate.shape[1] == N_EXPERTS
    w_gate_all = moe_w_gate.reshape((-1,) + moe_w_gate.shape[2:])
    w_up_all = moe_w_up.reshape((-1,) + moe_w_up.shape[2:])
    w_down_all = moe_w_down.reshape((-1,) + moe_w_down.shape[2:])

    moe = None
    for i in range(depth):
        proj, h = _inproj(h, norm_mix[i], w_in[i].astype(BF16), moe)
        proj3 = proj.reshape(B, L, -1)

        lg = jnp.stack([jax.nn.log_sigmoid(ret_decay_fwd[i].astype(F32)),
                        jax.nn.log_sigmoid(ret_decay_bwd[i].astype(F32))])
        ret = _retention(proj3, lg, cos_t, sin_t)

        hs, hd = _hyena_filters(L, hy_filt_w1[i], hy_filt_b1[i], hy_filt_w2[i], hy_filt_b2[i],
                                hy_filt_w3[i], hy_sin_freq[i])
        k1, k2 = _filter_spectrum(cmat, smat, hs, hd)
        vx = _short_conv(proj3, hy_col0, hy_short_w[i].astype(F32), hy_short_b[i].astype(F32))
        z = _long_conv(cmat, smat, vx, 0, vx, hy_w, hy_skip[i, 0].astype(F32), k1, k2, 0, hy_w)
        z = _long_conv(cmat, smat, z, 0, vx, 2 * hy_w, hy_skip[i, 1].astype(F32), k1, k2, hy_w, hy_w)

        h = _mix_out(ret.reshape(T, v_w), z.reshape(T, hy_w), proj, gate_col0, h,
                     w_ret_o[i].astype(BF16), w_hy_o[i].astype(BF16), w_out[i].astype(BF16))
        moe = _moe(h, norm_ffn[i], router[i], i, w_gate_all, w_up_all, w_down_all)

    h3 = h.reshape(B, L, D)
    meta3 = moe[0].reshape(B, L, -1)
    return (_final_norm(h3, meta3, moe[1], norm_final, 0, nbp),
            _final_norm(h3, meta3, moe[1], norm_final, nbp, nbs))
```

```python
import functools
import math

import jax
import jax.numpy as jnp
from jax import lax
from jax.experimental import pallas as pl
from jax.experimental.pallas import tpu as pltpu
from jax.experimental.pallas import tpu_sc as plsc

N_META = 16
RET_HEADS = 8
RET_DK = 128
RET_DV = 256
ROPE_THETA = 10000.0
HY_ORDER = 2
HY_SHORT = 3
HY_EMB = 33
HY_BANDS = (HY_EMB - 1) // 2
HY_DECAY_TARGET = 1e-2
HY_MIN_DECAY = math.log(HY_DECAY_TARGET) / 1.5
HY_MAX_DECAY = math.log(HY_DECAY_TARGET) / 0.3
N_GROUPS = 4
EXP_PER_GROUP = 8
N_EXPERTS = N_GROUPS * EXP_PER_GROUP
RMS_EPS = 1e-6

LANES = 128
BF16_SUBLANES = 16
MXU_DIM = 256
RET_CHUNK = 256
VMEM_LIMIT = 56 * 1024 * 1024

F32 = jnp.float32
BF16 = jnp.bfloat16


def _round_up(n, m):
    return (n + m - 1) // m * m


def _pick_tile(n, target, mult):
    best = None
    for t in range(mult, min(n, target) + 1, mult):
        if n % t == 0:
            best = t
    assert best is not None, (n, target, mult)
    return best


def _params(sem):
    return pltpu.CompilerParams(dimension_semantics=sem, vmem_limit_bytes=VMEM_LIMIT)


def _rms(x, gain):
    ms = jnp.mean(x * x, axis=-1, keepdims=True)
    return x * lax.rsqrt(ms + RMS_EPS) * gain


def _inproj_body(h_ref, g_ref, w_ref, o_ref, xn_ref):
    @pl.when(pl.program_id(1) == 0)
    def _():
        xn_ref[...] = _rms(h_ref[...], g_ref[...]).astype(BF16)

    o_ref[...] = jnp.dot(xn_ref[...], w_ref[...], preferred_element_type=F32).astype(o_ref.dtype)


def _inproj_moe_body(h_ref, meta_ref, y0_ref, y1_ref, g_ref, w_ref, o_ref, hn_ref, xn_ref):
    @pl.when(pl.program_id(1) == 0)
    def _():
        _moe_combine_into(hn_ref, h_ref[...], meta_ref[...], y0_ref[0], y1_ref[0])
        xn_ref[...] = _rms(hn_ref[...], g_ref[...]).astype(BF16)

    o_ref[...] = jnp.dot(xn_ref[...], w_ref[...], preferred_element_type=F32).astype(o_ref.dtype)


def _inproj(h2, gain, w, moe=None):
    T, D = h2.shape
    nc = w.shape[1]
    tm = _pick_tile(T, 1152, BF16_SUBLANES)
    tn = _pick_tile(nc, 1024, LANES)
    common = dict(
        grid=(T // tm, nc // tn),
        scratch_shapes=[pltpu.VMEM((tm, D), BF16)],
        compiler_params=_params(("parallel", "arbitrary")),
    )
    h_spec = pl.BlockSpec((tm, D), lambda i, j: (i, 0))
    g_spec = pl.BlockSpec((1, D), lambda i, j: (0, 0))
    w_spec = pl.BlockSpec((D, tn), lambda i, j: (0, j))
    o_spec = pl.BlockSpec((tm, tn), lambda i, j: (i, j))
    o_shape = jax.ShapeDtypeStruct((T, nc), BF16)
    if moe is None:
        proj = pl.pallas_call(_inproj_body, in_specs=[h_spec, g_spec, w_spec], out_specs=o_spec,
                              out_shape=o_shape, name="norm_inproj", **common)(h2, gain.reshape(1, D), w)
        return proj, h2
    meta, yg = moe
    return pl.pallas_call(
        _inproj_moe_body,
        in_specs=[h_spec, pl.BlockSpec((tm, meta.shape[1]), lambda i, j: (i, 0)),
                  pl.BlockSpec((1, tm, D // 2), lambda i, j: (0, i, 0)),
                  pl.BlockSpec((1, tm, D // 2), lambda i, j: (1, i, 0)),
                  g_spec, w_spec],
        out_specs=[o_spec, h_spec],
        out_shape=[o_shape, jax.ShapeDtypeStruct((T, D), F32)],
        name="combine_norm_inproj", **common,
    )(h2, meta, yg, yg, gain.reshape(1, D), w)


def _dot_t(a, b):
    return lax.dot_general(a, b, (((0,), (0,)), ((), ())), preferred_element_type=F32)


def _dot_nt(a, b):
    return lax.dot_general(a, b, (((1,), (1,)), ((), ())), preferred_element_type=F32)


def _ret_body(lg_ref, q_ref, k_ref, v_ref, g_ref, cos_ref, sin_ref, o_ref, rb_ref, *, seq_len):
    C = RET_CHUNK
    L = seq_len
    n_chunks = pl.cdiv(L, C)
    head = pl.program_id(1)
    lgf = lg_ref[0, head]
    lgb = lg_ref[1, head]

    def chunk(ref, n):
        lo, hi = n * C, min((n + 1) * C, L)
        x = ref[0, lo:hi, :]
        if hi - lo < C:
            x = jnp.concatenate([x, jnp.zeros((C - (hi - lo), x.shape[1]), x.dtype)], axis=0)
        return x

    def rotary(ref, n):
        x = chunk(ref, n).astype(F32)
        sl = slice(n * C, (n + 1) * C)
        return x * cos_ref[sl, :] + pltpu.roll(x, RET_DK // 2, 1) * sin_ref[sl, :]

    row = lax.broadcasted_iota(jnp.int32, (C, LANES), 0).astype(F32)
    ri = lax.broadcasted_iota(jnp.int32, (C, C), 0).astype(F32)
    ci = lax.broadcasted_iota(jnp.int32, (C, C), 1).astype(F32)
    decay = jnp.exp(jnp.where(ci <= ri, (ri - ci) * lgf, (ci - ri) * lgb))
    qf_dec = jnp.exp((row + 1.0) * lgf)
    qb_dec = jnp.exp((C - row) * lgb)
    kf_dec = jnp.exp((C - 1.0 - row) * lgf)
    kb_dec = jnp.exp(row * lgb)
    cf = jnp.exp(C * lgf)
    cb = jnp.exp(C * lgb)

    state = jnp.zeros((RET_DK, RET_DV), F32)
    for n in reversed(range(n_chunks)):
        rb_ref[n] = state.astype(BF16)
        if n > 0:
            kb = (rotary(k_ref, n) * kb_dec).astype(BF16)
            state = cb * state + _dot_t(kb, chunk(v_ref, n))

    state = jnp.zeros((RET_DK, RET_DV), F32)
    for n in range(n_chunks):
        q = rotary(q_ref, n) * (RET_DK ** -0.5)
        k = rotary(k_ref, n)
        v = chunk(v_ref, n)
        scores = _dot_nt(q.astype(BF16), k.astype(BF16)) * decay
        o = jnp.dot(scores.astype(BF16), v, preferred_element_type=F32)
        o += jnp.dot((q * qf_dec).astype(BF16), state.astype(BF16), preferred_element_type=F32)
        o += jnp.dot((q * qb_dec).astype(BF16), rb_ref[n], preferred_element_type=F32)
        if n + 1 < n_chunks:
            state = cf * state + _dot_t((k * kf_dec).astype(BF16), v)
        o = o * lax.rsqrt(jnp.mean(o * o, axis=-1, keepdims=True) + RMS_EPS)
        lo, hi = n * C, min((n + 1) * C, L)
        g = g_ref[0, lo:hi, :].astype(F32)
        o_ref[0, lo:hi, :] = (g * jax.nn.sigmoid(g) * o[: hi - lo]).astype(o_ref.dtype)


def _retention(proj3, lg, cos_t, sin_t):
    B, L, _ = proj3.shape
    n_chunks = pl.cdiv(L, RET_CHUNK)
    lp = n_chunks * RET_CHUNK
    k_blk = RET_HEADS
    v_blk = 2 * RET_HEADS * RET_DK // RET_DV
    g_blk = v_blk + RET_HEADS
    return pl.pallas_call(
        functools.partial(_ret_body, seq_len=L),
        grid=(B, RET_HEADS),
        in_specs=[
            pl.BlockSpec(memory_space=pltpu.SMEM),
            pl.BlockSpec((1, L, RET_DK), lambda b, h: (b, 0, h)),
            pl.BlockSpec((1, L, RET_DK), lambda b, h: (b, 0, k_blk + h)),
            pl.BlockSpec((1, L, RET_DV), lambda b, h: (b, 0, v_blk + h)),
            pl.BlockSpec((1, L, RET_DV), lambda b, h: (b, 0, g_blk + h)),
            pl.BlockSpec((lp, RET_DK), lambda b, h: (0, 0)),
            pl.BlockSpec((lp, RET_DK), lambda b, h: (0, 0)),
        ],
        out_specs=pl.BlockSpec((1, L, RET_DV), lambda b, h: (b, 0, h)),
        out_shape=jax.ShapeDtypeStruct((B, L, RET_HEADS * RET_DV), BF16),
        scratch_shapes=[pltpu.VMEM((n_chunks, RET_DK, RET_DV), BF16)],
        compiler_params=_params(("parallel", "arbitrary")),
        name="retention",
    )(lg, proj3, proj3, proj3, proj3, cos_t, sin_t)


def _rotary_tables(L):
    half = RET_DK // 2
    lp = _round_up(L, RET_CHUNK)
    inv = ROPE_THETA ** (-jnp.arange(half, dtype=F32) / half)
    ang = jnp.arange(lp, dtype=F32)[:, None] * inv[None, :]
    cos, sin = jnp.cos(ang), jnp.sin(ang)
    return jnp.concatenate([cos, cos], axis=1), jnp.concatenate([-sin, sin], axis=1)


def _row_tiles(n, tile):
    return [(s, min(tile, n - s)) for s in range(0, n, tile)]


def _dft_tables(L):
    fp = _round_up(L + 1, MXU_DIM)
    n = 2 * L
    a = jnp.arange(fp, dtype=jnp.int32)
    m = (a[:, None] * a[None, :]) % n
    ang = m.astype(F32) * (2.0 * math.pi / n)
    return jnp.cos(ang).astype(BF16), jnp.sin(ang).astype(BF16)


def _spec_body(c_ref, s_ref, hs_ref, hd_ref, k1_ref, k2_ref, pad_ref, *, seq_len):
    L = seq_len
    fp = c_ref.shape[0]
    f = lax.broadcasted_iota(jnp.int32, (fp, 1), 0)
    wgt = jnp.where((f == 0) | (f == L), 1.0, 2.0)
    wgt = jnp.where(f <= L, wgt, 0.0) / (2.0 * L)
    pad_ref[L:, :] = jnp.zeros((fp - L, pad_ref.shape[1]), BF16)
    pad_ref[:L, :] = hs_ref[...].astype(BF16)
    k1_ref[...] = wgt * jnp.dot(c_ref[...], pad_ref[...], preferred_element_type=F32)
    pad_ref[:L, :] = hd_ref[...].astype(BF16)
    k2_ref[...] = -wgt * jnp.dot(s_ref[...], pad_ref[...], preferred_element_type=F32)


def _filter_spectrum(cmat, smat, hs, hd):
    L, width = hs.shape
    fp = cmat.shape[0]
    cw = 256
    resident = pl.BlockSpec((fp, fp), lambda j: (0, 0), pipeline_mode=pl.Buffered(1))
    return pl.pallas_call(
        functools.partial(_spec_body, seq_len=L),
        grid=(width // cw,),
        in_specs=[resident, resident,
                  pl.BlockSpec((L, cw), lambda j: (0, j)),
                  pl.BlockSpec((L, cw), lambda j: (0, j))],
        out_specs=[pl.BlockSpec((fp, cw), lambda j: (0, j))] * 2,
        out_shape=[jax.ShapeDtypeStruct((fp, width), F32)] * 2,
        scratch_shapes=[pltpu.VMEM((fp, cw), BF16)],
        compiler_params=_params(("arbitrary",)),
        name="hyena_filter_spectrum",
    )(cmat, smat, hs, hd)


def _short_conv(ref, w_ref, b_ref):
    u = ref[0].astype(F32)
    L = u.shape[0]
    t = lax.broadcasted_iota(jnp.int32, u.shape, 0)
    prev = jnp.where(t == 0, 0.0, pltpu.roll(u, 1, 0))
    nxt = jnp.where(t == L - 1, 0.0, pltpu.roll(u, L - 1, 0))
    w = w_ref[...]
    return b_ref[...] + w[0:1] * prev + w[1:2] * u + w[2:3] * nxt


def _conv_body(*refs, seq_len, short_u):
    if short_u:
        (c_ref, s_ref, u_ref, x_ref, wu_ref, bu_ref, wx_ref, bx_ref, skip_ref, k1_ref, k2_ref,
         o_ref, pad_ref, e1_ref, e2_ref, xs_ref) = refs
    else:
        (c_ref, s_ref, u_ref, x_ref, wx_ref, bx_ref, skip_ref, k1_ref, k2_ref,
         o_ref, pad_ref, e1_ref, e2_ref, xs_ref) = refs
    L = seq_len
    fp = c_ref.shape[0]
    cw = pad_ref.shape[1]
    nf = _round_up(L + 1, BF16_SUBLANES)
    pad_ref[L:, :] = jnp.zeros((fp - L, cw), BF16)
    pad_ref[:L, :] = _short_conv(u_ref, wu_ref, bu_ref).astype(BF16) if short_u else u_ref[0]
    xs_ref[...] = _short_conv(x_ref, wx_ref, bx_ref).astype(BF16)
    if nf < fp:
        e1_ref[nf:, :] = jnp.zeros((fp - nf, cw), BF16)
        e2_ref[nf:, :] = jnp.zeros((fp - nf, cw), BF16)
    for lo, sz in _row_tiles(nf, MXU_DIM):
        p = jnp.dot(c_ref[lo:lo + sz, :], pad_ref[...], preferred_element_type=F32)
        a = jnp.dot(s_ref[lo:lo + sz, :], pad_ref[...], preferred_element_type=F32)
        k1 = k1_ref[lo:lo + sz, :]
        k2 = k2_ref[lo:lo + sz, :]
        e1_ref[lo:lo + sz, :] = (p * k1 + a * k2).astype(BF16)
        e2_ref[lo:lo + sz, :] = (a * k1 - p * k2).astype(BF16)
    skip = skip_ref[...]
    for lo, sz in _row_tiles(L, MXU_DIM):
        y = jnp.dot(c_ref[lo:lo + sz, :], e1_ref[...], preferred_element_type=F32)
        y += jnp.dot(s_ref[lo:lo + sz, :], e2_ref[...], preferred_element_type=F32)
        u = pad_ref[lo:lo + sz, :].astype(F32)
        x = xs_ref[lo:lo + sz, :].astype(F32)
        o_ref[0, lo:lo + sz, :] = (x * (y + skip * u)).astype(o_ref.dtype)


def _long_conv(cmat, smat, u_arr, u_col0, x_arr, x_col0, short_w, short_b, short_u_col0, short_x_col0,
               skip, k1, k2, k_col0, width):
    B, L, _ = u_arr.shape
    fp = cmat.shape[0]
    cw = MXU_DIM
    ub, xb, kb = u_col0 // cw, x_col0 // cw, k_col0 // cw
    short_u = short_u_col0 is not None
    resident = pl.BlockSpec((fp, fp), lambda c, b: (0, 0), pipeline_mode=pl.Buffered(1))

    def taps(col0):
        blk = col0 // cw
        return [pl.BlockSpec((HY_SHORT, cw), lambda c, b: (0, blk + c)),
                pl.BlockSpec((1, cw), lambda c, b: (0, blk + c))]

    in_specs = [resident, resident,
                pl.BlockSpec((1, L, cw), lambda c, b: (b, 0, ub + c)),
                pl.BlockSpec((1, L, cw), lambda c, b: (b, 0, xb + c))]
    args = [cmat, smat, u_arr, x_arr]
    for col0 in ([short_u_col0] if short_u else []) + [short_x_col0]:
        in_specs += taps(col0)
        args += [short_w, short_b]
    in_specs += [pl.BlockSpec((1, cw), lambda c, b: (0, c)),
                 pl.BlockSpec((fp, cw), lambda c, b: (0, kb + c)),
                 pl.BlockSpec((fp, cw), lambda c, b: (0, kb + c))]
    args += [skip.reshape(1, width), k1, k2]
    return pl.pallas_call(
        functools.partial(_conv_body, seq_len=L, short_u=short_u),
        grid=(width // cw, B),
        in_specs=in_specs,
        out_specs=pl.BlockSpec((1, L, cw), lambda c, b: (b, 0, c)),
        out_shape=jax.ShapeDtypeStruct((B, L, width), BF16),
        scratch_shapes=[pltpu.VMEM((fp, cw), BF16)] * 3 + [pltpu.VMEM((L, cw), BF16)],
        compiler_params=_params(("parallel", "arbitrary")),
        name="hyena_long_conv",
    )(*args)


def _hyena_filters(L, w1, b1, w2, b2, w3, freq):
    hp = lax.Precision.HIGHEST
    width = w3.shape[1] // (2 * HY_ORDER)
    t = jnp.linspace(0.0, 1.0, L, dtype=F32)
    w = (2.0 * math.pi / L) * jnp.arange(L, dtype=F32)
    bands = jnp.linspace(1e-4, HY_BANDS - 1, HY_BANDS, dtype=F32)
    fw = w[:, None] * bands[None, :]
    z = jnp.concatenate([t[:, None], jnp.cos(fw), -jnp.sin(fw)], axis=-1)
    h = jnp.sin(freq * (jnp.dot(z, w1, precision=hp) + b1))
    h = jnp.sin(freq * (jnp.dot(h, w2, precision=hp) + b2))
    h = jnp.dot(h, w3, precision=hp).reshape(L, HY_ORDER, 2, width)
    deltas = jnp.abs(jnp.linspace(HY_MIN_DECAY, HY_MAX_DECAY, width, dtype=F32))
    h = h * jnp.exp(-t[:, None] * deltas[None, :])[:, None, None, :]
    hf = h[:, :, 0]
    hb = h[:, :, 1].at[0].set(0.0)
    scale = lax.rsqrt(jnp.sum(hf * hf, axis=0) + jnp.sum(hb * hb, axis=0) + 1e-6)
    hf, hb = hf * scale, hb * scale
    return (hf + hb).reshape(L, HY_ORDER * width), (hf - hb).reshape(L, HY_ORDER * width)


def _mix_body(ret_ref, hy_ref, gr_ref, gh_ref, h_ref, wr_ref, wh_ref, wo_ref, o_ref):
    ret = jnp.dot(ret_ref[...], wr_ref[...], preferred_element_type=F32)
    hyo = jnp.dot(hy_ref[...], wh_ref[...], preferred_element_type=F32)
    merged = (jax.nn.sigmoid(gr_ref[...].astype(F32)) * ret
              + jax.nn.sigmoid(gh_ref[...].astype(F32)) * hyo)
    o_ref[...] = h_ref[...] + jnp.dot(merged.astype(BF16), wo_ref[...], preferred_element_type=F32)


def _mix_out(ret2, hy2, proj2, gate_col0, h2, w_ret_o, w_hy_o, w_out):
    T, D = h2.shape
    tm = _pick_tile(T, 384, BF16_SUBLANES)
    gb = gate_col0 // D
    row = lambda w: pl.BlockSpec((tm, w), lambda i: (i, 0))
    full = lambda a: pl.BlockSpec(a.shape, lambda i: (0, 0))
    return pl.pallas_call(
        _mix_body,
        grid=(T // tm,),
        in_specs=[
            row(ret2.shape[1]), row(hy2.shape[1]),
            pl.BlockSpec((tm, D), lambda i: (i, gb)),
            pl.BlockSpec((tm, D), lambda i: (i, gb + 1)),
            row(D), full(w_ret_o), full(w_hy_o), full(w_out),
        ],
        out_specs=row(D),
        out_shape=jax.ShapeDtypeStruct((T, D), F32),
        compiler_params=_params(("parallel",)),
        name="merge_outproj",
    )(ret2, hy2, proj2, proj2, h2, w_ret_o, w_hy_o, w_out)


EXPERT_TILE = 512
META_COLS = 8
SC_WINDOW = 64
SC_WORKERS = 32
SC_CHUNK = 2 * SC_WINDOW * SC_WORKERS
SC_SCATTER_WINDOW = 128


def _pack_bf16_pairs(x):
    n = x.shape[1] // 2
    xb = x.astype(BF16).astype(F32)
    hi = lax.bitcast_convert_type(xb[:, :n], jnp.uint32)
    lo = lax.bitcast_convert_type(xb[:, n:], jnp.uint32)
    return lax.bitcast_convert_type(hi | (lo >> 16), jnp.int32)


def _unpack_bf16_pairs(w):
    u = lax.bitcast_convert_type(w, jnp.uint32)
    hi = lax.bitcast_convert_type(u & jnp.uint32(0xFFFF0000), F32)
    lo = lax.bitcast_convert_type(u << 16, F32)
    return hi, lo


def _route(logits):
    lane = lax.broadcasted_iota(jnp.int32, logits.shape, 1)
    neg = -jnp.inf
    big = jnp.int32(1 << 20)
    gl = jnp.where(lane < N_GROUPS, logits, neg)
    gmax = jnp.max(gl, axis=1, keepdims=True)
    p_top = 1.0 / jnp.sum(jnp.exp(gl - gmax), axis=1, keepdims=True)
    g_idx = jnp.min(jnp.where(gl == gmax, lane, big), axis=1, keepdims=True)
    lo = N_GROUPS + g_idx * EXP_PER_GROUP
    el = jnp.where((lane >= lo) & (lane < lo + EXP_PER_GROUP), logits, neg)
    m1 = jnp.max(el, axis=1, keepdims=True)
    i1 = jnp.min(jnp.where(el == m1, lane, big), axis=1, keepdims=True)
    el2 = jnp.where(lane == i1, neg, el)
    m2 = jnp.max(el2, axis=1, keepdims=True)
    i2 = jnp.min(jnp.where(el2 == m2, lane, big), axis=1, keepdims=True)
    r = jnp.exp(m2 - m1)
    return i1 - N_GROUPS, i2 - N_GROUPS, p_top / (1.0 + r), p_top * r / (1.0 + r)


def _route_body(h_ref, g_ref, wr_ref, xpk_ref, meta_ref, meta_t_ref, cnt_ref, carry_ref):
    i = pl.program_id(0)

    @pl.when(i == 0)
    def _():
        carry_ref[...] = jnp.zeros_like(carry_ref)

    xn = _rms(h_ref[...], g_ref[...])
    xpk_ref[...] = _pack_bf16_pairs(xn)
    logits = jnp.dot(xn, wr_ref[...], preferred_element_type=F32, precision=lax.Precision.HIGHEST)
    e0, e1, w0, w1 = _route(logits)
    tm = logits.shape[0]
    lane = lax.broadcasted_iota(jnp.int32, (tm, LANES), 1)
    onehot = jnp.where((lane == e0) | (lane == e1), 1.0, 0.0)
    ri = lax.broadcasted_iota(jnp.int32, (tm, tm), 0)
    ci = lax.broadcasted_iota(jnp.int32, (tm, tm), 1)
    earlier = jnp.where(ci < ri, 1.0, 0.0).astype(BF16)
    prefix = carry_ref[...] + jnp.dot(earlier, onehot.astype(BF16), preferred_element_type=F32)
    r0 = jnp.sum(jnp.where(lane == e0, prefix, 0.0), axis=1, keepdims=True)
    r1 = jnp.sum(jnp.where(lane == e1, prefix, 0.0), axis=1, keepdims=True)
    cols = (e0.astype(F32), e1.astype(F32), r0, r1, w0, w1)
    meta = jnp.zeros((tm, LANES), F32)
    for c, val in enumerate(cols):
        meta = jnp.where(lane == c, val, meta)
    meta_ref[...] = meta[:, :META_COLS]
    meta_t_ref[...] = meta.T[:META_COLS, :]
    carry_ref[...] += jnp.sum(onehot, axis=0, keepdims=True)
    cnt_ref[...] = carry_ref[...]


def _moe_route(h2, gain, w_router):
    T, D = h2.shape
    tm = _pick_tile(T, 1152, LANES)
    return pl.pallas_call(
        _route_body,
        grid=(T // tm,),
        in_specs=[
            pl.BlockSpec((tm, D), lambda i: (i, 0)),
            pl.BlockSpec((1, D), lambda i: (0, 0)),
            pl.BlockSpec((D, LANES), lambda i: (0, 0)),
        ],
        out_specs=[
            pl.BlockSpec((tm, D // 2), lambda i: (i, 0)),
            pl.BlockSpec((tm, META_COLS), lambda i: (i, 0)),
            pl.BlockSpec((META_COLS, tm), lambda i: (0, i)),
            pl.BlockSpec((1, LANES), lambda i: (0, 0)),
        ],
        out_shape=[
            jax.ShapeDtypeStruct((T, D // 2), jnp.int32),
            jax.ShapeDtypeStruct((T, META_COLS), F32),
            jax.ShapeDtypeStruct((META_COLS, T), F32),
            jax.ShapeDtypeStruct((1, LANES), F32),
        ],
        scratch_shapes=[pltpu.VMEM((1, LANES), F32)],
        compiler_params=_params(("arbitrary",)),
        name="moe_route",
    )(h2, gain.reshape(1, D), w_router)


def _sc_gather(table, idx):
    n = idx.shape[0]
    width = table.shape[1]
    win = SC_WINDOW
    assert n % SC_CHUNK == 0
    per_worker = n // SC_WORKERS
    mesh = plsc.VectorSubcoreMesh(core_axis_name="c", subcore_axis_name="s")

    @functools.partial(
        pl.kernel, out_type=jax.ShapeDtypeStruct((n, width), table.dtype), mesh=mesh,
        scratch_types=[pltpu.VMEM((per_worker,), jnp.int32), pltpu.VMEM((2, win, width), table.dtype),
                       pltpu.SemaphoreType.DMA((2,)), pltpu.SemaphoreType.DMA((2,))],
        name="sc_row_gather")
    def gather(table_hbm, idx_hbm, out_hbm, idx_v, rows_v, gsem, osem):
        worker = lax.axis_index("s") * mesh.num_cores + lax.axis_index("c")
        base = worker * per_worker
        pltpu.sync_copy(idx_hbm.at[pl.ds(base, per_worker)], idx_v)

        @pl.loop(0, per_worker, step=2 * win)
        def _(off):
            fetch = [pltpu.async_copy(table_hbm.at[idx_v.at[pl.ds(off + s * win, win)]], rows_v.at[s], gsem.at[s])
                     for s in range(2)]
            store = []
            for s in range(2):
                fetch[s].wait()
                store.append(pltpu.async_copy(rows_v.at[s], out_hbm.at[pl.ds(base + off + s * win, win)],
                                              osem.at[s]))
            for s in range(2):
                store[s].wait()

    return gather(table, idx)


def _sc_dispatch(table, dest, n_out):
    n_rows, width = table.shape
    win = SC_SCATTER_WINDOW
    assert n_rows % win == 0
    n_win = n_rows // win
    per_worker = pl.cdiv(n_win, SC_WORKERS)
    idx = jnp.pad(dest, ((0, 0), (0, per_worker * SC_WORKERS * win - n_rows)))
    idx = idx.reshape(2, per_worker, SC_WORKERS, win).transpose(0, 2, 1, 3)
    mesh = plsc.VectorSubcoreMesh(core_axis_name="c", subcore_axis_name="s")

    @functools.partial(
        pl.kernel, out_type=jax.ShapeDtypeStruct((n_out, width), table.dtype), mesh=mesh,
        scratch_types=[pltpu.VMEM((per_worker, win), jnp.int32), pltpu.VMEM((per_worker, win), jnp.int32),
                       pltpu.VMEM((win, width), table.dtype),
                       pltpu.SemaphoreType.DMA, pltpu.SemaphoreType.DMA],
        name="sc_row_dispatch")
    def dispatch(table_hbm, idx_hbm, out_hbm, idx0_v, idx1_v, rows_v, sem0, sem1):
        worker = lax.axis_index("s") * mesh.num_cores + lax.axis_index("c")
        pltpu.sync_copy(idx_hbm.at[0, worker], idx0_v)
        pltpu.sync_copy(idx_hbm.at[1, worker], idx1_v)

        @pl.loop(0, per_worker)
        def _(j):
            window = j * SC_WORKERS + worker

            @pl.when(window < n_win)
            def _():
                pltpu.sync_copy(table_hbm.at[pl.ds(window * win, win)], rows_v)
                first = pltpu.async_copy(rows_v, out_hbm.at[idx0_v.at[j]], sem0)
                second = pltpu.async_copy(rows_v, out_hbm.at[idx1_v.at[j]], sem1)
                first.wait()
                second.wait()

    return dispatch(table, idx)


def _expert_body(te_ref, tv_ref, x_ref, wg_ref, wu_ref, wd_ref, y_ref, wg_s, wu_s, wd_s):
    t = pl.program_id(0)
    half = x_ref.shape[1]

    @pl.when((t == 0) | (te_ref[t] != te_ref[jnp.maximum(t - 1, 0)]))
    def _():
        wg_s[...] = wg_ref[0].astype(BF16)
        wu_s[...] = wu_ref[0].astype(BF16)
        wd_s[...] = wd_ref[0].astype(BF16)

    @pl.when(tv_ref[t] > 0)
    def _():
        row = lax.broadcasted_iota(jnp.int32, x_ref.shape, 0)
        hi, lo = _unpack_bf16_pairs(jnp.where(row < tv_ref[t], x_ref[...], 0))
        hi, lo = hi.astype(BF16), lo.astype(BF16)
        hg = jnp.dot(hi, wg_s[:half, :], preferred_element_type=F32)
        hg += jnp.dot(lo, wg_s[half:, :], preferred_element_type=F32)
        hu = jnp.dot(hi, wu_s[:half, :], preferred_element_type=F32)
        hu += jnp.dot(lo, wu_s[half:, :], preferred_element_type=F32)
        act = (hg * jax.nn.sigmoid(hg) * hu).astype(BF16)
        y_ref[...] = _pack_bf16_pairs(jnp.dot(act, wd_s[...], preferred_element_type=F32))

    @pl.when(tv_ref[t] == 0)
    def _():
        y_ref[...] = jnp.zeros_like(y_ref)


def _moe_experts(xs, n_sorted, tile_expert, tile_valid, w_gate, w_up, w_down):
    NP, half = n_sorted, xs.shape[1]
    _, D, FF = w_gate.shape
    tr = EXPERT_TILE
    grid_spec = pltpu.PrefetchScalarGridSpec(
        num_scalar_prefetch=2,
        grid=(NP // tr,),
        in_specs=[
            pl.BlockSpec((tr, half), lambda t, te, tv: (t, 0)),
            pl.BlockSpec((1, D, FF), lambda t, te, tv: (te[t], 0, 0)),
            pl.BlockSpec((1, D, FF), lambda t, te, tv: (te[t], 0, 0)),
            pl.BlockSpec((1, FF, D), lambda t, te, tv: (te[t], 0, 0)),
        ],
        out_specs=pl.BlockSpec((tr, half), lambda t, te, tv: (t, 0)),
        scratch_shapes=[pltpu.VMEM((D, FF), BF16), pltpu.VMEM((D, FF), BF16), pltpu.VMEM((FF, D), BF16)],
    )
    return pl.pallas_call(
        _expert_body,
        grid_spec=grid_spec,
        out_shape=jax.ShapeDtypeStruct((NP, half), jnp.int32),
        compiler_params=_params(("arbitrary",)),
        name="moe_experts",
    )(tile_expert, tile_valid, xs, w_gate, w_up, w_down)


def _moe_combine_into(o_ref, h, meta, y0, y1):
    half = y0.shape[1]
    w0 = meta[:, 4:5]
    w1 = meta[:, 5:6]
    hi0, lo0 = _unpack_bf16_pairs(y0)
    hi1, lo1 = _unpack_bf16_pairs(y1)
    o_ref[:, :half] = h[:, :half] + w0 * hi0 + w1 * hi1
    o_ref[:, half:] = h[:, half:] + w0 * lo0 + w1 * lo1


def _moe(h2, gain, w_router, layer, w_gate, w_up, w_down):
    T, D = h2.shape
    E = N_EXPERTS
    tr = EXPERT_TILE
    n_sorted = _round_up(2 * T + E * (tr - 1), tr)
    t_pad = _round_up(T, SC_CHUNK // 2)

    xpk, meta, meta_t, counts = _moe_route(h2, gain, w_router)

    cnt = counts[0, :E].astype(jnp.int32)
    padded = (cnt + tr - 1) // tr * tr
    ends = jnp.cumsum(padded)
    starts = ends - padded
    eid = meta_t[0:2].astype(jnp.int32)
    pos = meta_t[2:4].astype(jnp.int32)
    for e in range(E):
        pos = pos + jnp.where(eid == e, starts[e], 0)
    tile_start = jnp.arange(n_sorted // tr, dtype=jnp.int32) * tr
    tile_expert = jnp.minimum(jnp.sum(tile_start[:, None] >= ends[None, :], axis=1), E - 1).astype(jnp.int32)
    tile_valid = jnp.clip(cnt[tile_expert] - (tile_start - starts[tile_expert]), 0, tr).astype(jnp.int32)

    xs = _sc_dispatch(xpk, pos, n_sorted)
    ys = _moe_experts(xs, n_sorted, layer * E + tile_expert, tile_valid, w_gate, w_up, w_down)
    spare = jnp.arange(t_pad - T, dtype=jnp.int32)
    back = jnp.concatenate([pos, jnp.broadcast_to(spare[None], (2, t_pad - T))], axis=1)
    yg = _sc_gather(ys, back.reshape(-1)).reshape(2, t_pad, D // 2)
    return meta, yg


def _final_body(h_ref, meta_ref, y0_ref, y1_ref, g_ref, o_ref, hn_ref):
    _moe_combine_into(hn_ref, h_ref[0], meta_ref[0], y0_ref[0], y1_ref[0])
    o_ref[0] = _rms(hn_ref[N_META:, :], g_ref[...])


def _final_norm(h3, meta3, yg, gain, b0, nb):
    _, L, D = h3.shape
    return pl.pallas_call(
        _final_body,
        grid=(nb,),
        in_specs=[pl.BlockSpec((1, L, D), lambda b: (b0 + b, 0, 0)),
                  pl.BlockSpec((1, L, meta3.shape[2]), lambda b: (b0 + b, 0, 0)),
                  pl.BlockSpec((1, L, D // 2), lambda b: (0, b0 + b, 0)),
                  pl.BlockSpec((1, L, D // 2), lambda b: (1, b0 + b, 0)),
                  pl.BlockSpec((1, D), lambda b: (0, 0))],
        out_specs=pl.BlockSpec((1, L - N_META, D), lambda b: (b, 0, 0)),
        out_shape=jax.ShapeDtypeStruct((nb, L - N_META, D), F32),
        scratch_shapes=[pltpu.VMEM((L, D), F32)],
        compiler_params=_params(("parallel",)),
        name="combine_final_norm",
    )(h3, meta3, yg, yg, gain.reshape(1, D))


def kernel(x_prompt, x_sample, meta_tokens, norm_mix, w_in, ret_decay_fwd, ret_decay_bwd, hy_short_w, hy_short_b, hy_filt_w1, hy_filt_b1, hy_filt_w2, hy_filt_b2, hy_filt_w3, hy_sin_freq, hy_skip, w_ret_o, w_hy_o, w_out, norm_ffn, router_group, router_expert, moe_w_gate, moe_w_up, moe_w_down, norm_final):
    assert x_prompt.shape[1:] == x_sample.shape[1:]
    nbp, nbs = x_prompt.shape[0], x_sample.shape[0]
    B = nbp + nbs
    D = x_prompt.shape[2]
    L = N_META + x_prompt.shape[1]
    T = B * L
    depth = w_in.shape[0]
    q_w = RET_HEADS * RET_DK
    v_w = RET_HEADS * RET_DV
    hy_w = hy_skip.shape[2]
    hy_col0 = 2 * q_w + 2 * v_w
    gate_col0 = hy_col0 + 3 * hy_w
    assert D == q_w and w_in.shape[2] == gate_col0 + 2 * D

    x = jnp.concatenate([x_prompt, x_sample], axis=0)
    meta = jnp.broadcast_to(meta_tokens[None].astype(x.dtype), (B, N_META, D))
    h = jnp.concatenate([meta, x], axis=1).reshape(T, D)

    cos_t, sin_t = _rotary_tables(L)
    cmat, smat = _dft_tables(L)
    router = jnp.concatenate([router_group, router_expert], axis=2).astype(F32)
    router = jnp.pad(router, ((0, 0), (0, 0), (0, LANES - router.shape[2])))
    assert moe_w_gate.shape[1] == N_EXPERTS
    w_gate_all = moe_w_gate.reshape((-1,) + moe_w_gate.shape[2:])
    w_up_all = moe_w_up.reshape((-1,) + moe_w_up.shape[2:])
    w_down_all = moe_w_down.reshape((-1,) + moe_w_down.shape[2:])

    moe = None
    for i in range(depth):
        proj, h = _inproj(h, norm_mix[i], w_in[i].astype(BF16), moe)
        proj3 = proj.reshape(B, L, -1)

        lg = jnp.stack([jax.nn.log_sigmoid(ret_decay_fwd[i].astype(F32)),
                        jax.nn.log_sigmoid(ret_decay_bwd[i].astype(F32))])
        ret = _retention(proj3, lg, cos_t, sin_t)

        hs, hd = _hyena_filters(L, hy_filt_w1[i], hy_filt_b1[i], hy_filt_w2[i], hy_filt_b2[i],
                                hy_filt_w3[i], hy_sin_freq[i])
        k1, k2 = _filter_spectrum(cmat, smat, hs, hd)
        sw = hy_short_w[i].astype(F32)
        sb = hy_short_b[i].astype(F32).reshape(1, -1)
        z = _long_conv(cmat, smat, proj3, hy_col0, proj3, hy_col0 + hy_w, sw, sb, 0, hy_w,
                       hy_skip[i, 0].astype(F32), k1, k2, 0, hy_w)
        z = _long_conv(cmat, smat, z, 0, proj3, hy_col0 + 2 * hy_w, sw, sb, None, 2 * hy_w,
                       hy_skip[i, 1].astype(F32), k1, k2, hy_w, hy_w)

        h = _mix_out(ret.reshape(T, v_w), z.reshape(T, hy_w), proj, gate_col0, h,
                     w_ret_o[i].astype(BF16), w_hy_o[i].astype(BF16), w_out[i].astype(BF16))
        moe = _moe(h, norm_ffn[i], router[i], i, w_gate_all, w_up_all, w_down_all)

    h3 = h.reshape(B, L, D)
    meta3 = moe[0].reshape(B, L, -1)
    return (_final_norm(h3, meta3, moe[1], norm_final, 0, nbp),
            _final_norm(h3, meta3, moe[1], norm_final, nbp, nbs))
```

```python
import functools
import math

import jax
import jax.numpy as jnp
from jax import lax
from jax.experimental import pallas as pl
from jax.experimental.pallas import tpu as pltpu
from jax.experimental.pallas import tpu_sc as plsc

N_META = 16
RET_HEADS = 8
RET_DK = 128
RET_DV = 256
ROPE_THETA = 10000.0
HY_ORDER = 2
HY_SHORT = 3
HY_EMB = 33
HY_BANDS = (HY_EMB - 1) // 2
HY_DECAY_TARGET = 1e-2
HY_MIN_DECAY = math.log(HY_DECAY_TARGET) / 1.5
HY_MAX_DECAY = math.log(HY_DECAY_TARGET) / 0.3
N_GROUPS = 4
EXP_PER_GROUP = 8
N_EXPERTS = N_GROUPS * EXP_PER_GROUP
RMS_EPS = 1e-6

LANES = 128
BF16_SUBLANES = 16
MXU_DIM = 256
RET_CHUNK = 256
VMEM_LIMIT = 56 * 1024 * 1024

F32 = jnp.float32
BF16 = jnp.bfloat16


def _round_up(n, m):
    return (n + m - 1) // m * m


def _pick_tile(n, target, mult):
    best = None
    for t in range(mult, min(n, target) + 1, mult):
        if n % t == 0:
            best = t
    assert best is not None, (n, target, mult)
    return best


def _params(sem):
    return pltpu.CompilerParams(dimension_semantics=sem, vmem_limit_bytes=VMEM_LIMIT)


def _rms(x, gain):
    ms = jnp.mean(x * x, axis=-1, keepdims=True)
    return x * lax.rsqrt(ms + RMS_EPS) * gain


def _inproj_body(h_ref, g_ref, w_ref, o_ref, xn_ref):
    @pl.when(pl.program_id(1) == 0)
    def _():
        xn_ref[...] = _rms(h_ref[...], g_ref[...]).astype(BF16)

    o_ref[...] = jnp.dot(xn_ref[...], w_ref[...], preferred_element_type=F32).astype(o_ref.dtype)


def _inproj_moe_body(h_ref, meta_ref, y0_ref, y1_ref, g_ref, w_ref, o_ref, hn_ref, xn_ref):
    @pl.when(pl.program_id(1) == 0)
    def _():
        _moe_combine_into(hn_ref, h_ref[...], meta_ref[...], y0_ref[0], y1_ref[0])
        xn_ref[...] = _rms(hn_ref[...], g_ref[...]).astype(BF16)

    o_ref[...] = jnp.dot(xn_ref[...], w_ref[...], preferred_element_type=F32).astype(o_ref.dtype)


def _inproj(h2, gain, w, moe=None):
    T, D = h2.shape
    nc = w.shape[1]
    tm = _pick_tile(T, 1152, BF16_SUBLANES)
    tn = _pick_tile(nc, 1024, LANES)
    common = dict(
        grid=(T // tm, nc // tn),
        scratch_shapes=[pltpu.VMEM((tm, D), BF16)],
        compiler_params=_params(("parallel", "arbitrary")),
    )
    h_spec = pl.BlockSpec((tm, D), lambda i, j: (i, 0))
    g_spec = pl.BlockSpec((1, D), lambda i, j: (0, 0))
    w_spec = pl.BlockSpec((D, tn), lambda i, j: (0, j))
    o_spec = pl.BlockSpec((tm, tn), lambda i, j: (i, j))
    o_shape = jax.ShapeDtypeStruct((T, nc), BF16)
    if moe is None:
        proj = pl.pallas_call(_inproj_body, in_specs=[h_spec, g_spec, w_spec], out_specs=o_spec,
                              out_shape=o_shape, name="norm_inproj", **common)(h2, gain.reshape(1, D), w)
        return proj, h2
    meta, yg = moe
    return pl.pallas_call(
        _inproj_moe_body,
        in_specs=[h_spec, pl.BlockSpec((tm, meta.shape[1]), lambda i, j: (i, 0)),
                  pl.BlockSpec((1, tm, D // 2), lambda i, j: (0, i, 0)),
                  pl.BlockSpec((1, tm, D // 2), lambda i, j: (1, i, 0)),
                  g_spec, w_spec],
        out_specs=[o_spec, h_spec],
        out_shape=[o_shape, jax.ShapeDtypeStruct((T, D), F32)],
        name="combine_norm_inproj", **common,
    )(h2, meta, yg, yg, gain.reshape(1, D), w)


def _dot_t(a, b):
    return lax.dot_general(a, b, (((0,), (0,)), ((), ())), preferred_element_type=F32)


def _dot_nt(a, b):
    return lax.dot_general(a, b, (((1,), (1,)), ((), ())), preferred_element_type=F32)


def _ret_body(lg_ref, q_ref, k_ref, v_ref, g_ref, cos_ref, sin_ref, o_ref, rb_ref, *, seq_len):
    C = RET_CHUNK
    L = seq_len
    n_chunks = pl.cdiv(L, C)
    head = pl.program_id(1)
    lgf = lg_ref[0, head]
    lgb = lg_ref[1, head]

    def chunk(ref, n):
        lo, hi = n * C, min((n + 1) * C, L)
        x = ref[0, lo:hi, :]
        if hi - lo < C:
            x = jnp.concatenate([x, jnp.zeros((C - (hi - lo), x.shape[1]), x.dtype)], axis=0)
        return x

    def rotary(ref, n):
        x = chunk(ref, n).astype(F32)
        sl = slice(n * C, (n + 1) * C)
        return x * cos_ref[sl, :] + pltpu.roll(x, RET_DK // 2, 1) * sin_ref[sl, :]

    row = lax.broadcasted_iota(jnp.int32, (C, LANES), 0).astype(F32)
    ri = lax.broadcasted_iota(jnp.int32, (C, C), 0).astype(F32)
    ci = lax.broadcasted_iota(jnp.int32, (C, C), 1).astype(F32)
    decay = jnp.exp(jnp.where(ci <= ri, (ri - ci) * lgf, (ci - ri) * lgb))
    qf_dec = jnp.exp((row + 1.0) * lgf)
    qb_dec = jnp.exp((C - row) * lgb)
    kf_dec = jnp.exp((C - 1.0 - row) * lgf)
    kb_dec = jnp.exp(row * lgb)
    cf = jnp.exp(C * lgf)
    cb = jnp.exp(C * lgb)

    state = jnp.zeros((RET_DK, RET_DV), F32)
    for n in reversed(range(n_chunks)):
        rb_ref[n] = state.astype(BF16)
        if n > 0:
            kb = (rotary(k_ref, n) * kb_dec).astype(BF16)
            state = cb * state + _dot_t(kb, chunk(v_ref, n))

    state = jnp.zeros((RET_DK, RET_DV), F32)
    for n in range(n_chunks):
        q = rotary(q_ref, n) * (RET_DK ** -0.5)
        k = rotary(k_ref, n)
        v = chunk(v_ref, n)
        scores = _dot_nt(q.astype(BF16), k.astype(BF16)) * decay
        o = jnp.dot(scores.astype(BF16), v, preferred_element_type=F32)
        o += jnp.dot((q * qf_dec).astype(BF16), state.astype(BF16), preferred_element_type=F32)
        o += jnp.dot((q * qb_dec).astype(BF16), rb_ref[n], preferred_element_type=F32)
        if n + 1 < n_chunks:
            state = cf * state + _dot_t((k * kf_dec).astype(BF16), v)
        o = o * lax.rsqrt(jnp.mean(o * o, axis=-1, keepdims=True) + RMS_EPS)
        lo, hi = n * C, min((n + 1) * C, L)
        g = g_ref[0, lo:hi, :].astype(F32)
        o_ref[0, lo:hi, :] = (g * jax.nn.sigmoid(g) * o[: hi - lo]).astype(o_ref.dtype)


def _retention(proj3, lg, cos_t, sin_t):
    B, L, _ = proj3.shape
    n_chunks = pl.cdiv(L, RET_CHUNK)
    lp = n_chunks * RET_CHUNK
    k_blk = RET_HEADS
    v_blk = 2 * RET_HEADS * RET_DK // RET_DV
    g_blk = v_blk + RET_HEADS
    return pl.pallas_call(
        functools.partial(_ret_body, seq_len=L),
        grid=(B, RET_HEADS),
        in_specs=[
            pl.BlockSpec(memory_space=pltpu.SMEM),
            pl.BlockSpec((1, L, RET_DK), lambda b, h: (b, 0, h)),
            pl.BlockSpec((1, L, RET_DK), lambda b, h: (b, 0, k_blk + h)),
            pl.BlockSpec((1, L, RET_DV), lambda b, h: (b, 0, v_blk + h)),
            pl.BlockSpec((1, L, RET_DV), lambda b, h: (b, 0, g_blk + h)),
            pl.BlockSpec((lp, RET_DK), lambda b, h: (0, 0)),
            pl.BlockSpec((lp, RET_DK), lambda b, h: (0, 0)),
        ],
        out_specs=pl.BlockSpec((1, L, RET_DV), lambda b, h: (b, 0, h)),
        out_shape=jax.ShapeDtypeStruct((B, L, RET_HEADS * RET_DV), BF16),
        scratch_shapes=[pltpu.VMEM((n_chunks, RET_DK, RET_DV), BF16)],
        compiler_params=_params(("parallel", "arbitrary")),
        name="retention",
    )(lg, proj3, proj3, proj3, proj3, cos_t, sin_t)


def _rotary_tables(L):
    half = RET_DK // 2
    lp = _round_up(L, RET_CHUNK)
    inv = ROPE_THETA ** (-jnp.arange(half, dtype=F32) / half)
    ang = jnp.arange(lp, dtype=F32)[:, None] * inv[None, :]
    cos, sin = jnp.cos(ang), jnp.sin(ang)
    return jnp.concatenate([cos, cos], axis=1), jnp.concatenate([-sin, sin], axis=1)


def _row_tiles(n, tile):
    return [(s, min(tile, n - s)) for s in range(0, n, tile)]


class _HyenaDims:
    def __init__(self, L):
        assert L % 2 == 0
        self.L = L
        self.H = L // 2
        self.F = self.H + 1
        self.HP = _round_up(self.H, BF16_SUBLANES)
        self.HK = _round_up(self.H, MXU_DIM)
        self.FM = _round_up(self.F, BF16_SUBLANES)
        self.IK = _round_up(2 * self.FM, MXU_DIM)


def _dft_tables(dims):
    n = 2 * dims.L

    def trig(f, t, valid):
        ang = ((f * t) % n).astype(F32) * (2.0 * math.pi / n)
        return (jnp.where(valid, jnp.cos(ang), 0.0).astype(BF16),
                jnp.where(valid, jnp.sin(ang), 0.0).astype(BF16))

    f = jnp.arange(dims.FM, dtype=jnp.int32)[:, None]
    s = jnp.arange(dims.HK, dtype=jnp.int32)[None, :]
    ok = (f < dims.F) & (s < dims.H)
    fwd = trig(f, 2 * s, ok) + trig(f, 2 * s + 1, ok)

    s = jnp.arange(dims.HP, dtype=jnp.int32)[:, None]
    j = jnp.arange(dims.IK, dtype=jnp.int32)[None, :]
    f = j % dims.FM
    ok = (f < dims.F) & (s < dims.H)

    def inverse(t):
        cos, sin = trig(f, t, ok)
        return jnp.where(j < dims.FM, cos, jnp.where(j < 2 * dims.FM, sin, jnp.zeros_like(cos)))

    return fwd, (inverse(2 * s), inverse(2 * s + 1))


def _forward_dft(tables, even_ref, odd_ref, rows):
    ce_ref, se_ref, co_ref, so_ref = tables
    pe = jnp.dot(ce_ref[rows, :], even_ref[...], preferred_element_type=F32)
    po = jnp.dot(co_ref[rows, :], odd_ref[...], preferred_element_type=F32)
    ae = jnp.dot(se_ref[rows, :], even_ref[...], preferred_element_type=F32)
    ao = jnp.dot(so_ref[rows, :], odd_ref[...], preferred_element_type=F32)
    return pe + po, ae + ao, pe - po, ao - ae


def _spec_body(ce_ref, se_ref, co_ref, so_ref, hs_ref, hd_ref, k1_ref, k2_ref, k1m_ref, k2m_ref,
               se_pad, so_pad, de_pad, do_pad, *, dims):
    H, F, N = dims.H, dims.F, 2 * dims.L
    for pad, src, parity in ((se_pad, hs_ref, 0), (so_pad, hs_ref, 1), (de_pad, hd_ref, 0), (do_pad, hd_ref, 1)):
        pad[H:, :] = jnp.zeros((pad.shape[0] - H, pad.shape[1]), BF16)
        pad[:H, :] = src[parity].astype(BF16)
    rows = slice(None)
    tables = (ce_ref, se_ref, co_ref, so_ref)
    re, _, re_m, _ = _forward_dft(tables, se_pad, so_pad, rows)
    _, nim, _, nim_m = _forward_dft(tables, de_pad, do_pad, rows)
    f = lax.broadcasted_iota(jnp.int32, (dims.FM, 1), 0)
    w = jnp.where(f == 0, 1.0, 2.0) / N
    w_lo = jnp.where(f < F, w, 0.0)
    w_hi = jnp.where(f < H, w, 0.0)
    k1_ref[...] = w_lo * re
    k2_ref[...] = -w_lo * nim
    k1m_ref[...] = w_hi * re_m
    k2m_ref[...] = -w_hi * nim_m


def _filter_spectrum(dims, fwd, hs, hd):
    width = hs.shape[2]
    cw = MXU_DIM
    resident = pl.BlockSpec((dims.FM, dims.HK), lambda j: (0, 0), pipeline_mode=pl.Buffered(1))
    return pl.pallas_call(
        functools.partial(_spec_body, dims=dims),
        grid=(width // cw,),
        in_specs=[resident] * 4 + [pl.BlockSpec((2, dims.H, cw), lambda j: (0, 0, j))] * 2,
        out_specs=[pl.BlockSpec((dims.FM, cw), lambda j: (0, j))] * 4,
        out_shape=[jax.ShapeDtypeStruct((dims.FM, width), F32)] * 4,
        scratch_shapes=[pltpu.VMEM((dims.HK, cw), BF16)] * 4,
        compiler_params=_params(("arbitrary",)),
        name="hyena_filter_spectrum",
    )(*fwd, hs, hd)


def _short_conv(ref, w_ref, b_ref):
    u = ref[0].astype(F32)
    L = u.shape[0]
    t = lax.broadcasted_iota(jnp.int32, u.shape, 0)
    prev = jnp.where(t == 0, 0.0, pltpu.roll(u, 1, 0))
    nxt = jnp.where(t == L - 1, 0.0, pltpu.roll(u, L - 1, 0))
    w = w_ref[...]
    return b_ref[...] + w[0:1] * prev + w[1:2] * u + w[2:3] * nxt


def _conv_body(*refs, dims, first):
    tables = refs[:4]
    ie_ref, io_ref, u_ref, x_ref = refs[4:8]
    n_taps = 4 if first else 2
    taps = refs[8:8 + n_taps]
    (skip_ref, k1_ref, k2_ref, k1m_ref, k2m_ref, o_ref,
     ue_ref, uo_ref, xe_ref, xo_ref, ge_ref, go_ref, nat_ref) = refs[8 + n_taps:]
    L, H, HP, HK, FM, IK = dims.L, dims.H, dims.HP, dims.HK, dims.FM, dims.IK
    cw = ue_ref.shape[1]

    lane_slabs = [(k, slice(k * LANES, (k + 1) * LANES)) for k in range(cw // LANES)]

    def split_into(even_ref, odd_ref, value):
        for k, lanes in lane_slabs:
            nat_ref[k, :L, :] = value[:, lanes]
            even_ref[:HP, lanes] = nat_ref[k, pl.ds(0, HP, stride=2), :].astype(BF16)
            odd_ref[:HP, lanes] = nat_ref[k, pl.ds(1, HP, stride=2), :].astype(BF16)

    for k, _ in lane_slabs:
        nat_ref[k, L:, :] = jnp.zeros((2 * HP - L, LANES), F32)
    if HK > HP:
        ue_ref[HP:, :] = jnp.zeros((HK - HP, cw), BF16)
        uo_ref[HP:, :] = jnp.zeros((HK - HP, cw), BF16)
    if first:
        split_into(ue_ref, uo_ref, _short_conv(u_ref, taps[0], taps[1]))
    else:
        ue_ref[:HP, :] = u_ref[0, 0]
        uo_ref[:HP, :] = u_ref[0, 1]
    split_into(xe_ref, xo_ref, _short_conv(x_ref, taps[-2], taps[-1]))

    if IK > 2 * FM:
        ge_ref[2 * FM:, :] = jnp.zeros((IK - 2 * FM, cw), BF16)
        go_ref[2 * FM:, :] = jnp.zeros((IK - 2 * FM, cw), BF16)
    for lo, sz in _row_tiles(FM, MXU_DIM):
        rows = slice(lo, lo + sz)
        p, a, pm, am = _forward_dft(tables, ue_ref, uo_ref, rows)
        k1, k2, k1m, k2m = k1_ref[rows, :], k2_ref[rows, :], k1m_ref[rows, :], k2m_ref[rows, :]
        e1, e2 = p * k1 + a * k2, a * k1 - p * k2
        e1m, e2m = pm * k1m + am * k2m, am * k1m - pm * k2m
        ge_ref[lo:lo + sz, :] = (e1 + e1m).astype(BF16)
        ge_ref[FM + lo:FM + lo + sz, :] = (e2 - e2m).astype(BF16)
        go_ref[lo:lo + sz, :] = (e1 - e1m).astype(BF16)
        go_ref[FM + lo:FM + lo + sz, :] = (e2 + e2m).astype(BF16)

    skip = skip_ref[...]
    for lo, sz in _row_tiles(HP, MXU_DIM):
        rows = slice(lo, lo + sz)
        ye = jnp.dot(ie_ref[rows, :], ge_ref[...], preferred_element_type=F32)
        yo = jnp.dot(io_ref[rows, :], go_ref[...], preferred_element_type=F32)
        oe = xe_ref[rows, :].astype(F32) * (ye + skip * ue_ref[rows, :].astype(F32))
        oo = xo_ref[rows, :].astype(F32) * (yo + skip * uo_ref[rows, :].astype(F32))
        if first:
            o_ref[0, 0, rows, :] = oe.astype(o_ref.dtype)
            o_ref[0, 1, rows, :] = oo.astype(o_ref.dtype)
        else:
            valid = min(sz, H - lo)
            for k, lanes in lane_slabs:
                nat_ref[k, pl.ds(2 * lo, valid, stride=2), :] = oe[:valid, lanes]
                nat_ref[k, pl.ds(2 * lo + 1, valid, stride=2), :] = oo[:valid, lanes]
    if not first:
        for k, lanes in lane_slabs:
            o_ref[0, :, lanes] = nat_ref[k, :L, :]


def _long_conv(dims, fwd, inv, u_arr, u_col0, x_arr, x_col0, short_w, short_b, short_u_col0, short_x_col0,
               skip, spectrum, k_col0, width):
    B = u_arr.shape[0]
    L = dims.L
    cw = MXU_DIM
    ub, xb, kb = u_col0 // cw, x_col0 // cw, k_col0 // cw
    first = short_u_col0 is not None
    fwd_spec = pl.BlockSpec((dims.FM, dims.HK), lambda c, b: (0, 0), pipeline_mode=pl.Buffered(1))
    inv_spec = pl.BlockSpec((dims.HP, dims.IK), lambda c, b: (0, 0), pipeline_mode=pl.Buffered(1))

    def taps(col0):
        blk = col0 // cw
        return [pl.BlockSpec((HY_SHORT, cw), lambda c, b: (0, blk + c)),
                pl.BlockSpec((1, cw), lambda c, b: (0, blk + c))]

    if first:
        u_spec = pl.BlockSpec((1, L, cw), lambda c, b: (b, 0, ub + c))
        out_spec = pl.BlockSpec((1, 2, dims.HP, cw), lambda c, b: (b, 0, 0, c))
        out_shape = jax.ShapeDtypeStruct((B, 2, dims.HP, width), BF16)
    else:
        u_spec = pl.BlockSpec((1, 2, dims.HP, cw), lambda c, b: (b, 0, 0, ub + c))
        out_spec = pl.BlockSpec((1, L, cw), lambda c, b: (b, 0, c))
        out_shape = jax.ShapeDtypeStruct((B, L, width), F32)
    in_specs = [fwd_spec] * 4 + [inv_spec] * 2 + [u_spec, pl.BlockSpec((1, L, cw), lambda c, b: (b, 0, xb + c))]
    args = list(fwd) + list(inv) + [u_arr, x_arr]
    for col0 in ([short_u_col0] if first else []) + [short_x_col0]:
        in_specs += taps(col0)
        args += [short_w, short_b]
    in_specs += [pl.BlockSpec((1, cw), lambda c, b: (0, c))]
    in_specs += [pl.BlockSpec((dims.FM, cw), lambda c, b: (0, kb + c))] * 4
    args += [skip.reshape(1, width)] + list(spectrum)
    return pl.pallas_call(
        functools.partial(_conv_body, dims=dims, first=first),
        grid=(width // cw, B),
        in_specs=in_specs,
        out_specs=out_spec,
        out_shape=out_shape,
        scratch_shapes=([pltpu.VMEM((dims.HK, cw), BF16)] * 2 + [pltpu.VMEM((dims.HP, cw), BF16)] * 2
                        + [pltpu.VMEM((dims.IK, cw), BF16)] * 2 + [pltpu.VMEM((cw // LANES, 2 * dims.HP, LANES), F32)]),
        compiler_params=_params(("parallel", "arbitrary")),
        name="hyena_long_conv",
    )(*args)


def _hyena_filters(L, w1, b1, w2, b2, w3, freq):
    hp = lax.Precision.HIGHEST
    width = w3.shape[1] // (2 * HY_ORDER)
    t = jnp.linspace(0.0, 1.0, L, dtype=F32)
    w = (2.0 * math.pi / L) * jnp.arange(L, dtype=F32)
    bands = jnp.linspace(1e-4, HY_BANDS - 1, HY_BANDS, dtype=F32)
    fw = w[:, None] * bands[None, :]
    z = jnp.concatenate([t[:, None], jnp.cos(fw), -jnp.sin(fw)], axis=-1)
    h = jnp.sin(freq * (jnp.dot(z, w1, precision=hp) + b1))
    h = jnp.sin(freq * (jnp.dot(h, w2, precision=hp) + b2))
    h = jnp.dot(h, w3, precision=hp).reshape(L, HY_ORDER, 2, width)
    deltas = jnp.abs(jnp.linspace(HY_MIN_DECAY, HY_MAX_DECAY, width, dtype=F32))
    h = h * jnp.exp(-t[:, None] * deltas[None, :])[:, None, None, :]
    hf = h[:, :, 0]
    hb = h[:, :, 1].at[0].set(0.0)
    scale = lax.rsqrt(jnp.sum(hf * hf, axis=0) + jnp.sum(hb * hb, axis=0) + 1e-6)
    hf, hb = hf * scale, hb * scale
    def parity_split(x):
        x = x.reshape(L // 2, 2, HY_ORDER * width)
        return jnp.transpose(x, (1, 0, 2))

    return parity_split(hf + hb), parity_split(hf - hb)


def _mix_body(ret_ref, hy_ref, gr_ref, gh_ref, h_ref, wr_ref, wh_ref, wo_ref, o_ref):
    ret = jnp.dot(ret_ref[...], wr_ref[...], preferred_element_type=F32)
    hyo = jnp.dot(hy_ref[...].astype(BF16), wh_ref[...], preferred_element_type=F32)
    merged = (jax.nn.sigmoid(gr_ref[...].astype(F32)) * ret
              + jax.nn.sigmoid(gh_ref[...].astype(F32)) * hyo)
    o_ref[...] = h_ref[...] + jnp.dot(merged.astype(BF16), wo_ref[...], preferred_element_type=F32)


def _mix_out(ret2, hy2, proj2, gate_col0, h2, w_ret_o, w_hy_o, w_out):
    T, D = h2.shape
    tm = _pick_tile(T, 384, BF16_SUBLANES)
    gb = gate_col0 // D
    row = lambda w: pl.BlockSpec((tm, w), lambda i: (i, 0))
    full = lambda a: pl.BlockSpec(a.shape, lambda i: (0, 0))
    return pl.pallas_call(
        _mix_body,
        grid=(T // tm,),
        in_specs=[
            row(ret2.shape[1]), row(hy2.shape[1]),
            pl.BlockSpec((tm, D), lambda i: (i, gb)),
            pl.BlockSpec((tm, D), lambda i: (i, gb + 1)),
            row(D), full(w_ret_o), full(w_hy_o), full(w_out),
        ],
        out_specs=row(D),
        out_shape=jax.ShapeDtypeStruct((T, D), F32),
        compiler_params=_params(("parallel",)),
        name="merge_outproj",
    )(ret2, hy2, proj2, proj2, h2, w_ret_o, w_hy_o, w_out)


EXPERT_TILE = 512
META_COLS = 8
SC_WINDOW = 64
SC_WORKERS = 32
SC_CHUNK = 2 * SC_WINDOW * SC_WORKERS
SC_SCATTER_WINDOW = 128


def _pack_bf16_pairs(x):
    n = x.shape[1] // 2
    xb = x.astype(BF16).astype(F32)
    hi = lax.bitcast_convert_type(xb[:, :n], jnp.uint32)
    lo = lax.bitcast_convert_type(xb[:, n:], jnp.uint32)
    return lax.bitcast_convert_type(hi | (lo >> 16), jnp.int32)


def _unpack_bf16_pairs(w):
    u = lax.bitcast_convert_type(w, jnp.uint32)
    hi = lax.bitcast_convert_type(u & jnp.uint32(0xFFFF0000), F32)
    lo = lax.bitcast_convert_type(u << 16, F32)
    return hi, lo


def _route(logits):
    lane = lax.broadcasted_iota(jnp.int32, logits.shape, 1)
    neg = -jnp.inf
    big = jnp.int32(1 << 20)
    gl = jnp.where(lane < N_GROUPS, logits, neg)
    gmax = jnp.max(gl, axis=1, keepdims=True)
    p_top = 1.0 / jnp.sum(jnp.exp(gl - gmax), axis=1, keepdims=True)
    g_idx = jnp.min(jnp.where(gl == gmax, lane, big), axis=1, keepdims=True)
    lo = N_GROUPS + g_idx * EXP_PER_GROUP
    el = jnp.where((lane >= lo) & (lane < lo + EXP_PER_GROUP), logits, neg)
    m1 = jnp.max(el, axis=1, keepdims=True)
    i1 = jnp.min(jnp.where(el == m1, lane, big), axis=1, keepdims=True)
    el2 = jnp.where(lane == i1, neg, el)
    m2 = jnp.max(el2, axis=1, keepdims=True)
    i2 = jnp.min(jnp.where(el2 == m2, lane, big), axis=1, keepdims=True)
    r = jnp.exp(m2 - m1)
    return i1 - N_GROUPS, i2 - N_GROUPS, p_top / (1.0 + r), p_top * r / (1.0 + r)


def _route_body(h_ref, g_ref, wr_ref, xpk_ref, meta_ref, meta_t_ref, cnt_ref, carry_ref):
    i = pl.program_id(0)

    @pl.when(i == 0)
    def _():
        carry_ref[...] = jnp.zeros_like(carry_ref)

    xn = _rms(h_ref[...], g_ref[...])
    xpk_ref[...] = _pack_bf16_pairs(xn)
    logits = jnp.dot(xn, wr_ref[...], preferred_element_type=F32, precision=lax.Precision.HIGHEST)
    e0, e1, w0, w1 = _route(logits)
    tm = logits.shape[0]
    lane = lax.broadcasted_iota(jnp.int32, (tm, LANES), 1)
    onehot = jnp.where((lane == e0) | (lane == e1), 1.0, 0.0)
    ri = lax.broadcasted_iota(jnp.int32, (tm, tm), 0)
    ci = lax.broadcasted_iota(jnp.int32, (tm, tm), 1)
    earlier = jnp.where(ci < ri, 1.0, 0.0).astype(BF16)
    prefix = carry_ref[...] + jnp.dot(earlier, onehot.astype(BF16), preferred_element_type=F32)
    r0 = jnp.sum(jnp.where(lane == e0, prefix, 0.0), axis=1, keepdims=True)
    r1 = jnp.sum(jnp.where(lane == e1, prefix, 0.0), axis=1, keepdims=True)
    cols = (e0.astype(F32), e1.astype(F32), r0, r1, w0, w1)
    meta = jnp.zeros((tm, LANES), F32)
    for c, val in enumerate(cols):
        meta = jnp.where(lane == c, val, meta)
    meta_ref[...] = meta[:, :META_COLS]
    meta_t_ref[...] = meta.T[:META_COLS, :]
    carry_ref[...] += jnp.sum(onehot, axis=0, keepdims=True)
    cnt_ref[...] = carry_ref[...]


def _moe_route(h2, gain, w_router):
    T, D = h2.shape
    tm = _pick_tile(T, 1152, LANES)
    return pl.pallas_call(
        _route_body,
        grid=(T // tm,),
        in_specs=[
            pl.BlockSpec((tm, D), lambda i: (i, 0)),
            pl.BlockSpec((1, D), lambda i: (0, 0)),
            pl.BlockSpec((D, LANES), lambda i: (0, 0)),
        ],
        out_specs=[
            pl.BlockSpec((tm, D // 2), lambda i: (i, 0)),
            pl.BlockSpec((tm, META_COLS), lambda i: (i, 0)),
            pl.BlockSpec((META_COLS, tm), lambda i: (0, i)),
            pl.BlockSpec((1, LANES), lambda i: (0, 0)),
        ],
        out_shape=[
            jax.ShapeDtypeStruct((T, D // 2), jnp.int32),
            jax.ShapeDtypeStruct((T, META_COLS), F32),
            jax.ShapeDtypeStruct((META_COLS, T), F32),
            jax.ShapeDtypeStruct((1, LANES), F32),
        ],
        scratch_shapes=[pltpu.VMEM((1, LANES), F32)],
        compiler_params=_params(("arbitrary",)),
        name="moe_route",
    )(h2, gain.reshape(1, D), w_router)


def _sc_gather(table, idx):
    n = idx.shape[0]
    width = table.shape[1]
    win = SC_WINDOW
    assert n % SC_CHUNK == 0
    per_worker = n // SC_WORKERS
    mesh = plsc.VectorSubcoreMesh(core_axis_name="c", subcore_axis_name="s")

    @functools.partial(
        pl.kernel, out_type=jax.ShapeDtypeStruct((n, width), table.dtype), mesh=mesh,
        scratch_types=[pltpu.VMEM((per_worker,), jnp.int32), pltpu.VMEM((2, win, width), table.dtype),
                       pltpu.SemaphoreType.DMA((2,)), pltpu.SemaphoreType.DMA((2,))],
        name="sc_row_gather")
    def gather(table_hbm, idx_hbm, out_hbm, idx_v, rows_v, gsem, osem):
        worker = lax.axis_index("s") * mesh.num_cores + lax.axis_index("c")
        base = worker * per_worker
        pltpu.sync_copy(idx_hbm.at[pl.ds(base, per_worker)], idx_v)

        @pl.loop(0, per_worker, step=2 * win)
        def _(off):
            fetch = [pltpu.async_copy(table_hbm.at[idx_v.at[pl.ds(off + s * win, win)]], rows_v.at[s], gsem.at[s])
                     for s in range(2)]
            store = []
            for s in range(2):
                fetch[s].wait()
                store.append(pltpu.async_copy(rows_v.at[s], out_hbm.at[pl.ds(base + off + s * win, win)],
                                              osem.at[s]))
            for s in range(2):
                store[s].wait()

    return gather(table, idx)


def _sc_dispatch(table, dest, n_out):
    n_rows, width = table.shape
    win = SC_SCATTER_WINDOW
    assert n_rows % win == 0
    n_win = n_rows // win
    per_worker = pl.cdiv(n_win, SC_WORKERS)
    idx = jnp.pad(dest, ((0, 0), (0, per_worker * SC_WORKERS * win - n_rows)))
    idx = idx.reshape(2, per_worker, SC_WORKERS, win).transpose(0, 2, 1, 3)
    mesh = plsc.VectorSubcoreMesh(core_axis_name="c", subcore_axis_name="s")

    @functools.partial(
        pl.kernel, out_type=jax.ShapeDtypeStruct((n_out, width), table.dtype), mesh=mesh,
        scratch_types=[pltpu.VMEM((per_worker, win), jnp.int32), pltpu.VMEM((per_worker, win), jnp.int32),
                       pltpu.VMEM((win, width), table.dtype),
                       pltpu.SemaphoreType.DMA, pltpu.SemaphoreType.DMA],
        name="sc_row_dispatch")
    def dispatch(table_hbm, idx_hbm, out_hbm, idx0_v, idx1_v, rows_v, sem0, sem1):
        worker = lax.axis_index("s") * mesh.num_cores + lax.axis_index("c")
        pltpu.sync_copy(idx_hbm.at[0, worker], idx0_v)
        pltpu.sync_copy(idx_hbm.at[1, worker], idx1_v)

        @pl.loop(0, per_worker)
        def _(j):
            window = j * SC_WORKERS + worker

            @pl.when(window < n_win)
            def _():
                pltpu.sync_copy(table_hbm.at[pl.ds(window * win, win)], rows_v)
                first = pltpu.async_copy(rows_v, out_hbm.at[idx0_v.at[j]], sem0)
                second = pltpu.async_copy(rows_v, out_hbm.at[idx1_v.at[j]], sem1)
                first.wait()
                second.wait()

    return dispatch(table, idx)


def _expert_body(te_ref, tv_ref, x_ref, wg_ref, wu_ref, wd_ref, y_ref, wg_s, wu_s, wd_s):
    t = pl.program_id(0)
    half = x_ref.shape[1]

    @pl.when((t == 0) | (te_ref[t] != te_ref[jnp.maximum(t - 1, 0)]))
    def _():
        wg_s[...] = wg_ref[0].astype(BF16)
        wu_s[...] = wu_ref[0].astype(BF16)
        wd_s[...] = wd_ref[0].astype(BF16)

    @pl.when(tv_ref[t] > 0)
    def _():
        row = lax.broadcasted_iota(jnp.int32, x_ref.shape, 0)
        hi, lo = _unpack_bf16_pairs(jnp.where(row < tv_ref[t], x_ref[...], 0))
        hi, lo = hi.astype(BF16), lo.astype(BF16)
        hg = jnp.dot(hi, wg_s[:half, :], preferred_element_type=F32)
        hg += jnp.dot(lo, wg_s[half:, :], preferred_element_type=F32)
        hu = jnp.dot(hi, wu_s[:half, :], preferred_element_type=F32)
        hu += jnp.dot(lo, wu_s[half:, :], preferred_element_type=F32)
        act = (hg * jax.nn.sigmoid(hg) * hu).astype(BF16)
        y_ref[...] = _pack_bf16_pairs(jnp.dot(act, wd_s[...], preferred_element_type=F32))

    @pl.when(tv_ref[t] == 0)
    def _():
        y_ref[...] = jnp.zeros_like(y_ref)


def _moe_experts(xs, n_sorted, tile_expert, tile_valid, w_gate, w_up, w_down):
    NP, half = n_sorted, xs.shape[1]
    _, D, FF = w_gate.shape
    tr = EXPERT_TILE
    grid_spec = pltpu.PrefetchScalarGridSpec(
        num_scalar_prefetch=2,
        grid=(NP // tr,),
        in_specs=[
            pl.BlockSpec((tr, half), lambda t, te, tv: (t, 0)),
            pl.BlockSpec((1, D, FF), lambda t, te, tv: (te[t], 0, 0)),
            pl.BlockSpec((1, D, FF), lambda t, te, tv: (te[t], 0, 0)),
            pl.BlockSpec((1, FF, D), lambda t, te, tv: (te[t], 0, 0)),
        ],
        out_specs=pl.BlockSpec((tr, half), lambda t, te, tv: (t, 0)),
        scratch_shapes=[pltpu.VMEM((D, FF), BF16), pltpu.VMEM((D, FF), BF16), pltpu.VMEM((FF, D), BF16)],
    )
    return pl.pallas_call(
        _expert_body,
        grid_spec=grid_spec,
        out_shape=jax.ShapeDtypeStruct((NP, half), jnp.int32),
        compiler_params=_params(("arbitrary",)),
        name="moe_experts",
    )(tile_expert, tile_valid, xs, w_gate, w_up, w_down)


def _moe_combine_into(o_ref, h, meta, y0, y1):
    half = y0.shape[1]
    w0 = meta[:, 4:5]
    w1 = meta[:, 5:6]
    hi0, lo0 = _unpack_bf16_pairs(y0)
    hi1, lo1 = _unpack_bf16_pairs(y1)
    o_ref[:, :half] = h[:, :half] + w0 * hi0 + w1 * hi1
    o_ref[:, half:] = h[:, half:] + w0 * lo0 + w1 * lo1


def _moe(h2, gain, w_router, layer, w_gate, w_up, w_down):
    T, D = h2.shape
    E = N_EXPERTS
    tr = EXPERT_TILE
    n_sorted = _round_up(2 * T + E * (tr - 1), tr)
    t_pad = _round_up(T, SC_CHUNK // 2)

    xpk, meta, meta_t, counts = _moe_route(h2, gain, w_router)

    cnt = counts[0, :E].astype(jnp.int32)
    padded = (cnt + tr - 1) // tr * tr
    ends = jnp.cumsum(padded)
    starts = ends - padded
    eid = meta_t[0:2].astype(jnp.int32)
    pos = meta_t[2:4].astype(jnp.int32)
    for e in range(E):
        pos = pos + jnp.where(eid == e, starts[e], 0)
    tile_start = jnp.arange(n_sorted // tr, dtype=jnp.int32) * tr
    tile_expert = jnp.minimum(jnp.sum(tile_start[:, None] >= ends[None, :], axis=1), E - 1).astype(jnp.int32)
    tile_valid = jnp.clip(cnt[tile_expert] - (tile_start - starts[tile_expert]), 0, tr).astype(jnp.int32)

    xs = _sc_dispatch(xpk, pos, n_sorted)
    ys = _moe_experts(xs, n_sorted, layer * E + tile_expert, tile_valid, w_gate, w_up, w_down)
    spare = jnp.arange(t_pad - T, dtype=jnp.int32)
    back = jnp.concatenate([pos, jnp.broadcast_to(spare[None], (2, t_pad - T))], axis=1)
    yg = _sc_gather(ys, back.reshape(-1)).reshape(2, t_pad, D // 2)
    return meta, yg


def _final_body(h_ref, meta_ref, y0_ref, y1_ref, g_ref, o_ref, hn_ref):
    _moe_combine_into(hn_ref, h_ref[0], meta_ref[0], y0_ref[0], y1_ref[0])
    o_ref[0] = _rms(hn_ref[N_META:, :], g_ref[...])


def _final_norm(h3, meta3, yg, gain, b0, nb):
    _, L, D = h3.shape
    return pl.pallas_call(
        _final_body,
        grid=(nb,),
        in_specs=[pl.BlockSpec((1, L, D), lambda b: (b0 + b, 0, 0)),
                  pl.BlockSpec((1, L, meta3.shape[2]), lambda b: (b0 + b, 0, 0)),
                  pl.BlockSpec((1, L, D // 2), lambda b: (0, b0 + b, 0)),
                  pl.BlockSpec((1, L, D // 2), lambda b: (1, b0 + b, 0)),
                  pl.BlockSpec((1, D), lambda b: (0, 0))],
        out_specs=pl.BlockSpec((1, L - N_META, D), lambda b: (b, 0, 0)),
        out_shape=jax.ShapeDtypeStruct((nb, L - N_META, D), F32),
        scratch_shapes=[pltpu.VMEM((L, D), F32)],
        compiler_params=_params(("parallel",)),
        name="combine_final_norm",
    )(h3, meta3, yg, yg, gain.reshape(1, D))


def kernel(x_prompt, x_sample, meta_tokens, norm_mix, w_in, ret_decay_fwd, ret_decay_bwd, hy_short_w, hy_short_b, hy_filt_w1, hy_filt_b1, hy_filt_w2, hy_filt_b2, hy_filt_w3, hy_sin_freq, hy_skip, w_ret_o, w_hy_o, w_out, norm_ffn, router_group, router_expert, moe_w_gate, moe_w_up, moe_w_down, norm_final):
    assert x_prompt.shape[1:] == x_sample.shape[1:]
    nbp, nbs = x_prompt.shape[0], x_sample.shape[0]
    B = nbp + nbs
    D = x_prompt.shape[2]
    L = N_META + x_prompt.shape[1]
    T = B * L
    depth = w_in.shape[0]
    q_w = RET_HEADS * RET_DK
    v_w = RET_HEADS * RET_DV
    hy_w = hy_skip.shape[2]
    hy_col0 = 2 * q_w + 2 * v_w
    gate_col0 = hy_col0 + 3 * hy_w
    assert D == q_w and w_in.shape[2] == gate_col0 + 2 * D

    x = jnp.concatenate([x_prompt, x_sample], axis=0)
    meta = jnp.broadcast_to(meta_tokens[None].astype(x.dtype), (B, N_META, D))
    h = jnp.concatenate([meta, x], axis=1).reshape(T, D)

    cos_t, sin_t = _rotary_tables(L)
    dims = _HyenaDims(L)
    dft_fwd, dft_inv = _dft_tables(dims)
    router = jnp.concatenate([router_group, router_expert], axis=2).astype(F32)
    router = jnp.pad(router, ((0, 0), (0, 0), (0, LANES - router.shape[2])))
    assert moe_w_gate.shape[1] == N_EXPERTS
    w_gate_all = moe_w_gate.reshape((-1,) + moe_w_gate.shape[2:])
    w_up_all = moe_w_up.reshape((-1,) + moe_w_up.shape[2:])
    w_down_all = moe_w_down.reshape((-1,) + moe_w_down.shape[2:])

    moe = None
    for i in range(depth):
        proj, h = _inproj(h, norm_mix[i], w_in[i].astype(BF16), moe)
        proj3 = proj.reshape(B, L, -1)

        lg = jnp.stack([jax.nn.log_sigmoid(ret_decay_fwd[i].astype(F32)),
                        jax.nn.log_sigmoid(ret_decay_bwd[i].astype(F32))])
        ret = _retention(proj3, lg, cos_t, sin_t)

        hs, hd = _hyena_filters(L, hy_filt_w1[i], hy_filt_b1[i], hy_filt_w2[i], hy_filt_b2[i],
                                hy_filt_w3[i], hy_sin_freq[i])
        spectrum = _filter_spectrum(dims, dft_fwd, hs, hd)
        sw = hy_short_w[i].astype(F32)
        sb = hy_short_b[i].astype(F32).reshape(1, -1)
        z = _long_conv(dims, dft_fwd, dft_inv, proj3, hy_col0, proj3, hy_col0 + hy_w, sw, sb, 0, hy_w,
                       hy_skip[i, 0].astype(F32), spectrum, 0, hy_w)
        z = _long_conv(dims, dft_fwd, dft_inv, z, 0, proj3, hy_col0 + 2 * hy_w, sw, sb, None, 2 * hy_w,
                       hy_skip[i, 1].astype(F32), spectrum, hy_w, hy_w)

        h = _mix_out(ret.reshape(T, v_w), z.reshape(T, hy_w), proj, gate_col0, h,
                     w_ret_o[i].astype(BF16), w_hy_o[i].astype(BF16), w_out[i].astype(BF16))
        moe = _moe(h, norm_ffn[i], router[i], i, w_gate_all, w_up_all, w_down_all)

    h3 = h.reshape(B, L, D)
    meta3 = moe[0].reshape(B, L, -1)
    return (_final_norm(h3, meta3, moe[1], norm_final, 0, nbp),
            _final_norm(h3, meta3, moe[1], norm_final, nbp, nbs))
```

```python
import functools
import math

import jax
import jax.numpy as jnp
import numpy as np
from jax import lax
from jax.experimental import pallas as pl
from jax.experimental.pallas import tpu as pltpu
from jax.experimental.pallas import tpu_sc as plsc

N_META = 16
RET_HEADS = 8
RET_DK = 128
RET_DV = 256
ROPE_THETA = 10000.0
HY_ORDER = 2
HY_SHORT = 3
HY_EMB = 33
HY_BANDS = (HY_EMB - 1) // 2
HY_DECAY_TARGET = 1e-2
HY_MIN_DECAY = math.log(HY_DECAY_TARGET) / 1.5
HY_MAX_DECAY = math.log(HY_DECAY_TARGET) / 0.3
N_GROUPS = 4
EXP_PER_GROUP = 8
N_EXPERTS = N_GROUPS * EXP_PER_GROUP
RMS_EPS = 1e-6

LANES = 128
BF16_SUBLANES = 16
MXU_DIM = 256
RET_CHUNK = 256
VMEM_LIMIT = 56 * 1024 * 1024

F32 = jnp.float32
BF16 = jnp.bfloat16


def _round_up(n, m):
    return (n + m - 1) // m * m


def _pick_tile(n, target, mult):
    best = None
    for t in range(mult, min(n, target) + 1, mult):
        if n % t == 0:
            best = t
    assert best is not None, (n, target, mult)
    return best


def _params(sem):
    return pltpu.CompilerParams(dimension_semantics=sem, vmem_limit_bytes=VMEM_LIMIT)


def _rms(x, gain):
    ms = jnp.mean(x * x, axis=-1, keepdims=True)
    return x * lax.rsqrt(ms + RMS_EPS) * gain


def _inproj_body(h_ref, g_ref, w_ref, o_ref, xn_ref):
    @pl.when(pl.program_id(1) == 0)
    def _():
        xn_ref[...] = _rms(h_ref[...], g_ref[...]).astype(BF16)

    o_ref[...] = jnp.dot(xn_ref[...], w_ref[...], preferred_element_type=F32).astype(o_ref.dtype)


def _inproj_moe_body(h_ref, meta_ref, y0_ref, y1_ref, g_ref, w_ref, o_ref, hn_ref, xn_ref):
    @pl.when(pl.program_id(1) == 0)
    def _():
        _moe_combine_into(hn_ref, h_ref[...], meta_ref[...], y0_ref[0], y1_ref[0])
        xn_ref[...] = _rms(hn_ref[...], g_ref[...]).astype(BF16)

    o_ref[...] = jnp.dot(xn_ref[...], w_ref[...], preferred_element_type=F32).astype(o_ref.dtype)


def _inproj(h2, gain, w, moe=None):
    T, D = h2.shape
    nc = w.shape[1]
    tm = _pick_tile(T, 1152, BF16_SUBLANES)
    tn = _pick_tile(nc, 1024, LANES)
    common = dict(
        grid=(T // tm, nc // tn),
        scratch_shapes=[pltpu.VMEM((tm, D), BF16)],
        compiler_params=_params(("parallel", "arbitrary")),
    )
    h_spec = pl.BlockSpec((tm, D), lambda i, j: (i, 0))
    g_spec = pl.BlockSpec((1, D), lambda i, j: (0, 0))
    w_spec = pl.BlockSpec((D, tn), lambda i, j: (0, j))
    o_spec = pl.BlockSpec((tm, tn), lambda i, j: (i, j))
    o_shape = jax.ShapeDtypeStruct((T, nc), BF16)
    if moe is None:
        proj = pl.pallas_call(_inproj_body, in_specs=[h_spec, g_spec, w_spec], out_specs=o_spec,
                              out_shape=o_shape, name="norm_inproj", **common)(h2, gain.reshape(1, D), w)
        return proj, h2
    meta, yg = moe
    return pl.pallas_call(
        _inproj_moe_body,
        in_specs=[h_spec, pl.BlockSpec((tm, meta.shape[1]), lambda i, j: (i, 0)),
                  pl.BlockSpec((1, tm, D // 2), lambda i, j: (0, i, 0)),
                  pl.BlockSpec((1, tm, D // 2), lambda i, j: (1, i, 0)),
                  g_spec, w_spec],
        out_specs=[o_spec, h_spec],
        out_shape=[o_shape, jax.ShapeDtypeStruct((T, D), F32)],
        name="combine_norm_inproj", **common,
    )(h2, meta, yg, yg, gain.reshape(1, D), w)


def _dot_t(a, b):
    return lax.dot_general(a, b, (((0,), (0,)), ((), ())), preferred_element_type=F32)


def _dot_nt(a, b):
    return lax.dot_general(a, b, (((1,), (1,)), ((), ())), preferred_element_type=F32)


def _ret_body(lg_ref, q_ref, k_ref, v_ref, g_ref, cos_ref, sin_ref, o_ref, rb_ref, kr_ref, *, seq_len):
    C = RET_CHUNK
    L = seq_len
    n_chunks = pl.cdiv(L, C)
    head = pl.program_id(1)
    lgf = lg_ref[0, head]
    lgb = lg_ref[1, head]

    def chunk(ref, n):
        lo, hi = n * C, min((n + 1) * C, L)
        x = ref[0, lo:hi, :]
        if hi - lo < C:
            x = jnp.concatenate([x, jnp.zeros((C - (hi - lo), x.shape[1]), x.dtype)], axis=0)
        return x

    def rotary(ref, n):
        x = chunk(ref, n).astype(F32)
        sl = slice(n * C, (n + 1) * C)
        return x * cos_ref[sl, :] + pltpu.roll(x, RET_DK // 2, 1) * sin_ref[sl, :]

    row = lax.broadcasted_iota(jnp.int32, (C, LANES), 0).astype(F32)
    ri = lax.broadcasted_iota(jnp.int32, (C, C), 0).astype(F32)
    ci = lax.broadcasted_iota(jnp.int32, (C, C), 1).astype(F32)
    decay = jnp.exp(jnp.where(ci <= ri, (ri - ci) * lgf, (ci - ri) * lgb))
    qf_dec = jnp.exp((row + 1.0) * lgf)
    qb_dec = jnp.exp((C - row) * lgb)
    kf_dec = jnp.exp((C - 1.0 - row) * lgf)
    kb_dec = jnp.exp(row * lgb)
    cf = jnp.exp(C * lgf)
    cb = jnp.exp(C * lgb)

    state = jnp.zeros((RET_DK, RET_DV), F32)
    for n in reversed(range(n_chunks)):
        rb_ref[n] = state.astype(BF16)
        if n > 0:
            kr_ref[n] = rotary(k_ref, n)
            kb = (kr_ref[n] * kb_dec).astype(BF16)
            state = cb * state + _dot_t(kb, chunk(v_ref, n))

    state = jnp.zeros((RET_DK, RET_DV), F32)
    for n in range(n_chunks):
        q = rotary(q_ref, n) * (RET_DK ** -0.5)
        k = kr_ref[n] if n > 0 else rotary(k_ref, n)
        v = chunk(v_ref, n)
        scores = _dot_nt(q.astype(BF16), k.astype(BF16)) * decay
        o = jnp.dot(scores.astype(BF16), v, preferred_element_type=F32)
        o += jnp.dot((q * qf_dec).astype(BF16), state.astype(BF16), preferred_element_type=F32)
        o += jnp.dot((q * qb_dec).astype(BF16), rb_ref[n], preferred_element_type=F32)
        if n + 1 < n_chunks:
            state = cf * state + _dot_t((k * kf_dec).astype(BF16), v)
        o = o * lax.rsqrt(jnp.mean(o * o, axis=-1, keepdims=True) + RMS_EPS)
        lo, hi = n * C, min((n + 1) * C, L)
        g = g_ref[0, lo:hi, :].astype(F32)
        o_ref[0, lo:hi, :] = (g * jax.nn.sigmoid(g) * o[: hi - lo]).astype(o_ref.dtype)


def _retention(proj3, lg, cos_t, sin_t):
    B, L, _ = proj3.shape
    n_chunks = pl.cdiv(L, RET_CHUNK)
    lp = n_chunks * RET_CHUNK
    k_blk = RET_HEADS
    v_blk = 2 * RET_HEADS * RET_DK // RET_DV
    g_blk = v_blk + RET_HEADS
    return pl.pallas_call(
        functools.partial(_ret_body, seq_len=L),
        grid=(B, RET_HEADS),
        in_specs=[
            pl.BlockSpec(memory_space=pltpu.SMEM),
            pl.BlockSpec((1, L, RET_DK), lambda b, h: (b, 0, h)),
            pl.BlockSpec((1, L, RET_DK), lambda b, h: (b, 0, k_blk + h)),
            pl.BlockSpec((1, L, RET_DV), lambda b, h: (b, 0, v_blk + h)),
            pl.BlockSpec((1, L, RET_DV), lambda b, h: (b, 0, g_blk + h)),
            pl.BlockSpec((lp, RET_DK), lambda b, h: (0, 0)),
            pl.BlockSpec((lp, RET_DK), lambda b, h: (0, 0)),
        ],
        out_specs=pl.BlockSpec((1, L, RET_DV), lambda b, h: (b, 0, h)),
        out_shape=jax.ShapeDtypeStruct((B, L, RET_HEADS * RET_DV), BF16),
        scratch_shapes=[pltpu.VMEM((n_chunks, RET_DK, RET_DV), BF16),
                        pltpu.VMEM((n_chunks, RET_CHUNK, RET_DK), F32)],
        compiler_params=_params(("parallel", "arbitrary")),
        name="retention",
    )(lg, proj3, proj3, proj3, proj3, cos_t, sin_t)


def _rotary_tables(L):
    half = RET_DK // 2
    lp = _round_up(L, RET_CHUNK)
    inv = ROPE_THETA ** (-jnp.arange(half, dtype=F32) / half)
    ang = jnp.arange(lp, dtype=F32)[:, None] * inv[None, :]
    cos, sin = jnp.cos(ang), jnp.sin(ang)
    return jnp.concatenate([cos, cos], axis=1), jnp.concatenate([-sin, sin], axis=1)


def _row_tiles(n, tile):
    return [(s, min(tile, n - s)) for s in range(0, n, tile)]


class _HyenaDims:
    def __init__(self, L):
        assert L % 2 == 0
        self.L = L
        self.H = L // 2
        self.F = self.H + 1
        self.HP = _round_up(self.H, BF16_SUBLANES)
        self.HK = _round_up(self.H, MXU_DIM)
        self.FM = _round_up(self.F, BF16_SUBLANES)
        self.IK = _round_up(2 * self.FM, MXU_DIM)


def _dft_tables(dims):
    n = 2 * dims.L

    def trig(f, t, valid):
        ang = ((f * t) % n) * (2.0 * math.pi / n)
        return (np.where(valid, np.cos(ang), 0.0).astype(BF16), np.where(valid, np.sin(ang), 0.0).astype(BF16))

    f = np.arange(dims.FM, dtype=np.int64)[:, None]
    s = np.arange(dims.HK, dtype=np.int64)[None, :]
    ok = (f < dims.F) & (s < dims.H)
    fwd = trig(f, 2 * s, ok) + trig(f, 2 * s + 1, ok)

    s = np.arange(dims.HP, dtype=np.int64)[:, None]
    j = np.arange(dims.IK, dtype=np.int64)[None, :]
    f = j % dims.FM
    ok = (f < dims.F) & (s < dims.H)

    def inverse(t):
        cos, sin = trig(f, t, ok)
        return np.where(j < dims.FM, cos, np.where(j < 2 * dims.FM, sin, np.zeros_like(cos)))

    return fwd, (inverse(2 * s), inverse(2 * s + 1))


def _forward_dft(tables, even_ref, odd_ref, rows):
    ce_ref, se_ref, co_ref, so_ref = tables
    pe = jnp.dot(ce_ref[rows, :], even_ref[...], preferred_element_type=F32)
    po = jnp.dot(co_ref[rows, :], odd_ref[...], preferred_element_type=F32)
    ae = jnp.dot(se_ref[rows, :], even_ref[...], preferred_element_type=F32)
    ao = jnp.dot(so_ref[rows, :], odd_ref[...], preferred_element_type=F32)
    return pe + po, ae + ao, pe - po, ao - ae


def _spec_body(ce_ref, se_ref, co_ref, so_ref, hs_ref, hd_ref, k1_ref, k2_ref, k1m_ref, k2m_ref,
               se_pad, so_pad, de_pad, do_pad, *, dims):
    H, F, N = dims.H, dims.F, 2 * dims.L
    for pad, src, parity in ((se_pad, hs_ref, 0), (so_pad, hs_ref, 1), (de_pad, hd_ref, 0), (do_pad, hd_ref, 1)):
        pad[H:, :] = jnp.zeros((pad.shape[0] - H, pad.shape[1]), BF16)
        pad[:H, :] = src[parity].astype(BF16)
    rows = slice(None)
    tables = (ce_ref, se_ref, co_ref, so_ref)
    re, _, re_m, _ = _forward_dft(tables, se_pad, so_pad, rows)
    _, nim, _, nim_m = _forward_dft(tables, de_pad, do_pad, rows)
    f = lax.broadcasted_iota(jnp.int32, (dims.FM, 1), 0)
    w = jnp.where(f == 0, 1.0, 2.0) / N
    w_lo = jnp.where(f < F, w, 0.0)
    w_hi = jnp.where(f < H, w, 0.0)
    k1_ref[...] = w_lo * re
    k2_ref[...] = -w_lo * nim
    k1m_ref[...] = w_hi * re_m
    k2m_ref[...] = -w_hi * nim_m


def _filter_spectrum(dims, fwd, hs, hd):
    width = hs.shape[2]
    cw = MXU_DIM
    resident = pl.BlockSpec((dims.FM, dims.HK), lambda j: (0, 0), pipeline_mode=pl.Buffered(1))
    return pl.pallas_call(
        functools.partial(_spec_body, dims=dims),
        grid=(width // cw,),
        in_specs=[resident] * 4 + [pl.BlockSpec((2, dims.H, cw), lambda j: (0, 0, j))] * 2,
        out_specs=[pl.BlockSpec((dims.FM, cw), lambda j: (0, j))] * 4,
        out_shape=[jax.ShapeDtypeStruct((dims.FM, width), F32)] * 4,
        scratch_shapes=[pltpu.VMEM((dims.HK, cw), BF16)] * 4,
        compiler_params=_params(("arbitrary",)),
        name="hyena_filter_spectrum",
    )(*fwd, hs, hd)


def _short_conv(ref, w_ref, b_ref):
    u = ref[0].astype(F32)
    L = u.shape[0]
    t = lax.broadcasted_iota(jnp.int32, u.shape, 0)
    prev = jnp.where(t == 0, 0.0, pltpu.roll(u, 1, 0))
    nxt = jnp.where(t == L - 1, 0.0, pltpu.roll(u, L - 1, 0))
    w = w_ref[...]
    return b_ref[...] + w[0:1] * prev + w[1:2] * u + w[2:3] * nxt


def _conv_body(*refs, dims, first):
    tables = refs[:4]
    ie_ref, io_ref, u_ref, x_ref = refs[4:8]
    n_taps = 4 if first else 2
    taps = refs[8:8 + n_taps]
    (skip_ref, k1_ref, k2_ref, k1m_ref, k2m_ref, o_ref,
     ue_ref, uo_ref, xe_ref, xo_ref, ge_ref, go_ref, nat_ref) = refs[8 + n_taps:]
    L, H, HP, HK, FM, IK = dims.L, dims.H, dims.HP, dims.HK, dims.FM, dims.IK
    cw = ue_ref.shape[1]

    lane_slabs = [(k, slice(k * LANES, (k + 1) * LANES)) for k in range(cw // LANES)]

    def split_into(even_ref, odd_ref, value):
        for k, lanes in lane_slabs:
            nat_ref[k, :L, :] = value[:, lanes]
            even_ref[:HP, lanes] = nat_ref[k, pl.ds(0, HP, stride=2), :].astype(BF16)
            odd_ref[:HP, lanes] = nat_ref[k, pl.ds(1, HP, stride=2), :].astype(BF16)

    for k, _ in lane_slabs:
        nat_ref[k, L:, :] = jnp.zeros((2 * HP - L, LANES), F32)
    if HK > HP:
        ue_ref[HP:, :] = jnp.zeros((HK - HP, cw), BF16)
        uo_ref[HP:, :] = jnp.zeros((HK - HP, cw), BF16)
    if first:
        split_into(ue_ref, uo_ref, _short_conv(u_ref, taps[0], taps[1]))
    else:
        ue_ref[:HP, :] = u_ref[0, 0]
        uo_ref[:HP, :] = u_ref[0, 1]
    split_into(xe_ref, xo_ref, _short_conv(x_ref, taps[-2], taps[-1]))

    if IK > 2 * FM:
        ge_ref[2 * FM:, :] = jnp.zeros((IK - 2 * FM, cw), BF16)
        go_ref[2 * FM:, :] = jnp.zeros((IK - 2 * FM, cw), BF16)
    for lo, sz in _row_tiles(FM, MXU_DIM):
        rows = slice(lo, lo + sz)
        p, a, pm, am = _forward_dft(tables, ue_ref, uo_ref, rows)
        k1, k2, k1m, k2m = k1_ref[rows, :], k2_ref[rows, :], k1m_ref[rows, :], k2m_ref[rows, :]
        e1, e2 = p * k1 + a * k2, a * k1 - p * k2
        e1m, e2m = pm * k1m + am * k2m, am * k1m - pm * k2m
        ge_ref[lo:lo + sz, :] = (e1 + e1m).astype(BF16)
        ge_ref[FM + lo:FM + lo + sz, :] = (e2 - e2m).astype(BF16)
        go_ref[lo:lo + sz, :] = (e1 - e1m).astype(BF16)
        go_ref[FM + lo:FM + lo + sz, :] = (e2 + e2m).astype(BF16)

    skip = skip_ref[...]
    for lo, sz in _row_tiles(HP, MXU_DIM):
        rows = slice(lo, lo + sz)
        ye = jnp.dot(ie_ref[rows, :], ge_ref[...], preferred_element_type=F32)
        yo = jnp.dot(io_ref[rows, :], go_ref[...], preferred_element_type=F32)
        oe = xe_ref[rows, :].astype(F32) * (ye + skip * ue_ref[rows, :].astype(F32))
        oo = xo_ref[rows, :].astype(F32) * (yo + skip * uo_ref[rows, :].astype(F32))
        if first:
            o_ref[0, 0, rows, :] = oe.astype(o_ref.dtype)
            o_ref[0, 1, rows, :] = oo.astype(o_ref.dtype)
        else:
            valid = min(sz, H - lo)
            for k, lanes in lane_slabs:
                nat_ref[k, pl.ds(2 * lo, valid, stride=2), :] = oe[:valid, lanes]
                nat_ref[k, pl.ds(2 * lo + 1, valid, stride=2), :] = oo[:valid, lanes]
    if not first:
        for k, lanes in lane_slabs:
            o_ref[0, :, lanes] = nat_ref[k, :L, :]


def _long_conv(dims, fwd, inv, u_arr, u_col0, x_arr, x_col0, short_w, short_b, short_u_col0, short_x_col0,
               skip, spectrum, k_col0, width):
    B = u_arr.shape[0]
    L = dims.L
    cw = MXU_DIM
    ub, xb, kb = u_col0 // cw, x_col0 // cw, k_col0 // cw
    first = short_u_col0 is not None
    fwd_spec = pl.BlockSpec((dims.FM, dims.HK), lambda c, b: (0, 0), pipeline_mode=pl.Buffered(1))
    inv_spec = pl.BlockSpec((dims.HP, dims.IK), lambda c, b: (0, 0), pipeline_mode=pl.Buffered(1))

    def taps(col0):
        blk = col0 // cw
        return [pl.BlockSpec((HY_SHORT, cw), lambda c, b: (0, blk + c)),
                pl.BlockSpec((1, cw), lambda c, b: (0, blk + c))]

    if first:
        u_spec = pl.BlockSpec((1, L, cw), lambda c, b: (b, 0, ub + c))
        out_spec = pl.BlockSpec((1, 2, dims.HP, cw), lambda c, b: (b, 0, 0, c))
        out_shape = jax.ShapeDtypeStruct((B, 2, dims.HP, width), BF16)
    else:
        u_spec = pl.BlockSpec((1, 2, dims.HP, cw), lambda c, b: (b, 0, 0, ub + c))
        out_spec = pl.BlockSpec((1, L, cw), lambda c, b: (b, 0, c))
        out_shape = jax.ShapeDtypeStruct((B, L, width), F32)
    in_specs = [fwd_spec] * 4 + [inv_spec] * 2 + [u_spec, pl.BlockSpec((1, L, cw), lambda c, b: (b, 0, xb + c))]
    args = list(fwd) + list(inv) + [u_arr, x_arr]
    for col0 in ([short_u_col0] if first else []) + [short_x_col0]:
        in_specs += taps(col0)
        args += [short_w, short_b]
    in_specs += [pl.BlockSpec((1, cw), lambda c, b: (0, c))]
    in_specs += [pl.BlockSpec((dims.FM, cw), lambda c, b: (0, kb + c))] * 4
    args += [skip.reshape(1, width)] + list(spectrum)
    return pl.pallas_call(
        functools.partial(_conv_body, dims=dims, first=first),
        grid=(width // cw, B),
        in_specs=in_specs,
        out_specs=out_spec,
        out_shape=out_shape,
        scratch_shapes=([pltpu.VMEM((dims.HK, cw), BF16)] * 2 + [pltpu.VMEM((dims.HP, cw), BF16)] * 2
                        + [pltpu.VMEM((dims.IK, cw), BF16)] * 2 + [pltpu.VMEM((cw // LANES, 2 * dims.HP, LANES), F32)]),
        compiler_params=_params(("parallel", "arbitrary")),
        name="hyena_long_conv",
    )(*args)


def _hyena_filters(L, w1, b1, w2, b2, w3, freq):
    hp = lax.Precision.HIGHEST
    width = w3.shape[1] // (2 * HY_ORDER)
    t = jnp.linspace(0.0, 1.0, L, dtype=F32)
    w = (2.0 * math.pi / L) * jnp.arange(L, dtype=F32)
    bands = jnp.linspace(1e-4, HY_BANDS - 1, HY_BANDS, dtype=F32)
    fw = w[:, None] * bands[None, :]
    z = jnp.concatenate([t[:, None], jnp.cos(fw), -jnp.sin(fw)], axis=-1)
    h = jnp.sin(freq * (jnp.dot(z, w1, precision=hp) + b1))
    h = jnp.sin(freq * (jnp.dot(h, w2, precision=hp) + b2))
    h = jnp.dot(h, w3, precision=hp).reshape(L, HY_ORDER, 2, width)
    deltas = jnp.abs(jnp.linspace(HY_MIN_DECAY, HY_MAX_DECAY, width, dtype=F32))
    h = h * jnp.exp(-t[:, None] * deltas[None, :])[:, None, None, :]
    hf = h[:, :, 0]
    hb = h[:, :, 1].at[0].set(0.0)
    scale = lax.rsqrt(jnp.sum(hf * hf, axis=0) + jnp.sum(hb * hb, axis=0) + 1e-6)
    hf, hb = hf * scale, hb * scale
    def parity_split(x):
        x = x.reshape(L // 2, 2, HY_ORDER * width)
        return jnp.transpose(x, (1, 0, 2))

    return parity_split(hf + hb), parity_split(hf - hb)


def _mix_body(ret_ref, hy_ref, gr_ref, gh_ref, h_ref, wr_ref, wh_ref, wo_ref, o_ref):
    ret = jnp.dot(ret_ref[...], wr_ref[...], preferred_element_type=F32)
    hyo = jnp.dot(hy_ref[...].astype(BF16), wh_ref[...], preferred_element_type=F32)
    merged = (jax.nn.sigmoid(gr_ref[...].astype(F32)) * ret
              + jax.nn.sigmoid(gh_ref[...].astype(F32)) * hyo)
    o_ref[...] = h_ref[...] + jnp.dot(merged.astype(BF16), wo_ref[...], preferred_element_type=F32)


def _mix_out(ret2, hy2, proj2, gate_col0, h2, w_ret_o, w_hy_o, w_out):
    T, D = h2.shape
    tm = _pick_tile(T, 384, BF16_SUBLANES)
    gb = gate_col0 // D
    row = lambda w: pl.BlockSpec((tm, w), lambda i: (i, 0))
    full = lambda a: pl.BlockSpec(a.shape, lambda i: (0, 0))
    return pl.pallas_call(
        _mix_body,
        grid=(T // tm,),
        in_specs=[
            row(ret2.shape[1]), row(hy2.shape[1]),
            pl.BlockSpec((tm, D), lambda i: (i, gb)),
            pl.BlockSpec((tm, D), lambda i: (i, gb + 1)),
            row(D), full(w_ret_o), full(w_hy_o), full(w_out),
        ],
        out_specs=row(D),
        out_shape=jax.ShapeDtypeStruct((T, D), F32),
        compiler_params=_params(("parallel",)),
        name="merge_outproj",
    )(ret2, hy2, proj2, proj2, h2, w_ret_o, w_hy_o, w_out)


EXPERT_TILE = 512
META_COLS = 8
SC_WINDOW = 64
SC_WORKERS = 32
SC_CHUNK = 2 * SC_WINDOW * SC_WORKERS
SC_SCATTER_WINDOW = 128


def _pack_bf16_pairs(x):
    n = x.shape[1] // 2
    xb = x.astype(BF16).astype(F32)
    hi = lax.bitcast_convert_type(xb[:, :n], jnp.uint32)
    lo = lax.bitcast_convert_type(xb[:, n:], jnp.uint32)
    return lax.bitcast_convert_type(hi | (lo >> 16), jnp.int32)


def _unpack_bf16_pairs(w):
    u = lax.bitcast_convert_type(w, jnp.uint32)
    hi = lax.bitcast_convert_type(u & jnp.uint32(0xFFFF0000), F32)
    lo = lax.bitcast_convert_type(u << 16, F32)
    return hi, lo


def _route(lt):
    assert EXP_PER_GROUP == 8 and N_GROUPS <= 8
    tm = lt.shape[1]
    row = lax.broadcasted_iota(jnp.int32, (8, tm), 0)
    neg = -jnp.inf
    big = jnp.int32(1 << 20)
    gl = jnp.where(row < N_GROUPS, lt[N_EXPERTS:N_EXPERTS + 8, :], neg)
    gmax = jnp.max(gl, axis=0, keepdims=True)
    p_top = 1.0 / jnp.sum(jnp.exp(gl - gmax), axis=0, keepdims=True)
    g_idx = jnp.min(jnp.where(gl == gmax, row, big), axis=0, keepdims=True)
    el = lt[0:8, :]
    for g in range(1, N_GROUPS):
        el = jnp.where(g_idx == g, lt[8 * g:8 * g + 8, :], el)
    m1 = jnp.max(el, axis=0, keepdims=True)
    i1 = jnp.min(jnp.where(el == m1, row, big), axis=0, keepdims=True)
    el2 = jnp.where(row == i1, neg, el)
    m2 = jnp.max(el2, axis=0, keepdims=True)
    i2 = jnp.min(jnp.where(el2 == m2, row, big), axis=0, keepdims=True)
    r = jnp.exp(m2 - m1)
    base = g_idx * EXP_PER_GROUP
    return base + i1, base + i2, p_top / (1.0 + r), p_top * r / (1.0 + r)


def _route_body(h_ref, g_ref, whi_ref, wlo_ref, xpk_ref, meta_ref, meta_t_ref, cnt_ref, carry_ref):
    i = pl.program_id(0)

    @pl.when(i == 0)
    def _():
        carry_ref[...] = jnp.zeros_like(carry_ref)

    xn = _rms(h_ref[...], g_ref[...])
    xpk_ref[...] = _pack_bf16_pairs(xn)
    x_hi = xn.astype(BF16)
    x_lo = (xn - x_hi.astype(F32)).astype(BF16)
    logits = (jnp.dot(x_hi, whi_ref[...], preferred_element_type=F32)
              + jnp.dot(x_lo, whi_ref[...], preferred_element_type=F32)
              + jnp.dot(x_hi, wlo_ref[...], preferred_element_type=F32))
    e0, e1, w0, w1 = _route(logits.T)
    tm = logits.shape[0]
    row = lax.broadcasted_iota(jnp.int32, (LANES, tm), 0)
    onehot_t = jnp.where((row == e0) | (row == e1), 1.0, 0.0).astype(BF16)
    ri = lax.broadcasted_iota(jnp.int32, (tm, tm), 0)
    ci = lax.broadcasted_iota(jnp.int32, (tm, tm), 1)
    earlier = jnp.where(ci < ri, 1.0, 0.0).astype(BF16)
    prefix = carry_ref[...] + _dot_nt(earlier, onehot_t)
    prefix_t = prefix.T
    r0 = jnp.sum(jnp.where(row == e0, prefix_t, 0.0), axis=0, keepdims=True)
    r1 = jnp.sum(jnp.where(row == e1, prefix_t, 0.0), axis=0, keepdims=True)
    row8 = lax.broadcasted_iota(jnp.int32, (META_COLS, tm), 0)
    meta_t = jnp.zeros((META_COLS, tm), F32)
    for c, val in enumerate((e0.astype(F32), e1.astype(F32), r0, r1, w0, w1)):
        meta_t = jnp.where(row8 == c, val, meta_t)
    meta_t_ref[...] = meta_t
    padded = jnp.concatenate([meta_t, jnp.zeros((LANES - META_COLS, tm), F32)], axis=0)
    meta_ref[...] = padded.T[:, :META_COLS]
    counts = _dot_nt(jnp.ones((8, tm), BF16), onehot_t)
    carry_ref[...] += counts[0:1, :]
    cnt_ref[...] = carry_ref[...]


def _moe_route(h2, gain, w_router):
    T, D = h2.shape
    tm = _pick_tile(T, 1152, LANES)
    return pl.pallas_call(
        _route_body,
        grid=(T // tm,),
        in_specs=[
            pl.BlockSpec((tm, D), lambda i: (i, 0)),
            pl.BlockSpec((1, D), lambda i: (0, 0)),
            pl.BlockSpec((D, LANES), lambda i: (0, 0)),
            pl.BlockSpec((D, LANES), lambda i: (0, 0)),
        ],
        out_specs=[
            pl.BlockSpec((tm, D // 2), lambda i: (i, 0)),
            pl.BlockSpec((tm, META_COLS), lambda i: (i, 0)),
            pl.BlockSpec((META_COLS, tm), lambda i: (0, i)),
            pl.BlockSpec((1, LANES), lambda i: (0, 0)),
        ],
        out_shape=[
            jax.ShapeDtypeStruct((T, D // 2), jnp.int32),
            jax.ShapeDtypeStruct((T, META_COLS), F32),
            jax.ShapeDtypeStruct((META_COLS, T), F32),
            jax.ShapeDtypeStruct((1, LANES), F32),
        ],
        scratch_shapes=[pltpu.VMEM((1, LANES), F32)],
        compiler_params=_params(("arbitrary",)),
        name="moe_route",
    )(h2, gain.reshape(1, D), *w_router)


def _sc_gather(table, idx):
    n = idx.shape[0]
    width = table.shape[1]
    win = SC_WINDOW
    assert n % SC_CHUNK == 0
    per_worker = n // SC_WORKERS
    mesh = plsc.VectorSubcoreMesh(core_axis_name="c", subcore_axis_name="s")

    @functools.partial(
        pl.kernel, out_type=jax.ShapeDtypeStruct((n, width), table.dtype), mesh=mesh,
        scratch_types=[pltpu.VMEM((per_worker,), jnp.int32), pltpu.VMEM((2, win, width), table.dtype),
                       pltpu.SemaphoreType.DMA((2,)), pltpu.SemaphoreType.DMA((2,))],
        name="sc_row_gather")
    def gather(table_hbm, idx_hbm, out_hbm, idx_v, rows_v, gsem, osem):
        worker = lax.axis_index("s") * mesh.num_cores + lax.axis_index("c")
        base = worker * per_worker
        pltpu.sync_copy(idx_hbm.at[pl.ds(base, per_worker)], idx_v)

        @pl.loop(0, per_worker, step=2 * win)
        def _(off):
            fetch = [pltpu.async_copy(table_hbm.at[idx_v.at[pl.ds(off + s * win, win)]], rows_v.at[s], gsem.at[s])
                     for s in range(2)]
            store = []
            for s in range(2):
                fetch[s].wait()
                store.append(pltpu.async_copy(rows_v.at[s], out_hbm.at[pl.ds(base + off + s * win, win)],
                                              osem.at[s]))
            for s in range(2):
                store[s].wait()

    return gather(table, idx)


def _sc_dispatch(table, dest, n_out):
    n_rows, width = table.shape
    win = SC_SCATTER_WINDOW
    assert n_rows % win == 0
    n_win = n_rows // win
    per_worker = pl.cdiv(n_win, SC_WORKERS)
    idx = jnp.pad(dest, ((0, 0), (0, per_worker * SC_WORKERS * win - n_rows)))
    idx = idx.reshape(2, per_worker, SC_WORKERS, win).transpose(0, 2, 1, 3)
    mesh = plsc.VectorSubcoreMesh(core_axis_name="c", subcore_axis_name="s")

    @functools.partial(
        pl.kernel, out_type=jax.ShapeDtypeStruct((n_out, width), table.dtype), mesh=mesh,
        scratch_types=[pltpu.VMEM((per_worker, win), jnp.int32), pltpu.VMEM((per_worker, win), jnp.int32),
                       pltpu.VMEM((win, width), table.dtype),
                       pltpu.SemaphoreType.DMA, pltpu.SemaphoreType.DMA],
        name="sc_row_dispatch")
    def dispatch(table_hbm, idx_hbm, out_hbm, idx0_v, idx1_v, rows_v, sem0, sem1):
        worker = lax.axis_index("s") * mesh.num_cores + lax.axis_index("c")
        pltpu.sync_copy(idx_hbm.at[0, worker], idx0_v)
        pltpu.sync_copy(idx_hbm.at[1, worker], idx1_v)

        @pl.loop(0, per_worker)
        def _(j):
            window = j * SC_WORKERS + worker

            @pl.when(window < n_win)
            def _():
                pltpu.sync_copy(table_hbm.at[pl.ds(window * win, win)], rows_v)
                first = pltpu.async_copy(rows_v, out_hbm.at[idx0_v.at[j]], sem0)
                second = pltpu.async_copy(rows_v, out_hbm.at[idx1_v.at[j]], sem1)
                first.wait()
                second.wait()

    return dispatch(table, idx)


def _expert_body(te_ref, tv_ref, x_ref, wg_ref, wu_ref, wd_ref, y_ref, wg_s, wu_s, wd_s):
    t = pl.program_id(0)
    half = x_ref.shape[1]

    @pl.when((t == 0) | (te_ref[t] != te_ref[jnp.maximum(t - 1, 0)]))
    def _():
        wg_s[...] = wg_ref[0].astype(BF16)
        wu_s[...] = wu_ref[0].astype(BF16)
        wd_s[...] = wd_ref[0].astype(BF16)

    @pl.when(tv_ref[t] > 0)
    def _():
        row = lax.broadcasted_iota(jnp.int32, x_ref.shape, 0)
        hi, lo = _unpack_bf16_pairs(jnp.where(row < tv_ref[t], x_ref[...], 0))
        hi, lo = hi.astype(BF16), lo.astype(BF16)
        hg = jnp.dot(hi, wg_s[:half, :], preferred_element_type=F32)
        hg += jnp.dot(lo, wg_s[half:, :], preferred_element_type=F32)
        hu = jnp.dot(hi, wu_s[:half, :], preferred_element_type=F32)
        hu += jnp.dot(lo, wu_s[half:, :], preferred_element_type=F32)
        act = (hg * jax.nn.sigmoid(hg) * hu).astype(BF16)
        y_ref[...] = _pack_bf16_pairs(jnp.dot(act, wd_s[...], preferred_element_type=F32))

    @pl.when(tv_ref[t] == 0)
    def _():
        y_ref[...] = jnp.zeros_like(y_ref)


def _moe_experts(xs, n_sorted, tile_expert, tile_valid, w_gate, w_up, w_down):
    NP, half = n_sorted, xs.shape[1]
    _, D, FF = w_gate.shape
    tr = EXPERT_TILE
    grid_spec = pltpu.PrefetchScalarGridSpec(
        num_scalar_prefetch=2,
        grid=(NP // tr,),
        in_specs=[
            pl.BlockSpec((tr, half), lambda t, te, tv: (t, 0)),
            pl.BlockSpec((1, D, FF), lambda t, te, tv: (te[t], 0, 0)),
            pl.BlockSpec((1, D, FF), lambda t, te, tv: (te[t], 0, 0)),
            pl.BlockSpec((1, FF, D), lambda t, te, tv: (te[t], 0, 0)),
        ],
        out_specs=pl.BlockSpec((tr, half), lambda t, te, tv: (t, 0)),
        scratch_shapes=[pltpu.VMEM((D, FF), BF16), pltpu.VMEM((D, FF), BF16), pltpu.VMEM((FF, D), BF16)],
    )
    return pl.pallas_call(
        _expert_body,
        grid_spec=grid_spec,
        out_shape=jax.ShapeDtypeStruct((NP, half), jnp.int32),
        compiler_params=_params(("arbitrary",)),
        name="moe_experts",
    )(tile_expert, tile_valid, xs, w_gate, w_up, w_down)


def _moe_combine_into(o_ref, h, meta, y0, y1):
    half = y0.shape[1]
    w0 = meta[:, 4:5]
    w1 = meta[:, 5:6]
    hi0, lo0 = _unpack_bf16_pairs(y0)
    hi1, lo1 = _unpack_bf16_pairs(y1)
    o_ref[:, :half] = h[:, :half] + w0 * hi0 + w1 * hi1
    o_ref[:, half:] = h[:, half:] + w0 * lo0 + w1 * lo1


def _moe(h2, gain, w_router, layer, w_gate, w_up, w_down):
    T, D = h2.shape
    E = N_EXPERTS
    tr = EXPERT_TILE
    n_sorted = _round_up(2 * T + E * (tr - 1), tr)
    t_pad = _round_up(T, SC_CHUNK // 2)

    xpk, meta, meta_t, counts = _moe_route(h2, gain, w_router)

    cnt = counts[0, :E].astype(jnp.int32)
    padded = (cnt + tr - 1) // tr * tr
    ends = jnp.cumsum(padded)
    starts = ends - padded
    eid = meta_t[0:2].astype(jnp.int32)
    pos = meta_t[2:4].astype(jnp.int32)
    for e in range(E):
        pos = pos + jnp.where(eid == e, starts[e], 0)
    tile_start = jnp.arange(n_sorted // tr, dtype=jnp.int32) * tr
    tile_expert = jnp.minimum(jnp.sum(tile_start[:, None] >= ends[None, :], axis=1), E - 1).astype(jnp.int32)
    tile_valid = jnp.clip(cnt[tile_expert] - (tile_start - starts[tile_expert]), 0, tr).astype(jnp.int32)

    xs = _sc_dispatch(xpk, pos, n_sorted)
    ys = _moe_experts(xs, n_sorted, layer * E + tile_expert, tile_valid, w_gate, w_up, w_down)
    spare = jnp.arange(t_pad - T, dtype=jnp.int32)
    back = jnp.concatenate([pos, jnp.broadcast_to(spare[None], (2, t_pad - T))], axis=1)
    yg = _sc_gather(ys, back.reshape(-1)).reshape(2, t_pad, D // 2)
    return meta, yg


def _final_body(h_ref, meta_ref, y0_ref, y1_ref, g_ref, o_ref, hn_ref):
    _moe_combine_into(hn_ref, h_ref[0], meta_ref[0], y0_ref[0], y1_ref[0])
    o_ref[0] = _rms(hn_ref[N_META:, :], g_ref[...])


def _final_norm(h3, meta3, yg, gain, b0, nb):
    _, L, D = h3.shape
    return pl.pallas_call(
        _final_body,
        grid=(nb,),
        in_specs=[pl.BlockSpec((1, L, D), lambda b: (b0 + b, 0, 0)),
                  pl.BlockSpec((1, L, meta3.shape[2]), lambda b: (b0 + b, 0, 0)),
                  pl.BlockSpec((1, L, D // 2), lambda b: (0, b0 + b, 0)),
                  pl.BlockSpec((1, L, D // 2), lambda b: (1, b0 + b, 0)),
                  pl.BlockSpec((1, D), lambda b: (0, 0))],
        out_specs=pl.BlockSpec((1, L - N_META, D), lambda b: (b, 0, 0)),
        out_shape=jax.ShapeDtypeStruct((nb, L - N_META, D), F32),
        scratch_shapes=[pltpu.VMEM((L, D), F32)],
        compiler_params=_params(("parallel",)),
        name="combine_final_norm",
    )(h3, meta3, yg, yg, gain.reshape(1, D))


def kernel(x_prompt, x_sample, meta_tokens, norm_mix, w_in, ret_decay_fwd, ret_decay_bwd, hy_short_w, hy_short_b, hy_filt_w1, hy_filt_b1, hy_filt_w2, hy_filt_b2, hy_filt_w3, hy_sin_freq, hy_skip, w_ret_o, w_hy_o, w_out, norm_ffn, router_group, router_expert, moe_w_gate, moe_w_up, moe_w_down, norm_final):
    assert x_prompt.shape[1:] == x_sample.shape[1:]
    nbp, nbs = x_prompt.shape[0], x_sample.shape[0]
    B = nbp + nbs
    D = x_prompt.shape[2]
    L = N_META + x_prompt.shape[1]
    T = B * L
    depth = w_in.shape[0]
    q_w = RET_HEADS * RET_DK
    v_w = RET_HEADS * RET_DV
    hy_w = hy_skip.shape[2]
    hy_col0 = 2 * q_w + 2 * v_w
    gate_col0 = hy_col0 + 3 * hy_w
    assert D == q_w and w_in.shape[2] == gate_col0 + 2 * D

    x = jnp.concatenate([x_prompt, x_sample], axis=0)
    meta = jnp.broadcast_to(meta_tokens[None].astype(x.dtype), (B, N_META, D))
    h = jnp.concatenate([meta, x], axis=1).reshape(T, D)

    cos_t, sin_t = _rotary_tables(L)
    dims = _HyenaDims(L)
    dft_fwd, dft_inv = _dft_tables(dims)
    router = jnp.concatenate([router_expert, router_group], axis=2).astype(F32)
    router = jnp.pad(router, ((0, 0), (0, 0), (0, LANES - router.shape[2])))
    router_hi = router.astype(BF16)
    router_lo = (router - router_hi.astype(F32)).astype(BF16)
    assert moe_w_gate.shape[1] == N_EXPERTS
    w_gate_all = moe_w_gate.reshape((-1,) + moe_w_gate.shape[2:])
    w_up_all = moe_w_up.reshape((-1,) + moe_w_up.shape[2:])
    w_down_all = moe_w_down.reshape((-1,) + moe_w_down.shape[2:])

    moe = None
    for i in range(depth):
        proj, h = _inproj(h, norm_mix[i], w_in[i].astype(BF16), moe)
        proj3 = proj.reshape(B, L, -1)

        lg = jnp.stack([jax.nn.log_sigmoid(ret_decay_fwd[i].astype(F32)),
                        jax.nn.log_sigmoid(ret_decay_bwd[i].astype(F32))])
        ret = _retention(proj3, lg, cos_t, sin_t)

        hs, hd = _hyena_filters(L, hy_filt_w1[i], hy_filt_b1[i], hy_filt_w2[i], hy_filt_b2[i],
                                hy_filt_w3[i], hy_sin_freq[i])
        spectrum = _filter_spectrum(dims, dft_fwd, hs, hd)
        sw = hy_short_w[i].astype(F32)
        sb = hy_short_b[i].astype(F32).reshape(1, -1)
        z = _long_conv(dims, dft_fwd, dft_inv, proj3, hy_col0, proj3, hy_col0 + hy_w, sw, sb, 0, hy_w,
                       hy_skip[i, 0].astype(F32), spectrum, 0, hy_w)
        z = _long_conv(dims, dft_fwd, dft_inv, z, 0, proj3, hy_col0 + 2 * hy_w, sw, sb, None, 2 * hy_w,
                       hy_skip[i, 1].astype(F32), spectrum, hy_w, hy_w)

        h = _mix_out(ret.reshape(T, v_w), z.reshape(T, hy_w), proj, gate_col0, h,
                     w_ret_o[i].astype(BF16), w_hy_o[i].astype(BF16), w_out[i].astype(BF16))
        moe = _moe(h, norm_ffn[i], (router_hi[i], router_lo[i]), i, w_gate_all, w_up_all, w_down_all)

    h3 = h.reshape(B, L, D)
    meta3 = moe[0].reshape(B, L, -1)
    return (_final_norm(h3, meta3, moe[1], norm_final, 0, nbp),
            _final_norm(h3, meta3, moe[1], norm_final, nbp, nbs))
```

```python
import functools
import math

import jax
import jax.numpy as jnp
import numpy as np
from jax import lax
from jax.experimental import pallas as pl
from jax.experimental.pallas import tpu as pltpu
from jax.experimental.pallas import tpu_sc as plsc

N_META = 16
RET_HEADS = 8
RET_DK = 128
RET_DV = 256
ROPE_THETA = 10000.0
HY_ORDER = 2
HY_SHORT = 3
HY_EMB = 33
HY_BANDS = (HY_EMB - 1) // 2
HY_DECAY_TARGET = 1e-2
HY_MIN_DECAY = math.log(HY_DECAY_TARGET) / 1.5
HY_MAX_DECAY = math.log(HY_DECAY_TARGET) / 0.3
N_GROUPS = 4
EXP_PER_GROUP = 8
N_EXPERTS = N_GROUPS * EXP_PER_GROUP
RMS_EPS = 1e-6

LANES = 128
BF16_SUBLANES = 16
MXU_DIM = 256
RET_CHUNK = 256
VMEM_LIMIT = 56 * 1024 * 1024

F32 = jnp.float32
BF16 = jnp.bfloat16


def _round_up(n, m):
    return (n + m - 1) // m * m


def _pick_tile(n, target, mult):
    best = None
    for t in range(mult, min(n, target) + 1, mult):
        if n % t == 0:
            best = t
    assert best is not None, (n, target, mult)
    return best


def _params(sem):
    return pltpu.CompilerParams(dimension_semantics=sem, vmem_limit_bytes=VMEM_LIMIT)


def _rms(x, gain):
    ms = jnp.mean(x * x, axis=-1, keepdims=True)
    return x * lax.rsqrt(ms + RMS_EPS) * gain


def _inproj_body(h_ref, g_ref, w_ref, o_ref, xn_ref):
    @pl.when(pl.program_id(1) == 0)
    def _():
        xn_ref[...] = _rms(h_ref[...], g_ref[...]).astype(BF16)

    o_ref[...] = jnp.dot(xn_ref[...], w_ref[...], preferred_element_type=F32).astype(o_ref.dtype)


def _inproj_moe_body(h_ref, meta_ref, y0_ref, y1_ref, g_ref, w_ref, o_ref, hn_ref, xn_ref):
    @pl.when(pl.program_id(1) == 0)
    def _():
        _moe_combine_into(hn_ref, h_ref[...], meta_ref[...], y0_ref[0], y1_ref[0])
        xn_ref[...] = _rms(hn_ref[...], g_ref[...]).astype(BF16)

    o_ref[...] = jnp.dot(xn_ref[...], w_ref[...], preferred_element_type=F32).astype(o_ref.dtype)


def _inproj(h2, gain, w, moe=None):
    T, D = h2.shape
    nc = w.shape[1]
    tm = _pick_tile(T, 1152, BF16_SUBLANES)
    tn = _pick_tile(nc, 1024, LANES)
    common = dict(
        grid=(T // tm, nc // tn),
        scratch_shapes=[pltpu.VMEM((tm, D), BF16)],
        compiler_params=_params(("parallel", "arbitrary")),
    )
    h_spec = pl.BlockSpec((tm, D), lambda i, j: (i, 0))
    g_spec = pl.BlockSpec((1, D), lambda i, j: (0, 0))
    w_spec = pl.BlockSpec((D, tn), lambda i, j: (0, j))
    o_spec = pl.BlockSpec((tm, tn), lambda i, j: (i, j))
    o_shape = jax.ShapeDtypeStruct((T, nc), BF16)
    if moe is None:
        proj = pl.pallas_call(_inproj_body, in_specs=[h_spec, g_spec, w_spec], out_specs=o_spec,
                              out_shape=o_shape, name="norm_inproj", **common)(h2, gain.reshape(1, D), w)
        return proj, h2
    meta, yg = moe
    return pl.pallas_call(
        _inproj_moe_body,
        in_specs=[h_spec, pl.BlockSpec((tm, meta.shape[1]), lambda i, j: (i, 0)),
                  pl.BlockSpec((1, tm, D // 2), lambda i, j: (0, i, 0)),
                  pl.BlockSpec((1, tm, D // 2), lambda i, j: (1, i, 0)),
                  g_spec, w_spec],
        out_specs=[o_spec, h_spec],
        out_shape=[o_shape, jax.ShapeDtypeStruct((T, D), F32)],
        name="combine_norm_inproj", **common,
    )(h2, meta, yg, yg, gain.reshape(1, D), w)


def _dot_t(a, b):
    return lax.dot_general(a, b, (((0,), (0,)), ((), ())), preferred_element_type=F32)


def _dot_nt(a, b):
    return lax.dot_general(a, b, (((1,), (1,)), ((), ())), preferred_element_type=F32)


def _ret_body(lg_ref, q_ref, k_ref, v_ref, g_ref, cos_ref, sin_ref, o_ref, rb_ref, kr_ref, *, seq_len):
    C = RET_CHUNK
    L = seq_len
    n_chunks = pl.cdiv(L, C)
    head = pl.program_id(1)
    lgf = lg_ref[0, head]
    lgb = lg_ref[1, head]

    def chunk(ref, n):
        lo, hi = n * C, min((n + 1) * C, L)
        x = ref[0, lo:hi, :]
        if hi - lo < C:
            x = jnp.concatenate([x, jnp.zeros((C - (hi - lo), x.shape[1]), x.dtype)], axis=0)
        return x

    def rotary(ref, n):
        x = chunk(ref, n).astype(F32)
        sl = slice(n * C, (n + 1) * C)
        return x * cos_ref[sl, :] + pltpu.roll(x, RET_DK // 2, 1) * sin_ref[sl, :]

    row = lax.broadcasted_iota(jnp.int32, (C, LANES), 0).astype(F32)
    ri = lax.broadcasted_iota(jnp.int32, (C, C), 0).astype(F32)
    ci = lax.broadcasted_iota(jnp.int32, (C, C), 1).astype(F32)
    decay = jnp.exp(jnp.where(ci <= ri, (ri - ci) * lgf, (ci - ri) * lgb))
    qf_dec = jnp.exp((row + 1.0) * lgf)
    qb_dec = jnp.exp((C - row) * lgb)
    kf_dec = jnp.exp((C - 1.0 - row) * lgf)
    kb_dec = jnp.exp(row * lgb)
    cf = jnp.exp(C * lgf)
    cb = jnp.exp(C * lgb)

    state = jnp.zeros((RET_DK, RET_DV), F32)
    for n in reversed(range(n_chunks)):
        rb_ref[n] = state.astype(BF16)
        if n > 0:
            kr_ref[n] = rotary(k_ref, n)
            kb = (kr_ref[n] * kb_dec).astype(BF16)
            state = cb * state + _dot_t(kb, chunk(v_ref, n))

    state = jnp.zeros((RET_DK, RET_DV), F32)
    for n in range(n_chunks):
        q = rotary(q_ref, n) * (RET_DK ** -0.5)
        k = kr_ref[n] if n > 0 else rotary(k_ref, n)
        v = chunk(v_ref, n)
        scores = _dot_nt(q.astype(BF16), k.astype(BF16)) * decay
        o = jnp.dot(scores.astype(BF16), v, preferred_element_type=F32)
        o += jnp.dot((q * qf_dec).astype(BF16), state.astype(BF16), preferred_element_type=F32)
        o += jnp.dot((q * qb_dec).astype(BF16), rb_ref[n], preferred_element_type=F32)
        if n + 1 < n_chunks:
            state = cf * state + _dot_t((k * kf_dec).astype(BF16), v)
        o = o * lax.rsqrt(jnp.mean(o * o, axis=-1, keepdims=True) + RMS_EPS)
        lo, hi = n * C, min((n + 1) * C, L)
        g = g_ref[0, lo:hi, :].astype(F32)
        o_ref[0, lo:hi, :] = (g * jax.nn.sigmoid(g) * o[: hi - lo]).astype(o_ref.dtype)


def _retention(proj3, lg, cos_t, sin_t):
    B, L, _ = proj3.shape
    n_chunks = pl.cdiv(L, RET_CHUNK)
    lp = n_chunks * RET_CHUNK
    k_blk = RET_HEADS
    v_blk = 2 * RET_HEADS * RET_DK // RET_DV
    g_blk = v_blk + RET_HEADS
    return pl.pallas_call(
        functools.partial(_ret_body, seq_len=L),
        grid=(B, RET_HEADS),
        in_specs=[
            pl.BlockSpec(memory_space=pltpu.SMEM),
            pl.BlockSpec((1, L, RET_DK), lambda b, h: (b, 0, h)),
            pl.BlockSpec((1, L, RET_DK), lambda b, h: (b, 0, k_blk + h)),
            pl.BlockSpec((1, L, RET_DV), lambda b, h: (b, 0, v_blk + h)),
            pl.BlockSpec((1, L, RET_DV), lambda b, h: (b, 0, g_blk + h)),
            pl.BlockSpec((lp, RET_DK), lambda b, h: (0, 0)),
            pl.BlockSpec((lp, RET_DK), lambda b, h: (0, 0)),
        ],
        out_specs=pl.BlockSpec((1, L, RET_DV), lambda b, h: (b, 0, h)),
        out_shape=jax.ShapeDtypeStruct((B, L, RET_HEADS * RET_DV), BF16),
        scratch_shapes=[pltpu.VMEM((n_chunks, RET_DK, RET_DV), BF16),
                        pltpu.VMEM((n_chunks, RET_CHUNK, RET_DK), F32)],
        compiler_params=_params(("parallel", "arbitrary")),
        name="retention",
    )(lg, proj3, proj3, proj3, proj3, cos_t, sin_t)


def _rotary_tables(L):
    half = RET_DK // 2
    lp = _round_up(L, RET_CHUNK)
    inv = ROPE_THETA ** (-jnp.arange(half, dtype=F32) / half)
    ang = jnp.arange(lp, dtype=F32)[:, None] * inv[None, :]
    cos, sin = jnp.cos(ang), jnp.sin(ang)
    return jnp.concatenate([cos, cos], axis=1), jnp.concatenate([-sin, sin], axis=1)


def _row_tiles(n, tile):
    return [(s, min(tile, n - s)) for s in range(0, n, tile)]


class _HyenaDims:
    def __init__(self, L):
        assert L % 2 == 0
        self.L = L
        self.H = L // 2
        self.F = self.H + 1
        self.HP = _round_up(self.H, BF16_SUBLANES)
        self.HK = _round_up(self.H, MXU_DIM)
        self.FM = _round_up(self.F, BF16_SUBLANES)
        self.IK = _round_up(2 * self.FM, MXU_DIM)


def _dft_tables(dims):
    n = 2 * dims.L

    def trig(f, t, valid):
        ang = ((f * t) % n) * (2.0 * math.pi / n)
        return (np.where(valid, np.cos(ang), 0.0).astype(BF16), np.where(valid, np.sin(ang), 0.0).astype(BF16))

    f = np.arange(dims.FM, dtype=np.int64)[:, None]
    s = np.arange(dims.HK, dtype=np.int64)[None, :]
    ok = (f < dims.F) & (s < dims.H)
    fwd = trig(f, 2 * s, ok) + trig(f, 2 * s + 1, ok)

    s = np.arange(dims.HP, dtype=np.int64)[:, None]
    j = np.arange(dims.IK, dtype=np.int64)[None, :]
    f = j % dims.FM
    ok = (f < dims.F) & (s < dims.H)

    def inverse(t):
        cos, sin = trig(f, t, ok)
        return np.where(j < dims.FM, cos, np.where(j < 2 * dims.FM, sin, np.zeros_like(cos)))

    return fwd, (inverse(2 * s), inverse(2 * s + 1))


def _forward_dft(tables, even_ref, odd_ref, rows):
    ce_ref, se_ref, co_ref, so_ref = tables
    pe = jnp.dot(ce_ref[rows, :], even_ref[...], preferred_element_type=F32)
    po = jnp.dot(co_ref[rows, :], odd_ref[...], preferred_element_type=F32)
    ae = jnp.dot(se_ref[rows, :], even_ref[...], preferred_element_type=F32)
    ao = jnp.dot(so_ref[rows, :], odd_ref[...], preferred_element_type=F32)
    return pe + po, ae + ao, pe - po, ao - ae


def _spec_body(ce_ref, se_ref, co_ref, so_ref, hs_ref, hd_ref, k1_ref, k2_ref, k1m_ref, k2m_ref,
               se_pad, so_pad, de_pad, do_pad, *, dims):
    H, F, N = dims.H, dims.F, 2 * dims.L
    for pad, src, parity in ((se_pad, hs_ref, 0), (so_pad, hs_ref, 1), (de_pad, hd_ref, 0), (do_pad, hd_ref, 1)):
        pad[H:, :] = jnp.zeros((pad.shape[0] - H, pad.shape[1]), BF16)
        pad[:H, :] = src[parity].astype(BF16)
    rows = slice(None)
    tables = (ce_ref, se_ref, co_ref, so_ref)
    re, _, re_m, _ = _forward_dft(tables, se_pad, so_pad, rows)
    _, nim, _, nim_m = _forward_dft(tables, de_pad, do_pad, rows)
    f = lax.broadcasted_iota(jnp.int32, (dims.FM, 1), 0)
    w = jnp.where(f == 0, 1.0, 2.0) / N
    w_lo = jnp.where(f < F, w, 0.0)
    w_hi = jnp.where(f < H, w, 0.0)
    k1_ref[...] = w_lo * re
    k2_ref[...] = -w_lo * nim
    k1m_ref[...] = w_hi * re_m
    k2m_ref[...] = -w_hi * nim_m


def _filter_spectrum(dims, fwd, hs, hd):
    width = hs.shape[2]
    cw = MXU_DIM
    resident = pl.BlockSpec((dims.FM, dims.HK), lambda j: (0, 0), pipeline_mode=pl.Buffered(1))
    return pl.pallas_call(
        functools.partial(_spec_body, dims=dims),
        grid=(width // cw,),
        in_specs=[resident] * 4 + [pl.BlockSpec((2, dims.H, cw), lambda j: (0, 0, j))] * 2,
        out_specs=[pl.BlockSpec((dims.FM, cw), lambda j: (0, j))] * 4,
        out_shape=[jax.ShapeDtypeStruct((dims.FM, width), F32)] * 4,
        scratch_shapes=[pltpu.VMEM((dims.HK, cw), BF16)] * 4,
        compiler_params=_params(("arbitrary",)),
        name="hyena_filter_spectrum",
    )(*fwd, hs, hd)


def _short_conv(ref, w_ref, b_ref):
    u = ref[0].astype(F32)
    L = u.shape[0]
    t = lax.broadcasted_iota(jnp.int32, u.shape, 0)
    prev = jnp.where(t == 0, 0.0, pltpu.roll(u, 1, 0))
    nxt = jnp.where(t == L - 1, 0.0, pltpu.roll(u, L - 1, 0))
    w = w_ref[...]
    return b_ref[...] + w[0:1] * prev + w[1:2] * u + w[2:3] * nxt


def _conv_body(*refs, dims, first):
    tables = refs[:4]
    ie_ref, io_ref, u_ref, x_ref = refs[4:8]
    n_taps = 4 if first else 2
    taps = refs[8:8 + n_taps]
    (skip_ref, k1_ref, k2_ref, k1m_ref, k2m_ref, o_ref,
     ue_ref, uo_ref, xe_ref, xo_ref, ge_ref, go_ref, nat_ref) = refs[8 + n_taps:]
    L, H, HP, HK, FM, IK = dims.L, dims.H, dims.HP, dims.HK, dims.FM, dims.IK
    cw = ue_ref.shape[1]

    lane_slabs = [(k, slice(k * LANES, (k + 1) * LANES)) for k in range(cw // LANES)]

    def split_into(even_ref, odd_ref, value):
        for k, lanes in lane_slabs:
            nat_ref[k, :L, :] = value[:, lanes]
            even_ref[:HP, lanes] = nat_ref[k, pl.ds(0, HP, stride=2), :].astype(BF16)
            odd_ref[:HP, lanes] = nat_ref[k, pl.ds(1, HP, stride=2), :].astype(BF16)

    for k, _ in lane_slabs:
        nat_ref[k, L:, :] = jnp.zeros((2 * HP - L, LANES), F32)
    if HK > HP:
        ue_ref[HP:, :] = jnp.zeros((HK - HP, cw), BF16)
        uo_ref[HP:, :] = jnp.zeros((HK - HP, cw), BF16)
    if first:
        split_into(ue_ref, uo_ref, _short_conv(u_ref, taps[0], taps[1]))
    else:
        ue_ref[:HP, :] = u_ref[0, 0]
        uo_ref[:HP, :] = u_ref[0, 1]
    split_into(xe_ref, xo_ref, _short_conv(x_ref, taps[-2], taps[-1]))

    if IK > 2 * FM:
        ge_ref[2 * FM:, :] = jnp.zeros((IK - 2 * FM, cw), BF16)
        go_ref[2 * FM:, :] = jnp.zeros((IK - 2 * FM, cw), BF16)
    for lo, sz in _row_tiles(FM, MXU_DIM):
        rows = slice(lo, lo + sz)
        p, a, pm, am = _forward_dft(tables, ue_ref, uo_ref, rows)
        k1, k2, k1m, k2m = k1_ref[rows, :], k2_ref[rows, :], k1m_ref[rows, :], k2m_ref[rows, :]
        e1, e2 = p * k1 + a * k2, a * k1 - p * k2
        e1m, e2m = pm * k1m + am * k2m, am * k1m - pm * k2m
        ge_ref[lo:lo + sz, :] = (e1 + e1m).astype(BF16)
        ge_ref[FM + lo:FM + lo + sz, :] = (e2 - e2m).astype(BF16)
        go_ref[lo:lo + sz, :] = (e1 - e1m).astype(BF16)
        go_ref[FM + lo:FM + lo + sz, :] = (e2 + e2m).astype(BF16)

    skip = skip_ref[...]
    for lo, sz in _row_tiles(HP, MXU_DIM):
        rows = slice(lo, lo + sz)
        ye = jnp.dot(ie_ref[rows, :], ge_ref[...], preferred_element_type=F32)
        yo = jnp.dot(io_ref[rows, :], go_ref[...], preferred_element_type=F32)
        oe = xe_ref[rows, :].astype(F32) * (ye + skip * ue_ref[rows, :].astype(F32))
        oo = xo_ref[rows, :].astype(F32) * (yo + skip * uo_ref[rows, :].astype(F32))
        if first:
            o_ref[0, 0, rows, :] = oe.astype(o_ref.dtype)
            o_ref[0, 1, rows, :] = oo.astype(o_ref.dtype)
        else:
            valid = min(sz, H - lo)
            for k, lanes in lane_slabs:
                nat_ref[k, pl.ds(2 * lo, valid, stride=2), :] = oe[:valid, lanes]
                nat_ref[k, pl.ds(2 * lo + 1, valid, stride=2), :] = oo[:valid, lanes]
    if not first:
        for k, lanes in lane_slabs:
            o_ref[0, :, lanes] = nat_ref[k, :L, :]


def _long_conv(dims, fwd, inv, u_arr, u_col0, x_arr, x_col0, short_w, short_b, short_u_col0, short_x_col0,
               skip, spectrum, k_col0, width):
    B = u_arr.shape[0]
    L = dims.L
    cw = MXU_DIM
    ub, xb, kb = u_col0 // cw, x_col0 // cw, k_col0 // cw
    first = short_u_col0 is not None
    fwd_spec = pl.BlockSpec((dims.FM, dims.HK), lambda c, b: (0, 0), pipeline_mode=pl.Buffered(1))
    inv_spec = pl.BlockSpec((dims.HP, dims.IK), lambda c, b: (0, 0), pipeline_mode=pl.Buffered(1))

    def taps(col0):
        blk = col0 // cw
        return [pl.BlockSpec((HY_SHORT, cw), lambda c, b: (0, blk + c)),
                pl.BlockSpec((1, cw), lambda c, b: (0, blk + c))]

    if first:
        u_spec = pl.BlockSpec((1, L, cw), lambda c, b: (b, 0, ub + c))
        out_spec = pl.BlockSpec((1, 2, dims.HP, cw), lambda c, b: (b, 0, 0, c))
        out_shape = jax.ShapeDtypeStruct((B, 2, dims.HP, width), BF16)
    else:
        u_spec = pl.BlockSpec((1, 2, dims.HP, cw), lambda c, b: (b, 0, 0, ub + c))
        out_spec = pl.BlockSpec((1, L, cw), lambda c, b: (b, 0, c))
        out_shape = jax.ShapeDtypeStruct((B, L, width), F32)
    in_specs = [fwd_spec] * 4 + [inv_spec] * 2 + [u_spec, pl.BlockSpec((1, L, cw), lambda c, b: (b, 0, xb + c))]
    args = list(fwd) + list(inv) + [u_arr, x_arr]
    for col0 in ([short_u_col0] if first else []) + [short_x_col0]:
        in_specs += taps(col0)
        args += [short_w, short_b]
    in_specs += [pl.BlockSpec((1, cw), lambda c, b: (0, c))]
    in_specs += [pl.BlockSpec((dims.FM, cw), lambda c, b: (0, kb + c))] * 4
    args += [skip.reshape(1, width)] + list(spectrum)
    return pl.pallas_call(
        functools.partial(_conv_body, dims=dims, first=first),
        grid=(width // cw, B),
        in_specs=in_specs,
        out_specs=out_spec,
        out_shape=out_shape,
        scratch_shapes=([pltpu.VMEM((dims.HK, cw), BF16)] * 2 + [pltpu.VMEM((dims.HP, cw), BF16)] * 2
                        + [pltpu.VMEM((dims.IK, cw), BF16)] * 2 + [pltpu.VMEM((cw // LANES, 2 * dims.HP, LANES), F32)]),
        compiler_params=_params(("parallel", "arbitrary")),
        name="hyena_long_conv",
    )(*args)


def _filter_body(z_ref, w1_ref, b1_ref, w2_ref, b2_ref, fr_ref, w3f_ref, w3b_ref, dec_ref, hs_ref, hd_ref,
                 h2_ref, nat_ref, *, dims):
    hp = lax.Precision.HIGHEST

    @pl.when(pl.program_id(0) == 0)
    def _():
        fr = fr_ref[...]
        h1 = jnp.sin(fr * (jnp.dot(z_ref[...], w1_ref[...], precision=hp, preferred_element_type=F32)
                           + b1_ref[...]))
        h2_ref[...] = jnp.sin(fr * (jnp.dot(h1, w2_ref[...], precision=hp, preferred_element_type=F32)
                                    + b2_ref[...]))

    h2 = h2_ref[...]
    dec = dec_ref[...]
    hf = jnp.dot(h2, w3f_ref[...], precision=hp, preferred_element_type=F32) * dec
    hb = jnp.dot(h2, w3b_ref[...], precision=hp, preferred_element_type=F32) * dec
    hb = jnp.where(lax.broadcasted_iota(jnp.int32, hb.shape, 0) == 0, 0.0, hb)
    scale = lax.rsqrt(jnp.sum(hf * hf + hb * hb, axis=0, keepdims=True) + 1e-6)
    for ref, val in ((hs_ref, (hf + hb) * scale), (hd_ref, (hf - hb) * scale)):
        for k in range(val.shape[1] // LANES):
            lanes = slice(k * LANES, (k + 1) * LANES)
            nat_ref[k] = val[:, lanes]
            ref[0, :, lanes] = nat_ref[k, pl.ds(0, dims.H, stride=2), :]
            ref[1, :, lanes] = nat_ref[k, pl.ds(1, dims.H, stride=2), :]


def _filter_constants(L, width):
    t = np.linspace(0.0, 1.0, L)
    w = (2.0 * math.pi / L) * np.arange(L)
    bands = np.linspace(1e-4, HY_BANDS - 1, HY_BANDS)
    fw = w[:, None] * bands[None, :]
    z = np.concatenate([t[:, None], np.cos(fw), -np.sin(fw)], axis=-1)
    z = np.pad(z, ((0, 0), (0, LANES - z.shape[1])))
    deltas = np.abs(np.linspace(HY_MIN_DECAY, HY_MAX_DECAY, width))
    return z.astype(np.float32), np.exp(-t[:, None] * deltas[None, :]).astype(np.float32)


def _hyena_filters(dims, consts, w1, b1, w2, b2, w3, freq):
    L = dims.L
    z, dec = consts
    hidden = w2.shape[0]
    width = w3.shape[1] // (2 * HY_ORDER)
    cw = MXU_DIM
    per_order = width // cw

    def lanes128(a):
        return jnp.pad(a.astype(F32), [(0, 0)] * (a.ndim - 1) + [(0, LANES - a.shape[-1])])

    w1p = jnp.pad(lanes128(w1), ((0, LANES - w1.shape[0]), (0, 0)))
    w2p = jnp.pad(lanes128(w2), ((0, LANES - hidden), (0, 0)))
    w3p = jnp.pad(w3.astype(F32), ((0, LANES - hidden), (0, 0)))
    row = lambda a: lanes128(a).reshape(1, LANES)
    full = lambda shape: pl.BlockSpec(shape, lambda j: (0, 0))
    fwd_col = lambda j: (0, (j // per_order) * 2 * per_order + j % per_order)
    bwd_col = lambda j: (0, (j // per_order) * 2 * per_order + per_order + j % per_order)
    out_spec = pl.BlockSpec((2, dims.H, cw), lambda j: (0, 0, j))
    return pl.pallas_call(
        functools.partial(_filter_body, dims=dims),
        grid=(HY_ORDER * per_order,),
        in_specs=[full((L, LANES)), full((LANES, LANES)), full((1, LANES)), full((LANES, LANES)),
                  full((1, LANES)), full((1, LANES)),
                  pl.BlockSpec((LANES, cw), fwd_col), pl.BlockSpec((LANES, cw), bwd_col),
                  pl.BlockSpec((L, cw), lambda j: (0, j % per_order))],
        out_specs=[out_spec, out_spec],
        out_shape=[jax.ShapeDtypeStruct((2, dims.H, HY_ORDER * width), F32)] * 2,
        scratch_shapes=[pltpu.VMEM((L, LANES), F32), pltpu.VMEM((cw // LANES, L, LANES), F32)],
        compiler_params=_params(("arbitrary",)),
        name="hyena_filters",
    )(z, w1p, row(b1), w2p, row(b2), row(freq), w3p, w3p, dec)


def _mix_body(ret_ref, hy_ref, gr_ref, gh_ref, h_ref, wr_ref, wh_ref, wo_ref, o_ref):
    ret = jnp.dot(ret_ref[...], wr_ref[...], preferred_element_type=F32)
    hyo = jnp.dot(hy_ref[...].astype(BF16), wh_ref[...], preferred_element_type=F32)
    merged = (jax.nn.sigmoid(gr_ref[...].astype(F32)) * ret
              + jax.nn.sigmoid(gh_ref[...].astype(F32)) * hyo)
    o_ref[...] = h_ref[...] + jnp.dot(merged.astype(BF16), wo_ref[...], preferred_element_type=F32)


def _mix_out(ret2, hy2, proj2, gate_col0, h2, w_ret_o, w_hy_o, w_out):
    T, D = h2.shape
    tm = _pick_tile(T, 384, BF16_SUBLANES)
    gb = gate_col0 // D
    row = lambda w: pl.BlockSpec((tm, w), lambda i: (i, 0))
    full = lambda a: pl.BlockSpec(a.shape, lambda i: (0, 0))
    return pl.pallas_call(
        _mix_body,
        grid=(T // tm,),
        in_specs=[
            row(ret2.shape[1]), row(hy2.shape[1]),
            pl.BlockSpec((tm, D), lambda i: (i, gb)),
            pl.BlockSpec((tm, D), lambda i: (i, gb + 1)),
            row(D), full(w_ret_o), full(w_hy_o), full(w_out),
        ],
        out_specs=row(D),
        out_shape=jax.ShapeDtypeStruct((T, D), F32),
        compiler_params=_params(("parallel",)),
        name="merge_outproj",
    )(ret2, hy2, proj2, proj2, h2, w_ret_o, w_hy_o, w_out)


EXPERT_TILE = 512
META_COLS = 8
SC_WINDOW = 64
SC_WORKERS = 32
SC_CHUNK = 2 * SC_WINDOW * SC_WORKERS
SC_SCATTER_WINDOW = 128


def _pack_bf16_pairs(x):
    n = x.shape[1] // 2
    xb = x.astype(BF16).astype(F32)
    hi = lax.bitcast_convert_type(xb[:, :n], jnp.uint32)
    lo = lax.bitcast_convert_type(xb[:, n:], jnp.uint32)
    return lax.bitcast_convert_type(hi | (lo >> 16), jnp.int32)


def _unpack_bf16_pairs(w):
    u = lax.bitcast_convert_type(w, jnp.uint32)
    hi = lax.bitcast_convert_type(u & jnp.uint32(0xFFFF0000), F32)
    lo = lax.bitcast_convert_type(u << 16, F32)
    return hi, lo


def _route(lt):
    assert EXP_PER_GROUP == 8 and N_GROUPS <= 8
    tm = lt.shape[1]
    row = lax.broadcasted_iota(jnp.int32, (8, tm), 0)
    neg = -jnp.inf
    big = jnp.int32(1 << 20)
    gl = jnp.where(row < N_GROUPS, lt[N_EXPERTS:N_EXPERTS + 8, :], neg)
    gmax = jnp.max(gl, axis=0, keepdims=True)
    p_top = 1.0 / jnp.sum(jnp.exp(gl - gmax), axis=0, keepdims=True)
    g_idx = jnp.min(jnp.where(gl == gmax, row, big), axis=0, keepdims=True)
    el = lt[0:8, :]
    for g in range(1, N_GROUPS):
        el = jnp.where(g_idx == g, lt[8 * g:8 * g + 8, :], el)
    m1 = jnp.max(el, axis=0, keepdims=True)
    i1 = jnp.min(jnp.where(el == m1, row, big), axis=0, keepdims=True)
    el2 = jnp.where(row == i1, neg, el)
    m2 = jnp.max(el2, axis=0, keepdims=True)
    i2 = jnp.min(jnp.where(el2 == m2, row, big), axis=0, keepdims=True)
    r = jnp.exp(m2 - m1)
    base = g_idx * EXP_PER_GROUP
    return base + i1, base + i2, p_top / (1.0 + r), p_top * r / (1.0 + r)


def _route_body(h_ref, g_ref, whi_ref, wlo_ref, xpk_ref, meta_ref, meta_t_ref, cnt_ref, carry_ref):
    i = pl.program_id(0)

    @pl.when(i == 0)
    def _():
        carry_ref[...] = jnp.zeros_like(carry_ref)

    xn = _rms(h_ref[...], g_ref[...])
    xpk_ref[...] = _pack_bf16_pairs(xn)
    x_hi = xn.astype(BF16)
    x_lo = (xn - x_hi.astype(F32)).astype(BF16)
    logits = (jnp.dot(x_hi, whi_ref[...], preferred_element_type=F32)
              + jnp.dot(x_lo, whi_ref[...], preferred_element_type=F32)
              + jnp.dot(x_hi, wlo_ref[...], preferred_element_type=F32))
    e0, e1, w0, w1 = _route(logits.T)
    tm = logits.shape[0]
    row = lax.broadcasted_iota(jnp.int32, (LANES, tm), 0)
    onehot_t = jnp.where((row == e0) | (row == e1), 1.0, 0.0).astype(BF16)
    ri = lax.broadcasted_iota(jnp.int32, (tm, tm), 0)
    ci = lax.broadcasted_iota(jnp.int32, (tm, tm), 1)
    earlier = jnp.where(ci < ri, 1.0, 0.0).astype(BF16)
    prefix = carry_ref[...] + _dot_nt(earlier, onehot_t)
    prefix_t = prefix.T
    r0 = jnp.sum(jnp.where(row == e0, prefix_t, 0.0), axis=0, keepdims=True)
    r1 = jnp.sum(jnp.where(row == e1, prefix_t, 0.0), axis=0, keepdims=True)
    row8 = lax.broadcasted_iota(jnp.int32, (META_COLS, tm), 0)
    meta_t = jnp.zeros((META_COLS, tm), F32)
    for c, val in enumerate((e0.astype(F32), e1.astype(F32), r0, r1, w0, w1)):
        meta_t = jnp.where(row8 == c, val, meta_t)
    meta_t_ref[...] = meta_t
    padded = jnp.concatenate([meta_t, jnp.zeros((LANES - META_COLS, tm), F32)], axis=0)
    meta_ref[...] = padded.T[:, :META_COLS]
    counts = _dot_nt(jnp.ones((8, tm), BF16), onehot_t)
    carry_ref[...] += counts[0:1, :]
    cnt_ref[...] = carry_ref[...]


def _moe_route(h2, gain, w_router):
    T, D = h2.shape
    tm = _pick_tile(T, 1152, LANES)
    return pl.pallas_call(
        _route_body,
        grid=(T // tm,),
        in_specs=[
            pl.BlockSpec((tm, D), lambda i: (i, 0)),
            pl.BlockSpec((1, D), lambda i: (0, 0)),
            pl.BlockSpec((D, LANES), lambda i: (0, 0)),
            pl.BlockSpec((D, LANES), lambda i: (0, 0)),
        ],
        out_specs=[
            pl.BlockSpec((tm, D // 2), lambda i: (i, 0)),
            pl.BlockSpec((tm, META_COLS), lambda i: (i, 0)),
            pl.BlockSpec((META_COLS, tm), lambda i: (0, i)),
            pl.BlockSpec((1, LANES), lambda i: (0, 0)),
        ],
        out_shape=[
            jax.ShapeDtypeStruct((T, D // 2), jnp.int32),
            jax.ShapeDtypeStruct((T, META_COLS), F32),
            jax.ShapeDtypeStruct((META_COLS, T), F32),
            jax.ShapeDtypeStruct((1, LANES), F32),
        ],
        scratch_shapes=[pltpu.VMEM((1, LANES), F32)],
        compiler_params=_params(("arbitrary",)),
        name="moe_route",
    )(h2, gain.reshape(1, D), *w_router)


def _sc_gather(table, idx):
    n = idx.shape[0]
    width = table.shape[1]
    win = SC_WINDOW
    assert n % SC_CHUNK == 0
    per_worker = n // SC_WORKERS
    mesh = plsc.VectorSubcoreMesh(core_axis_name="c", subcore_axis_name="s")

    @functools.partial(
        pl.kernel, out_type=jax.ShapeDtypeStruct((n, width), table.dtype), mesh=mesh,
        scratch_types=[pltpu.VMEM((per_worker,), jnp.int32), pltpu.VMEM((2, win, width), table.dtype),
                       pltpu.SemaphoreType.DMA((2,)), pltpu.SemaphoreType.DMA((2,))],
        name="sc_row_gather")
    def gather(table_hbm, idx_hbm, out_hbm, idx_v, rows_v, gsem, osem):
        worker = lax.axis_index("s") * mesh.num_cores + lax.axis_index("c")
        base = worker * per_worker
        pltpu.sync_copy(idx_hbm.at[pl.ds(base, per_worker)], idx_v)

        @pl.loop(0, per_worker, step=2 * win)
        def _(off):
            fetch = [pltpu.async_copy(table_hbm.at[idx_v.at[pl.ds(off + s * win, win)]], rows_v.at[s], gsem.at[s])
                     for s in range(2)]
            store = []
            for s in range(2):
                fetch[s].wait()
                store.append(pltpu.async_copy(rows_v.at[s], out_hbm.at[pl.ds(base + off + s * win, win)],
                                              osem.at[s]))
            for s in range(2):
                store[s].wait()

    return gather(table, idx)


def _sc_dispatch(table, dest, n_out):
    n_rows, width = table.shape
    win = SC_SCATTER_WINDOW
    assert n_rows % win == 0
    n_win = n_rows // win
    per_worker = pl.cdiv(n_win, SC_WORKERS)
    idx = jnp.pad(dest, ((0, 0), (0, per_worker * SC_WORKERS * win - n_rows)))
    idx = idx.reshape(2, per_worker, SC_WORKERS, win).transpose(0, 2, 1, 3)
    mesh = plsc.VectorSubcoreMesh(core_axis_name="c", subcore_axis_name="s")

    @functools.partial(
        pl.kernel, out_type=jax.ShapeDtypeStruct((n_out, width), table.dtype), mesh=mesh,
        scratch_types=[pltpu.VMEM((per_worker, win), jnp.int32), pltpu.VMEM((per_worker, win), jnp.int32),
                       pltpu.VMEM((win, width), table.dtype),
                       pltpu.SemaphoreType.DMA, pltpu.SemaphoreType.DMA],
        name="sc_row_dispatch")
    def dispatch(table_hbm, idx_hbm, out_hbm, idx0_v, idx1_v, rows_v, sem0, sem1):
        worker = lax.axis_index("s") * mesh.num_cores + lax.axis_index("c")
        pltpu.sync_copy(idx_hbm.at[0, worker], idx0_v)
        pltpu.sync_copy(idx_hbm.at[1, worker], idx1_v)

        @pl.loop(0, per_worker)
        def _(j):
            window = j * SC_WORKERS + worker

            @pl.when(window < n_win)
            def _():
                pltpu.sync_copy(table_hbm.at[pl.ds(window * win, win)], rows_v)
                first = pltpu.async_copy(rows_v, out_hbm.at[idx0_v.at[j]], sem0)
                second = pltpu.async_copy(rows_v, out_hbm.at[idx1_v.at[j]], sem1)
                first.wait()
                second.wait()

    return dispatch(table, idx)


def _expert_body(te_ref, tv_ref, x_ref, wg_ref, wu_ref, wd_ref, y_ref, wg_s, wu_s, wd_s):
    t = pl.program_id(0)
    half = x_ref.shape[1]

    @pl.when((t == 0) | (te_ref[t] != te_ref[jnp.maximum(t - 1, 0)]))
    def _():
        wg_s[...] = wg_ref[0].astype(BF16)
        wu_s[...] = wu_ref[0].astype(BF16)
        wd_s[...] = wd_ref[0].astype(BF16)

    @pl.when(tv_ref[t] > 0)
    def _():
        row = lax.broadcasted_iota(jnp.int32, x_ref.shape, 0)
        hi, lo = _unpack_bf16_pairs(jnp.where(row < tv_ref[t], x_ref[...], 0))
        hi, lo = hi.astype(BF16), lo.astype(BF16)
        hg = jnp.dot(hi, wg_s[:half, :], preferred_element_type=F32)
        hg += jnp.dot(lo, wg_s[half:, :], preferred_element_type=F32)
        hu = jnp.dot(hi, wu_s[:half, :], preferred_element_type=F32)
        hu += jnp.dot(lo, wu_s[half:, :], preferred_element_type=F32)
        act = (hg * jax.nn.sigmoid(hg) * hu).astype(BF16)
        y_ref[...] = _pack_bf16_pairs(jnp.dot(act, wd_s[...], preferred_element_type=F32))

    @pl.when(tv_ref[t] == 0)
    def _():
        y_ref[...] = jnp.zeros_like(y_ref)


def _moe_experts(xs, n_sorted, tile_expert, tile_valid, w_gate, w_up, w_down):
    NP, half = n_sorted, xs.shape[1]
    _, D, FF = w_gate.shape
    tr = EXPERT_TILE
    grid_spec = pltpu.PrefetchScalarGridSpec(
        num_scalar_prefetch=2,
        grid=(NP // tr,),
        in_specs=[
            pl.BlockSpec((tr, half), lambda t, te, tv: (t, 0)),
            pl.BlockSpec((1, D, FF), lambda t, te, tv: (te[t], 0, 0)),
            pl.BlockSpec((1, D, FF), lambda t, te, tv: (te[t], 0, 0)),
            pl.BlockSpec((1, FF, D), lambda t, te, tv: (te[t], 0, 0)),
        ],
        out_specs=pl.BlockSpec((tr, half), lambda t, te, tv: (t, 0)),
        scratch_shapes=[pltpu.VMEM((D, FF), BF16), pltpu.VMEM((D, FF), BF16), pltpu.VMEM((FF, D), BF16)],
    )
    return pl.pallas_call(
        _expert_body,
        grid_spec=grid_spec,
        out_shape=jax.ShapeDtypeStruct((NP, half), jnp.int32),
        compiler_params=_params(("arbitrary",)),
        name="moe_experts",
    )(tile_expert, tile_valid, xs, w_gate, w_up, w_down)


def _moe_combine_into(o_ref, h, meta, y0, y1):
    half = y0.shape[1]
    w0 = meta[:, 4:5]
    w1 = meta[:, 5:6]
    hi0, lo0 = _unpack_bf16_pairs(y0)
    hi1, lo1 = _unpack_bf16_pairs(y1)
    o_ref[:, :half] = h[:, :half] + w0 * hi0 + w1 * hi1
    o_ref[:, half:] = h[:, half:] + w0 * lo0 + w1 * lo1


def _moe(h2, gain, w_router, layer, w_gate, w_up, w_down):
    T, D = h2.shape
    E = N_EXPERTS
    tr = EXPERT_TILE
    n_sorted = _round_up(2 * T + E * (tr - 1), tr)
    t_pad = _round_up(T, SC_CHUNK // 2)

    xpk, meta, meta_t, counts = _moe_route(h2, gain, w_router)

    cnt = counts[0, :E].astype(jnp.int32)
    padded = (cnt + tr - 1) // tr * tr
    ends = jnp.cumsum(padded)
    starts = ends - padded
    eid = meta_t[0:2].astype(jnp.int32)
    pos = meta_t[2:4].astype(jnp.int32)
    for e in range(E):
        pos = pos + jnp.where(eid == e, starts[e], 0)
    tile_start = jnp.arange(n_sorted // tr, dtype=jnp.int32) * tr
    tile_expert = jnp.minimum(jnp.sum(tile_start[:, None] >= ends[None, :], axis=1), E - 1).astype(jnp.int32)
    tile_valid = jnp.clip(cnt[tile_expert] - (tile_start - starts[tile_expert]), 0, tr).astype(jnp.int32)

    xs = _sc_dispatch(xpk, pos, n_sorted)
    ys = _moe_experts(xs, n_sorted, layer * E + tile_expert, tile_valid, w_gate, w_up, w_down)
    spare = jnp.arange(t_pad - T, dtype=jnp.int32)
    back = jnp.concatenate([pos, jnp.broadcast_to(spare[None], (2, t_pad - T))], axis=1)
    yg = _sc_gather(ys, back.reshape(-1)).reshape(2, t_pad, D // 2)
    return meta, yg


def _final_body(h_ref, meta_ref, y0_ref, y1_ref, g_ref, o_ref, hn_ref):
    _moe_combine_into(hn_ref, h_ref[0], meta_ref[0], y0_ref[0], y1_ref[0])
    o_ref[0] = _rms(hn_ref[N_META:, :], g_ref[...])


def _final_norm(h3, meta3, yg, gain, b0, nb):
    _, L, D = h3.shape
    return pl.pallas_call(
        _final_body,
        grid=(nb,),
        in_specs=[pl.BlockSpec((1, L, D), lambda b: (b0 + b, 0, 0)),
                  pl.BlockSpec((1, L, meta3.shape[2]), lambda b: (b0 + b, 0, 0)),
                  pl.BlockSpec((1, L, D // 2), lambda b: (0, b0 + b, 0)),
                  pl.BlockSpec((1, L, D // 2), lambda b: (1, b0 + b, 0)),
                  pl.BlockSpec((1, D), lambda b: (0, 0))],
        out_specs=pl.BlockSpec((1, L - N_META, D), lambda b: (b, 0, 0)),
        out_shape=jax.ShapeDtypeStruct((nb, L - N_META, D), F32),
        scratch_shapes=[pltpu.VMEM((L, D), F32)],
        compiler_params=_params(("parallel",)),
        name="combine_final_norm",
    )(h3, meta3, yg, yg, gain.reshape(1, D))


def kernel(x_prompt, x_sample, meta_tokens, norm_mix, w_in, ret_decay_fwd, ret_decay_bwd, hy_short_w, hy_short_b, hy_filt_w1, hy_filt_b1, hy_filt_w2, hy_filt_b2, hy_filt_w3, hy_sin_freq, hy_skip, w_ret_o, w_hy_o, w_out, norm_ffn, router_group, router_expert, moe_w_gate, moe_w_up, moe_w_down, norm_final):
    assert x_prompt.shape[1:] == x_sample.shape[1:]
    nbp, nbs = x_prompt.shape[0], x_sample.shape[0]
    B = nbp + nbs
    D = x_prompt.shape[2]
    L = N_META + x_prompt.shape[1]
    T = B * L
    depth = w_in.shape[0]
    q_w = RET_HEADS * RET_DK
    v_w = RET_HEADS * RET_DV
    hy_w = hy_skip.shape[2]
    hy_col0 = 2 * q_w + 2 * v_w
    gate_col0 = hy_col0 + 3 * hy_w
    assert D == q_w and w_in.shape[2] == gate_col0 + 2 * D

    x = jnp.concatenate([x_prompt, x_sample], axis=0)
    meta = jnp.broadcast_to(meta_tokens[None].astype(x.dtype), (B, N_META, D))
    h = jnp.concatenate([meta, x], axis=1).reshape(T, D)

    cos_t, sin_t = _rotary_tables(L)
    dims = _HyenaDims(L)
    dft_fwd, dft_inv = _dft_tables(dims)
    filt_consts = _filter_constants(L, hy_w)
    router = jnp.concatenate([router_expert, router_group], axis=2).astype(F32)
    router = jnp.pad(router, ((0, 0), (0, 0), (0, LANES - router.shape[2])))
    router_hi = router.astype(BF16)
    router_lo = (router - router_hi.astype(F32)).astype(BF16)
    assert moe_w_gate.shape[1] == N_EXPERTS
    w_gate_all = moe_w_gate.reshape((-1,) + moe_w_gate.shape[2:])
    w_up_all = moe_w_up.reshape((-1,) + moe_w_up.shape[2:])
    w_down_all = moe_w_down.reshape((-1,) + moe_w_down.shape[2:])

    moe = None
    for i in range(depth):
        proj, h = _inproj(h, norm_mix[i], w_in[i].astype(BF16), moe)
        proj3 = proj.reshape(B, L, -1)

        lg = jnp.stack([jax.nn.log_sigmoid(ret_decay_fwd[i].astype(F32)),
                        jax.nn.log_sigmoid(ret_decay_bwd[i].astype(F32))])
        ret = _retention(proj3, lg, cos_t, sin_t)

        hs, hd = _hyena_filters(dims, filt_consts, hy_filt_w1[i], hy_filt_b1[i], hy_filt_w2[i], hy_filt_b2[i],
                                hy_filt_w3[i], hy_sin_freq[i])
        spectrum = _filter_spectrum(dims, dft_fwd, hs, hd)
        sw = hy_short_w[i].astype(F32)
        sb = hy_short_b[i].astype(F32).reshape(1, -1)
        z = _long_conv(dims, dft_fwd, dft_inv, proj3, hy_col0, proj3, hy_col0 + hy_w, sw, sb, 0, hy_w,
                       hy_skip[i, 0].astype(F32), spectrum, 0, hy_w)
        z = _long_conv(dims, dft_fwd, dft_inv, z, 0, proj3, hy_col0 + 2 * hy_w, sw, sb, None, 2 * hy_w,
                       hy_skip[i, 1].astype(F32), spectrum, hy_w, hy_w)

        h = _mix_out(ret.reshape(T, v_w), z.reshape(T, hy_w), proj, gate_col0, h,
                     w_ret_o[i].astype(BF16), w_hy_o[i].astype(BF16), w_out[i].astype(BF16))
        moe = _moe(h, norm_ffn[i], (router_hi[i], router_lo[i]), i, w_gate_all, w_up_all, w_down_all)

    h3 = h.reshape(B, L, D)
    meta3 = moe[0].reshape(B, L, -1)
    return (_final_norm(h3, meta3, moe[1], norm_final, 0, nbp),
            _final_norm(h3, meta3, moe[1], norm_final, nbp, nbs))
```

```python
import functools
import math

import jax
import jax.numpy as jnp
import numpy as np
from jax import lax
from jax.experimental import pallas as pl
from jax.experimental.pallas import tpu as pltpu
from jax.experimental.pallas import tpu_sc as plsc

N_META = 16
RET_HEADS = 8
RET_DK = 128
RET_DV = 256
ROPE_THETA = 10000.0
HY_ORDER = 2
HY_SHORT = 3
HY_EMB = 33
HY_BANDS = (HY_EMB - 1) // 2
HY_DECAY_TARGET = 1e-2
HY_MIN_DECAY = math.log(HY_DECAY_TARGET) / 1.5
HY_MAX_DECAY = math.log(HY_DECAY_TARGET) / 0.3
N_GROUPS = 4
EXP_PER_GROUP = 8
N_EXPERTS = N_GROUPS * EXP_PER_GROUP
RMS_EPS = 1e-6

LANES = 128
BF16_SUBLANES = 16
MXU_DIM = 256
RET_CHUNK = 256
VMEM_LIMIT = 56 * 1024 * 1024

F32 = jnp.float32
BF16 = jnp.bfloat16


def _round_up(n, m):
    return (n + m - 1) // m * m


def _pick_tile(n, target, mult):
    best = None
    for t in range(mult, min(n, target) + 1, mult):
        if n % t == 0:
            best = t
    assert best is not None, (n, target, mult)
    return best


def _params(sem):
    return pltpu.CompilerParams(dimension_semantics=sem, vmem_limit_bytes=VMEM_LIMIT)


def _rms(x, gain):
    ms = jnp.mean(x * x, axis=-1, keepdims=True)
    return x * lax.rsqrt(ms + RMS_EPS) * gain


def _inproj_body(h_ref, g_ref, w_ref, o_ref, xn_ref):
    @pl.when(pl.program_id(1) == 0)
    def _():
        xn_ref[...] = _rms(h_ref[...], g_ref[...]).astype(BF16)

    o_ref[...] = jnp.dot(xn_ref[...], w_ref[...], preferred_element_type=F32).astype(o_ref.dtype)


def _inproj_moe_body(h_ref, meta_ref, y0_ref, y1_ref, g_ref, w_ref, o_ref, hn_ref, xn_ref):
    @pl.when(pl.program_id(1) == 0)
    def _():
        _moe_combine_into(hn_ref, h_ref[...], meta_ref[...], y0_ref[0], y1_ref[0])
        xn_ref[...] = _rms(hn_ref[...], g_ref[...]).astype(BF16)

    o_ref[...] = jnp.dot(xn_ref[...], w_ref[...], preferred_element_type=F32).astype(o_ref.dtype)


def _inproj(h2, gain, w, moe=None):
    T, D = h2.shape
    nc = w.shape[1]
    tm = _pick_tile(T, 1152, BF16_SUBLANES)
    tn = _pick_tile(nc, 1024, LANES)
    common = dict(
        grid=(T // tm, nc // tn),
        scratch_shapes=[pltpu.VMEM((tm, D), BF16)],
        compiler_params=_params(("parallel", "arbitrary")),
    )
    h_spec = pl.BlockSpec((tm, D), lambda i, j: (i, 0))
    g_spec = pl.BlockSpec((1, D), lambda i, j: (0, 0))
    w_spec = pl.BlockSpec((D, tn), lambda i, j: (0, j))
    o_spec = pl.BlockSpec((tm, tn), lambda i, j: (i, j))
    o_shape = jax.ShapeDtypeStruct((T, nc), BF16)
    if moe is None:
        proj = pl.pallas_call(_inproj_body, in_specs=[h_spec, g_spec, w_spec], out_specs=o_spec,
                              out_shape=o_shape, name="norm_inproj", **common)(h2, gain.reshape(1, D), w)
        return proj, h2
    meta, yg = moe
    return pl.pallas_call(
        _inproj_moe_body,
        in_specs=[h_spec, pl.BlockSpec((tm, meta.shape[1]), lambda i, j: (i, 0)),
                  pl.BlockSpec((1, tm, D // 2), lambda i, j: (0, i, 0)),
                  pl.BlockSpec((1, tm, D // 2), lambda i, j: (1, i, 0)),
                  g_spec, w_spec],
        out_specs=[o_spec, h_spec],
        out_shape=[o_shape, jax.ShapeDtypeStruct((T, D), F32)],
        name="combine_norm_inproj", **common,
    )(h2, meta, yg, yg, gain.reshape(1, D), w)


def _dot_t(a, b):
    return lax.dot_general(a, b, (((0,), (0,)), ((), ())), preferred_element_type=F32)


def _dot_nt(a, b):
    return lax.dot_general(a, b, (((1,), (1,)), ((), ())), preferred_element_type=F32)


def _ret_body(lg_ref, q_ref, k_ref, v_ref, g_ref, cos_ref, sin_ref, o_ref, rb_ref, kr_ref, *, seq_len):
    C = RET_CHUNK
    L = seq_len
    n_chunks = pl.cdiv(L, C)
    head = pl.program_id(1)
    lgf = lg_ref[0, head]
    lgb = lg_ref[1, head]

    def chunk(ref, n):
        lo, hi = n * C, min((n + 1) * C, L)
        x = ref[0, lo:hi, :]
        if hi - lo < C:
            x = jnp.concatenate([x, jnp.zeros((C - (hi - lo), x.shape[1]), x.dtype)], axis=0)
        return x

    def rotary(ref, n):
        x = chunk(ref, n).astype(F32)
        sl = slice(n * C, (n + 1) * C)
        return x * cos_ref[sl, :] + pltpu.roll(x, RET_DK // 2, 1) * sin_ref[sl, :]

    row = lax.broadcasted_iota(jnp.int32, (C, LANES), 0).astype(F32)
    ri = lax.broadcasted_iota(jnp.int32, (C, C), 0).astype(F32)
    ci = lax.broadcasted_iota(jnp.int32, (C, C), 1).astype(F32)
    decay = jnp.exp(jnp.where(ci <= ri, (ri - ci) * lgf, (ci - ri) * lgb))
    qf_dec = jnp.exp((row + 1.0) * lgf)
    qb_dec = jnp.exp((C - row) * lgb)
    kf_dec = jnp.exp((C - 1.0 - row) * lgf)
    kb_dec = jnp.exp(row * lgb)
    cf = jnp.exp(C * lgf)
    cb = jnp.exp(C * lgb)

    state = jnp.zeros((RET_DK, RET_DV), F32)
    for n in reversed(range(n_chunks)):
        rb_ref[n] = state.astype(BF16)
        if n > 0:
            kr_ref[n] = rotary(k_ref, n)
            kb = (kr_ref[n] * kb_dec).astype(BF16)
            state = cb * state + _dot_t(kb, chunk(v_ref, n))

    state = jnp.zeros((RET_DK, RET_DV), F32)
    for n in range(n_chunks):
        q = rotary(q_ref, n) * (RET_DK ** -0.5)
        k = kr_ref[n] if n > 0 else rotary(k_ref, n)
        v = chunk(v_ref, n)
        scores = _dot_nt(q.astype(BF16), k.astype(BF16)) * decay
        o = jnp.dot(scores.astype(BF16), v, preferred_element_type=F32)
        o += jnp.dot((q * qf_dec).astype(BF16), state.astype(BF16), preferred_element_type=F32)
        o += jnp.dot((q * qb_dec).astype(BF16), rb_ref[n], preferred_element_type=F32)
        if n + 1 < n_chunks:
            state = cf * state + _dot_t((k * kf_dec).astype(BF16), v)
        o = o * lax.rsqrt(jnp.mean(o * o, axis=-1, keepdims=True) + RMS_EPS)
        lo, hi = n * C, min((n + 1) * C, L)
        g = g_ref[0, lo:hi, :].astype(F32)
        o_ref[0, lo:hi, :] = (g * jax.nn.sigmoid(g) * o[: hi - lo]).astype(o_ref.dtype)


def _retention(proj3, lg, cos_t, sin_t):
    B, L, _ = proj3.shape
    n_chunks = pl.cdiv(L, RET_CHUNK)
    lp = n_chunks * RET_CHUNK
    k_blk = RET_HEADS
    v_blk = 2 * RET_HEADS * RET_DK // RET_DV
    g_blk = v_blk + RET_HEADS
    return pl.pallas_call(
        functools.partial(_ret_body, seq_len=L),
        grid=(B, RET_HEADS),
        in_specs=[
            pl.BlockSpec(memory_space=pltpu.SMEM),
            pl.BlockSpec((1, L, RET_DK), lambda b, h: (b, 0, h)),
            pl.BlockSpec((1, L, RET_DK), lambda b, h: (b, 0, k_blk + h)),
            pl.BlockSpec((1, L, RET_DV), lambda b, h: (b, 0, v_blk + h)),
            pl.BlockSpec((1, L, RET_DV), lambda b, h: (b, 0, g_blk + h)),
            pl.BlockSpec((lp, RET_DK), lambda b, h: (0, 0)),
            pl.BlockSpec((lp, RET_DK), lambda b, h: (0, 0)),
        ],
        out_specs=pl.BlockSpec((1, L, RET_DV), lambda b, h: (b, 0, h)),
        out_shape=jax.ShapeDtypeStruct((B, L, RET_HEADS * RET_DV), BF16),
        scratch_shapes=[pltpu.VMEM((n_chunks, RET_DK, RET_DV), BF16),
                        pltpu.VMEM((n_chunks, RET_CHUNK, RET_DK), F32)],
        compiler_params=_params(("parallel", "arbitrary")),
        name="retention",
    )(lg, proj3, proj3, proj3, proj3, cos_t, sin_t)


def _rotary_tables(L):
    half = RET_DK // 2
    lp = _round_up(L, RET_CHUNK)
    inv = ROPE_THETA ** (-jnp.arange(half, dtype=F32) / half)
    ang = jnp.arange(lp, dtype=F32)[:, None] * inv[None, :]
    cos, sin = jnp.cos(ang), jnp.sin(ang)
    return jnp.concatenate([cos, cos], axis=1), jnp.concatenate([-sin, sin], axis=1)


def _row_tiles(n, tile):
    return [(s, min(tile, n - s)) for s in range(0, n, tile)]


class _HyenaDims:
    def __init__(self, L):
        assert L % 2 == 0
        self.L = L
        self.H = L // 2
        self.F = self.H + 1
        self.HP = _round_up(self.H, BF16_SUBLANES)
        self.HK = _round_up(self.H, MXU_DIM)
        self.FM = _round_up(self.F, BF16_SUBLANES)
        self.IK = _round_up(2 * self.FM, MXU_DIM)


def _dft_tables(dims):
    n = 2 * dims.L

    def trig(f, t, valid):
        ang = ((f * t) % n) * (2.0 * math.pi / n)
        return (np.where(valid, np.cos(ang), 0.0).astype(BF16), np.where(valid, np.sin(ang), 0.0).astype(BF16))

    f = np.arange(dims.FM, dtype=np.int64)[:, None]
    s = np.arange(dims.HK, dtype=np.int64)[None, :]
    ok = (f < dims.F) & (s < dims.H)
    fwd = trig(f, 2 * s, ok) + trig(f, 2 * s + 1, ok)

    s = np.arange(dims.HP, dtype=np.int64)[:, None]
    j = np.arange(dims.IK, dtype=np.int64)[None, :]
    f = j % dims.FM
    ok = (f < dims.F) & (s < dims.H)

    def inverse(t):
        cos, sin = trig(f, t, ok)
        return np.where(j < dims.FM, cos, np.where(j < 2 * dims.FM, sin, np.zeros_like(cos)))

    return fwd, (inverse(2 * s), inverse(2 * s + 1))


def _forward_dft(tables, even_ref, odd_ref, rows):
    ce_ref, se_ref, co_ref, so_ref = tables
    pe = jnp.dot(ce_ref[rows, :], even_ref[...], preferred_element_type=F32)
    po = jnp.dot(co_ref[rows, :], odd_ref[...], preferred_element_type=F32)
    ae = jnp.dot(se_ref[rows, :], even_ref[...], preferred_element_type=F32)
    ao = jnp.dot(so_ref[rows, :], odd_ref[...], preferred_element_type=F32)
    return pe + po, ae + ao, pe - po, ao - ae


def _spec_body(ce_ref, se_ref, co_ref, so_ref, hs_ref, hd_ref, k1_ref, k2_ref, k1m_ref, k2m_ref,
               se_pad, so_pad, de_pad, do_pad, *, dims):
    H, F, N = dims.H, dims.F, 2 * dims.L
    for pad, src, parity in ((se_pad, hs_ref, 0), (so_pad, hs_ref, 1), (de_pad, hd_ref, 0), (do_pad, hd_ref, 1)):
        pad[H:, :] = jnp.zeros((pad.shape[0] - H, pad.shape[1]), BF16)
        pad[:H, :] = src[parity].astype(BF16)
    rows = slice(None)
    tables = (ce_ref, se_ref, co_ref, so_ref)
    re, _, re_m, _ = _forward_dft(tables, se_pad, so_pad, rows)
    _, nim, _, nim_m = _forward_dft(tables, de_pad, do_pad, rows)
    f = lax.broadcasted_iota(jnp.int32, (dims.FM, 1), 0)
    w = jnp.where(f == 0, 1.0, 2.0) / N
    w_lo = jnp.where(f < F, w, 0.0)
    w_hi = jnp.where(f < H, w, 0.0)
    k1_ref[...] = w_lo * re
    k2_ref[...] = -w_lo * nim
    k1m_ref[...] = w_hi * re_m
    k2m_ref[...] = -w_hi * nim_m


def _filter_spectrum(dims, fwd, hs, hd):
    width = hs.shape[2]
    cw = MXU_DIM
    resident = pl.BlockSpec((dims.FM, dims.HK), lambda j: (0, 0), pipeline_mode=pl.Buffered(1))
    return pl.pallas_call(
        functools.partial(_spec_body, dims=dims),
        grid=(width // cw,),
        in_specs=[resident] * 4 + [pl.BlockSpec((2, dims.H, cw), lambda j: (0, 0, j))] * 2,
        out_specs=[pl.BlockSpec((dims.FM, cw), lambda j: (0, j))] * 4,
        out_shape=[jax.ShapeDtypeStruct((dims.FM, width), F32)] * 4,
        scratch_shapes=[pltpu.VMEM((dims.HK, cw), BF16)] * 4,
        compiler_params=_params(("arbitrary",)),
        name="hyena_filter_spectrum",
    )(*fwd, hs, hd)


def _short_conv(ref, w_ref, b_ref, anchor=None):
    u = ref[0].astype(F32)
    if anchor is not None:
        u = u + anchor
    L = u.shape[0]
    t = lax.broadcasted_iota(jnp.int32, u.shape, 0)
    prev = jnp.where(t == 0, 0.0, pltpu.roll(u, 1, 0))
    nxt = jnp.where(t == L - 1, 0.0, pltpu.roll(u, L - 1, 0))
    w = w_ref[...]
    return b_ref[...] + w[0:1] * prev + w[1:2] * u + w[2:3] * nxt


def _conv_body(*refs, dims, first):
    tables = refs[:4]
    ie_ref, io_ref, u_ref, x_ref = refs[4:8]
    n_taps = 4 if first else 2
    taps = refs[8:8 + n_taps]
    (skip_ref, k1_ref, k2_ref, k1m_ref, k2m_ref, o_ref,
     ue_ref, uo_ref, xe_ref, xo_ref, ge_ref, go_ref, nat_ref, natx_ref) = refs[8 + n_taps:]
    L, H, HP, HK, FM, IK = dims.L, dims.H, dims.HP, dims.HK, dims.FM, dims.IK
    cw = ue_ref.shape[1]

    lane_slabs = [(k, slice(k * LANES, (k + 1) * LANES)) for k in range(cw // LANES)]

    def split_into(even_ref, odd_ref, value, stage_ref):
        for k, lanes in lane_slabs:
            stage_ref[k, L:, :] = jnp.zeros((2 * HP - L, LANES), F32)
            stage_ref[k, :L, :] = value[:, lanes]
            even_ref[:HP, lanes] = stage_ref[k, pl.ds(0, HP, stride=2), :].astype(BF16)
            odd_ref[:HP, lanes] = stage_ref[k, pl.ds(1, HP, stride=2), :].astype(BF16)

    if HK > HP:
        ue_ref[HP:, :] = jnp.zeros((HK - HP, cw), BF16)
        uo_ref[HP:, :] = jnp.zeros((HK - HP, cw), BF16)
    if first:
        split_into(ue_ref, uo_ref, _short_conv(u_ref, taps[0], taps[1]), nat_ref)
    else:
        ue_ref[:HP, :] = u_ref[0, 0]
        uo_ref[:HP, :] = u_ref[0, 1]

    if IK > 2 * FM:
        ge_ref[2 * FM:, :] = jnp.zeros((IK - 2 * FM, cw), BF16)
        go_ref[2 * FM:, :] = jnp.zeros((IK - 2 * FM, cw), BF16)
    for lo, sz in _row_tiles(FM, MXU_DIM):
        rows = slice(lo, lo + sz)
        p, a, pm, am = _forward_dft(tables, ue_ref, uo_ref, rows)
        k1, k2, k1m, k2m = k1_ref[rows, :], k2_ref[rows, :], k1m_ref[rows, :], k2m_ref[rows, :]
        e1, e2 = p * k1 + a * k2, a * k1 - p * k2
        e1m, e2m = pm * k1m + am * k2m, am * k1m - pm * k2m
        ge_ref[lo:lo + sz, :] = (e1 + e1m).astype(BF16)
        ge_ref[FM + lo:FM + lo + sz, :] = (e2 - e2m).astype(BF16)
        go_ref[lo:lo + sz, :] = (e1 - e1m).astype(BF16)
        go_ref[FM + lo:FM + lo + sz, :] = (e2 + e2m).astype(BF16)

    anchor = go_ref[2 * FM - 1:2 * FM, :].astype(F32) * 0.0
    split_into(xe_ref, xo_ref, _short_conv(x_ref, taps[-2], taps[-1], anchor), natx_ref)
    skip = skip_ref[...]
    for lo, sz in _row_tiles(HP, MXU_DIM):
        rows = slice(lo, lo + sz)
        ye = jnp.dot(ie_ref[rows, :], ge_ref[...], preferred_element_type=F32)
        yo = jnp.dot(io_ref[rows, :], go_ref[...], preferred_element_type=F32)
        oe = xe_ref[rows, :].astype(F32) * (ye + skip * ue_ref[rows, :].astype(F32))
        oo = xo_ref[rows, :].astype(F32) * (yo + skip * uo_ref[rows, :].astype(F32))
        if first:
            o_ref[0, 0, rows, :] = oe.astype(o_ref.dtype)
            o_ref[0, 1, rows, :] = oo.astype(o_ref.dtype)
        else:
            valid = min(sz, H - lo)
            for k, lanes in lane_slabs:
                nat_ref[k, pl.ds(2 * lo, valid, stride=2), :] = oe[:valid, lanes]
                nat_ref[k, pl.ds(2 * lo + 1, valid, stride=2), :] = oo[:valid, lanes]
    if not first:
        for k, lanes in lane_slabs:
            o_ref[0, :, lanes] = nat_ref[k, :L, :]


def _long_conv(dims, fwd, inv, u_arr, u_col0, x_arr, x_col0, short_w, short_b, short_u_col0, short_x_col0,
               skip, spectrum, k_col0, width):
    B = u_arr.shape[0]
    L = dims.L
    cw = MXU_DIM
    ub, xb, kb = u_col0 // cw, x_col0 // cw, k_col0 // cw
    first = short_u_col0 is not None
    fwd_spec = pl.BlockSpec((dims.FM, dims.HK), lambda c, b: (0, 0), pipeline_mode=pl.Buffered(1))
    inv_spec = pl.BlockSpec((dims.HP, dims.IK), lambda c, b: (0, 0), pipeline_mode=pl.Buffered(1))

    def taps(col0):
        blk = col0 // cw
        return [pl.BlockSpec((HY_SHORT, cw), lambda c, b: (0, blk + c)),
                pl.BlockSpec((1, cw), lambda c, b: (0, blk + c))]

    if first:
        u_spec = pl.BlockSpec((1, L, cw), lambda c, b: (b, 0, ub + c))
        out_spec = pl.BlockSpec((1, 2, dims.HP, cw), lambda c, b: (b, 0, 0, c))
        out_shape = jax.ShapeDtypeStruct((B, 2, dims.HP, width), BF16)
    else:
        u_spec = pl.BlockSpec((1, 2, dims.HP, cw), lambda c, b: (b, 0, 0, ub + c))
        out_spec = pl.BlockSpec((1, L, cw), lambda c, b: (b, 0, c))
        out_shape = jax.ShapeDtypeStruct((B, L, width), F32)
    in_specs = [fwd_spec] * 4 + [inv_spec] * 2 + [u_spec, pl.BlockSpec((1, L, cw), lambda c, b: (b, 0, xb + c))]
    args = list(fwd) + list(inv) + [u_arr, x_arr]
    for col0 in ([short_u_col0] if first else []) + [short_x_col0]:
        in_specs += taps(col0)
        args += [short_w, short_b]
    in_specs += [pl.BlockSpec((1, cw), lambda c, b: (0, c))]
    in_specs += [pl.BlockSpec((dims.FM, cw), lambda c, b: (0, kb + c))] * 4
    args += [skip.reshape(1, width)] + list(spectrum)
    return pl.pallas_call(
        functools.partial(_conv_body, dims=dims, first=first),
        grid=(width // cw, B),
        in_specs=in_specs,
        out_specs=out_spec,
        out_shape=out_shape,
        scratch_shapes=([pltpu.VMEM((dims.HK, cw), BF16)] * 2 + [pltpu.VMEM((dims.HP, cw), BF16)] * 2
                        + [pltpu.VMEM((dims.IK, cw), BF16)] * 2
                        + [pltpu.VMEM((cw // LANES, 2 * dims.HP, LANES), F32)] * 2),
        compiler_params=_params(("parallel", "arbitrary")),
        name="hyena_long_conv",
    )(*args)


def _filter_body(z_ref, w1_ref, b1_ref, w2_ref, b2_ref, fr_ref, w3f_ref, w3b_ref, dec_ref, hs_ref, hd_ref,
                 h2_ref, nat_ref, *, dims):
    hp = lax.Precision.HIGHEST

    @pl.when(pl.program_id(0) == 0)
    def _():
        fr = fr_ref[...]
        h1 = jnp.sin(fr * (jnp.dot(z_ref[...], w1_ref[...], precision=hp, preferred_element_type=F32)
                           + b1_ref[...]))
        h2_ref[...] = jnp.sin(fr * (jnp.dot(h1, w2_ref[...], precision=hp, preferred_element_type=F32)
                                    + b2_ref[...]))

    h2 = h2_ref[...]
    dec = dec_ref[...]
    hf = jnp.dot(h2, w3f_ref[...], precision=hp, preferred_element_type=F32) * dec
    hb = jnp.dot(h2, w3b_ref[...], precision=hp, preferred_element_type=F32) * dec
    hb = jnp.where(lax.broadcasted_iota(jnp.int32, hb.shape, 0) == 0, 0.0, hb)
    scale = lax.rsqrt(jnp.sum(hf * hf + hb * hb, axis=0, keepdims=True) + 1e-6)
    for ref, val in ((hs_ref, (hf + hb) * scale), (hd_ref, (hf - hb) * scale)):
        for k in range(val.shape[1] // LANES):
            lanes = slice(k * LANES, (k + 1) * LANES)
            nat_ref[k] = val[:, lanes]
            ref[0, :, lanes] = nat_ref[k, pl.ds(0, dims.H, stride=2), :]
            ref[1, :, lanes] = nat_ref[k, pl.ds(1, dims.H, stride=2), :]


def _filter_constants(L, width):
    t = np.linspace(0.0, 1.0, L)
    w = (2.0 * math.pi / L) * np.arange(L)
    bands = np.linspace(1e-4, HY_BANDS - 1, HY_BANDS)
    fw = w[:, None] * bands[None, :]
    z = np.concatenate([t[:, None], np.cos(fw), -np.sin(fw)], axis=-1)
    z = np.pad(z, ((0, 0), (0, LANES - z.shape[1])))
    deltas = np.abs(np.linspace(HY_MIN_DECAY, HY_MAX_DECAY, width))
    return z.astype(np.float32), np.exp(-t[:, None] * deltas[None, :]).astype(np.float32)


def _hyena_filters(dims, consts, w1, b1, w2, b2, w3, freq):
    L = dims.L
    z, dec = consts
    hidden = w2.shape[0]
    width = w3.shape[1] // (2 * HY_ORDER)
    cw = MXU_DIM
    per_order = width // cw

    def lanes128(a):
        return jnp.pad(a.astype(F32), [(0, 0)] * (a.ndim - 1) + [(0, LANES - a.shape[-1])])

    w1p = jnp.pad(lanes128(w1), ((0, LANES - w1.shape[0]), (0, 0)))
    w2p = jnp.pad(lanes128(w2), ((0, LANES - hidden), (0, 0)))
    w3p = jnp.pad(w3.astype(F32), ((0, LANES - hidden), (0, 0)))
    row = lambda a: lanes128(a).reshape(1, LANES)
    full = lambda shape: pl.BlockSpec(shape, lambda j: (0, 0))
    fwd_col = lambda j: (0, (j // per_order) * 2 * per_order + j % per_order)
    bwd_col = lambda j: (0, (j // per_order) * 2 * per_order + per_order + j % per_order)
    out_spec = pl.BlockSpec((2, dims.H, cw), lambda j: (0, 0, j))
    return pl.pallas_call(
        functools.partial(_filter_body, dims=dims),
        grid=(HY_ORDER * per_order,),
        in_specs=[full((L, LANES)), full((LANES, LANES)), full((1, LANES)), full((LANES, LANES)),
                  full((1, LANES)), full((1, LANES)),
                  pl.BlockSpec((LANES, cw), fwd_col), pl.BlockSpec((LANES, cw), bwd_col),
                  pl.BlockSpec((L, cw), lambda j: (0, j % per_order))],
        out_specs=[out_spec, out_spec],
        out_shape=[jax.ShapeDtypeStruct((2, dims.H, HY_ORDER * width), F32)] * 2,
        scratch_shapes=[pltpu.VMEM((L, LANES), F32), pltpu.VMEM((cw // LANES, L, LANES), F32)],
        compiler_params=_params(("arbitrary",)),
        name="hyena_filters",
    )(z, w1p, row(b1), w2p, row(b2), row(freq), w3p, w3p, dec)


def _mix_body(ret_ref, hy_ref, gr_ref, gh_ref, h_ref, wr_ref, wh_ref, wo_ref, o_ref):
    ret = jnp.dot(ret_ref[...], wr_ref[...], preferred_element_type=F32)
    hyo = jnp.dot(hy_ref[...].astype(BF16), wh_ref[...], preferred_element_type=F32)
    merged = (jax.nn.sigmoid(gr_ref[...].astype(F32)) * ret
              + jax.nn.sigmoid(gh_ref[...].astype(F32)) * hyo)
    o_ref[...] = h_ref[...] + jnp.dot(merged.astype(BF16), wo_ref[...], preferred_element_type=F32)


def _mix_out(ret2, hy2, proj2, gate_col0, h2, w_ret_o, w_hy_o, w_out):
    T, D = h2.shape
    tm = _pick_tile(T, 384, BF16_SUBLANES)
    gb = gate_col0 // D
    row = lambda w: pl.BlockSpec((tm, w), lambda i: (i, 0))
    full = lambda a: pl.BlockSpec(a.shape, lambda i: (0, 0))
    return pl.pallas_call(
        _mix_body,
        grid=(T // tm,),
        in_specs=[
            row(ret2.shape[1]), row(hy2.shape[1]),
            pl.BlockSpec((tm, D), lambda i: (i, gb)),
            pl.BlockSpec((tm, D), lambda i: (i, gb + 1)),
            row(D), full(w_ret_o), full(w_hy_o), full(w_out),
        ],
        out_specs=row(D),
        out_shape=jax.ShapeDtypeStruct((T, D), F32),
        compiler_params=_params(("parallel",)),
        name="merge_outproj",
    )(ret2, hy2, proj2, proj2, h2, w_ret_o, w_hy_o, w_out)


EXPERT_TILE = 512
META_COLS = 8
SC_WINDOW = 64
SC_WORKERS = 32
SC_CHUNK = 2 * SC_WINDOW * SC_WORKERS
SC_SCATTER_WINDOW = 128


def _pack_bf16_pairs(x):
    n = x.shape[1] // 2
    xb = x.astype(BF16).astype(F32)
    hi = lax.bitcast_convert_type(xb[:, :n], jnp.uint32)
    lo = lax.bitcast_convert_type(xb[:, n:], jnp.uint32)
    return lax.bitcast_convert_type(hi | (lo >> 16), jnp.int32)


def _unpack_bf16_pairs(w):
    u = lax.bitcast_convert_type(w, jnp.uint32)
    hi = lax.bitcast_convert_type(u & jnp.uint32(0xFFFF0000), F32)
    lo = lax.bitcast_convert_type(u << 16, F32)
    return hi, lo


def _route(lt):
    assert EXP_PER_GROUP == 8 and N_GROUPS <= 8
    tm = lt.shape[1]
    row = lax.broadcasted_iota(jnp.int32, (8, tm), 0)
    neg = -jnp.inf
    big = jnp.int32(1 << 20)
    gl = jnp.where(row < N_GROUPS, lt[N_EXPERTS:N_EXPERTS + 8, :], neg)
    gmax = jnp.max(gl, axis=0, keepdims=True)
    p_top = 1.0 / jnp.sum(jnp.exp(gl - gmax), axis=0, keepdims=True)
    g_idx = jnp.min(jnp.where(gl == gmax, row, big), axis=0, keepdims=True)
    el = lt[0:8, :]
    for g in range(1, N_GROUPS):
        el = jnp.where(g_idx == g, lt[8 * g:8 * g + 8, :], el)
    m1 = jnp.max(el, axis=0, keepdims=True)
    i1 = jnp.min(jnp.where(el == m1, row, big), axis=0, keepdims=True)
    el2 = jnp.where(row == i1, neg, el)
    m2 = jnp.max(el2, axis=0, keepdims=True)
    i2 = jnp.min(jnp.where(el2 == m2, row, big), axis=0, keepdims=True)
    r = jnp.exp(m2 - m1)
    base = g_idx * EXP_PER_GROUP
    return base + i1, base + i2, p_top / (1.0 + r), p_top * r / (1.0 + r)


def _route_body(h_ref, g_ref, whi_ref, wlo_ref, xpk_ref, meta_ref, meta_t_ref, cnt_ref, carry_ref):
    i = pl.program_id(0)

    @pl.when(i == 0)
    def _():
        carry_ref[...] = jnp.zeros_like(carry_ref)

    xn = _rms(h_ref[...], g_ref[...])
    xpk_ref[...] = _pack_bf16_pairs(xn)
    x_hi = xn.astype(BF16)
    x_lo = (xn - x_hi.astype(F32)).astype(BF16)
    logits = (jnp.dot(x_hi, whi_ref[...], preferred_element_type=F32)
              + jnp.dot(x_lo, whi_ref[...], preferred_element_type=F32)
              + jnp.dot(x_hi, wlo_ref[...], preferred_element_type=F32))
    e0, e1, w0, w1 = _route(logits.T)
    tm = logits.shape[0]
    row = lax.broadcasted_iota(jnp.int32, (LANES, tm), 0)
    onehot_t = jnp.where((row == e0) | (row == e1), 1.0, 0.0).astype(BF16)
    ri = lax.broadcasted_iota(jnp.int32, (tm, tm), 0)
    ci = lax.broadcasted_iota(jnp.int32, (tm, tm), 1)
    earlier = jnp.where(ci < ri, 1.0, 0.0).astype(BF16)
    prefix = carry_ref[...] + _dot_nt(earlier, onehot_t)
    prefix_t = prefix.T
    r0 = jnp.sum(jnp.where(row == e0, prefix_t, 0.0), axis=0, keepdims=True)
    r1 = jnp.sum(jnp.where(row == e1, prefix_t, 0.0), axis=0, keepdims=True)
    row8 = lax.broadcasted_iota(jnp.int32, (META_COLS, tm), 0)
    meta_t = jnp.zeros((META_COLS, tm), F32)
    for c, val in enumerate((e0.astype(F32), e1.astype(F32), r0, r1, w0, w1)):
        meta_t = jnp.where(row8 == c, val, meta_t)
    meta_t_ref[...] = meta_t
    padded = jnp.concatenate([meta_t, jnp.zeros((LANES - META_COLS, tm), F32)], axis=0)
    meta_ref[...] = padded.T[:, :META_COLS]
    counts = _dot_nt(jnp.ones((8, tm), BF16), onehot_t)
    carry_ref[...] += counts[0:1, :]
    cnt_ref[...] = carry_ref[...]


def _moe_route(h2, gain, w_router):
    T, D = h2.shape
    tm = _pick_tile(T, 1152, LANES)
    return pl.pallas_call(
        _route_body,
        grid=(T // tm,),
        in_specs=[
            pl.BlockSpec((tm, D), lambda i: (i, 0)),
            pl.BlockSpec((1, D), lambda i: (0, 0)),
            pl.BlockSpec((D, LANES), lambda i: (0, 0)),
            pl.BlockSpec((D, LANES), lambda i: (0, 0)),
        ],
        out_specs=[
            pl.BlockSpec((tm, D // 2), lambda i: (i, 0)),
            pl.BlockSpec((tm, META_COLS), lambda i: (i, 0)),
            pl.BlockSpec((META_COLS, tm), lambda i: (0, i)),
            pl.BlockSpec((1, LANES), lambda i: (0, 0)),
        ],
        out_shape=[
            jax.ShapeDtypeStruct((T, D // 2), jnp.int32),
            jax.ShapeDtypeStruct((T, META_COLS), F32),
            jax.ShapeDtypeStruct((META_COLS, T), F32),
            jax.ShapeDtypeStruct((1, LANES), F32),
        ],
        scratch_shapes=[pltpu.VMEM((1, LANES), F32)],
        compiler_params=_params(("arbitrary",)),
        name="moe_route",
    )(h2, gain.reshape(1, D), *w_router)


def _sc_gather(table, idx):
    n = idx.shape[0]
    width = table.shape[1]
    win = SC_WINDOW
    assert n % SC_CHUNK == 0
    per_worker = n // SC_WORKERS
    mesh = plsc.VectorSubcoreMesh(core_axis_name="c", subcore_axis_name="s")

    @functools.partial(
        pl.kernel, out_type=jax.ShapeDtypeStruct((n, width), table.dtype), mesh=mesh,
        scratch_types=[pltpu.VMEM((per_worker,), jnp.int32), pltpu.VMEM((2, win, width), table.dtype),
                       pltpu.SemaphoreType.DMA((2,)), pltpu.SemaphoreType.DMA((2,))],
        name="sc_row_gather")
    def gather(table_hbm, idx_hbm, out_hbm, idx_v, rows_v, gsem, osem):
        worker = lax.axis_index("s") * mesh.num_cores + lax.axis_index("c")
        base = worker * per_worker
        pltpu.sync_copy(idx_hbm.at[pl.ds(base, per_worker)], idx_v)

        @pl.loop(0, per_worker, step=2 * win)
        def _(off):
            fetch = [pltpu.async_copy(table_hbm.at[idx_v.at[pl.ds(off + s * win, win)]], rows_v.at[s], gsem.at[s])
                     for s in range(2)]
            store = []
            for s in range(2):
                fetch[s].wait()
                store.append(pltpu.async_copy(rows_v.at[s], out_hbm.at[pl.ds(base + off + s * win, win)],
                                              osem.at[s]))
            for s in range(2):
                store[s].wait()

    return gather(table, idx)


def _sc_dispatch(table, dest, n_out):
    n_rows, width = table.shape
    win = SC_SCATTER_WINDOW
    assert n_rows % win == 0
    n_win = n_rows // win
    per_worker = pl.cdiv(n_win, SC_WORKERS)
    idx = jnp.pad(dest, ((0, 0), (0, per_worker * SC_WORKERS * win - n_rows)))
    idx = idx.reshape(2, per_worker, SC_WORKERS, win).transpose(0, 2, 1, 3)
    mesh = plsc.VectorSubcoreMesh(core_axis_name="c", subcore_axis_name="s")

    @functools.partial(
        pl.kernel, out_type=jax.ShapeDtypeStruct((n_out, width), table.dtype), mesh=mesh,
        scratch_types=[pltpu.VMEM((per_worker, win), jnp.int32), pltpu.VMEM((per_worker, win), jnp.int32),
                       pltpu.VMEM((win, width), table.dtype),
                       pltpu.SemaphoreType.DMA, pltpu.SemaphoreType.DMA],
        name="sc_row_dispatch")
    def dispatch(table_hbm, idx_hbm, out_hbm, idx0_v, idx1_v, rows_v, sem0, sem1):
        worker = lax.axis_index("s") * mesh.num_cores + lax.axis_index("c")
        pltpu.sync_copy(idx_hbm.at[0, worker], idx0_v)
        pltpu.sync_copy(idx_hbm.at[1, worker], idx1_v)

        @pl.loop(0, per_worker)
        def _(j):
            window = j * SC_WORKERS + worker

            @pl.when(window < n_win)
            def _():
                pltpu.sync_copy(table_hbm.at[pl.ds(window * win, win)], rows_v)
                first = pltpu.async_copy(rows_v, out_hbm.at[idx0_v.at[j]], sem0)
                second = pltpu.async_copy(rows_v, out_hbm.at[idx1_v.at[j]], sem1)
                first.wait()
                second.wait()

    return dispatch(table, idx)


def _expert_body(te_ref, tv_ref, x_ref, wg_ref, wu_ref, wd_ref, y_ref, wg_s, wu_s, wd_s):
    t = pl.program_id(0)
    half = x_ref.shape[1]

    @pl.when((t == 0) | (te_ref[t] != te_ref[jnp.maximum(t - 1, 0)]))
    def _():
        wg_s[...] = wg_ref[0].astype(BF16)
        wu_s[...] = wu_ref[0].astype(BF16)
        wd_s[...] = wd_ref[0].astype(BF16)

    @pl.when(tv_ref[t] > 0)
    def _():
        sub = x_ref.shape[0] // 2
        for r0 in (0, sub):
            row = r0 + lax.broadcasted_iota(jnp.int32, (sub, half), 0)
            hi, lo = _unpack_bf16_pairs(jnp.where(row < tv_ref[t], x_ref[r0:r0 + sub, :], 0))
            hi, lo = hi.astype(BF16), lo.astype(BF16)
            hg = jnp.dot(hi, wg_s[:half, :], preferred_element_type=F32)
            hg += jnp.dot(lo, wg_s[half:, :], preferred_element_type=F32)
            hu = jnp.dot(hi, wu_s[:half, :], preferred_element_type=F32)
            hu += jnp.dot(lo, wu_s[half:, :], preferred_element_type=F32)
            act = (hg * jax.nn.sigmoid(hg) * hu).astype(BF16)
            y_ref[r0:r0 + sub, :] = _pack_bf16_pairs(jnp.dot(act, wd_s[...], preferred_element_type=F32))

    @pl.when(tv_ref[t] == 0)
    def _():
        y_ref[...] = jnp.zeros_like(y_ref)


def _moe_experts(xs, n_sorted, tile_expert, tile_valid, w_gate, w_up, w_down):
    NP, half = n_sorted, xs.shape[1]
    _, D, FF = w_gate.shape
    tr = EXPERT_TILE
    grid_spec = pltpu.PrefetchScalarGridSpec(
        num_scalar_prefetch=2,
        grid=(NP // tr,),
        in_specs=[
            pl.BlockSpec((tr, half), lambda t, te, tv: (t, 0)),
            pl.BlockSpec((1, D, FF), lambda t, te, tv: (te[t], 0, 0)),
            pl.BlockSpec((1, D, FF), lambda t, te, tv: (te[t], 0, 0)),
            pl.BlockSpec((1, FF, D), lambda t, te, tv: (te[t], 0, 0)),
        ],
        out_specs=pl.BlockSpec((tr, half), lambda t, te, tv: (t, 0)),
        scratch_shapes=[pltpu.VMEM((D, FF), BF16), pltpu.VMEM((D, FF), BF16), pltpu.VMEM((FF, D), BF16)],
    )
    return pl.pallas_call(
        _expert_body,
        grid_spec=grid_spec,
        out_shape=jax.ShapeDtypeStruct((NP, half), jnp.int32),
        compiler_params=_params(("arbitrary",)),
        name="moe_experts",
    )(tile_expert, tile_valid, xs, w_gate, w_up, w_down)


def _moe_combine_into(o_ref, h, meta, y0, y1):
    half = y0.shape[1]
    w0 = meta[:, 4:5]
    w1 = meta[:, 5:6]
    hi0, lo0 = _unpack_bf16_pairs(y0)
    hi1, lo1 = _unpack_bf16_pairs(y1)
    o_ref[:, :half] = h[:, :half] + w0 * hi0 + w1 * hi1
    o_ref[:, half:] = h[:, half:] + w0 * lo0 + w1 * lo1


def _moe(h2, gain, w_router, layer, w_gate, w_up, w_down):
    T, D = h2.shape
    E = N_EXPERTS
    tr = EXPERT_TILE
    n_sorted = _round_up(2 * T + E * (tr - 1), tr)
    t_pad = _round_up(T, SC_CHUNK // 2)

    xpk, meta, meta_t, counts = _moe_route(h2, gain, w_router)

    cnt = counts[0, :E].astype(jnp.int32)
    padded = (cnt + tr - 1) // tr * tr
    ends = jnp.cumsum(padded)
    starts = ends - padded
    eid = meta_t[0:2].astype(jnp.int32)
    pos = meta_t[2:4].astype(jnp.int32)
    for e in range(E):
        pos = pos + jnp.where(eid == e, starts[e], 0)
    tile_start = jnp.arange(n_sorted // tr, dtype=jnp.int32) * tr
    tile_expert = jnp.minimum(jnp.sum(tile_start[:, None] >= ends[None, :], axis=1), E - 1).astype(jnp.int32)
    tile_valid = jnp.clip(cnt[tile_expert] - (tile_start - starts[tile_expert]), 0, tr).astype(jnp.int32)

    xs = _sc_dispatch(xpk, pos, n_sorted)
    ys = _moe_experts(xs, n_sorted, layer * E + tile_expert, tile_valid, w_gate, w_up, w_down)
    spare = jnp.arange(t_pad - T, dtype=jnp.int32)
    back = jnp.concatenate([pos, jnp.broadcast_to(spare[None], (2, t_pad - T))], axis=1)
    yg = _sc_gather(ys, back.reshape(-1)).reshape(2, t_pad, D // 2)
    return meta, yg


def _final_body(h_ref, meta_ref, y0_ref, y1_ref, g_ref, o_ref, hn_ref):
    _moe_combine_into(hn_ref, h_ref[0], meta_ref[0], y0_ref[0], y1_ref[0])
    o_ref[0] = _rms(hn_ref[N_META:, :], g_ref[...])


def _final_norm(h3, meta3, yg, gain, b0, nb):
    _, L, D = h3.shape
    return pl.pallas_call(
        _final_body,
        grid=(nb,),
        in_specs=[pl.BlockSpec((1, L, D), lambda b: (b0 + b, 0, 0)),
                  pl.BlockSpec((1, L, meta3.shape[2]), lambda b: (b0 + b, 0, 0)),
                  pl.BlockSpec((1, L, D // 2), lambda b: (0, b0 + b, 0)),
                  pl.BlockSpec((1, L, D // 2), lambda b: (1, b0 + b, 0)),
                  pl.BlockSpec((1, D), lambda b: (0, 0))],
        out_specs=pl.BlockSpec((1, L - N_META, D), lambda b: (b, 0, 0)),
        out_shape=jax.ShapeDtypeStruct((nb, L - N_META, D), F32),
        scratch_shapes=[pltpu.VMEM((L, D), F32)],
        compiler_params=_params(("parallel",)),
        name="combine_final_norm",
    )(h3, meta3, yg, yg, gain.reshape(1, D))


def kernel(x_prompt, x_sample, meta_tokens, norm_mix, w_in, ret_decay_fwd, ret_decay_bwd, hy_short_w, hy_short_b, hy_filt_w1, hy_filt_b1, hy_filt_w2, hy_filt_b2, hy_filt_w3, hy_sin_freq, hy_skip, w_ret_o, w_hy_o, w_out, norm_ffn, router_group, router_expert, moe_w_gate, moe_w_up, moe_w_down, norm_final):
    assert x_prompt.shape[1:] == x_sample.shape[1:]
    nbp, nbs = x_prompt.shape[0], x_sample.shape[0]
    B = nbp + nbs
    D = x_prompt.shape[2]
    L = N_META + x_prompt.shape[1]
    T = B * L
    depth = w_in.shape[0]
    q_w = RET_HEADS * RET_DK
    v_w = RET_HEADS * RET_DV
    hy_w = hy_skip.shape[2]
    hy_col0 = 2 * q_w + 2 * v_w
    gate_col0 = hy_col0 + 3 * hy_w
    assert D == q_w and w_in.shape[2] == gate_col0 + 2 * D

    x = jnp.concatenate([x_prompt, x_sample], axis=0)
    meta = jnp.broadcast_to(meta_tokens[None].astype(x.dtype), (B, N_META, D))
    h = jnp.concatenate([meta, x], axis=1).reshape(T, D)

    cos_t, sin_t = _rotary_tables(L)
    dims = _HyenaDims(L)
    dft_fwd, dft_inv = _dft_tables(dims)
    filt_consts = _filter_constants(L, hy_w)
    router = jnp.concatenate([router_expert, router_group], axis=2).astype(F32)
    router = jnp.pad(router, ((0, 0), (0, 0), (0, LANES - router.shape[2])))
    router_hi = router.astype(BF16)
    router_lo = (router - router_hi.astype(F32)).astype(BF16)
    assert moe_w_gate.shape[1] == N_EXPERTS
    w_gate_all = moe_w_gate.reshape((-1,) + moe_w_gate.shape[2:])
    w_up_all = moe_w_up.reshape((-1,) + moe_w_up.shape[2:])
    w_down_all = moe_w_down.reshape((-1,) + moe_w_down.shape[2:])

    moe = None
    for i in range(depth):
        proj, h = _inproj(h, norm_mix[i], w_in[i].astype(BF16), moe)
        proj3 = proj.reshape(B, L, -1)

        lg = jnp.stack([jax.nn.log_sigmoid(ret_decay_fwd[i].astype(F32)),
                        jax.nn.log_sigmoid(ret_decay_bwd[i].astype(F32))])
        ret = _retention(proj3, lg, cos_t, sin_t)

        hs, hd = _hyena_filters(dims, filt_consts, hy_filt_w1[i], hy_filt_b1[i], hy_filt_w2[i], hy_filt_b2[i],
                                hy_filt_w3[i], hy_sin_freq[i])
        spectrum = _filter_spectrum(dims, dft_fwd, hs, hd)
        sw = hy_short_w[i].astype(F32)
        sb = hy_short_b[i].astype(F32).reshape(1, -1)
        z = _long_conv(dims, dft_fwd, dft_inv, proj3, hy_col0, proj3, hy_col0 + hy_w, sw, sb, 0, hy_w,
                       hy_skip[i, 0].astype(F32), spectrum, 0, hy_w)
        z = _long_conv(dims, dft_fwd, dft_inv, z, 0, proj3, hy_col0 + 2 * hy_w, sw, sb, None, 2 * hy_w,
                       hy_skip[i, 1].astype(F32), spectrum, hy_w, hy_w)

        h = _mix_out(ret.reshape(T, v_w), z.reshape(T, hy_w), proj, gate_col0, h,
                     w_ret_o[i].astype(BF16), w_hy_o[i].astype(BF16), w_out[i].astype(BF16))
        moe = _moe(h, norm_ffn[i], (router_hi[i], router_lo[i]), i, w_gate_all, w_up_all, w_down_all)

    h3 = h.reshape(B, L, D)
    meta3 = moe[0].reshape(B, L, -1)
    return (_final_norm(h3, meta3, moe[1], norm_final, 0, nbp),
            _final_norm(h3, meta3, moe[1], norm_final, nbp, nbs))
```

```python
import functools
import math

import jax
import jax.numpy as jnp
import numpy as np
from jax import lax
from jax.experimental import pallas as pl
from jax.experimental.pallas import tpu as pltpu
from jax.experimental.pallas import tpu_sc as plsc

N_META = 16
RET_HEADS = 8
RET_DK = 128
RET_DV = 256
ROPE_THETA = 10000.0
HY_ORDER = 2
HY_SHORT = 3
HY_EMB = 33
HY_BANDS = (HY_EMB - 1) // 2
HY_DECAY_TARGET = 1e-2
HY_MIN_DECAY = math.log(HY_DECAY_TARGET) / 1.5
HY_MAX_DECAY = math.log(HY_DECAY_TARGET) / 0.3
N_GROUPS = 4
EXP_PER_GROUP = 8
N_EXPERTS = N_GROUPS * EXP_PER_GROUP
RMS_EPS = 1e-6

LANES = 128
BF16_SUBLANES = 16
MXU_DIM = 256
RET_CHUNK = 256
VMEM_LIMIT = 56 * 1024 * 1024

F32 = jnp.float32
BF16 = jnp.bfloat16


def _round_up(n, m):
    return (n + m - 1) // m * m


def _pick_tile(n, target, mult):
    best = None
    for t in range(mult, min(n, target) + 1, mult):
        if n % t == 0:
            best = t
    assert best is not None, (n, target, mult)
    return best


def _params(sem):
    return pltpu.CompilerParams(dimension_semantics=sem, vmem_limit_bytes=VMEM_LIMIT)


def _rms(x, gain):
    ms = jnp.mean(x * x, axis=-1, keepdims=True)
    return x * lax.rsqrt(ms + RMS_EPS) * gain


def _inproj_body(h_ref, g_ref, w_ref, o_ref, xn_ref):
    @pl.when(pl.program_id(1) == 0)
    def _():
        xn_ref[...] = _rms(h_ref[...], g_ref[...]).astype(BF16)

    o_ref[...] = jnp.dot(xn_ref[...], w_ref[...], preferred_element_type=F32).astype(o_ref.dtype)


def _inproj_moe_body(h_ref, meta_ref, y0_ref, y1_ref, g_ref, w_ref, o_ref, hn_ref, xn_ref):
    @pl.when(pl.program_id(1) == 0)
    def _():
        _moe_combine_into(hn_ref, h_ref[...], meta_ref[...], y0_ref[0], y1_ref[0])
        xn_ref[...] = _rms(hn_ref[...], g_ref[...]).astype(BF16)

    o_ref[...] = jnp.dot(xn_ref[...], w_ref[...], preferred_element_type=F32).astype(o_ref.dtype)


def _inproj(h2, gain, w, moe=None):
    T, D = h2.shape
    nc = w.shape[1]
    tm = _pick_tile(T, 1152, BF16_SUBLANES)
    tn = _pick_tile(nc, 1024, LANES)
    common = dict(
        grid=(T // tm, nc // tn),
        scratch_shapes=[pltpu.VMEM((tm, D), BF16)],
        compiler_params=_params(("parallel", "arbitrary")),
    )
    h_spec = pl.BlockSpec((tm, D), lambda i, j: (i, 0))
    g_spec = pl.BlockSpec((1, D), lambda i, j: (0, 0))
    w_spec = pl.BlockSpec((D, tn), lambda i, j: (0, j))
    o_spec = pl.BlockSpec((tm, tn), lambda i, j: (i, j))
    o_shape = jax.ShapeDtypeStruct((T, nc), BF16)
    if moe is None:
        proj = pl.pallas_call(_inproj_body, in_specs=[h_spec, g_spec, w_spec], out_specs=o_spec,
                              out_shape=o_shape, name="norm_inproj", **common)(h2, gain.reshape(1, D), w)
        return proj, h2
    meta, yg = moe
    return pl.pallas_call(
        _inproj_moe_body,
        in_specs=[h_spec, pl.BlockSpec((tm, meta.shape[1]), lambda i, j: (i, 0)),
                  pl.BlockSpec((1, tm, D // 2), lambda i, j: (0, i, 0)),
                  pl.BlockSpec((1, tm, D // 2), lambda i, j: (1, i, 0)),
                  g_spec, w_spec],
        out_specs=[o_spec, h_spec],
        out_shape=[o_shape, jax.ShapeDtypeStruct((T, D), F32)],
        name="combine_norm_inproj", **common,
    )(h2, meta, yg, yg, gain.reshape(1, D), w)


def _dot_t(a, b):
    return lax.dot_general(a, b, (((0,), (0,)), ((), ())), preferred_element_type=F32)


def _dot_nt(a, b):
    return lax.dot_general(a, b, (((1,), (1,)), ((), ())), preferred_element_type=F32)


def _ret_body(lg_ref, q_ref, k_ref, v_ref, g_ref, cos_ref, sin_ref, o_ref, rb_ref, kr_ref, *, seq_len):
    C = RET_CHUNK
    L = seq_len
    n_chunks = pl.cdiv(L, C)
    head = pl.program_id(1)
    lgf = lg_ref[0, head]
    lgb = lg_ref[1, head]

    def chunk(ref, n):
        lo, hi = n * C, min((n + 1) * C, L)
        x = ref[0, lo:hi, :]
        if hi - lo < C:
            x = jnp.concatenate([x, jnp.zeros((C - (hi - lo), x.shape[1]), x.dtype)], axis=0)
        return x

    def rotary(ref, n):
        x = chunk(ref, n).astype(F32)
        sl = slice(n * C, (n + 1) * C)
        return x * cos_ref[sl, :] + pltpu.roll(x, RET_DK // 2, 1) * sin_ref[sl, :]

    row = lax.broadcasted_iota(jnp.int32, (C, LANES), 0).astype(F32)
    ri = lax.broadcasted_iota(jnp.int32, (C, C), 0).astype(F32)
    ci = lax.broadcasted_iota(jnp.int32, (C, C), 1).astype(F32)
    decay = jnp.exp(jnp.where(ci <= ri, (ri - ci) * lgf, (ci - ri) * lgb))
    qf_dec = jnp.exp((row + 1.0) * lgf)
    qb_dec = jnp.exp((C - row) * lgb)
    kf_dec = jnp.exp((C - 1.0 - row) * lgf)
    kb_dec = jnp.exp(row * lgb)
    cf = jnp.exp(C * lgf)
    cb = jnp.exp(C * lgb)

    state = jnp.zeros((RET_DK, RET_DV), F32)
    for n in reversed(range(n_chunks)):
        rb_ref[n] = state.astype(BF16)
        if n > 0:
            kr_ref[n] = rotary(k_ref, n)
            kb = (kr_ref[n] * kb_dec).astype(BF16)
            state = cb * state + _dot_t(kb, chunk(v_ref, n))

    state = jnp.zeros((RET_DK, RET_DV), F32)
    for n in range(n_chunks):
        q = rotary(q_ref, n) * (RET_DK ** -0.5)
        k = kr_ref[n] if n > 0 else rotary(k_ref, n)
        v = chunk(v_ref, n)
        scores = _dot_nt(q.astype(BF16), k.astype(BF16)) * decay
        o = jnp.dot(scores.astype(BF16), v, preferred_element_type=F32)
        o += jnp.dot((q * qf_dec).astype(BF16), state.astype(BF16), preferred_element_type=F32)
        o += jnp.dot((q * qb_dec).astype(BF16), rb_ref[n], preferred_element_type=F32)
        if n + 1 < n_chunks:
            state = cf * state + _dot_t((k * kf_dec).astype(BF16), v)
        o = o * lax.rsqrt(jnp.mean(o * o, axis=-1, keepdims=True) + RMS_EPS)
        lo, hi = n * C, min((n + 1) * C, L)
        g = g_ref[0, lo:hi, :].astype(F32)
        o_ref[0, lo:hi, :] = (g * jax.nn.sigmoid(g) * o[: hi - lo]).astype(o_ref.dtype)


def _retention(proj3, lg, cos_t, sin_t):
    B, L, _ = proj3.shape
    n_chunks = pl.cdiv(L, RET_CHUNK)
    lp = n_chunks * RET_CHUNK
    k_blk = RET_HEADS
    v_blk = 2 * RET_HEADS * RET_DK // RET_DV
    g_blk = v_blk + RET_HEADS
    return pl.pallas_call(
        functools.partial(_ret_body, seq_len=L),
        grid=(B, RET_HEADS),
        in_specs=[
            pl.BlockSpec(memory_space=pltpu.SMEM),
            pl.BlockSpec((1, L, RET_DK), lambda b, h: (b, 0, h)),
            pl.BlockSpec((1, L, RET_DK), lambda b, h: (b, 0, k_blk + h)),
            pl.BlockSpec((1, L, RET_DV), lambda b, h: (b, 0, v_blk + h)),
            pl.BlockSpec((1, L, RET_DV), lambda b, h: (b, 0, g_blk + h)),
            pl.BlockSpec((lp, RET_DK), lambda b, h: (0, 0)),
            pl.BlockSpec((lp, RET_DK), lambda b, h: (0, 0)),
        ],
        out_specs=pl.BlockSpec((1, L, RET_DV), lambda b, h: (b, 0, h)),
        out_shape=jax.ShapeDtypeStruct((B, L, RET_HEADS * RET_DV), BF16),
        scratch_shapes=[pltpu.VMEM((n_chunks, RET_DK, RET_DV), BF16),
                        pltpu.VMEM((n_chunks, RET_CHUNK, RET_DK), F32)],
        compiler_params=_params(("parallel", "arbitrary")),
        name="retention",
    )(lg, proj3, proj3, proj3, proj3, cos_t, sin_t)


def _rotary_tables(L):
    half = RET_DK // 2
    lp = _round_up(L, RET_CHUNK)
    inv = ROPE_THETA ** (-jnp.arange(half, dtype=F32) / half)
    ang = jnp.arange(lp, dtype=F32)[:, None] * inv[None, :]
    cos, sin = jnp.cos(ang), jnp.sin(ang)
    return jnp.concatenate([cos, cos], axis=1), jnp.concatenate([-sin, sin], axis=1)


def _row_tiles(n, tile):
    return [(s, min(tile, n - s)) for s in range(0, n, tile)]


RADIX = 4


class _HyenaDims:
    def __init__(self, L):
        assert L % RADIX == 0
        self.L = L
        self.Q = L // RADIX
        self.F = self.Q + 1
        self.QP = _round_up(self.Q, BF16_SUBLANES)
        self.QK = _round_up(self.Q, MXU_DIM)
        self.FM = _round_up(self.F, BF16_SUBLANES)
        self.IK = _round_up(2 * self.FM, MXU_DIM)


def _dft_tables(dims):
    n = 2 * dims.L

    def trig(f, t, valid):
        ang = ((f * t) % n) * (2.0 * math.pi / n)
        return (np.where(valid, np.cos(ang), 0.0).astype(BF16), np.where(valid, np.sin(ang), 0.0).astype(BF16))

    f = np.arange(dims.FM, dtype=np.int64)[:, None]
    s = np.arange(dims.QK, dtype=np.int64)[None, :]
    ok = (f < dims.F) & (s < dims.Q)
    fwd = [trig(f, RADIX * s + r, ok) for r in range(RADIX)]

    s = np.arange(dims.QP, dtype=np.int64)[:, None]
    j = np.arange(dims.IK, dtype=np.int64)[None, :]
    f = j % dims.FM
    ok = (f < dims.F) & (s < dims.Q)
    inv = []
    for r in range(RADIX):
        cos, sin = trig(f, RADIX * s + r, ok)
        inv.append(np.where(j < dims.FM, cos, np.where(j < 2 * dims.FM, sin, np.zeros_like(cos))))
    return [t for pair in fwd for t in pair], inv


def _forward_dft(tables, parts, rows):
    p = [jnp.dot(tables[2 * r][rows, :], parts[r][...], preferred_element_type=F32) for r in range(RADIX)]
    a = [jnp.dot(tables[2 * r + 1][rows, :], parts[r][...], preferred_element_type=F32) for r in range(RADIX)]
    ps02, pd02, ps13, pd13 = p[0] + p[2], p[0] - p[2], p[1] + p[3], p[1] - p[3]
    as02, ad02, as13, ad13 = a[0] + a[2], a[0] - a[2], a[1] + a[3], a[1] - a[3]
    return [(ps02 + ps13, as02 + as13),
            (pd02 + ad13, pd13 - ad02),
            (pd02 - ad13, pd13 + ad02),
            (ps02 - ps13, as13 - as02)]


def _class_weights(dims):
    f = lax.broadcasted_iota(jnp.int32, (dims.FM, 1), 0)
    n = 2.0 * dims.L
    edge = jnp.where(f == 0, 1.0, 2.0) / n
    return [jnp.where(f <= dims.Q, edge, 0.0),
            jnp.where(f < dims.Q, 2.0 / n, 0.0),
            jnp.where((f >= 1) & (f <= dims.Q), 2.0 / n, 0.0),
            jnp.where(f < dims.Q, edge, 0.0)]


def _spec_body(*refs, dims):
    tables = refs[:2 * RADIX]
    hs_ref, hd_ref = refs[2 * RADIX:2 * RADIX + 2]
    outs = refs[2 * RADIX + 2:4 * RADIX + 2]
    s_pads = refs[4 * RADIX + 2:5 * RADIX + 2]
    d_pads = refs[5 * RADIX + 2:]
    Q = dims.Q
    for pads, src in ((s_pads, hs_ref), (d_pads, hd_ref)):
        for r in range(RADIX):
            pads[r][Q:, :] = jnp.zeros((pads[r].shape[0] - Q, pads[r].shape[1]), BF16)
            pads[r][:Q, :] = src[r].astype(BF16)
    rows = slice(None)
    re = [c[0] for c in _forward_dft(tables, s_pads, rows)]
    nim = [c[1] for c in _forward_dft(tables, d_pads, rows)]
    for c, w in enumerate(_class_weights(dims)):
        outs[2 * c][...] = w * re[c]
        outs[2 * c + 1][...] = -w * nim[c]


def _filter_spectrum(dims, fwd, hs, hd):
    width = hs.shape[2]
    cw = MXU_DIM
    resident = pl.BlockSpec((dims.FM, dims.QK), lambda j: (0, 0), pipeline_mode=pl.Buffered(1))
    return pl.pallas_call(
        functools.partial(_spec_body, dims=dims),
        grid=(width // cw,),
        in_specs=[resident] * (2 * RADIX) + [pl.BlockSpec((RADIX, dims.Q, cw), lambda j: (0, 0, j))] * 2,
        out_specs=[pl.BlockSpec((dims.FM, cw), lambda j: (0, j))] * (2 * RADIX),
        out_shape=[jax.ShapeDtypeStruct((dims.FM, width), F32)] * (2 * RADIX),
        scratch_shapes=[pltpu.VMEM((dims.QK, cw), BF16)] * (2 * RADIX),
        compiler_params=_params(("arbitrary",)),
        name="hyena_filter_spectrum",
    )(*fwd, hs, hd)


def _short_conv(ref, w_ref, b_ref, anchor=None):
    u = ref[0].astype(F32)
    if anchor is not None:
        u = u + anchor
    L = u.shape[0]
    t = lax.broadcasted_iota(jnp.int32, u.shape, 0)
    prev = jnp.where(t == 0, 0.0, pltpu.roll(u, 1, 0))
    nxt = jnp.where(t == L - 1, 0.0, pltpu.roll(u, L - 1, 0))
    w = w_ref[...]
    return b_ref[...] + w[0:1] * prev + w[1:2] * u + w[2:3] * nxt


def _conv_body(*refs, dims, first):
    n_tab = 2 * RADIX
    tables = refs[:n_tab]
    inv_refs = refs[n_tab:n_tab + RADIX]
    u_ref, x_ref = refs[n_tab + RADIX:n_tab + RADIX + 2]
    at = n_tab + RADIX + 2
    n_taps = 4 if first else 2
    taps = refs[at:at + n_taps]
    at += n_taps
    skip_ref = refs[at]
    spec_refs = refs[at + 1:at + 1 + 2 * RADIX]
    o_ref = refs[at + 1 + 2 * RADIX]
    scratch = refs[at + 2 + 2 * RADIX:]
    u_parts, x_parts, g_parts = scratch[:RADIX], scratch[RADIX:2 * RADIX], scratch[2 * RADIX:3 * RADIX]
    nat_ref, natx_ref = scratch[3 * RADIX:]
    L, Q, QP, QK, FM, IK = dims.L, dims.Q, dims.QP, dims.QK, dims.FM, dims.IK
    cw = u_parts[0].shape[1]

    lane_slabs = [(k, slice(k * LANES, (k + 1) * LANES)) for k in range(cw // LANES)]

    def split_into(parts, value, stage_ref):
        for k, lanes in lane_slabs:
            stage_ref[k, L:, :] = jnp.zeros((RADIX * QP - L, LANES), F32)
            stage_ref[k, :L, :] = value[:, lanes]
            for r in range(RADIX):
                parts[r][:QP, lanes] = stage_ref[k, pl.ds(r, QP, stride=RADIX), :].astype(BF16)

    for r in range(RADIX):
        if QK > QP:
            u_parts[r][QP:, :] = jnp.zeros((QK - QP, cw), BF16)
        if not first:
            u_parts[r][:QP, :] = u_ref[0, r]
    if first:
        split_into(u_parts, _short_conv(u_ref, taps[0], taps[1]), nat_ref)

    for r in range(RADIX):
        if IK > 2 * FM:
            g_parts[r][2 * FM:, :] = jnp.zeros((IK - 2 * FM, cw), BF16)
    for lo, sz in _row_tiles(FM, MXU_DIM):
        rows = slice(lo, lo + sz)
        e1, e2 = [], []
        for c, (p, a) in enumerate(_forward_dft(tables, u_parts, rows)):
            k1, k2 = spec_refs[2 * c][rows, :], spec_refs[2 * c + 1][rows, :]
            e1.append(p * k1 + a * k2)
            e2.append(a * k1 - p * k2)
        for r, (gc, gs) in enumerate(_fold_classes(e1, e2)):
            g_parts[r][lo:lo + sz, :] = gc.astype(BF16)
            g_parts[r][FM + lo:FM + lo + sz, :] = gs.astype(BF16)

    anchor = g_parts[RADIX - 1][2 * FM - 1:2 * FM, :].astype(F32) * 0.0
    split_into(x_parts, _short_conv(x_ref, taps[-2], taps[-1], anchor), natx_ref)
    skip = skip_ref[...]
    for lo, sz in _row_tiles(QP, MXU_DIM):
        rows = slice(lo, lo + sz)
        valid = min(sz, Q - lo)
        for r in range(RADIX):
            y = jnp.dot(inv_refs[r][rows, :], g_parts[r][...], preferred_element_type=F32)
            o = x_parts[r][rows, :].astype(F32) * (y + skip * u_parts[r][rows, :].astype(F32))
            if first:
                o_ref[0, r, rows, :] = o.astype(o_ref.dtype)
            elif valid > 0:
                for k, lanes in lane_slabs:
                    nat_ref[k, pl.ds(RADIX * lo + r, valid, stride=RADIX), :] = o[:valid, lanes]
    if not first:
        for k, lanes in lane_slabs:
            o_ref[0, :, lanes] = nat_ref[k, :L, :]


def _fold_classes(e1, e2):
    a_p, a_m = e1[0] + e1[3], e1[0] - e1[3]
    b_p, b_m = e1[1] + e1[2], e1[1] - e1[2]
    c_p, c_m = e2[0] + e2[3], e2[0] - e2[3]
    d_p, d_m = e2[1] + e2[2], e2[2] - e2[1]
    return [(a_p + b_p, c_m + d_m),
            (a_m + d_p, c_p + b_m),
            (a_p - b_p, c_m - d_m),
            (a_m - d_p, c_p - b_m)]


def _long_conv(dims, fwd, inv, u_arr, u_col0, x_arr, x_col0, short_w, short_b, short_u_col0, short_x_col0,
               skip, spectrum, k_col0, width):
    B = u_arr.shape[0]
    L = dims.L
    cw = MXU_DIM
    ub, xb, kb = u_col0 // cw, x_col0 // cw, k_col0 // cw
    first = short_u_col0 is not None
    fwd_spec = pl.BlockSpec((dims.FM, dims.QK), lambda c, b: (0, 0), pipeline_mode=pl.Buffered(1))
    inv_spec = pl.BlockSpec((dims.QP, dims.IK), lambda c, b: (0, 0), pipeline_mode=pl.Buffered(1))

    def taps(col0):
        blk = col0 // cw
        return [pl.BlockSpec((HY_SHORT, cw), lambda c, b: (0, blk + c)),
                pl.BlockSpec((1, cw), lambda c, b: (0, blk + c))]

    if first:
        u_spec = pl.BlockSpec((1, L, cw), lambda c, b: (b, 0, ub + c))
        out_spec = pl.BlockSpec((1, RADIX, dims.QP, cw), lambda c, b: (b, 0, 0, c))
        out_shape = jax.ShapeDtypeStruct((B, RADIX, dims.QP, width), BF16)
    else:
        u_spec = pl.BlockSpec((1, RADIX, dims.QP, cw), lambda c, b: (b, 0, 0, ub + c))
        out_spec = pl.BlockSpec((1, L, cw), lambda c, b: (b, 0, c))
        out_shape = jax.ShapeDtypeStruct((B, L, width), F32)
    in_specs = ([fwd_spec] * (2 * RADIX) + [inv_spec] * RADIX
                + [u_spec, pl.BlockSpec((1, L, cw), lambda c, b: (b, 0, xb + c))])
    args = list(fwd) + list(inv) + [u_arr, x_arr]
    for col0 in ([short_u_col0] if first else []) + [short_x_col0]:
        in_specs += taps(col0)
        args += [short_w, short_b]
    in_specs += [pl.BlockSpec((1, cw), lambda c, b: (0, c))]
    in_specs += [pl.BlockSpec((dims.FM, cw), lambda c, b: (0, kb + c))] * (2 * RADIX)
    args += [skip.reshape(1, width)] + list(spectrum)
    stage = pltpu.VMEM((cw // LANES, RADIX * dims.QP, LANES), F32)
    return pl.pallas_call(
        functools.partial(_conv_body, dims=dims, first=first),
        grid=(width // cw, B),
        in_specs=in_specs,
        out_specs=out_spec,
        out_shape=out_shape,
        scratch_shapes=([pltpu.VMEM((dims.QK, cw), BF16)] * RADIX + [pltpu.VMEM((dims.QP, cw), BF16)] * RADIX
                        + [pltpu.VMEM((dims.IK, cw), BF16)] * RADIX + [stage, stage]),
        compiler_params=_params(("parallel", "arbitrary")),
        name="hyena_long_conv",
    )(*args)


def _filter_body(z_ref, w1_ref, b1_ref, w2_ref, b2_ref, fr_ref, w3f_ref, w3b_ref, dec_ref, hs_ref, hd_ref,
                 h2_ref, nat_ref, *, dims):
    hp = lax.Precision.HIGHEST

    @pl.when(pl.program_id(0) == 0)
    def _():
        fr = fr_ref[...]
        h1 = jnp.sin(fr * (jnp.dot(z_ref[...], w1_ref[...], precision=hp, preferred_element_type=F32)
                           + b1_ref[...]))
        h2_ref[...] = jnp.sin(fr * (jnp.dot(h1, w2_ref[...], precision=hp, preferred_element_type=F32)
                                    + b2_ref[...]))

    h2 = h2_ref[...]
    dec = dec_ref[...]
    hf = jnp.dot(h2, w3f_ref[...], precision=hp, preferred_element_type=F32) * dec
    hb = jnp.dot(h2, w3b_ref[...], precision=hp, preferred_element_type=F32) * dec
    hb = jnp.where(lax.broadcasted_iota(jnp.int32, hb.shape, 0) == 0, 0.0, hb)
    scale = lax.rsqrt(jnp.sum(hf * hf + hb * hb, axis=0, keepdims=True) + 1e-6)
    for ref, val in ((hs_ref, (hf + hb) * scale), (hd_ref, (hf - hb) * scale)):
        for k in range(val.shape[1] // LANES):
            lanes = slice(k * LANES, (k + 1) * LANES)
            nat_ref[k] = val[:, lanes]
            for r in range(RADIX):
                ref[r, :, lanes] = nat_ref[k, pl.ds(r, dims.Q, stride=RADIX), :]


def _filter_constants(L, width):
    t = np.linspace(0.0, 1.0, L)
    w = (2.0 * math.pi / L) * np.arange(L)
    bands = np.linspace(1e-4, HY_BANDS - 1, HY_BANDS)
    fw = w[:, None] * bands[None, :]
    z = np.concatenate([t[:, None], np.cos(fw), -np.sin(fw)], axis=-1)
    z = np.pad(z, ((0, 0), (0, LANES - z.shape[1])))
    deltas = np.abs(np.linspace(HY_MIN_DECAY, HY_MAX_DECAY, width))
    return z.astype(np.float32), np.exp(-t[:, None] * deltas[None, :]).astype(np.float32)


def _hyena_filters(dims, consts, w1, b1, w2, b2, w3, freq):
    L = dims.L
    z, dec = consts
    hidden = w2.shape[0]
    width = w3.shape[1] // (2 * HY_ORDER)
    cw = MXU_DIM
    per_order = width // cw

    def lanes128(a):
        return jnp.pad(a.astype(F32), [(0, 0)] * (a.ndim - 1) + [(0, LANES - a.shape[-1])])

    w1p = jnp.pad(lanes128(w1), ((0, LANES - w1.shape[0]), (0, 0)))
    w2p = jnp.pad(lanes128(w2), ((0, LANES - hidden), (0, 0)))
    w3p = jnp.pad(w3.astype(F32), ((0, LANES - hidden), (0, 0)))
    row = lambda a: lanes128(a).reshape(1, LANES)
    full = lambda shape: pl.BlockSpec(shape, lambda j: (0, 0))
    fwd_col = lambda j: (0, (j // per_order) * 2 * per_order + j % per_order)
    bwd_col = lambda j: (0, (j // per_order) * 2 * per_order + per_order + j % per_order)
    out_spec = pl.BlockSpec((RADIX, dims.Q, cw), lambda j: (0, 0, j))
    return pl.pallas_call(
        functools.partial(_filter_body, dims=dims),
        grid=(HY_ORDER * per_order,),
        in_specs=[full((L, LANES)), full((LANES, LANES)), full((1, LANES)), full((LANES, LANES)),
                  full((1, LANES)), full((1, LANES)),
                  pl.BlockSpec((LANES, cw), fwd_col), pl.BlockSpec((LANES, cw), bwd_col),
                  pl.BlockSpec((L, cw), lambda j: (0, j % per_order))],
        out_specs=[out_spec, out_spec],
        out_shape=[jax.ShapeDtypeStruct((RADIX, dims.Q, HY_ORDER * width), F32)] * 2,
        scratch_shapes=[pltpu.VMEM((L, LANES), F32), pltpu.VMEM((cw // LANES, L, LANES), F32)],
        compiler_params=_params(("arbitrary",)),
        name="hyena_filters",
    )(z, w1p, row(b1), w2p, row(b2), row(freq), w3p, w3p, dec)


def _mix_body(ret_ref, hy_ref, gr_ref, gh_ref, h_ref, wr_ref, wh_ref, wo_ref, o_ref):
    ret = jnp.dot(ret_ref[...], wr_ref[...], preferred_element_type=F32)
    hyo = jnp.dot(hy_ref[...].astype(BF16), wh_ref[...], preferred_element_type=F32)
    merged = (jax.nn.sigmoid(gr_ref[...].astype(F32)) * ret
              + jax.nn.sigmoid(gh_ref[...].astype(F32)) * hyo)
    o_ref[...] = h_ref[...] + jnp.dot(merged.astype(BF16), wo_ref[...], preferred_element_type=F32)


def _mix_out(ret2, hy2, proj2, gate_col0, h2, w_ret_o, w_hy_o, w_out):
    T, D = h2.shape
    tm = _pick_tile(T, 384, BF16_SUBLANES)
    gb = gate_col0 // D
    row = lambda w: pl.BlockSpec((tm, w), lambda i: (i, 0))
    full = lambda a: pl.BlockSpec(a.shape, lambda i: (0, 0))
    return pl.pallas_call(
        _mix_body,
        grid=(T // tm,),
        in_specs=[
            row(ret2.shape[1]), row(hy2.shape[1]),
            pl.BlockSpec((tm, D), lambda i: (i, gb)),
            pl.BlockSpec((tm, D), lambda i: (i, gb + 1)),
            row(D), full(w_ret_o), full(w_hy_o), full(w_out),
        ],
        out_specs=row(D),
        out_shape=jax.ShapeDtypeStruct((T, D), F32),
        compiler_params=_params(("parallel",)),
        name="merge_outproj",
    )(ret2, hy2, proj2, proj2, h2, w_ret_o, w_hy_o, w_out)


EXPERT_TILE = 512
META_COLS = 8
SC_WINDOW = 64
SC_WORKERS = 32
SC_CHUNK = 2 * SC_WINDOW * SC_WORKERS
SC_SCATTER_WINDOW = 128


def _pack_bf16_pairs(x):
    n = x.shape[1] // 2
    xb = x.astype(BF16).astype(F32)
    hi = lax.bitcast_convert_type(xb[:, :n], jnp.uint32)
    lo = lax.bitcast_convert_type(xb[:, n:], jnp.uint32)
    return lax.bitcast_convert_type(hi | (lo >> 16), jnp.int32)


def _unpack_bf16_pairs(w):
    u = lax.bitcast_convert_type(w, jnp.uint32)
    hi = lax.bitcast_convert_type(u & jnp.uint32(0xFFFF0000), F32)
    lo = lax.bitcast_convert_type(u << 16, F32)
    return hi, lo


def _route(lt):
    assert EXP_PER_GROUP == 8 and N_GROUPS <= 8
    tm = lt.shape[1]
    row = lax.broadcasted_iota(jnp.int32, (8, tm), 0)
    neg = -jnp.inf
    big = jnp.int32(1 << 20)
    gl = jnp.where(row < N_GROUPS, lt[N_EXPERTS:N_EXPERTS + 8, :], neg)
    gmax = jnp.max(gl, axis=0, keepdims=True)
    p_top = 1.0 / jnp.sum(jnp.exp(gl - gmax), axis=0, keepdims=True)
    g_idx = jnp.min(jnp.where(gl == gmax, row, big), axis=0, keepdims=True)
    el = lt[0:8, :]
    for g in range(1, N_GROUPS):
        el = jnp.where(g_idx == g, lt[8 * g:8 * g + 8, :], el)
    m1 = jnp.max(el, axis=0, keepdims=True)
    i1 = jnp.min(jnp.where(el == m1, row, big), axis=0, keepdims=True)
    el2 = jnp.where(row == i1, neg, el)
    m2 = jnp.max(el2, axis=0, keepdims=True)
    i2 = jnp.min(jnp.where(el2 == m2, row, big), axis=0, keepdims=True)
    r = jnp.exp(m2 - m1)
    base = g_idx * EXP_PER_GROUP
    return base + i1, base + i2, p_top / (1.0 + r), p_top * r / (1.0 + r)


def _route_body(h_ref, g_ref, whi_ref, wlo_ref, xpk_ref, meta_ref, meta_t_ref, cnt_ref, carry_ref):
    i = pl.program_id(0)

    @pl.when(i == 0)
    def _():
        carry_ref[...] = jnp.zeros_like(carry_ref)

    xn = _rms(h_ref[...], g_ref[...])
    xpk_ref[...] = _pack_bf16_pairs(xn)
    x_hi = xn.astype(BF16)
    x_lo = (xn - x_hi.astype(F32)).astype(BF16)
    logits = (jnp.dot(x_hi, whi_ref[...], preferred_element_type=F32)
              + jnp.dot(x_lo, whi_ref[...], preferred_element_type=F32)
              + jnp.dot(x_hi, wlo_ref[...], preferred_element_type=F32))
    e0, e1, w0, w1 = _route(logits.T)
    tm = logits.shape[0]
    row = lax.broadcasted_iota(jnp.int32, (LANES, tm), 0)
    onehot_t = jnp.where((row == e0) | (row == e1), 1.0, 0.0).astype(BF16)
    ri = lax.broadcasted_iota(jnp.int32, (tm, tm), 0)
    ci = lax.broadcasted_iota(jnp.int32, (tm, tm), 1)
    earlier = jnp.where(ci < ri, 1.0, 0.0).astype(BF16)
    prefix = carry_ref[...] + _dot_nt(earlier, onehot_t)
    prefix_t = prefix.T
    r0 = jnp.sum(jnp.where(row == e0, prefix_t, 0.0), axis=0, keepdims=True)
    r1 = jnp.sum(jnp.where(row == e1, prefix_t, 0.0), axis=0, keepdims=True)
    row8 = lax.broadcasted_iota(jnp.int32, (META_COLS, tm), 0)
    meta_t = jnp.zeros((META_COLS, tm), F32)
    for c, val in enumerate((e0.astype(F32), e1.astype(F32), r0, r1, w0, w1)):
        meta_t = jnp.where(row8 == c, val, meta_t)
    meta_t_ref[...] = meta_t
    padded = jnp.concatenate([meta_t, jnp.zeros((LANES - META_COLS, tm), F32)], axis=0)
    meta_ref[...] = padded.T[:, :META_COLS]
    counts = _dot_nt(jnp.ones((8, tm), BF16), onehot_t)
    carry_ref[...] += counts[0:1, :]
    cnt_ref[...] = carry_ref[...]


def _moe_route(h2, gain, w_router):
    T, D = h2.shape
    tm = _pick_tile(T, 1152, LANES)
    return pl.pallas_call(
        _route_body,
        grid=(T // tm,),
        in_specs=[
            pl.BlockSpec((tm, D), lambda i: (i, 0)),
            pl.BlockSpec((1, D), lambda i: (0, 0)),
            pl.BlockSpec((D, LANES), lambda i: (0, 0)),
            pl.BlockSpec((D, LANES), lambda i: (0, 0)),
        ],
        out_specs=[
            pl.BlockSpec((tm, D // 2), lambda i: (i, 0)),
            pl.BlockSpec((tm, META_COLS), lambda i: (i, 0)),
            pl.BlockSpec((META_COLS, tm), lambda i: (0, i)),
            pl.BlockSpec((1, LANES), lambda i: (0, 0)),
        ],
        out_shape=[
            jax.ShapeDtypeStruct((T, D // 2), jnp.int32),
            jax.ShapeDtypeStruct((T, META_COLS), F32),
            jax.ShapeDtypeStruct((META_COLS, T), F32),
            jax.ShapeDtypeStruct((1, LANES), F32),
        ],
        scratch_shapes=[pltpu.VMEM((1, LANES), F32)],
        compiler_params=_params(("arbitrary",)),
        name="moe_route",
    )(h2, gain.reshape(1, D), *w_router)


def _sc_gather(table, idx):
    n = idx.shape[0]
    width = table.shape[1]
    win = SC_WINDOW
    assert n % SC_CHUNK == 0
    per_worker = n // SC_WORKERS
    mesh = plsc.VectorSubcoreMesh(core_axis_name="c", subcore_axis_name="s")

    @functools.partial(
        pl.kernel, out_type=jax.ShapeDtypeStruct((n, width), table.dtype), mesh=mesh,
        scratch_types=[pltpu.VMEM((per_worker,), jnp.int32), pltpu.VMEM((2, win, width), table.dtype),
                       pltpu.SemaphoreType.DMA((2,)), pltpu.SemaphoreType.DMA((2,))],
        name="sc_row_gather")
    def gather(table_hbm, idx_hbm, out_hbm, idx_v, rows_v, gsem, osem):
        worker = lax.axis_index("s") * mesh.num_cores + lax.axis_index("c")
        base = worker * per_worker
        pltpu.sync_copy(idx_hbm.at[pl.ds(base, per_worker)], idx_v)

        @pl.loop(0, per_worker, step=2 * win)
        def _(off):
            fetch = [pltpu.async_copy(table_hbm.at[idx_v.at[pl.ds(off + s * win, win)]], rows_v.at[s], gsem.at[s])
                     for s in range(2)]
            store = []
            for s in range(2):
                fetch[s].wait()
                store.append(pltpu.async_copy(rows_v.at[s], out_hbm.at[pl.ds(base + off + s * win, win)],
                                              osem.at[s]))
            for s in range(2):
                store[s].wait()

    return gather(table, idx)


def _sc_dispatch(table, dest, n_out):
    n_rows, width = table.shape
    win = SC_SCATTER_WINDOW
    assert n_rows % win == 0
    n_win = n_rows // win
    per_worker = pl.cdiv(n_win, SC_WORKERS)
    idx = jnp.pad(dest, ((0, 0), (0, per_worker * SC_WORKERS * win - n_rows)))
    idx = idx.reshape(2, per_worker, SC_WORKERS, win).transpose(0, 2, 1, 3)
    mesh = plsc.VectorSubcoreMesh(core_axis_name="c", subcore_axis_name="s")

    @functools.partial(
        pl.kernel, out_type=jax.ShapeDtypeStruct((n_out, width), table.dtype), mesh=mesh,
        scratch_types=[pltpu.VMEM((per_worker, win), jnp.int32), pltpu.VMEM((per_worker, win), jnp.int32),
                       pltpu.VMEM((win, width), table.dtype),
                       pltpu.SemaphoreType.DMA, pltpu.SemaphoreType.DMA],
        name="sc_row_dispatch")
    def dispatch(table_hbm, idx_hbm, out_hbm, idx0_v, idx1_v, rows_v, sem0, sem1):
        worker = lax.axis_index("s") * mesh.num_cores + lax.axis_index("c")
        pltpu.sync_copy(idx_hbm.at[0, worker], idx0_v)
        pltpu.sync_copy(idx_hbm.at[1, worker], idx1_v)

        @pl.loop(0, per_worker)
        def _(j):
            window = j * SC_WORKERS + worker

            @pl.when(window < n_win)
            def _():
                pltpu.sync_copy(table_hbm.at[pl.ds(window * win, win)], rows_v)
                first = pltpu.async_copy(rows_v, out_hbm.at[idx0_v.at[j]], sem0)
                second = pltpu.async_copy(rows_v, out_hbm.at[idx1_v.at[j]], sem1)
                first.wait()
                second.wait()

    return dispatch(table, idx)


def _expert_body(te_ref, tv_ref, x_ref, wg_ref, wu_ref, wd_ref, y_ref, wg_s, wu_s, wd_s):
    t = pl.program_id(0)
    half = x_ref.shape[1]

    @pl.when((t == 0) | (te_ref[t] != te_ref[jnp.maximum(t - 1, 0)]))
    def _():
        wg_s[...] = wg_ref[0].astype(BF16)
        wu_s[...] = wu_ref[0].astype(BF16)
        wd_s[...] = wd_ref[0].astype(BF16)

    @pl.when(tv_ref[t] > 0)
    def _():
        row = lax.broadcasted_iota(jnp.int32, x_ref.shape, 0)
        hi, lo = _unpack_bf16_pairs(jnp.where(row < tv_ref[t], x_ref[...], 0))
        hi, lo = hi.astype(BF16), lo.astype(BF16)
        hg = jnp.dot(hi, wg_s[:half, :], preferred_element_type=F32)
        hg += jnp.dot(lo, wg_s[half:, :], preferred_element_type=F32)
        hu = jnp.dot(hi, wu_s[:half, :], preferred_element_type=F32)
        hu += jnp.dot(lo, wu_s[half:, :], preferred_element_type=F32)
        act = (hg * jax.nn.sigmoid(hg) * hu).astype(BF16)
        y_ref[...] = _pack_bf16_pairs(jnp.dot(act, wd_s[...], preferred_element_type=F32))

    @pl.when(tv_ref[t] == 0)
    def _():
        y_ref[...] = jnp.zeros_like(y_ref)


def _moe_experts(xs, n_sorted, tile_expert, tile_valid, w_gate, w_up, w_down):
    NP, half = n_sorted, xs.shape[1]
    _, D, FF = w_gate.shape
    tr = EXPERT_TILE
    grid_spec = pltpu.PrefetchScalarGridSpec(
        num_scalar_prefetch=2,
        grid=(NP // tr,),
        in_specs=[
            pl.BlockSpec((tr, half), lambda t, te, tv: (t, 0)),
            pl.BlockSpec((1, D, FF), lambda t, te, tv: (te[t], 0, 0)),
            pl.BlockSpec((1, D, FF), lambda t, te, tv: (te[t], 0, 0)),
            pl.BlockSpec((1, FF, D), lambda t, te, tv: (te[t], 0, 0)),
        ],
        out_specs=pl.BlockSpec((tr, half), lambda t, te, tv: (t, 0)),
        scratch_shapes=[pltpu.VMEM((D, FF), BF16), pltpu.VMEM((D, FF), BF16), pltpu.VMEM((FF, D), BF16)],
    )
    return pl.pallas_call(
        _expert_body,
        grid_spec=grid_spec,
        out_shape=jax.ShapeDtypeStruct((NP, half), jnp.int32),
        compiler_params=_params(("arbitrary",)),
        name="moe_experts",
    )(tile_expert, tile_valid, xs, w_gate, w_up, w_down)


def _moe_combine_into(o_ref, h, meta, y0, y1):
    half = y0.shape[1]
    w0 = meta[:, 4:5]
    w1 = meta[:, 5:6]
    hi0, lo0 = _unpack_bf16_pairs(y0)
    hi1, lo1 = _unpack_bf16_pairs(y1)
    o_ref[:, :half] = h[:, :half] + w0 * hi0 + w1 * hi1
    o_ref[:, half:] = h[:, half:] + w0 * lo0 + w1 * lo1


def _moe(h2, gain, w_router, layer, w_gate, w_up, w_down):
    T, D = h2.shape
    E = N_EXPERTS
    tr = EXPERT_TILE
    n_sorted = _round_up(2 * T + E * (tr - 1), tr)
    t_pad = _round_up(T, SC_CHUNK // 2)

    xpk, meta, meta_t, counts = _moe_route(h2, gain, w_router)

    cnt = counts[0, :E].astype(jnp.int32)
    padded = (cnt + tr - 1) // tr * tr
    ends = jnp.cumsum(padded)
    starts = ends - padded
    eid = meta_t[0:2].astype(jnp.int32)
    pos = meta_t[2:4].astype(jnp.int32)
    for e in range(E):
        pos = pos + jnp.where(eid == e, starts[e], 0)
    tile_start = jnp.arange(n_sorted // tr, dtype=jnp.int32) * tr
    tile_expert = jnp.minimum(jnp.sum(tile_start[:, None] >= ends[None, :], axis=1), E - 1).astype(jnp.int32)
    tile_valid = jnp.clip(cnt[tile_expert] - (tile_start - starts[tile_expert]), 0, tr).astype(jnp.int32)

    xs = _sc_dispatch(xpk, pos, n_sorted)
    ys = _moe_experts(xs, n_sorted, layer * E + tile_expert, tile_valid, w_gate, w_up, w_down)
    spare = jnp.arange(t_pad - T, dtype=jnp.int32)
    back = jnp.concatenate([pos, jnp.broadcast_to(spare[None], (2, t_pad - T))], axis=1)
    yg = _sc_gather(ys, back.reshape(-1)).reshape(2, t_pad, D // 2)
    return meta, yg


def _final_body(h_ref, meta_ref, y0_ref, y1_ref, g_ref, o_ref, hn_ref):
    _moe_combine_into(hn_ref, h_ref[0], meta_ref[0], y0_ref[0], y1_ref[0])
    o_ref[0] = _rms(hn_ref[N_META:, :], g_ref[...])


def _final_norm(h3, meta3, yg, gain, b0, nb):
    _, L, D = h3.shape
    return pl.pallas_call(
        _final_body,
        grid=(nb,),
        in_specs=[pl.BlockSpec((1, L, D), lambda b: (b0 + b, 0, 0)),
                  pl.BlockSpec((1, L, meta3.shape[2]), lambda b: (b0 + b, 0, 0)),
                  pl.BlockSpec((1, L, D // 2), lambda b: (0, b0 + b, 0)),
                  pl.BlockSpec((1, L, D // 2), lambda b: (1, b0 + b, 0)),
                  pl.BlockSpec((1, D), lambda b: (0, 0))],
        out_specs=pl.BlockSpec((1, L - N_META, D), lambda b: (b, 0, 0)),
        out_shape=jax.ShapeDtypeStruct((nb, L - N_META, D), F32),
        scratch_shapes=[pltpu.VMEM((L, D), F32)],
        compiler_params=_params(("parallel",)),
        name="combine_final_norm",
    )(h3, meta3, yg, yg, gain.reshape(1, D))


def kernel(x_prompt, x_sample, meta_tokens, norm_mix, w_in, ret_decay_fwd, ret_decay_bwd, hy_short_w, hy_short_b, hy_filt_w1, hy_filt_b1, hy_filt_w2, hy_filt_b2, hy_filt_w3, hy_sin_freq, hy_skip, w_ret_o, w_hy_o, w_out, norm_ffn, router_group, router_expert, moe_w_gate, moe_w_up, moe_w_down, norm_final):
    assert x_prompt.shape[1:] == x_sample.shape[1:]
    nbp, nbs = x_prompt.shape[0], x_sample.shape[0]
    B = nbp + nbs
    D = x_prompt.shape[2]
    L = N_META + x_prompt.shape[1]
    T = B * L
    depth = w_in.shape[0]
    q_w = RET_HEADS * RET_DK
    v_w = RET_HEADS * RET_DV
    hy_w = hy_skip.shape[2]
    hy_col0 = 2 * q_w + 2 * v_w
    gate_col0 = hy_col0 + 3 * hy_w
    assert D == q_w and w_in.shape[2] == gate_col0 + 2 * D

    x = jnp.concatenate([x_prompt, x_sample], axis=0)
    meta = jnp.broadcast_to(meta_tokens[None].astype(x.dtype), (B, N_META, D))
    h = jnp.concatenate([meta, x], axis=1).reshape(T, D)

    cos_t, sin_t = _rotary_tables(L)
    dims = _HyenaDims(L)
    dft_fwd, dft_inv = _dft_tables(dims)
    filt_consts = _filter_constants(L, hy_w)
    router = jnp.concatenate([router_expert, router_group], axis=2).astype(F32)
    router = jnp.pad(router, ((0, 0), (0, 0), (0, LANES - router.shape[2])))
    router_hi = router.astype(BF16)
    router_lo = (router - router_hi.astype(F32)).astype(BF16)
    assert moe_w_gate.shape[1] == N_EXPERTS
    w_gate_all = moe_w_gate.reshape((-1,) + moe_w_gate.shape[2:])
    w_up_all = moe_w_up.reshape((-1,) + moe_w_up.shape[2:])
    w_down_all = moe_w_down.reshape((-1,) + moe_w_down.shape[2:])

    moe = None
    for i in range(depth):
        proj, h = _inproj(h, norm_mix[i], w_in[i].astype(BF16), moe)
        proj3 = proj.reshape(B, L, -1)

        lg = jnp.stack([jax.nn.log_sigmoid(ret_decay_fwd[i].astype(F32)),
                        jax.nn.log_sigmoid(ret_decay_bwd[i].astype(F32))])
        ret = _retention(proj3, lg, cos_t, sin_t)

        hs, hd = _hyena_filters(dims, filt_consts, hy_filt_w1[i], hy_filt_b1[i], hy_filt_w2[i], hy_filt_b2[i],
                                hy_filt_w3[i], hy_sin_freq[i])
        spectrum = _filter_spectrum(dims, dft_fwd, hs, hd)
        sw = hy_short_w[i].astype(F32)
        sb = hy_short_b[i].astype(F32).reshape(1, -1)
        z = _long_conv(dims, dft_fwd, dft_inv, proj3, hy_col0, proj3, hy_col0 + hy_w, sw, sb, 0, hy_w,
                       hy_skip[i, 0].astype(F32), spectrum, 0, hy_w)
        z = _long_conv(dims, dft_fwd, dft_inv, z, 0, proj3, hy_col0 + 2 * hy_w, sw, sb, None, 2 * hy_w,
                       hy_skip[i, 1].astype(F32), spectrum, hy_w, hy_w)

        h = _mix_out(ret.reshape(T, v_w), z.reshape(T, hy_w), proj, gate_col0, h,
                     w_ret_o[i].astype(BF16), w_hy_o[i].astype(BF16), w_out[i].astype(BF16))
        moe = _moe(h, norm_ffn[i], (router_hi[i], router_lo[i]), i, w_gate_all, w_up_all, w_down_all)

    h3 = h.reshape(B, L, D)
    meta3 = moe[0].reshape(B, L, -1)
    return (_final_norm(h3, meta3, moe[1], norm_final, 0, nbp),
            _final_norm(h3, meta3, moe[1], norm_final, nbp, nbs))
```

```python
import functools
import math

import jax
import jax.numpy as jnp
import numpy as np
from jax import lax
from jax.experimental import pallas as pl
from jax.experimental.pallas import tpu as pltpu
from jax.experimental.pallas import tpu_sc as plsc

N_META = 16
RET_HEADS = 8
RET_DK = 128
RET_DV = 256
ROPE_THETA = 10000.0
HY_ORDER = 2
HY_SHORT = 3
HY_EMB = 33
HY_BANDS = (HY_EMB - 1) // 2
HY_DECAY_TARGET = 1e-2
HY_MIN_DECAY = math.log(HY_DECAY_TARGET) / 1.5
HY_MAX_DECAY = math.log(HY_DECAY_TARGET) / 0.3
N_GROUPS = 4
EXP_PER_GROUP = 8
N_EXPERTS = N_GROUPS * EXP_PER_GROUP
RMS_EPS = 1e-6

LANES = 128
BF16_SUBLANES = 16
MXU_DIM = 256
RET_CHUNK = 256
VMEM_LIMIT = 56 * 1024 * 1024

F32 = jnp.float32
BF16 = jnp.bfloat16


def _round_up(n, m):
    return (n + m - 1) // m * m


def _pick_tile(n, target, mult):
    best = None
    for t in range(mult, min(n, target) + 1, mult):
        if n % t == 0:
            best = t
    assert best is not None, (n, target, mult)
    return best


def _params(sem):
    return pltpu.CompilerParams(dimension_semantics=sem, vmem_limit_bytes=VMEM_LIMIT)


def _rms(x, gain):
    ms = jnp.mean(x * x, axis=-1, keepdims=True)
    return x * lax.rsqrt(ms + RMS_EPS) * gain


def _inproj_body(h_ref, g_ref, w_ref, o_ref, xn_ref):
    @pl.when(pl.program_id(1) == 0)
    def _():
        xn_ref[...] = _rms(h_ref[...], g_ref[...]).astype(BF16)

    o_ref[...] = jnp.dot(xn_ref[...], w_ref[...], preferred_element_type=F32).astype(o_ref.dtype)


def _inproj_moe_body(h_ref, meta_ref, y0_ref, y1_ref, g_ref, w_ref, o_ref, hn_ref, xn_ref):
    @pl.when(pl.program_id(1) == 0)
    def _():
        _moe_combine_into(hn_ref, h_ref[...], meta_ref[...], y0_ref[0], y1_ref[0])
        xn_ref[...] = _rms(hn_ref[...], g_ref[...]).astype(BF16)

    o_ref[...] = jnp.dot(xn_ref[...], w_ref[...], preferred_element_type=F32).astype(o_ref.dtype)


def _inproj(h2, gain, w, moe=None):
    T, D = h2.shape
    nc = w.shape[1]
    tm = _pick_tile(T, 1152, BF16_SUBLANES)
    tn = _pick_tile(nc, 1024, LANES)
    common = dict(
        grid=(T // tm, nc // tn),
        scratch_shapes=[pltpu.VMEM((tm, D), BF16)],
        compiler_params=_params(("parallel", "arbitrary")),
    )
    h_spec = pl.BlockSpec((tm, D), lambda i, j: (i, 0))
    g_spec = pl.BlockSpec((1, D), lambda i, j: (0, 0))
    w_spec = pl.BlockSpec((D, tn), lambda i, j: (0, j))
    o_spec = pl.BlockSpec((tm, tn), lambda i, j: (i, j))
    o_shape = jax.ShapeDtypeStruct((T, nc), BF16)
    if moe is None:
        proj = pl.pallas_call(_inproj_body, in_specs=[h_spec, g_spec, w_spec], out_specs=o_spec,
                              out_shape=o_shape, name="norm_inproj", **common)(h2, gain.reshape(1, D), w)
        return proj, h2
    meta, yg = moe
    return pl.pallas_call(
        _inproj_moe_body,
        in_specs=[h_spec, pl.BlockSpec((tm, meta.shape[1]), lambda i, j: (i, 0)),
                  pl.BlockSpec((1, tm, D // 2), lambda i, j: (0, i, 0)),
                  pl.BlockSpec((1, tm, D // 2), lambda i, j: (1, i, 0)),
                  g_spec, w_spec],
        out_specs=[o_spec, h_spec],
        out_shape=[o_shape, jax.ShapeDtypeStruct((T, D), F32)],
        name="combine_norm_inproj", **common,
    )(h2, meta, yg, yg, gain.reshape(1, D), w)


def _dot_t(a, b):
    return lax.dot_general(a, b, (((0,), (0,)), ((), ())), preferred_element_type=F32)


def _dot_nt(a, b):
    return lax.dot_general(a, b, (((1,), (1,)), ((), ())), preferred_element_type=F32)


def _ret_body(lg_ref, q_ref, k_ref, v_ref, cos_ref, sin_ref, o_ref, rb_ref, kr_ref, *, seq_len):
    C = RET_CHUNK
    L = seq_len
    n_chunks = pl.cdiv(L, C)
    head = pl.program_id(1)
    lgf = lg_ref[0, head]
    lgb = lg_ref[1, head]

    def chunk(ref, n):
        lo, hi = n * C, min((n + 1) * C, L)
        x = ref[0, lo:hi, :]
        if hi - lo < C:
            x = jnp.concatenate([x, jnp.zeros((C - (hi - lo), x.shape[1]), x.dtype)], axis=0)
        return x

    def rotary(ref, n):
        x = chunk(ref, n).astype(F32)
        sl = slice(n * C, (n + 1) * C)
        return x * cos_ref[sl, :] + pltpu.roll(x, RET_DK // 2, 1) * sin_ref[sl, :]

    row = lax.broadcasted_iota(jnp.int32, (C, LANES), 0).astype(F32)
    ri = lax.broadcasted_iota(jnp.int32, (C, C), 0).astype(F32)
    ci = lax.broadcasted_iota(jnp.int32, (C, C), 1).astype(F32)
    decay = jnp.exp(jnp.where(ci <= ri, (ri - ci) * lgf, (ci - ri) * lgb))
    qf_dec = jnp.exp((row + 1.0) * lgf)
    qb_dec = jnp.exp((C - row) * lgb)
    kf_dec = jnp.exp((C - 1.0 - row) * lgf)
    kb_dec = jnp.exp(row * lgb)
    cf = jnp.exp(C * lgf)
    cb = jnp.exp(C * lgb)

    state = jnp.zeros((RET_DK, RET_DV), F32)
    for n in reversed(range(n_chunks)):
        rb_ref[n] = state.astype(BF16)
        if n > 0:
            kr_ref[n] = rotary(k_ref, n)
            kb = (kr_ref[n] * kb_dec).astype(BF16)
            state = cb * state + _dot_t(kb, chunk(v_ref, n))

    state = jnp.zeros((RET_DK, RET_DV), F32)
    for n in range(n_chunks):
        q = rotary(q_ref, n) * (RET_DK ** -0.5)
        k = kr_ref[n] if n > 0 else rotary(k_ref, n)
        v = chunk(v_ref, n)
        scores = _dot_nt(q.astype(BF16), k.astype(BF16)) * decay
        o = jnp.dot(scores.astype(BF16), v, preferred_element_type=F32)
        o += jnp.dot((q * qf_dec).astype(BF16), state.astype(BF16), preferred_element_type=F32)
        o += jnp.dot((q * qb_dec).astype(BF16), rb_ref[n], preferred_element_type=F32)
        if n + 1 < n_chunks:
            state = cf * state + _dot_t((k * kf_dec).astype(BF16), v)
        lo, hi = n * C, min((n + 1) * C, L)
        o_ref[0, lo:hi, :] = o[: hi - lo].astype(o_ref.dtype)


def _retention(proj3, lg, cos_t, sin_t):
    B, L, _ = proj3.shape
    n_chunks = pl.cdiv(L, RET_CHUNK)
    lp = n_chunks * RET_CHUNK
    k_blk = RET_HEADS
    v_blk = 2 * RET_HEADS * RET_DK // RET_DV
    return pl.pallas_call(
        functools.partial(_ret_body, seq_len=L),
        grid=(B, RET_HEADS),
        in_specs=[
            pl.BlockSpec(memory_space=pltpu.SMEM),
            pl.BlockSpec((1, L, RET_DK), lambda b, h: (b, 0, h)),
            pl.BlockSpec((1, L, RET_DK), lambda b, h: (b, 0, k_blk + h)),
            pl.BlockSpec((1, L, RET_DV), lambda b, h: (b, 0, v_blk + h)),
            pl.BlockSpec((lp, RET_DK), lambda b, h: (0, 0)),
            pl.BlockSpec((lp, RET_DK), lambda b, h: (0, 0)),
        ],
        out_specs=pl.BlockSpec((1, L, RET_DV), lambda b, h: (b, 0, h)),
        out_shape=jax.ShapeDtypeStruct((B, L, RET_HEADS * RET_DV), BF16),
        scratch_shapes=[pltpu.VMEM((n_chunks, RET_DK, RET_DV), BF16),
                        pltpu.VMEM((n_chunks, RET_CHUNK, RET_DK), F32)],
        compiler_params=_params(("parallel", "arbitrary")),
        name="retention",
    )(lg, proj3, proj3, proj3, cos_t, sin_t)


def _rotary_tables(L):
    half = RET_DK // 2
    lp = _round_up(L, RET_CHUNK)
    inv = ROPE_THETA ** (-jnp.arange(half, dtype=F32) / half)
    ang = jnp.arange(lp, dtype=F32)[:, None] * inv[None, :]
    cos, sin = jnp.cos(ang), jnp.sin(ang)
    return jnp.concatenate([cos, cos], axis=1), jnp.concatenate([-sin, sin], axis=1)


def _row_tiles(n, tile):
    return [(s, min(tile, n - s)) for s in range(0, n, tile)]


RADIX = 4


class _HyenaDims:
    def __init__(self, L):
        assert L % RADIX == 0
        self.L = L
        self.Q = L // RADIX
        self.F = self.Q + 1
        self.QP = _round_up(self.Q, BF16_SUBLANES)
        self.QK = _round_up(self.Q, MXU_DIM)
        self.FM = _round_up(self.F, BF16_SUBLANES)
        self.IK = _round_up(2 * self.FM, MXU_DIM)


def _dft_tables(dims):
    n = 2 * dims.L

    def trig(f, t, valid):
        ang = ((f * t) % n) * (2.0 * math.pi / n)
        return (np.where(valid, np.cos(ang), 0.0).astype(BF16), np.where(valid, np.sin(ang), 0.0).astype(BF16))

    f = np.arange(dims.FM, dtype=np.int64)[:, None]
    s = np.arange(dims.QK, dtype=np.int64)[None, :]
    ok = (f < dims.F) & (s < dims.Q)
    fwd = [trig(f, RADIX * s + r, ok) for r in range(RADIX)]

    s = np.arange(dims.QP, dtype=np.int64)[:, None]
    j = np.arange(dims.IK, dtype=np.int64)[None, :]
    f = j % dims.FM
    ok = (f < dims.F) & (s < dims.Q)
    inv = []
    for r in range(RADIX):
        cos, sin = trig(f, RADIX * s + r, ok)
        inv.append(np.where(j < dims.FM, cos, np.where(j < 2 * dims.FM, sin, np.zeros_like(cos))))
    return [t for pair in fwd for t in pair], inv


def _forward_dft(tables, parts, rows):
    p = [jnp.dot(tables[2 * r][rows, :], parts[r][...], preferred_element_type=F32) for r in range(RADIX)]
    a = [jnp.dot(tables[2 * r + 1][rows, :], parts[r][...], preferred_element_type=F32) for r in range(RADIX)]
    ps02, pd02, ps13, pd13 = p[0] + p[2], p[0] - p[2], p[1] + p[3], p[1] - p[3]
    as02, ad02, as13, ad13 = a[0] + a[2], a[0] - a[2], a[1] + a[3], a[1] - a[3]
    return [(ps02 + ps13, as02 + as13),
            (pd02 + ad13, pd13 - ad02),
            (pd02 - ad13, pd13 + ad02),
            (ps02 - ps13, as13 - as02)]


def _class_weights(dims):
    f = lax.broadcasted_iota(jnp.int32, (dims.FM, 1), 0)
    n = 2.0 * dims.L
    edge = jnp.where(f == 0, 1.0, 2.0) / n
    return [jnp.where(f <= dims.Q, edge, 0.0),
            jnp.where(f < dims.Q, 2.0 / n, 0.0),
            jnp.where((f >= 1) & (f <= dims.Q), 2.0 / n, 0.0),
            jnp.where(f < dims.Q, edge, 0.0)]


def _spec_body(*refs, dims):
    tables = refs[:2 * RADIX]
    hs_ref, hd_ref = refs[2 * RADIX:2 * RADIX + 2]
    outs = refs[2 * RADIX + 2:4 * RADIX + 2]
    s_pads = refs[4 * RADIX + 2:5 * RADIX + 2]
    d_pads = refs[5 * RADIX + 2:]
    Q = dims.Q
    for pads, src in ((s_pads, hs_ref), (d_pads, hd_ref)):
        for r in range(RADIX):
            pads[r][Q:, :] = jnp.zeros((pads[r].shape[0] - Q, pads[r].shape[1]), BF16)
            pads[r][:Q, :] = src[r].astype(BF16)
    rows = slice(None)
    re = [c[0] for c in _forward_dft(tables, s_pads, rows)]
    nim = [c[1] for c in _forward_dft(tables, d_pads, rows)]
    for c, w in enumerate(_class_weights(dims)):
        outs[2 * c][...] = w * re[c]
        outs[2 * c + 1][...] = -w * nim[c]


def _filter_spectrum(dims, fwd, hs, hd):
    width = hs.shape[2]
    cw = MXU_DIM
    resident = pl.BlockSpec((dims.FM, dims.QK), lambda j: (0, 0), pipeline_mode=pl.Buffered(1))
    return pl.pallas_call(
        functools.partial(_spec_body, dims=dims),
        grid=(width // cw,),
        in_specs=[resident] * (2 * RADIX) + [pl.BlockSpec((RADIX, dims.Q, cw), lambda j: (0, 0, j))] * 2,
        out_specs=[pl.BlockSpec((dims.FM, cw), lambda j: (0, j))] * (2 * RADIX),
        out_shape=[jax.ShapeDtypeStruct((dims.FM, width), F32)] * (2 * RADIX),
        scratch_shapes=[pltpu.VMEM((dims.QK, cw), BF16)] * (2 * RADIX),
        compiler_params=_params(("arbitrary",)),
        name="hyena_filter_spectrum",
    )(*fwd, hs, hd)


def _short_conv(ref, w_ref, b_ref, anchor=None):
    u = ref[0].astype(F32)
    if anchor is not None:
        u = u + anchor
    L = u.shape[0]
    t = lax.broadcasted_iota(jnp.int32, u.shape, 0)
    prev = jnp.where(t == 0, 0.0, pltpu.roll(u, 1, 0))
    nxt = jnp.where(t == L - 1, 0.0, pltpu.roll(u, L - 1, 0))
    w = w_ref[...]
    return b_ref[...] + w[0:1] * prev + w[1:2] * u + w[2:3] * nxt


def _conv_body(*refs, dims, first):
    n_tab = 2 * RADIX
    tables = refs[:n_tab]
    inv_refs = refs[n_tab:n_tab + RADIX]
    u_ref, x_ref = refs[n_tab + RADIX:n_tab + RADIX + 2]
    at = n_tab + RADIX + 2
    n_taps = 4 if first else 2
    taps = refs[at:at + n_taps]
    at += n_taps
    skip_ref = refs[at]
    spec_refs = refs[at + 1:at + 1 + 2 * RADIX]
    o_ref = refs[at + 1 + 2 * RADIX]
    scratch = refs[at + 2 + 2 * RADIX:]
    u_parts, x_parts, g_parts = scratch[:RADIX], scratch[RADIX:2 * RADIX], scratch[2 * RADIX:3 * RADIX]
    nat_ref, natx_ref = scratch[3 * RADIX:]
    L, Q, QP, QK, FM, IK = dims.L, dims.Q, dims.QP, dims.QK, dims.FM, dims.IK
    cw = u_parts[0].shape[1]

    lane_slabs = [(k, slice(k * LANES, (k + 1) * LANES)) for k in range(cw // LANES)]

    def split_into(parts, value, stage_ref):
        for k, lanes in lane_slabs:
            stage_ref[k, L:, :] = jnp.zeros((RADIX * QP - L, LANES), F32)
            stage_ref[k, :L, :] = value[:, lanes]
            for r in range(RADIX):
                parts[r][:QP, lanes] = stage_ref[k, pl.ds(r, QP, stride=RADIX), :].astype(BF16)

    for r in range(RADIX):
        if QK > QP:
            u_parts[r][QP:, :] = jnp.zeros((QK - QP, cw), BF16)
        if not first:
            u_parts[r][:QP, :] = u_ref[0, r]
    if first:
        split_into(u_parts, _short_conv(u_ref, taps[0], taps[1]), nat_ref)

    for r in range(RADIX):
        if IK > 2 * FM:
            g_parts[r][2 * FM:, :] = jnp.zeros((IK - 2 * FM, cw), BF16)
    for lo, sz in _row_tiles(FM, MXU_DIM):
        rows = slice(lo, lo + sz)
        e1, e2 = [], []
        for c, (p, a) in enumerate(_forward_dft(tables, u_parts, rows)):
            k1, k2 = spec_refs[2 * c][rows, :], spec_refs[2 * c + 1][rows, :]
            e1.append(p * k1 + a * k2)
            e2.append(a * k1 - p * k2)
        for r, (gc, gs) in enumerate(_fold_classes(e1, e2)):
            g_parts[r][lo:lo + sz, :] = gc.astype(BF16)
            g_parts[r][FM + lo:FM + lo + sz, :] = gs.astype(BF16)

    anchor = g_parts[RADIX - 1][2 * FM - 1:2 * FM, :].astype(F32) * 0.0
    split_into(x_parts, _short_conv(x_ref, taps[-2], taps[-1], anchor), natx_ref)
    skip = skip_ref[...]
    for lo, sz in _row_tiles(QP, MXU_DIM):
        rows = slice(lo, lo + sz)
        valid = min(sz, Q - lo)
        for r in range(RADIX):
            y = jnp.dot(inv_refs[r][rows, :], g_parts[r][...], preferred_element_type=F32)
            o = x_parts[r][rows, :].astype(F32) * (y + skip * u_parts[r][rows, :].astype(F32))
            if first:
                o_ref[0, r, rows, :] = o.astype(o_ref.dtype)
            elif valid > 0:
                for k, lanes in lane_slabs:
                    nat_ref[k, pl.ds(RADIX * lo + r, valid, stride=RADIX), :] = o[:valid, lanes]
    if not first:
        for k, lanes in lane_slabs:
            o_ref[0, :, lanes] = nat_ref[k, :L, :]


def _fold_classes(e1, e2):
    a_p, a_m = e1[0] + e1[3], e1[0] - e1[3]
    b_p, b_m = e1[1] + e1[2], e1[1] - e1[2]
    c_p, c_m = e2[0] + e2[3], e2[0] - e2[3]
    d_p, d_m = e2[1] + e2[2], e2[2] - e2[1]
    return [(a_p + b_p, c_m + d_m),
            (a_m + d_p, c_p + b_m),
            (a_p - b_p, c_m - d_m),
            (a_m - d_p, c_p - b_m)]


def _long_conv(dims, fwd, inv, u_arr, u_col0, x_arr, x_col0, short_w, short_b, short_u_col0, short_x_col0,
               skip, spectrum, k_col0, width):
    B = u_arr.shape[0]
    L = dims.L
    cw = MXU_DIM
    ub, xb, kb = u_col0 // cw, x_col0 // cw, k_col0 // cw
    first = short_u_col0 is not None
    fwd_spec = pl.BlockSpec((dims.FM, dims.QK), lambda c, b: (0, 0), pipeline_mode=pl.Buffered(1))
    inv_spec = pl.BlockSpec((dims.QP, dims.IK), lambda c, b: (0, 0), pipeline_mode=pl.Buffered(1))

    def taps(col0):
        blk = col0 // cw
        return [pl.BlockSpec((HY_SHORT, cw), lambda c, b: (0, blk + c)),
                pl.BlockSpec((1, cw), lambda c, b: (0, blk + c))]

    if first:
        u_spec = pl.BlockSpec((1, L, cw), lambda c, b: (b, 0, ub + c))
        out_spec = pl.BlockSpec((1, RADIX, dims.QP, cw), lambda c, b: (b, 0, 0, c))
        out_shape = jax.ShapeDtypeStruct((B, RADIX, dims.QP, width), BF16)
    else:
        u_spec = pl.BlockSpec((1, RADIX, dims.QP, cw), lambda c, b: (b, 0, 0, ub + c))
        out_spec = pl.BlockSpec((1, L, cw), lambda c, b: (b, 0, c))
        out_shape = jax.ShapeDtypeStruct((B, L, width), F32)
    in_specs = ([fwd_spec] * (2 * RADIX) + [inv_spec] * RADIX
                + [u_spec, pl.BlockSpec((1, L, cw), lambda c, b: (b, 0, xb + c))])
    args = list(fwd) + list(inv) + [u_arr, x_arr]
    for col0 in ([short_u_col0] if first else []) + [short_x_col0]:
        in_specs += taps(col0)
        args += [short_w, short_b]
    in_specs += [pl.BlockSpec((1, cw), lambda c, b: (0, c))]
    in_specs += [pl.BlockSpec((dims.FM, cw), lambda c, b: (0, kb + c))] * (2 * RADIX)
    args += [skip.reshape(1, width)] + list(spectrum)
    stage = pltpu.VMEM((cw // LANES, RADIX * dims.QP, LANES), F32)
    return pl.pallas_call(
        functools.partial(_conv_body, dims=dims, first=first),
        grid=(width // cw, B),
        in_specs=in_specs,
        out_specs=out_spec,
        out_shape=out_shape,
        scratch_shapes=([pltpu.VMEM((dims.QK, cw), BF16)] * RADIX + [pltpu.VMEM((dims.QP, cw), BF16)] * RADIX
                        + [pltpu.VMEM((dims.IK, cw), BF16)] * RADIX + [stage, stage]),
        compiler_params=_params(("parallel", "arbitrary")),
        name="hyena_long_conv",
    )(*args)


def _filter_body(z_ref, w1_ref, b1_ref, w2_ref, b2_ref, fr_ref, w3f_ref, w3b_ref, dec_ref, hs_ref, hd_ref,
                 h2_ref, nat_ref, *, dims):
    hp = lax.Precision.HIGHEST

    @pl.when(pl.program_id(0) == 0)
    def _():
        fr = fr_ref[...]
        h1 = jnp.sin(fr * (jnp.dot(z_ref[...], w1_ref[...], precision=hp, preferred_element_type=F32)
                           + b1_ref[...]))
        h2_ref[...] = jnp.sin(fr * (jnp.dot(h1, w2_ref[...], precision=hp, preferred_element_type=F32)
                                    + b2_ref[...]))

    h2 = h2_ref[...]
    dec = dec_ref[...]
    hf = jnp.dot(h2, w3f_ref[...], precision=hp, preferred_element_type=F32) * dec
    hb = jnp.dot(h2, w3b_ref[...], precision=hp, preferred_element_type=F32) * dec
    hb = jnp.where(lax.broadcasted_iota(jnp.int32, hb.shape, 0) == 0, 0.0, hb)
    scale = lax.rsqrt(jnp.sum(hf * hf + hb * hb, axis=0, keepdims=True) + 1e-6)
    for ref, val in ((hs_ref, (hf + hb) * scale), (hd_ref, (hf - hb) * scale)):
        for k in range(val.shape[1] // LANES):
            lanes = slice(k * LANES, (k + 1) * LANES)
            nat_ref[k] = val[:, lanes]
            for r in range(RADIX):
                ref[r, :, lanes] = nat_ref[k, pl.ds(r, dims.Q, stride=RADIX), :]


def _filter_constants(L, width):
    t = np.linspace(0.0, 1.0, L)
    w = (2.0 * math.pi / L) * np.arange(L)
    bands = np.linspace(1e-4, HY_BANDS - 1, HY_BANDS)
    fw = w[:, None] * bands[None, :]
    z = np.concatenate([t[:, None], np.cos(fw), -np.sin(fw)], axis=-1)
    z = np.pad(z, ((0, 0), (0, LANES - z.shape[1])))
    deltas = np.abs(np.linspace(HY_MIN_DECAY, HY_MAX_DECAY, width))
    return z.astype(np.float32), np.exp(-t[:, None] * deltas[None, :]).astype(np.float32)


def _hyena_filters(dims, consts, w1, b1, w2, b2, w3, freq):
    L = dims.L
    z, dec = consts
    hidden = w2.shape[0]
    width = w3.shape[1] // (2 * HY_ORDER)
    cw = MXU_DIM
    per_order = width // cw

    def lanes128(a):
        return jnp.pad(a.astype(F32), [(0, 0)] * (a.ndim - 1) + [(0, LANES - a.shape[-1])])

    w1p = jnp.pad(lanes128(w1), ((0, LANES - w1.shape[0]), (0, 0)))
    w2p = jnp.pad(lanes128(w2), ((0, LANES - hidden), (0, 0)))
    w3p = jnp.pad(w3.astype(F32), ((0, LANES - hidden), (0, 0)))
    row = lambda a: lanes128(a).reshape(1, LANES)
    full = lambda shape: pl.BlockSpec(shape, lambda j: (0, 0))
    fwd_col = lambda j: (0, (j // per_order) * 2 * per_order + j % per_order)
    bwd_col = lambda j: (0, (j // per_order) * 2 * per_order + per_order + j % per_order)
    out_spec = pl.BlockSpec((RADIX, dims.Q, cw), lambda j: (0, 0, j))
    return pl.pallas_call(
        functools.partial(_filter_body, dims=dims),
        grid=(HY_ORDER * per_order,),
        in_specs=[full((L, LANES)), full((LANES, LANES)), full((1, LANES)), full((LANES, LANES)),
                  full((1, LANES)), full((1, LANES)),
                  pl.BlockSpec((LANES, cw), fwd_col), pl.BlockSpec((LANES, cw), bwd_col),
                  pl.BlockSpec((L, cw), lambda j: (0, j % per_order))],
        out_specs=[out_spec, out_spec],
        out_shape=[jax.ShapeDtypeStruct((RADIX, dims.Q, HY_ORDER * width), F32)] * 2,
        scratch_shapes=[pltpu.VMEM((L, LANES), F32), pltpu.VMEM((cw // LANES, L, LANES), F32)],
        compiler_params=_params(("arbitrary",)),
        name="hyena_filters",
    )(z, w1p, row(b1), w2p, row(b2), row(freq), w3p, w3p, dec)


def _mix_body(ret_ref, g_ref, hy_ref, gr_ref, gh_ref, h_ref, wr_ref, wh_ref, wo_ref, o_ref):
    ret = None
    for hd in range(RET_HEADS):
        cols = slice(hd * RET_DV, (hd + 1) * RET_DV)
        o = ret_ref[:, cols].astype(F32)
        o = o * lax.rsqrt(jnp.mean(o * o, axis=-1, keepdims=True) + RMS_EPS)
        g = g_ref[:, cols].astype(F32)
        part = jnp.dot((g * jax.nn.sigmoid(g) * o).astype(BF16), wr_ref[cols, :], preferred_element_type=F32)
        ret = part if ret is None else ret + part
    hyo = jnp.dot(hy_ref[...].astype(BF16), wh_ref[...], preferred_element_type=F32)
    merged = (jax.nn.sigmoid(gr_ref[...].astype(F32)) * ret
              + jax.nn.sigmoid(gh_ref[...].astype(F32)) * hyo)
    o_ref[...] = h_ref[...] + jnp.dot(merged.astype(BF16), wo_ref[...], preferred_element_type=F32)


def _mix_out(ret2, hy2, proj2, g_col0, gate_col0, h2, w_ret_o, w_hy_o, w_out):
    T, D = h2.shape
    tm = _pick_tile(T, 384, BF16_SUBLANES)
    gb = gate_col0 // D
    v_w = ret2.shape[1]
    assert g_col0 % v_w == 0
    row = lambda w: pl.BlockSpec((tm, w), lambda i: (i, 0))
    full = lambda a: pl.BlockSpec(a.shape, lambda i: (0, 0))
    return pl.pallas_call(
        _mix_body,
        grid=(T // tm,),
        in_specs=[
            row(v_w), pl.BlockSpec((tm, v_w), lambda i: (i, g_col0 // v_w)), row(hy2.shape[1]),
            pl.BlockSpec((tm, D), lambda i: (i, gb)),
            pl.BlockSpec((tm, D), lambda i: (i, gb + 1)),
            row(D), full(w_ret_o), full(w_hy_o), full(w_out),
        ],
        out_specs=row(D),
        out_shape=jax.ShapeDtypeStruct((T, D), F32),
        compiler_params=_params(("parallel",)),
        name="merge_outproj",
    )(ret2, proj2, hy2, proj2, proj2, h2, w_ret_o, w_hy_o, w_out)


EXPERT_TILE = 512
META_COLS = 8
SC_WINDOW = 64
SC_WORKERS = 32
SC_CHUNK = 2 * SC_WINDOW * SC_WORKERS
SC_SCATTER_WINDOW = 128


def _pack_bf16_pairs(x):
    n = x.shape[1] // 2
    xb = x.astype(BF16).astype(F32)
    hi = lax.bitcast_convert_type(xb[:, :n], jnp.uint32)
    lo = lax.bitcast_convert_type(xb[:, n:], jnp.uint32)
    return lax.bitcast_convert_type(hi | (lo >> 16), jnp.int32)


def _unpack_bf16_pairs(w):
    u = lax.bitcast_convert_type(w, jnp.uint32)
    hi = lax.bitcast_convert_type(u & jnp.uint32(0xFFFF0000), F32)
    lo = lax.bitcast_convert_type(u << 16, F32)
    return hi, lo


def _route(lt):
    assert EXP_PER_GROUP == 8 and N_GROUPS <= 8
    tm = lt.shape[1]
    row = lax.broadcasted_iota(jnp.int32, (8, tm), 0)
    neg = -jnp.inf
    big = jnp.int32(1 << 20)
    gl = jnp.where(row < N_GROUPS, lt[N_EXPERTS:N_EXPERTS + 8, :], neg)
    gmax = jnp.max(gl, axis=0, keepdims=True)
    p_top = 1.0 / jnp.sum(jnp.exp(gl - gmax), axis=0, keepdims=True)
    g_idx = jnp.min(jnp.where(gl == gmax, row, big), axis=0, keepdims=True)
    el = lt[0:8, :]
    for g in range(1, N_GROUPS):
        el = jnp.where(g_idx == g, lt[8 * g:8 * g + 8, :], el)
    m1 = jnp.max(el, axis=0, keepdims=True)
    i1 = jnp.min(jnp.where(el == m1, row, big), axis=0, keepdims=True)
    el2 = jnp.where(row == i1, neg, el)
    m2 = jnp.max(el2, axis=0, keepdims=True)
    i2 = jnp.min(jnp.where(el2 == m2, row, big), axis=0, keepdims=True)
    r = jnp.exp(m2 - m1)
    base = g_idx * EXP_PER_GROUP
    return base + i1, base + i2, p_top / (1.0 + r), p_top * r / (1.0 + r)


def _route_body(h_ref, g_ref, whi_ref, wlo_ref, xpk_ref, meta_ref, meta_t_ref, cnt_ref, carry_ref):
    i = pl.program_id(0)

    @pl.when(i == 0)
    def _():
        carry_ref[...] = jnp.zeros_like(carry_ref)

    xn = _rms(h_ref[...], g_ref[...])
    xpk_ref[...] = _pack_bf16_pairs(xn)
    x_hi = xn.astype(BF16)
    x_lo = (xn - x_hi.astype(F32)).astype(BF16)
    logits = (jnp.dot(x_hi, whi_ref[...], preferred_element_type=F32)
              + jnp.dot(x_lo, whi_ref[...], preferred_element_type=F32)
              + jnp.dot(x_hi, wlo_ref[...], preferred_element_type=F32))
    e0, e1, w0, w1 = _route(logits.T)
    tm = logits.shape[0]
    row = lax.broadcasted_iota(jnp.int32, (LANES, tm), 0)
    onehot_t = jnp.where((row == e0) | (row == e1), 1.0, 0.0).astype(BF16)
    ri = lax.broadcasted_iota(jnp.int32, (tm, tm), 0)
    ci = lax.broadcasted_iota(jnp.int32, (tm, tm), 1)
    earlier = jnp.where(ci < ri, 1.0, 0.0).astype(BF16)
    prefix = carry_ref[...] + _dot_nt(earlier, onehot_t)
    prefix_t = prefix.T
    r0 = jnp.sum(jnp.where(row == e0, prefix_t, 0.0), axis=0, keepdims=True)
    r1 = jnp.sum(jnp.where(row == e1, prefix_t, 0.0), axis=0, keepdims=True)
    row8 = lax.broadcasted_iota(jnp.int32, (META_COLS, tm), 0)
    meta_t = jnp.zeros((META_COLS, tm), F32)
    for c, val in enumerate((e0.astype(F32), e1.astype(F32), r0, r1, w0, w1)):
        meta_t = jnp.where(row8 == c, val, meta_t)
    meta_t_ref[...] = meta_t
    padded = jnp.concatenate([meta_t, jnp.zeros((LANES - META_COLS, tm), F32)], axis=0)
    meta_ref[...] = padded.T[:, :META_COLS]
    counts = _dot_nt(jnp.ones((8, tm), BF16), onehot_t)
    carry_ref[...] += counts[0:1, :]
    cnt_ref[...] = carry_ref[...]


def _moe_route(h2, gain, w_router):
    T, D = h2.shape
    tm = _pick_tile(T, 1152, LANES)
    return pl.pallas_call(
        _route_body,
        grid=(T // tm,),
        in_specs=[
            pl.BlockSpec((tm, D), lambda i: (i, 0)),
            pl.BlockSpec((1, D), lambda i: (0, 0)),
            pl.BlockSpec((D, LANES), lambda i: (0, 0)),
            pl.BlockSpec((D, LANES), lambda i: (0, 0)),
        ],
        out_specs=[
            pl.BlockSpec((tm, D // 2), lambda i: (i, 0)),
            pl.BlockSpec((tm, META_COLS), lambda i: (i, 0)),
            pl.BlockSpec((META_COLS, tm), lambda i: (0, i)),
            pl.BlockSpec((1, LANES), lambda i: (0, 0)),
        ],
        out_shape=[
            jax.ShapeDtypeStruct((T, D // 2), jnp.int32),
            jax.ShapeDtypeStruct((T, META_COLS), F32),
            jax.ShapeDtypeStruct((META_COLS, T), F32),
            jax.ShapeDtypeStruct((1, LANES), F32),
        ],
        scratch_shapes=[pltpu.VMEM((1, LANES), F32)],
        compiler_params=_params(("arbitrary",)),
        name="moe_route",
    )(h2, gain.reshape(1, D), *w_router)


def _sc_gather(table, idx):
    n = idx.shape[0]
    width = table.shape[1]
    win = SC_WINDOW
    assert n % SC_CHUNK == 0
    per_worker = n // SC_WORKERS
    mesh = plsc.VectorSubcoreMesh(core_axis_name="c", subcore_axis_name="s")

    @functools.partial(
        pl.kernel, out_type=jax.ShapeDtypeStruct((n, width), table.dtype), mesh=mesh,
        scratch_types=[pltpu.VMEM((per_worker,), jnp.int32), pltpu.VMEM((2, win, width), table.dtype),
                       pltpu.SemaphoreType.DMA((2,)), pltpu.SemaphoreType.DMA((2,))],
        name="sc_row_gather")
    def gather(table_hbm, idx_hbm, out_hbm, idx_v, rows_v, gsem, osem):
        worker = lax.axis_index("s") * mesh.num_cores + lax.axis_index("c")
        base = worker * per_worker
        pltpu.sync_copy(idx_hbm.at[pl.ds(base, per_worker)], idx_v)

        @pl.loop(0, per_worker, step=2 * win)
        def _(off):
            fetch = [pltpu.async_copy(table_hbm.at[idx_v.at[pl.ds(off + s * win, win)]], rows_v.at[s], gsem.at[s])
                     for s in range(2)]
            store = []
            for s in range(2):
                fetch[s].wait()
                store.append(pltpu.async_copy(rows_v.at[s], out_hbm.at[pl.ds(base + off + s * win, win)],
                                              osem.at[s]))
            for s in range(2):
                store[s].wait()

    return gather(table, idx)


def _sc_dispatch(table, dest, n_out):
    n_rows, width = table.shape
    win = SC_SCATTER_WINDOW
    assert n_rows % win == 0
    n_win = n_rows // win
    per_worker = pl.cdiv(n_win, SC_WORKERS)
    idx = jnp.pad(dest, ((0, 0), (0, per_worker * SC_WORKERS * win - n_rows)))
    idx = idx.reshape(2, per_worker, SC_WORKERS, win).transpose(0, 2, 1, 3)
    mesh = plsc.VectorSubcoreMesh(core_axis_name="c", subcore_axis_name="s")

    @functools.partial(
        pl.kernel, out_type=jax.ShapeDtypeStruct((n_out, width), table.dtype), mesh=mesh,
        scratch_types=[pltpu.VMEM((per_worker, win), jnp.int32), pltpu.VMEM((per_worker, win), jnp.int32),
                       pltpu.VMEM((win, width), table.dtype),
                       pltpu.SemaphoreType.DMA, pltpu.SemaphoreType.DMA],
        name="sc_row_dispatch")
    def dispatch(table_hbm, idx_hbm, out_hbm, idx0_v, idx1_v, rows_v, sem0, sem1):
        worker = lax.axis_index("s") * mesh.num_cores + lax.axis_index("c")
        pltpu.sync_copy(idx_hbm.at[0, worker], idx0_v)
        pltpu.sync_copy(idx_hbm.at[1, worker], idx1_v)

        @pl.loop(0, per_worker)
        def _(j):
            window = j * SC_WORKERS + worker

            @pl.when(window < n_win)
            def _():
                pltpu.sync_copy(table_hbm.at[pl.ds(window * win, win)], rows_v)
                first = pltpu.async_copy(rows_v, out_hbm.at[idx0_v.at[j]], sem0)
                second = pltpu.async_copy(rows_v, out_hbm.at[idx1_v.at[j]], sem1)
                first.wait()
                second.wait()

    return dispatch(table, idx)


def _expert_body(te_ref, tv_ref, x_ref, wg_ref, wu_ref, wd_ref, y_ref, wg_s, wu_s, wd_s):
    t = pl.program_id(0)
    half = x_ref.shape[1]

    @pl.when((t == 0) | (te_ref[t] != te_ref[jnp.maximum(t - 1, 0)]))
    def _():
        wg_s[...] = wg_ref[0].astype(BF16)
        wu_s[...] = wu_ref[0].astype(BF16)
        wd_s[...] = wd_ref[0].astype(BF16)

    @pl.when(tv_ref[t] > 0)
    def _():
        row = lax.broadcasted_iota(jnp.int32, x_ref.shape, 0)
        hi, lo = _unpack_bf16_pairs(jnp.where(row < tv_ref[t], x_ref[...], 0))
        hi, lo = hi.astype(BF16), lo.astype(BF16)
        hg = jnp.dot(hi, wg_s[:half, :], preferred_element_type=F32)
        hg += jnp.dot(lo, wg_s[half:, :], preferred_element_type=F32)
        hu = jnp.dot(hi, wu_s[:half, :], preferred_element_type=F32)
        hu += jnp.dot(lo, wu_s[half:, :], preferred_element_type=F32)
        act = (hg * jax.nn.sigmoid(hg) * hu).astype(BF16)
        y_ref[...] = _pack_bf16_pairs(jnp.dot(act, wd_s[...], preferred_element_type=F32))

    @pl.when(tv_ref[t] == 0)
    def _():
        y_ref[...] = jnp.zeros_like(y_ref)


def _moe_experts(xs, n_sorted, tile_expert, tile_valid, w_gate, w_up, w_down):
    NP, half = n_sorted, xs.shape[1]
    _, D, FF = w_gate.shape
    tr = EXPERT_TILE
    grid_spec = pltpu.PrefetchScalarGridSpec(
        num_scalar_prefetch=2,
        grid=(NP // tr,),
        in_specs=[
            pl.BlockSpec((tr, half), lambda t, te, tv: (t, 0)),
            pl.BlockSpec((1, D, FF), lambda t, te, tv: (te[t], 0, 0)),
            pl.BlockSpec((1, D, FF), lambda t, te, tv: (te[t], 0, 0)),
            pl.BlockSpec((1, FF, D), lambda t, te, tv: (te[t], 0, 0)),
        ],
        out_specs=pl.BlockSpec((tr, half), lambda t, te, tv: (t, 0)),
        scratch_shapes=[pltpu.VMEM((D, FF), BF16), pltpu.VMEM((D, FF), BF16), pltpu.VMEM((FF, D), BF16)],
    )
    return pl.pallas_call(
        _expert_body,
        grid_spec=grid_spec,
        out_shape=jax.ShapeDtypeStruct((NP, half), jnp.int32),
        compiler_params=_params(("arbitrary",)),
        name="moe_experts",
    )(tile_expert, tile_valid, xs, w_gate, w_up, w_down)


def _moe_combine_into(o_ref, h, meta, y0, y1):
    half = y0.shape[1]
    w0 = meta[:, 4:5]
    w1 = meta[:, 5:6]
    hi0, lo0 = _unpack_bf16_pairs(y0)
    hi1, lo1 = _unpack_bf16_pairs(y1)
    o_ref[:, :half] = h[:, :half] + w0 * hi0 + w1 * hi1
    o_ref[:, half:] = h[:, half:] + w0 * lo0 + w1 * lo1


def _moe(h2, gain, w_router, layer, w_gate, w_up, w_down):
    T, D = h2.shape
    E = N_EXPERTS
    tr = EXPERT_TILE
    n_sorted = _round_up(2 * T + E * (tr - 1), tr)
    t_pad = _round_up(T, SC_CHUNK // 2)

    xpk, meta, meta_t, counts = _moe_route(h2, gain, w_router)

    cnt = counts[0, :E].astype(jnp.int32)
    padded = (cnt + tr - 1) // tr * tr
    ends = jnp.cumsum(padded)
    starts = ends - padded
    eid = meta_t[0:2].astype(jnp.int32)
    pos = meta_t[2:4].astype(jnp.int32)
    for e in range(E):
        pos = pos + jnp.where(eid == e, starts[e], 0)
    tile_start = jnp.arange(n_sorted // tr, dtype=jnp.int32) * tr
    tile_expert = jnp.minimum(jnp.sum(tile_start[:, None] >= ends[None, :], axis=1), E - 1).astype(jnp.int32)
    tile_valid = jnp.clip(cnt[tile_expert] - (tile_start - starts[tile_expert]), 0, tr).astype(jnp.int32)

    xs = _sc_dispatch(xpk, pos, n_sorted)
    ys = _moe_experts(xs, n_sorted, layer * E + tile_expert, tile_valid, w_gate, w_up, w_down)
    spare = jnp.arange(t_pad - T, dtype=jnp.int32)
    back = jnp.concatenate([pos, jnp.broadcast_to(spare[None], (2, t_pad - T))], axis=1)
    yg = _sc_gather(ys, back.reshape(-1)).reshape(2, t_pad, D // 2)
    return meta, yg


def _final_body(h_ref, meta_ref, y0_ref, y1_ref, g_ref, o_ref, hn_ref):
    _moe_combine_into(hn_ref, h_ref[0], meta_ref[0], y0_ref[0], y1_ref[0])
    o_ref[0] = _rms(hn_ref[N_META:, :], g_ref[...])


def _final_norm(h3, meta3, yg, gain, b0, nb):
    _, L, D = h3.shape
    return pl.pallas_call(
        _final_body,
        grid=(nb,),
        in_specs=[pl.BlockSpec((1, L, D), lambda b: (b0 + b, 0, 0)),
                  pl.BlockSpec((1, L, meta3.shape[2]), lambda b: (b0 + b, 0, 0)),
                  pl.BlockSpec((1, L, D // 2), lambda b: (0, b0 + b, 0)),
                  pl.BlockSpec((1, L, D // 2), lambda b: (1, b0 + b, 0)),
                  pl.BlockSpec((1, D), lambda b: (0, 0))],
        out_specs=pl.BlockSpec((1, L - N_META, D), lambda b: (b, 0, 0)),
        out_shape=jax.ShapeDtypeStruct((nb, L - N_META, D), F32),
        scratch_shapes=[pltpu.VMEM((L, D), F32)],
        compiler_params=_params(("parallel",)),
        name="combine_final_norm",
    )(h3, meta3, yg, yg, gain.reshape(1, D))


def kernel(x_prompt, x_sample, meta_tokens, norm_mix, w_in, ret_decay_fwd, ret_decay_bwd, hy_short_w, hy_short_b, hy_filt_w1, hy_filt_b1, hy_filt_w2, hy_filt_b2, hy_filt_w3, hy_sin_freq, hy_skip, w_ret_o, w_hy_o, w_out, norm_ffn, router_group, router_expert, moe_w_gate, moe_w_up, moe_w_down, norm_final):
    assert x_prompt.shape[1:] == x_sample.shape[1:]
    nbp, nbs = x_prompt.shape[0], x_sample.shape[0]
    B = nbp + nbs
    D = x_prompt.shape[2]
    L = N_META + x_prompt.shape[1]
    T = B * L
    depth = w_in.shape[0]
    q_w = RET_HEADS * RET_DK
    v_w = RET_HEADS * RET_DV
    hy_w = hy_skip.shape[2]
    hy_col0 = 2 * q_w + 2 * v_w
    gate_col0 = hy_col0 + 3 * hy_w
    assert D == q_w and w_in.shape[2] == gate_col0 + 2 * D

    x = jnp.concatenate([x_prompt, x_sample], axis=0)
    meta = jnp.broadcast_to(meta_tokens[None].astype(x.dtype), (B, N_META, D))
    h = jnp.concatenate([meta, x], axis=1).reshape(T, D)

    cos_t, sin_t = _rotary_tables(L)
    dims = _HyenaDims(L)
    dft_fwd, dft_inv = _dft_tables(dims)
    filt_consts = _filter_constants(L, hy_w)
    router = jnp.concatenate([router_expert, router_group], axis=2).astype(F32)
    router = jnp.pad(router, ((0, 0), (0, 0), (0, LANES - router.shape[2])))
    router_hi = router.astype(BF16)
    router_lo = (router - router_hi.astype(F32)).astype(BF16)
    assert moe_w_gate.shape[1] == N_EXPERTS
    w_gate_all = moe_w_gate.reshape((-1,) + moe_w_gate.shape[2:])
    w_up_all = moe_w_up.reshape((-1,) + moe_w_up.shape[2:])
    w_down_all = moe_w_down.reshape((-1,) + moe_w_down.shape[2:])

    moe = None
    for i in range(depth):
        proj, h = _inproj(h, norm_mix[i], w_in[i].astype(BF16), moe)
        proj3 = proj.reshape(B, L, -1)

        lg = jnp.stack([jax.nn.log_sigmoid(ret_decay_fwd[i].astype(F32)),
                        jax.nn.log_sigmoid(ret_decay_bwd[i].astype(F32))])
        ret = _retention(proj3, lg, cos_t, sin_t)

        hs, hd = _hyena_filters(dims, filt_consts, hy_filt_w1[i], hy_filt_b1[i], hy_filt_w2[i], hy_filt_b2[i],
                                hy_filt_w3[i], hy_sin_freq[i])
        spectrum = _filter_spectrum(dims, dft_fwd, hs, hd)
        sw = hy_short_w[i].astype(F32)
        sb = hy_short_b[i].astype(F32).reshape(1, -1)
        z = _long_conv(dims, dft_fwd, dft_inv, proj3, hy_col0, proj3, hy_col0 + hy_w, sw, sb, 0, hy_w,
                       hy_skip[i, 0].astype(F32), spectrum, 0, hy_w)
        z = _long_conv(dims, dft_fwd, dft_inv, z, 0, proj3, hy_col0 + 2 * hy_w, sw, sb, None, 2 * hy_w,
                       hy_skip[i, 1].astype(F32), spectrum, hy_w, hy_w)

        h = _mix_out(ret.reshape(T, v_w), z.reshape(T, hy_w), proj, 2 * q_w + v_w, gate_col0, h,
                     w_ret_o[i].astype(BF16), w_hy_o[i].astype(BF16), w_out[i].astype(BF16))
        moe = _moe(h, norm_ffn[i], (router_hi[i], router_lo[i]), i, w_gate_all, w_up_all, w_down_all)

    h3 = h.reshape(B, L, D)
    meta3 = moe[0].reshape(B, L, -1)
    return (_final_norm(h3, meta3, moe[1], norm_final, 0, nbp),
            _final_norm(h3, meta3, moe[1], norm_final, nbp, nbs))
```

```python
import functools
import math

import jax
import jax.numpy as jnp
import numpy as np
from jax import lax
from jax.experimental import pallas as pl
from jax.experimental.pallas import tpu as pltpu
from jax.experimental.pallas import tpu_sc as plsc

N_META = 16
RET_HEADS = 8
RET_DK = 128
RET_DV = 256
ROPE_THETA = 10000.0
HY_ORDER = 2
HY_SHORT = 3
HY_EMB = 33
HY_BANDS = (HY_EMB - 1) // 2
HY_DECAY_TARGET = 1e-2
HY_MIN_DECAY = math.log(HY_DECAY_TARGET) / 1.5
HY_MAX_DECAY = math.log(HY_DECAY_TARGET) / 0.3
N_GROUPS = 4
EXP_PER_GROUP = 8
N_EXPERTS = N_GROUPS * EXP_PER_GROUP
RMS_EPS = 1e-6

LANES = 128
BF16_SUBLANES = 16
MXU_DIM = 256
RET_CHUNK = 256
VMEM_LIMIT = 56 * 1024 * 1024

F32 = jnp.float32
BF16 = jnp.bfloat16


def _round_up(n, m):
    return (n + m - 1) // m * m


def _pick_tile(n, target, mult):
    best = None
    for t in range(mult, min(n, target) + 1, mult):
        if n % t == 0:
            best = t
    assert best is not None, (n, target, mult)
    return best


def _params(sem):
    return pltpu.CompilerParams(dimension_semantics=sem, vmem_limit_bytes=VMEM_LIMIT)


def _rms(x, gain):
    ms = jnp.mean(x * x, axis=-1, keepdims=True)
    return x * lax.rsqrt(ms + RMS_EPS) * gain


def _inproj_body(h_ref, g_ref, w_ref, o_ref, xn_ref):
    @pl.when(pl.program_id(1) == 0)
    def _():
        xn_ref[...] = _rms(h_ref[...], g_ref[...]).astype(BF16)

    o_ref[...] = jnp.dot(xn_ref[...], w_ref[...], preferred_element_type=F32).astype(o_ref.dtype)


def _inproj_moe_body(h_ref, meta_ref, y0_ref, y1_ref, g_ref, w_ref, o_ref, hn_ref, xn_ref):
    @pl.when(pl.program_id(1) == 0)
    def _():
        _moe_combine_into(hn_ref, h_ref[...], meta_ref[...], y0_ref[0], y1_ref[0])
        xn_ref[...] = _rms(hn_ref[...], g_ref[...]).astype(BF16)

    o_ref[...] = jnp.dot(xn_ref[...], w_ref[...], preferred_element_type=F32).astype(o_ref.dtype)


def _inproj(h2, gain, w, moe=None):
    T, D = h2.shape
    nc = w.shape[1]
    tm = _pick_tile(T, 1152, BF16_SUBLANES)
    tn = _pick_tile(nc, 2816 if moe is None else 1408, LANES)
    common = dict(
        grid=(T // tm, nc // tn),
        scratch_shapes=[pltpu.VMEM((tm, D), BF16)],
        compiler_params=_params(("parallel", "arbitrary")),
    )
    h_spec = pl.BlockSpec((tm, D), lambda i, j: (i, 0))
    g_spec = pl.BlockSpec((1, D), lambda i, j: (0, 0))
    w_spec = pl.BlockSpec((D, tn), lambda i, j: (0, j))
    o_spec = pl.BlockSpec((tm, tn), lambda i, j: (i, j))
    o_shape = jax.ShapeDtypeStruct((T, nc), BF16)
    if moe is None:
        proj = pl.pallas_call(_inproj_body, in_specs=[h_spec, g_spec, w_spec], out_specs=o_spec,
                              out_shape=o_shape, name="norm_inproj", **common)(h2, gain.reshape(1, D), w)
        return proj, h2
    meta, yg = moe
    return pl.pallas_call(
        _inproj_moe_body,
        in_specs=[h_spec, pl.BlockSpec((tm, meta.shape[1]), lambda i, j: (i, 0)),
                  pl.BlockSpec((1, tm, D // 2), lambda i, j: (0, i, 0)),
                  pl.BlockSpec((1, tm, D // 2), lambda i, j: (1, i, 0)),
                  g_spec, w_spec],
        out_specs=[o_spec, h_spec],
        out_shape=[o_shape, jax.ShapeDtypeStruct((T, D), F32)],
        name="combine_norm_inproj", **common,
    )(h2, meta, yg, yg, gain.reshape(1, D), w)


def _dot_t(a, b):
    return lax.dot_general(a, b, (((0,), (0,)), ((), ())), preferred_element_type=F32)


def _dot_nt(a, b):
    return lax.dot_general(a, b, (((1,), (1,)), ((), ())), preferred_element_type=F32)


def _ret_body(lg_ref, q_ref, k_ref, v_ref, cos_ref, sin_ref, o_ref, rb_ref, kr_ref, *, seq_len):
    C = RET_CHUNK
    L = seq_len
    n_chunks = pl.cdiv(L, C)
    head = pl.program_id(1)
    lgf = lg_ref[0, head]
    lgb = lg_ref[1, head]

    def chunk(ref, n):
        lo, hi = n * C, min((n + 1) * C, L)
        x = ref[0, lo:hi, :]
        if hi - lo < C:
            x = jnp.concatenate([x, jnp.zeros((C - (hi - lo), x.shape[1]), x.dtype)], axis=0)
        return x

    def rotary(ref, n):
        x = chunk(ref, n).astype(F32)
        sl = slice(n * C, (n + 1) * C)
        return x * cos_ref[sl, :] + pltpu.roll(x, RET_DK // 2, 1) * sin_ref[sl, :]

    row = lax.broadcasted_iota(jnp.int32, (C, LANES), 0).astype(F32)
    ri = lax.broadcasted_iota(jnp.int32, (C, C), 0).astype(F32)
    ci = lax.broadcasted_iota(jnp.int32, (C, C), 1).astype(F32)
    decay = jnp.exp(jnp.where(ci <= ri, (ri - ci) * lgf, (ci - ri) * lgb))
    qf_dec = jnp.exp((row + 1.0) * lgf)
    qb_dec = jnp.exp((C - row) * lgb)
    kf_dec = jnp.exp((C - 1.0 - row) * lgf)
    kb_dec = jnp.exp(row * lgb)
    cf = jnp.exp(C * lgf)
    cb = jnp.exp(C * lgb)

    state = jnp.zeros((RET_DK, RET_DV), F32)
    for n in reversed(range(n_chunks)):
        rb_ref[n] = state.astype(BF16)
        if n > 0:
            kr_ref[n] = rotary(k_ref, n)
            kb = (kr_ref[n] * kb_dec).astype(BF16)
            state = cb * state + _dot_t(kb, chunk(v_ref, n))

    state = jnp.zeros((RET_DK, RET_DV), F32)
    for n in range(n_chunks):
        q = rotary(q_ref, n) * (RET_DK ** -0.5)
        k = kr_ref[n] if n > 0 else rotary(k_ref, n)
        v = chunk(v_ref, n)
        scores = _dot_nt(q.astype(BF16), k.astype(BF16)) * decay
        o = jnp.dot(scores.astype(BF16), v, preferred_element_type=F32)
        o += jnp.dot((q * qf_dec).astype(BF16), state.astype(BF16), preferred_element_type=F32)
        o += jnp.dot((q * qb_dec).astype(BF16), rb_ref[n], preferred_element_type=F32)
        if n + 1 < n_chunks:
            state = cf * state + _dot_t((k * kf_dec).astype(BF16), v)
        lo, hi = n * C, min((n + 1) * C, L)
        o_ref[0, lo:hi, :] = o[: hi - lo].astype(o_ref.dtype)


def _retention(proj3, lg, cos_t, sin_t):
    B, L, _ = proj3.shape
    n_chunks = pl.cdiv(L, RET_CHUNK)
    lp = n_chunks * RET_CHUNK
    k_blk = RET_HEADS
    v_blk = 2 * RET_HEADS * RET_DK // RET_DV
    return pl.pallas_call(
        functools.partial(_ret_body, seq_len=L),
        grid=(B, RET_HEADS),
        in_specs=[
            pl.BlockSpec(memory_space=pltpu.SMEM),
            pl.BlockSpec((1, L, RET_DK), lambda b, h: (b, 0, h)),
            pl.BlockSpec((1, L, RET_DK), lambda b, h: (b, 0, k_blk + h)),
            pl.BlockSpec((1, L, RET_DV), lambda b, h: (b, 0, v_blk + h)),
            pl.BlockSpec((lp, RET_DK), lambda b, h: (0, 0)),
            pl.BlockSpec((lp, RET_DK), lambda b, h: (0, 0)),
        ],
        out_specs=pl.BlockSpec((1, L, RET_DV), lambda b, h: (b, 0, h)),
        out_shape=jax.ShapeDtypeStruct((B, L, RET_HEADS * RET_DV), BF16),
        scratch_shapes=[pltpu.VMEM((n_chunks, RET_DK, RET_DV), BF16),
                        pltpu.VMEM((n_chunks, RET_CHUNK, RET_DK), F32)],
        compiler_params=_params(("parallel", "arbitrary")),
        name="retention",
    )(lg, proj3, proj3, proj3, cos_t, sin_t)


def _rotary_tables(L):
    half = RET_DK // 2
    lp = _round_up(L, RET_CHUNK)
    inv = ROPE_THETA ** (-jnp.arange(half, dtype=F32) / half)
    ang = jnp.arange(lp, dtype=F32)[:, None] * inv[None, :]
    cos, sin = jnp.cos(ang), jnp.sin(ang)
    return jnp.concatenate([cos, cos], axis=1), jnp.concatenate([-sin, sin], axis=1)


def _row_tiles(n, tile):
    return [(s, min(tile, n - s)) for s in range(0, n, tile)]


RADIX = 4


class _HyenaDims:
    def __init__(self, L):
        assert L % RADIX == 0
        self.L = L
        self.Q = L // RADIX
        self.F = self.Q + 1
        self.QP = _round_up(self.Q, BF16_SUBLANES)
        self.QK = _round_up(self.Q, MXU_DIM)
        self.FM = _round_up(self.F, BF16_SUBLANES)
        self.IK = _round_up(2 * self.FM, MXU_DIM)


def _dft_tables(dims):
    n = 2 * dims.L

    def trig(f, t, valid):
        ang = ((f * t) % n) * (2.0 * math.pi / n)
        return (np.where(valid, np.cos(ang), 0.0).astype(BF16), np.where(valid, np.sin(ang), 0.0).astype(BF16))

    f = np.arange(dims.FM, dtype=np.int64)[:, None]
    s = np.arange(dims.QK, dtype=np.int64)[None, :]
    ok = (f < dims.F) & (s < dims.Q)
    fwd = [trig(f, RADIX * s + r, ok) for r in range(RADIX)]

    s = np.arange(dims.QP, dtype=np.int64)[:, None]
    j = np.arange(dims.IK, dtype=np.int64)[None, :]
    f = j % dims.FM
    ok = (f < dims.F) & (s < dims.Q)
    inv = []
    for r in range(RADIX):
        cos, sin = trig(f, RADIX * s + r, ok)
        inv.append(np.where(j < dims.FM, cos, np.where(j < 2 * dims.FM, sin, np.zeros_like(cos))))
    return [t for pair in fwd for t in pair], inv


def _forward_dft(tables, parts, rows, dtype=F32):
    p = [jnp.dot(tables[2 * r][rows, :], parts[r][...], preferred_element_type=F32).astype(dtype)
         for r in range(RADIX)]
    a = [jnp.dot(tables[2 * r + 1][rows, :], parts[r][...], preferred_element_type=F32).astype(dtype)
         for r in range(RADIX)]
    ps02, pd02, ps13, pd13 = p[0] + p[2], p[0] - p[2], p[1] + p[3], p[1] - p[3]
    as02, ad02, as13, ad13 = a[0] + a[2], a[0] - a[2], a[1] + a[3], a[1] - a[3]
    return [(ps02 + ps13, as02 + as13),
            (pd02 + ad13, pd13 - ad02),
            (pd02 - ad13, pd13 + ad02),
            (ps02 - ps13, as13 - as02)]


def _class_weights(dims):
    f = lax.broadcasted_iota(jnp.int32, (dims.FM, 1), 0)
    n = 2.0 * dims.L
    edge = jnp.where(f == 0, 1.0, 2.0) / n
    return [jnp.where(f <= dims.Q, edge, 0.0),
            jnp.where(f < dims.Q, 2.0 / n, 0.0),
            jnp.where((f >= 1) & (f <= dims.Q), 2.0 / n, 0.0),
            jnp.where(f < dims.Q, edge, 0.0)]


def _spec_body(*refs, dims):
    tables = refs[:2 * RADIX]
    hs_ref, hd_ref = refs[2 * RADIX:2 * RADIX + 2]
    outs = refs[2 * RADIX + 2:4 * RADIX + 2]
    s_pads = refs[4 * RADIX + 2:5 * RADIX + 2]
    d_pads = refs[5 * RADIX + 2:]
    Q = dims.Q
    for pads, src in ((s_pads, hs_ref), (d_pads, hd_ref)):
        for r in range(RADIX):
            pads[r][Q:, :] = jnp.zeros((pads[r].shape[0] - Q, pads[r].shape[1]), BF16)
            pads[r][:Q, :] = src[r].astype(BF16)
    rows = slice(None)
    re = [c[0] for c in _forward_dft(tables, s_pads, rows)]
    nim = [c[1] for c in _forward_dft(tables, d_pads, rows)]
    for c, w in enumerate(_class_weights(dims)):
        outs[2 * c][...] = (w * re[c]).astype(outs[2 * c].dtype)
        outs[2 * c + 1][...] = (-w * nim[c]).astype(outs[2 * c + 1].dtype)


def _filter_spectrum(dims, fwd, hs, hd):
    width = hs.shape[2]
    cw = MXU_DIM
    resident = pl.BlockSpec((dims.FM, dims.QK), lambda j: (0, 0), pipeline_mode=pl.Buffered(1))
    return pl.pallas_call(
        functools.partial(_spec_body, dims=dims),
        grid=(width // cw,),
        in_specs=[resident] * (2 * RADIX) + [pl.BlockSpec((RADIX, dims.Q, cw), lambda j: (0, 0, j))] * 2,
        out_specs=[pl.BlockSpec((dims.FM, cw), lambda j: (0, j))] * (2 * RADIX),
        out_shape=[jax.ShapeDtypeStruct((dims.FM, width), BF16)] * (2 * RADIX),
        scratch_shapes=[pltpu.VMEM((dims.QK, cw), BF16)] * (2 * RADIX),
        compiler_params=_params(("arbitrary",)),
        name="hyena_filter_spectrum",
    )(*fwd, hs, hd)


def _short_conv(ref, w_ref, b_ref, anchor=None):
    u = ref[0].astype(F32)
    if anchor is not None:
        u = u + anchor
    L = u.shape[0]
    t = lax.broadcasted_iota(jnp.int32, u.shape, 0)
    prev = jnp.where(t == 0, 0.0, pltpu.roll(u, 1, 0))
    nxt = jnp.where(t == L - 1, 0.0, pltpu.roll(u, L - 1, 0))
    w = w_ref[...]
    return b_ref[...] + w[0:1] * prev + w[1:2] * u + w[2:3] * nxt


def _conv_body(*refs, dims, first):
    n_tab = 2 * RADIX
    tables = refs[:n_tab]
    inv_refs = refs[n_tab:n_tab + RADIX]
    u_ref, x_ref = refs[n_tab + RADIX:n_tab + RADIX + 2]
    at = n_tab + RADIX + 2
    n_taps = 4 if first else 2
    taps = refs[at:at + n_taps]
    at += n_taps
    skip_ref = refs[at]
    spec_refs = refs[at + 1:at + 1 + 2 * RADIX]
    o_ref = refs[at + 1 + 2 * RADIX]
    scratch = refs[at + 2 + 2 * RADIX:]
    u_parts, x_parts, g_parts = scratch[:RADIX], scratch[RADIX:2 * RADIX], scratch[2 * RADIX:3 * RADIX]
    nat_ref, natx_ref = scratch[3 * RADIX:]
    L, Q, QP, QK, FM, IK = dims.L, dims.Q, dims.QP, dims.QK, dims.FM, dims.IK
    cw = u_parts[0].shape[1]

    lane_slabs = [(k, slice(k * LANES, (k + 1) * LANES)) for k in range(cw // LANES)]

    def split_into(parts, value, stage_ref):
        for k, lanes in lane_slabs:
            stage_ref[k, L:, :] = jnp.zeros((RADIX * QP - L, LANES), F32)
            stage_ref[k, :L, :] = value[:, lanes]
            for r in range(RADIX):
                parts[r][:QP, lanes] = stage_ref[k, pl.ds(r, QP, stride=RADIX), :].astype(BF16)

    for r in range(RADIX):
        if QK > QP:
            u_parts[r][QP:, :] = jnp.zeros((QK - QP, cw), BF16)
        if not first:
            u_parts[r][:QP, :] = u_ref[0, r]
    if first:
        split_into(u_parts, _short_conv(u_ref, taps[0], taps[1]), nat_ref)

    for r in range(RADIX):
        if IK > 2 * FM:
            g_parts[r][2 * FM:, :] = jnp.zeros((IK - 2 * FM, cw), BF16)
    for lo, sz in _row_tiles(FM, MXU_DIM):
        rows = slice(lo, lo + sz)
        e1, e2 = [], []
        for c, (p, a) in enumerate(_forward_dft(tables, u_parts, rows, BF16)):
            k1, k2 = spec_refs[2 * c][rows, :], spec_refs[2 * c + 1][rows, :]
            e1.append(p * k1 + a * k2)
            e2.append(a * k1 - p * k2)
        for r, (gc, gs) in enumerate(_fold_classes(e1, e2)):
            g_parts[r][lo:lo + sz, :] = gc
            g_parts[r][FM + lo:FM + lo + sz, :] = gs

    anchor = g_parts[RADIX - 1][2 * FM - 1:2 * FM, :].astype(F32) * 0.0
    split_into(x_parts, _short_conv(x_ref, taps[-2], taps[-1], anchor), natx_ref)
    skip = skip_ref[...]
    for lo, sz in _row_tiles(QP, MXU_DIM):
        rows = slice(lo, lo + sz)
        valid = min(sz, Q - lo)
        for r in range(RADIX):
            y = jnp.dot(inv_refs[r][rows, :], g_parts[r][...], preferred_element_type=F32)
            o = x_parts[r][rows, :].astype(F32) * (y + skip * u_parts[r][rows, :].astype(F32))
            if first:
                o_ref[0, r, rows, :] = o.astype(o_ref.dtype)
            elif valid > 0:
                for k, lanes in lane_slabs:
                    nat_ref[k, pl.ds(RADIX * lo + r, valid, stride=RADIX), :] = o[:valid, lanes]
    if not first:
        for k, lanes in lane_slabs:
            o_ref[0, :, lanes] = nat_ref[k, :L, :]


def _fold_classes(e1, e2):
    a_p, a_m = e1[0] + e1[3], e1[0] - e1[3]
    b_p, b_m = e1[1] + e1[2], e1[1] - e1[2]
    c_p, c_m = e2[0] + e2[3], e2[0] - e2[3]
    d_p, d_m = e2[1] + e2[2], e2[2] - e2[1]
    return [(a_p + b_p, c_m + d_m),
            (a_m + d_p, c_p + b_m),
            (a_p - b_p, c_m - d_m),
            (a_m - d_p, c_p - b_m)]


def _long_conv(dims, fwd, inv, u_arr, u_col0, x_arr, x_col0, short_w, short_b, short_u_col0, short_x_col0,
               skip, spectrum, k_col0, width):
    B = u_arr.shape[0]
    L = dims.L
    cw = MXU_DIM
    ub, xb, kb = u_col0 // cw, x_col0 // cw, k_col0 // cw
    first = short_u_col0 is not None
    fwd_spec = pl.BlockSpec((dims.FM, dims.QK), lambda c, b: (0, 0), pipeline_mode=pl.Buffered(1))
    inv_spec = pl.BlockSpec((dims.QP, dims.IK), lambda c, b: (0, 0), pipeline_mode=pl.Buffered(1))

    def taps(col0):
        blk = col0 // cw
        return [pl.BlockSpec((HY_SHORT, cw), lambda c, b: (0, blk + c)),
                pl.BlockSpec((1, cw), lambda c, b: (0, blk + c))]

    if first:
        u_spec = pl.BlockSpec((1, L, cw), lambda c, b: (b, 0, ub + c))
        out_spec = pl.BlockSpec((1, RADIX, dims.QP, cw), lambda c, b: (b, 0, 0, c))
        out_shape = jax.ShapeDtypeStruct((B, RADIX, dims.QP, width), BF16)
    else:
        u_spec = pl.BlockSpec((1, RADIX, dims.QP, cw), lambda c, b: (b, 0, 0, ub + c))
        out_spec = pl.BlockSpec((1, L, cw), lambda c, b: (b, 0, c))
        out_shape = jax.ShapeDtypeStruct((B, L, width), F32)
    in_specs = ([fwd_spec] * (2 * RADIX) + [inv_spec] * RADIX
                + [u_spec, pl.BlockSpec((1, L, cw), lambda c, b: (b, 0, xb + c))])
    args = list(fwd) + list(inv) + [u_arr, x_arr]
    for col0 in ([short_u_col0] if first else []) + [short_x_col0]:
        in_specs += taps(col0)
        args += [short_w, short_b]
    in_specs += [pl.BlockSpec((1, cw), lambda c, b: (0, c))]
    in_specs += [pl.BlockSpec((dims.FM, cw), lambda c, b: (0, kb + c))] * (2 * RADIX)
    args += [skip.reshape(1, width)] + list(spectrum)
    stage = pltpu.VMEM((cw // LANES, RADIX * dims.QP, LANES), F32)
    return pl.pallas_call(
        functools.partial(_conv_body, dims=dims, first=first),
        grid=(width // cw, B),
        in_specs=in_specs,
        out_specs=out_spec,
        out_shape=out_shape,
        scratch_shapes=([pltpu.VMEM((dims.QK, cw), BF16)] * RADIX + [pltpu.VMEM((dims.QP, cw), BF16)] * RADIX
                        + [pltpu.VMEM((dims.IK, cw), BF16)] * RADIX + [stage, stage]),
        compiler_params=_params(("parallel", "arbitrary")),
        name="hyena_long_conv",
    )(*args)


def _filter_body(z_ref, w1_ref, b1_ref, w2_ref, b2_ref, fr_ref, w3f_ref, w3b_ref, dec_ref, hs_ref, hd_ref,
                 h2_ref, nat_ref, *, dims):
    hp = lax.Precision.HIGHEST

    @pl.when(pl.program_id(0) == 0)
    def _():
        fr = fr_ref[...]
        h1 = jnp.sin(fr * (jnp.dot(z_ref[...], w1_ref[...], precision=hp, preferred_element_type=F32)
                           + b1_ref[...]))
        h2_ref[...] = jnp.sin(fr * (jnp.dot(h1, w2_ref[...], precision=hp, preferred_element_type=F32)
                                    + b2_ref[...]))

    h2 = h2_ref[...]
    dec = dec_ref[...]
    hf = jnp.dot(h2, w3f_ref[...], precision=hp, preferred_element_type=F32) * dec
    hb = jnp.dot(h2, w3b_ref[...], precision=hp, preferred_element_type=F32) * dec
    hb = jnp.where(lax.broadcasted_iota(jnp.int32, hb.shape, 0) == 0, 0.0, hb)
    scale = lax.rsqrt(jnp.sum(hf * hf + hb * hb, axis=0, keepdims=True) + 1e-6)
    for ref, val in ((hs_ref, (hf + hb) * scale), (hd_ref, (hf - hb) * scale)):
        for k in range(val.shape[1] // LANES):
            lanes = slice(k * LANES, (k + 1) * LANES)
            nat_ref[k] = val[:, lanes]
            for r in range(RADIX):
                ref[r, :, lanes] = nat_ref[k, pl.ds(r, dims.Q, stride=RADIX), :]


def _filter_constants(L, width):
    t = np.linspace(0.0, 1.0, L)
    w = (2.0 * math.pi / L) * np.arange(L)
    bands = np.linspace(1e-4, HY_BANDS - 1, HY_BANDS)
    fw = w[:, None] * bands[None, :]
    z = np.concatenate([t[:, None], np.cos(fw), -np.sin(fw)], axis=-1)
    z = np.pad(z, ((0, 0), (0, LANES - z.shape[1])))
    deltas = np.abs(np.linspace(HY_MIN_DECAY, HY_MAX_DECAY, width))
    return z.astype(np.float32), np.exp(-t[:, None] * deltas[None, :]).astype(np.float32)


def _hyena_filters(dims, consts, w1, b1, w2, b2, w3, freq):
    L = dims.L
    z, dec = consts
    hidden = w2.shape[0]
    width = w3.shape[1] // (2 * HY_ORDER)
    cw = MXU_DIM
    per_order = width // cw

    def lanes128(a):
        return jnp.pad(a.astype(F32), [(0, 0)] * (a.ndim - 1) + [(0, LANES - a.shape[-1])])

    w1p = jnp.pad(lanes128(w1), ((0, LANES - w1.shape[0]), (0, 0)))
    w2p = jnp.pad(lanes128(w2), ((0, LANES - hidden), (0, 0)))
    w3p = jnp.pad(w3.astype(F32), ((0, LANES - hidden), (0, 0)))
    row = lambda a: lanes128(a).reshape(1, LANES)
    full = lambda shape: pl.BlockSpec(shape, lambda j: (0, 0))
    fwd_col = lambda j: (0, (j // per_order) * 2 * per_order + j % per_order)
    bwd_col = lambda j: (0, (j // per_order) * 2 * per_order + per_order + j % per_order)
    out_spec = pl.BlockSpec((RADIX, dims.Q, cw), lambda j: (0, 0, j))
    return pl.pallas_call(
        functools.partial(_filter_body, dims=dims),
        grid=(HY_ORDER * per_order,),
        in_specs=[full((L, LANES)), full((LANES, LANES)), full((1, LANES)), full((LANES, LANES)),
                  full((1, LANES)), full((1, LANES)),
                  pl.BlockSpec((LANES, cw), fwd_col), pl.BlockSpec((LANES, cw), bwd_col),
                  pl.BlockSpec((L, cw), lambda j: (0, j % per_order))],
        out_specs=[out_spec, out_spec],
        out_shape=[jax.ShapeDtypeStruct((RADIX, dims.Q, HY_ORDER * width), F32)] * 2,
        scratch_shapes=[pltpu.VMEM((L, LANES), F32), pltpu.VMEM((cw // LANES, L, LANES), F32)],
        compiler_params=_params(("arbitrary",)),
        name="hyena_filters",
    )(z, w1p, row(b1), w2p, row(b2), row(freq), w3p, w3p, dec)


def _mix_body(ret_ref, g_ref, hy_ref, gr_ref, gh_ref, h_ref, wr_ref, wh_ref, wo_ref, o_ref):
    ret = None
    for hd in range(RET_HEADS):
        cols = slice(hd * RET_DV, (hd + 1) * RET_DV)
        o = ret_ref[:, cols].astype(F32)
        o = o * lax.rsqrt(jnp.mean(o * o, axis=-1, keepdims=True) + RMS_EPS)
        g = g_ref[:, cols].astype(F32)
        part = jnp.dot((g * jax.nn.sigmoid(g) * o).astype(BF16), wr_ref[cols, :], preferred_element_type=F32)
        ret = part if ret is None else ret + part
    hyo = jnp.dot(hy_ref[...].astype(BF16), wh_ref[...], preferred_element_type=F32)
    merged = (jax.nn.sigmoid(gr_ref[...].astype(F32)) * ret
              + jax.nn.sigmoid(gh_ref[...].astype(F32)) * hyo)
    o_ref[...] = h_ref[...] + jnp.dot(merged.astype(BF16), wo_ref[...], preferred_element_type=F32)


def _mix_out(ret2, hy2, proj2, g_col0, gate_col0, h2, w_ret_o, w_hy_o, w_out):
    T, D = h2.shape
    tm = _pick_tile(T, 384, BF16_SUBLANES)
    gb = gate_col0 // D
    v_w = ret2.shape[1]
    assert g_col0 % v_w == 0
    row = lambda w: pl.BlockSpec((tm, w), lambda i: (i, 0))
    full = lambda a: pl.BlockSpec(a.shape, lambda i: (0, 0))
    return pl.pallas_call(
        _mix_body,
        grid=(T // tm,),
        in_specs=[
            row(v_w), pl.BlockSpec((tm, v_w), lambda i: (i, g_col0 // v_w)), row(hy2.shape[1]),
            pl.BlockSpec((tm, D), lambda i: (i, gb)),
            pl.BlockSpec((tm, D), lambda i: (i, gb + 1)),
            row(D), full(w_ret_o), full(w_hy_o), full(w_out),
        ],
        out_specs=row(D),
        out_shape=jax.ShapeDtypeStruct((T, D), F32),
        compiler_params=_params(("parallel",)),
        name="merge_outproj",
    )(ret2, proj2, hy2, proj2, proj2, h2, w_ret_o, w_hy_o, w_out)


EXPERT_TILE = 512
META_COLS = 8
SC_WINDOW = 64
SC_WORKERS = 32
SC_CHUNK = 2 * SC_WINDOW * SC_WORKERS
SC_SCATTER_WINDOW = 128


def _pack_bf16_pairs(x):
    n = x.shape[1] // 2
    xb = x.astype(BF16).astype(F32)
    hi = lax.bitcast_convert_type(xb[:, :n], jnp.uint32)
    lo = lax.bitcast_convert_type(xb[:, n:], jnp.uint32)
    return lax.bitcast_convert_type(hi | (lo >> 16), jnp.int32)


def _unpack_bf16_pairs(w):
    u = lax.bitcast_convert_type(w, jnp.uint32)
    hi = lax.bitcast_convert_type(u & jnp.uint32(0xFFFF0000), F32)
    lo = lax.bitcast_convert_type(u << 16, F32)
    return hi, lo


def _route(lt):
    assert EXP_PER_GROUP == 8 and N_GROUPS <= 8
    tm = lt.shape[1]
    row = lax.broadcasted_iota(jnp.int32, (8, tm), 0)
    neg = -jnp.inf
    big = jnp.int32(1 << 20)
    gl = jnp.where(row < N_GROUPS, lt[N_EXPERTS:N_EXPERTS + 8, :], neg)
    gmax = jnp.max(gl, axis=0, keepdims=True)
    p_top = 1.0 / jnp.sum(jnp.exp(gl - gmax), axis=0, keepdims=True)
    g_idx = jnp.min(jnp.where(gl == gmax, row, big), axis=0, keepdims=True)
    el = lt[0:8, :]
    for g in range(1, N_GROUPS):
        el = jnp.where(g_idx == g, lt[8 * g:8 * g + 8, :], el)
    m1 = jnp.max(el, axis=0, keepdims=True)
    i1 = jnp.min(jnp.where(el == m1, row, big), axis=0, keepdims=True)
    el2 = jnp.where(row == i1, neg, el)
    m2 = jnp.max(el2, axis=0, keepdims=True)
    i2 = jnp.min(jnp.where(el2 == m2, row, big), axis=0, keepdims=True)
    r = jnp.exp(m2 - m1)
    base = g_idx * EXP_PER_GROUP
    return base + i1, base + i2, p_top / (1.0 + r), p_top * r / (1.0 + r)


def _route_body(h_ref, g_ref, whi_ref, wlo_ref, xpk_ref, meta_ref, meta_t_ref, cnt_ref, carry_ref):
    i = pl.program_id(0)

    @pl.when(i == 0)
    def _():
        carry_ref[...] = jnp.zeros_like(carry_ref)

    xn = _rms(h_ref[...], g_ref[...])
    xpk_ref[...] = _pack_bf16_pairs(xn)
    x_hi = xn.astype(BF16)
    x_lo = (xn - x_hi.astype(F32)).astype(BF16)
    logits = (jnp.dot(x_hi, whi_ref[...], preferred_element_type=F32)
              + jnp.dot(x_lo, whi_ref[...], preferred_element_type=F32)
              + jnp.dot(x_hi, wlo_ref[...], preferred_element_type=F32))
    e0, e1, w0, w1 = _route(logits.T)
    tm = logits.shape[0]
    row = lax.broadcasted_iota(jnp.int32, (LANES, tm), 0)
    onehot_t = jnp.where((row == e0) | (row == e1), 1.0, 0.0).astype(BF16)
    ri = lax.broadcasted_iota(jnp.int32, (tm, tm), 0)
    ci = lax.broadcasted_iota(jnp.int32, (tm, tm), 1)
    earlier = jnp.where(ci < ri, 1.0, 0.0).astype(BF16)
    prefix = carry_ref[...] + _dot_nt(earlier, onehot_t)
    prefix_t = prefix.T
    r0 = jnp.sum(jnp.where(row == e0, prefix_t, 0.0), axis=0, keepdims=True)
    r1 = jnp.sum(jnp.where(row == e1, prefix_t, 0.0), axis=0, keepdims=True)
    row8 = lax.broadcasted_iota(jnp.int32, (META_COLS, tm), 0)
    meta_t = jnp.zeros((META_COLS, tm), F32)
    for c, val in enumerate((e0.astype(F32), e1.astype(F32), r0, r1, w0, w1)):
        meta_t = jnp.where(row8 == c, val, meta_t)
    meta_t_ref[...] = meta_t
    padded = jnp.concatenate([meta_t, jnp.zeros((LANES - META_COLS, tm), F32)], axis=0)
    meta_ref[...] = padded.T[:, :META_COLS]
    counts = _dot_nt(jnp.ones((8, tm), BF16), onehot_t)
    carry_ref[...] += counts[0:1, :]
    cnt_ref[...] = carry_ref[...]


def _moe_route(h2, gain, w_router):
    T, D = h2.shape
    tm = _pick_tile(T, 1152, LANES)
    return pl.pallas_call(
        _route_body,
        grid=(T // tm,),
        in_specs=[
            pl.BlockSpec((tm, D), lambda i: (i, 0)),
            pl.BlockSpec((1, D), lambda i: (0, 0)),
            pl.BlockSpec((D, LANES), lambda i: (0, 0)),
            pl.BlockSpec((D, LANES), lambda i: (0, 0)),
        ],
        out_specs=[
            pl.BlockSpec((tm, D // 2), lambda i: (i, 0)),
            pl.BlockSpec((tm, META_COLS), lambda i: (i, 0)),
            pl.BlockSpec((META_COLS, tm), lambda i: (0, i)),
            pl.BlockSpec((1, LANES), lambda i: (0, 0)),
        ],
        out_shape=[
            jax.ShapeDtypeStruct((T, D // 2), jnp.int32),
            jax.ShapeDtypeStruct((T, META_COLS), F32),
            jax.ShapeDtypeStruct((META_COLS, T), F32),
            jax.ShapeDtypeStruct((1, LANES), F32),
        ],
        scratch_shapes=[pltpu.VMEM((1, LANES), F32)],
        compiler_params=_params(("arbitrary",)),
        name="moe_route",
    )(h2, gain.reshape(1, D), *w_router)


def _sc_gather(table, idx):
    n = idx.shape[0]
    width = table.shape[1]
    win = SC_WINDOW
    assert n % SC_CHUNK == 0
    per_worker = n // SC_WORKERS
    mesh = plsc.VectorSubcoreMesh(core_axis_name="c", subcore_axis_name="s")

    @functools.partial(
        pl.kernel, out_type=jax.ShapeDtypeStruct((n, width), table.dtype), mesh=mesh,
        scratch_types=[pltpu.VMEM((per_worker,), jnp.int32), pltpu.VMEM((2, win, width), table.dtype),
                       pltpu.SemaphoreType.DMA((2,)), pltpu.SemaphoreType.DMA((2,))],
        name="sc_row_gather")
    def gather(table_hbm, idx_hbm, out_hbm, idx_v, rows_v, gsem, osem):
        worker = lax.axis_index("s") * mesh.num_cores + lax.axis_index("c")
        base = worker * per_worker
        pltpu.sync_copy(idx_hbm.at[pl.ds(base, per_worker)], idx_v)

        @pl.loop(0, per_worker, step=2 * win)
        def _(off):
            fetch = [pltpu.async_copy(table_hbm.at[idx_v.at[pl.ds(off + s * win, win)]], rows_v.at[s], gsem.at[s])
                     for s in range(2)]
            store = []
            for s in range(2):
                fetch[s].wait()
                store.append(pltpu.async_copy(rows_v.at[s], out_hbm.at[pl.ds(base + off + s * win, win)],
                                              osem.at[s]))
            for s in range(2):
                store[s].wait()

    return gather(table, idx)


def _sc_dispatch(table, dest, n_out):
    n_rows, width = table.shape
    win = SC_SCATTER_WINDOW
    assert n_rows % win == 0
    n_win = n_rows // win
    per_worker = pl.cdiv(n_win, SC_WORKERS)
    idx = jnp.pad(dest, ((0, 0), (0, per_worker * SC_WORKERS * win - n_rows)))
    idx = idx.reshape(2, per_worker, SC_WORKERS, win).transpose(0, 2, 1, 3)
    mesh = plsc.VectorSubcoreMesh(core_axis_name="c", subcore_axis_name="s")

    @functools.partial(
        pl.kernel, out_type=jax.ShapeDtypeStruct((n_out, width), table.dtype), mesh=mesh,
        scratch_types=[pltpu.VMEM((per_worker, win), jnp.int32), pltpu.VMEM((per_worker, win), jnp.int32),
                       pltpu.VMEM((win, width), table.dtype),
                       pltpu.SemaphoreType.DMA, pltpu.SemaphoreType.DMA],
        name="sc_row_dispatch")
    def dispatch(table_hbm, idx_hbm, out_hbm, idx0_v, idx1_v, rows_v, sem0, sem1):
        worker = lax.axis_index("s") * mesh.num_cores + lax.axis_index("c")
        pltpu.sync_copy(idx_hbm.at[0, worker], idx0_v)
        pltpu.sync_copy(idx_hbm.at[1, worker], idx1_v)

        @pl.loop(0, per_worker)
        def _(j):
            window = j * SC_WORKERS + worker

            @pl.when(window < n_win)
            def _():
                pltpu.sync_copy(table_hbm.at[pl.ds(window * win, win)], rows_v)
                first = pltpu.async_copy(rows_v, out_hbm.at[idx0_v.at[j]], sem0)
                second = pltpu.async_copy(rows_v, out_hbm.at[idx1_v.at[j]], sem1)
                first.wait()
                second.wait()

    return dispatch(table, idx)


def _expert_body(te_ref, tv_ref, x_ref, wg_ref, wu_ref, wd_ref, y_ref, wg_s, wu_s, wd_s):
    t = pl.program_id(0)
    half = x_ref.shape[1]

    @pl.when((t == 0) | (te_ref[t] != te_ref[jnp.maximum(t - 1, 0)]))
    def _():
        wg_s[...] = wg_ref[0].astype(BF16)
        wu_s[...] = wu_ref[0].astype(BF16)
        wd_s[...] = wd_ref[0].astype(BF16)

    @pl.when(tv_ref[t] > 0)
    def _():
        row = lax.broadcasted_iota(jnp.int32, x_ref.shape, 0)
        hi, lo = _unpack_bf16_pairs(jnp.where(row < tv_ref[t], x_ref[...], 0))
        hi, lo = hi.astype(BF16), lo.astype(BF16)
        hg = jnp.dot(hi, wg_s[:half, :], preferred_element_type=F32)
        hg += jnp.dot(lo, wg_s[half:, :], preferred_element_type=F32)
        hu = jnp.dot(hi, wu_s[:half, :], preferred_element_type=F32)
        hu += jnp.dot(lo, wu_s[half:, :], preferred_element_type=F32)
        act = (hg * jax.nn.sigmoid(hg) * hu).astype(BF16)
        y_ref[...] = _pack_bf16_pairs(jnp.dot(act, wd_s[...], preferred_element_type=F32))

    @pl.when(tv_ref[t] == 0)
    def _():
        y_ref[...] = jnp.zeros_like(y_ref)


def _moe_experts(xs, n_sorted, tile_expert, tile_valid, w_gate, w_up, w_down):
    NP, half = n_sorted, xs.shape[1]
    _, D, FF = w_gate.shape
    tr = EXPERT_TILE
    grid_spec = pltpu.PrefetchScalarGridSpec(
        num_scalar_prefetch=2,
        grid=(NP // tr,),
        in_specs=[
            pl.BlockSpec((tr, half), lambda t, te, tv: (t, 0)),
            pl.BlockSpec((1, D, FF), lambda t, te, tv: (te[t], 0, 0)),
            pl.BlockSpec((1, D, FF), lambda t, te, tv: (te[t], 0, 0)),
            pl.BlockSpec((1, FF, D), lambda t, te, tv: (te[t], 0, 0)),
        ],
        out_specs=pl.BlockSpec((tr, half), lambda t, te, tv: (t, 0)),
        scratch_shapes=[pltpu.VMEM((D, FF), BF16), pltpu.VMEM((D, FF), BF16), pltpu.VMEM((FF, D), BF16)],
    )
    return pl.pallas_call(
        _expert_body,
        grid_spec=grid_spec,
        out_shape=jax.ShapeDtypeStruct((NP, half), jnp.int32),
        compiler_params=_params(("arbitrary",)),
        name="moe_experts",
    )(tile_expert, tile_valid, xs, w_gate, w_up, w_down)


def _moe_combine_into(o_ref, h, meta, y0, y1):
    half = y0.shape[1]
    w0 = meta[:, 4:5]
    w1 = meta[:, 5:6]
    hi0, lo0 = _unpack_bf16_pairs(y0)
    hi1, lo1 = _unpack_bf16_pairs(y1)
    o_ref[:, :half] = h[:, :half] + w0 * hi0 + w1 * hi1
    o_ref[:, half:] = h[:, half:] + w0 * lo0 + w1 * lo1


def _moe(h2, gain, w_router, layer, w_gate, w_up, w_down):
    T, D = h2.shape
    E = N_EXPERTS
    tr = EXPERT_TILE
    n_sorted = _round_up(2 * T + E * (tr - 1), tr)
    t_pad = _round_up(T, SC_CHUNK // 2)

    xpk, meta, meta_t, counts = _moe_route(h2, gain, w_router)

    cnt = counts[0, :E].astype(jnp.int32)
    padded = (cnt + tr - 1) // tr * tr
    ends = jnp.cumsum(padded)
    starts = ends - padded
    eid = meta_t[0:2].astype(jnp.int32)
    pos = meta_t[2:4].astype(jnp.int32)
    for e in range(E):
        pos = pos + jnp.where(eid == e, starts[e], 0)
    tile_start = jnp.arange(n_sorted // tr, dtype=jnp.int32) * tr
    tile_expert = jnp.minimum(jnp.sum(tile_start[:, None] >= ends[None, :], axis=1), E - 1).astype(jnp.int32)
    tile_valid = jnp.clip(cnt[tile_expert] - (tile_start - starts[tile_expert]), 0, tr).astype(jnp.int32)

    xs = _sc_dispatch(xpk, pos, n_sorted)
    ys = _moe_experts(xs, n_sorted, layer * E + tile_expert, tile_valid, w_gate, w_up, w_down)
    spare = jnp.arange(t_pad - T, dtype=jnp.int32)
    back = jnp.concatenate([pos, jnp.broadcast_to(spare[None], (2, t_pad - T))], axis=1)
    yg = _sc_gather(ys, back.reshape(-1)).reshape(2, t_pad, D // 2)
    return meta, yg


def _final_body(h_ref, meta_ref, y0_ref, y1_ref, g_ref, o_ref, hn_ref):
    _moe_combine_into(hn_ref, h_ref[0], meta_ref[0], y0_ref[0], y1_ref[0])
    o_ref[0] = _rms(hn_ref[N_META:, :], g_ref[...])


def _final_norm(h3, meta3, yg, gain, b0, nb):
    _, L, D = h3.shape
    return pl.pallas_call(
        _final_body,
        grid=(nb,),
        in_specs=[pl.BlockSpec((1, L, D), lambda b: (b0 + b, 0, 0)),
                  pl.BlockSpec((1, L, meta3.shape[2]), lambda b: (b0 + b, 0, 0)),
                  pl.BlockSpec((1, L, D // 2), lambda b: (0, b0 + b, 0)),
                  pl.BlockSpec((1, L, D // 2), lambda b: (1, b0 + b, 0)),
                  pl.BlockSpec((1, D), lambda b: (0, 0))],
        out_specs=pl.BlockSpec((1, L - N_META, D), lambda b: (b, 0, 0)),
        out_shape=jax.ShapeDtypeStruct((nb, L - N_META, D), F32),
        scratch_shapes=[pltpu.VMEM((L, D), F32)],
        compiler_params=_params(("parallel",)),
        name="combine_final_norm",
    )(h3, meta3, yg, yg, gain.reshape(1, D))


def kernel(x_prompt, x_sample, meta_tokens, norm_mix, w_in, ret_decay_fwd, ret_decay_bwd, hy_short_w, hy_short_b, hy_filt_w1, hy_filt_b1, hy_filt_w2, hy_filt_b2, hy_filt_w3, hy_sin_freq, hy_skip, w_ret_o, w_hy_o, w_out, norm_ffn, router_group, router_expert, moe_w_gate, moe_w_up, moe_w_down, norm_final):
    assert x_prompt.shape[1:] == x_sample.shape[1:]
    nbp, nbs = x_prompt.shape[0], x_sample.shape[0]
    B = nbp + nbs
    D = x_prompt.shape[2]
    L = N_META + x_prompt.shape[1]
    T = B * L
    depth = w_in.shape[0]
    q_w = RET_HEADS * RET_DK
    v_w = RET_HEADS * RET_DV
    hy_w = hy_skip.shape[2]
    hy_col0 = 2 * q_w + 2 * v_w
    gate_col0 = hy_col0 + 3 * hy_w
    assert D == q_w and w_in.shape[2] == gate_col0 + 2 * D

    x = jnp.concatenate([x_prompt, x_sample], axis=0)
    meta = jnp.broadcast_to(meta_tokens[None].astype(x.dtype), (B, N_META, D))
    h = jnp.concatenate([meta, x], axis=1).reshape(T, D)

    cos_t, sin_t = _rotary_tables(L)
    dims = _HyenaDims(L)
    dft_fwd, dft_inv = _dft_tables(dims)
    filt_consts = _filter_constants(L, hy_w)
    router = jnp.concatenate([router_expert, router_group], axis=2).astype(F32)
    router = jnp.pad(router, ((0, 0), (0, 0), (0, LANES - router.shape[2])))
    router_hi = router.astype(BF16)
    router_lo = (router - router_hi.astype(F32)).astype(BF16)
    assert moe_w_gate.shape[1] == N_EXPERTS
    w_gate_all = moe_w_gate.reshape((-1,) + moe_w_gate.shape[2:])
    w_up_all = moe_w_up.reshape((-1,) + moe_w_up.shape[2:])
    w_down_all = moe_w_down.reshape((-1,) + moe_w_down.shape[2:])

    moe = None
    for i in range(depth):
        proj, h = _inproj(h, norm_mix[i], w_in[i].astype(BF16), moe)
        proj3 = proj.reshape(B, L, -1)

        lg = jnp.stack([jax.nn.log_sigmoid(ret_decay_fwd[i].astype(F32)),
                        jax.nn.log_sigmoid(ret_decay_bwd[i].astype(F32))])
        ret = _retention(proj3, lg, cos_t, sin_t)

        hs, hd = _hyena_filters(dims, filt_consts, hy_filt_w1[i], hy_filt_b1[i], hy_filt_w2[i], hy_filt_b2[i],
                                hy_filt_w3[i], hy_sin_freq[i])
        spectrum = _filter_spectrum(dims, dft_fwd, hs, hd)
        sw = hy_short_w[i].astype(F32)
        sb = hy_short_b[i].astype(F32).reshape(1, -1)
        z = _long_conv(dims, dft_fwd, dft_inv, proj3, hy_col0, proj3, hy_col0 + hy_w, sw, sb, 0, hy_w,
                       hy_skip[i, 0].astype(F32), spectrum, 0, hy_w)
        z = _long_conv(dims, dft_fwd, dft_inv, z, 0, proj3, hy_col0 + 2 * hy_w, sw, sb, None, 2 * hy_w,
                       hy_skip[i, 1].astype(F32), spectrum, hy_w, hy_w)

        h = _mix_out(ret.reshape(T, v_w), z.reshape(T, hy_w), proj, 2 * q_w + v_w, gate_col0, h,
                     w_ret_o[i].astype(BF16), w_hy_o[i].astype(BF16), w_out[i].astype(BF16))
        moe = _moe(h, norm_ffn[i], (router_hi[i], router_lo[i]), i, w_gate_all, w_up_all, w_down_all)

    h3 = h.reshape(B, L, D)
    meta3 = moe[0].reshape(B, L, -1)
    return (_final_norm(h3, meta3, moe[1], norm_final, 0, nbp),
            _final_norm(h3, meta3, moe[1], norm_final, nbp, nbs))
```

```python
import functools
import math

import jax
import jax.numpy as jnp
import numpy as np
from jax import lax
from jax.experimental import pallas as pl
from jax.experimental.pallas import tpu as pltpu
from jax.experimental.pallas import tpu_sc as plsc

N_META = 16
RET_HEADS = 8
RET_DK = 128
RET_DV = 256
ROPE_THETA = 10000.0
HY_ORDER = 2
HY_SHORT = 3
HY_EMB = 33
HY_BANDS = (HY_EMB - 1) // 2
HY_DECAY_TARGET = 1e-2
HY_MIN_DECAY = math.log(HY_DECAY_TARGET) / 1.5
HY_MAX_DECAY = math.log(HY_DECAY_TARGET) / 0.3
N_GROUPS = 4
EXP_PER_GROUP = 8
N_EXPERTS = N_GROUPS * EXP_PER_GROUP
RMS_EPS = 1e-6

LANES = 128
BF16_SUBLANES = 16
MXU_DIM = 256
RET_CHUNK = 256
VMEM_LIMIT = 56 * 1024 * 1024

F32 = jnp.float32
BF16 = jnp.bfloat16


def _round_up(n, m):
    return (n + m - 1) // m * m


def _pick_tile(n, target, mult):
    best = None
    for t in range(mult, min(n, target) + 1, mult):
        if n % t == 0:
            best = t
    assert best is not None, (n, target, mult)
    return best


def _params(sem):
    return pltpu.CompilerParams(dimension_semantics=sem, vmem_limit_bytes=VMEM_LIMIT)


def _rms(x, gain):
    ms = jnp.mean(x * x, axis=-1, keepdims=True)
    return x * lax.rsqrt(ms + RMS_EPS) * gain


def _inproj_body(h_ref, g_ref, w_ref, o_ref, xn_ref):
    @pl.when(pl.program_id(1) == 0)
    def _():
        xn_ref[...] = _rms(h_ref[...], g_ref[...]).astype(BF16)

    o_ref[...] = jnp.dot(xn_ref[...], w_ref[...], preferred_element_type=F32).astype(o_ref.dtype)


def _inproj_moe_body(h_ref, meta_ref, y0_ref, y1_ref, g_ref, w_ref, o_ref, hn_ref, xn_ref):
    @pl.when(pl.program_id(1) == 0)
    def _():
        _moe_combine_into(hn_ref, h_ref[...], meta_ref[...], y0_ref[0], y1_ref[0])
        xn_ref[...] = _rms(hn_ref[...], g_ref[...]).astype(BF16)

    o_ref[...] = jnp.dot(xn_ref[...], w_ref[...], preferred_element_type=F32).astype(o_ref.dtype)


def _inproj(h2, gain, w, moe=None):
    T, D = h2.shape
    nc = w.shape[1]
    tm = _pick_tile(T, 1152, BF16_SUBLANES)
    tn = _pick_tile(nc, 2816 if moe is None else 1024, LANES)
    common = dict(
        grid=(T // tm, nc // tn),
        scratch_shapes=[pltpu.VMEM((tm, D), BF16)],
        compiler_params=_params(("parallel", "arbitrary")),
    )
    h_spec = pl.BlockSpec((tm, D), lambda i, j: (i, 0))
    g_spec = pl.BlockSpec((1, D), lambda i, j: (0, 0))
    w_spec = pl.BlockSpec((D, tn), lambda i, j: (0, j))
    o_spec = pl.BlockSpec((tm, tn), lambda i, j: (i, j))
    o_shape = jax.ShapeDtypeStruct((T, nc), BF16)
    if moe is None:
        proj = pl.pallas_call(_inproj_body, in_specs=[h_spec, g_spec, w_spec], out_specs=o_spec,
                              out_shape=o_shape, name="norm_inproj", **common)(h2, gain.reshape(1, D), w)
        return proj, h2
    meta, yg = moe
    return pl.pallas_call(
        _inproj_moe_body,
        in_specs=[h_spec, pl.BlockSpec((tm, meta.shape[1]), lambda i, j: (i, 0)),
                  pl.BlockSpec((1, tm, D // 2), lambda i, j: (0, i, 0)),
                  pl.BlockSpec((1, tm, D // 2), lambda i, j: (1, i, 0)),
                  g_spec, w_spec],
        out_specs=[o_spec, h_spec],
        out_shape=[o_shape, jax.ShapeDtypeStruct((T, D), F32)],
        name="combine_norm_inproj", **common,
    )(h2, meta, yg, yg, gain.reshape(1, D), w)


def _dot_t(a, b):
    return lax.dot_general(a, b, (((0,), (0,)), ((), ())), preferred_element_type=F32)


def _dot_nt(a, b):
    return lax.dot_general(a, b, (((1,), (1,)), ((), ())), preferred_element_type=F32)


def _ret_body(lg_ref, q_ref, k_ref, v_ref, cos_ref, sin_ref, o_ref, rb_ref, kr_ref, *, seq_len):
    C = RET_CHUNK
    L = seq_len
    n_chunks = pl.cdiv(L, C)
    head = pl.program_id(1)
    lgf = lg_ref[0, head]
    lgb = lg_ref[1, head]

    def chunk(ref, n):
        lo, hi = n * C, min((n + 1) * C, L)
        x = ref[0, lo:hi, :]
        if hi - lo < C:
            x = jnp.concatenate([x, jnp.zeros((C - (hi - lo), x.shape[1]), x.dtype)], axis=0)
        return x

    def rotary(ref, n):
        x = chunk(ref, n).astype(F32)
        sl = slice(n * C, (n + 1) * C)
        return x * cos_ref[sl, :] + pltpu.roll(x, RET_DK // 2, 1) * sin_ref[sl, :]

    row = lax.broadcasted_iota(jnp.int32, (C, LANES), 0).astype(F32)
    ri = lax.broadcasted_iota(jnp.int32, (C, C), 0).astype(F32)
    ci = lax.broadcasted_iota(jnp.int32, (C, C), 1).astype(F32)
    decay = jnp.exp(jnp.where(ci <= ri, (ri - ci) * lgf, (ci - ri) * lgb))
    qf_dec = jnp.exp((row + 1.0) * lgf)
    qb_dec = jnp.exp((C - row) * lgb)
    kf_dec = jnp.exp((C - 1.0 - row) * lgf)
    kb_dec = jnp.exp(row * lgb)
    cf = jnp.exp(C * lgf)
    cb = jnp.exp(C * lgb)

    state = jnp.zeros((RET_DK, RET_DV), F32)
    for n in reversed(range(n_chunks)):
        rb_ref[n] = state.astype(BF16)
        if n > 0:
            kr_ref[n] = rotary(k_ref, n)
            kb = (kr_ref[n] * kb_dec).astype(BF16)
            state = cb * state + _dot_t(kb, chunk(v_ref, n))

    state = jnp.zeros((RET_DK, RET_DV), F32)
    for n in range(n_chunks):
        q = rotary(q_ref, n) * (RET_DK ** -0.5)
        k = kr_ref[n] if n > 0 else rotary(k_ref, n)
        v = chunk(v_ref, n)
        scores = _dot_nt(q.astype(BF16), k.astype(BF16)) * decay
        o = jnp.dot(scores.astype(BF16), v, preferred_element_type=F32)
        o += jnp.dot((q * qf_dec).astype(BF16), state.astype(BF16), preferred_element_type=F32)
        o += jnp.dot((q * qb_dec).astype(BF16), rb_ref[n], preferred_element_type=F32)
        if n + 1 < n_chunks:
            state = cf * state + _dot_t((k * kf_dec).astype(BF16), v)
        lo, hi = n * C, min((n + 1) * C, L)
        o_ref[0, lo:hi, :] = o[: hi - lo].astype(o_ref.dtype)


def _retention(proj3, lg, cos_t, sin_t):
    B, L, _ = proj3.shape
    n_chunks = pl.cdiv(L, RET_CHUNK)
    lp = n_chunks * RET_CHUNK
    k_blk = RET_HEADS
    v_blk = 2 * RET_HEADS * RET_DK // RET_DV
    return pl.pallas_call(
        functools.partial(_ret_body, seq_len=L),
        grid=(B, RET_HEADS),
        in_specs=[
            pl.BlockSpec(memory_space=pltpu.SMEM),
            pl.BlockSpec((1, L, RET_DK), lambda b, h: (b, 0, h)),
            pl.BlockSpec((1, L, RET_DK), lambda b, h: (b, 0, k_blk + h)),
            pl.BlockSpec((1, L, RET_DV), lambda b, h: (b, 0, v_blk + h)),
            pl.BlockSpec((lp, RET_DK), lambda b, h: (0, 0)),
            pl.BlockSpec((lp, RET_DK), lambda b, h: (0, 0)),
        ],
        out_specs=pl.BlockSpec((1, L, RET_DV), lambda b, h: (b, 0, h)),
        out_shape=jax.ShapeDtypeStruct((B, L, RET_HEADS * RET_DV), BF16),
        scratch_shapes=[pltpu.VMEM((n_chunks, RET_DK, RET_DV), BF16),
                        pltpu.VMEM((n_chunks, RET_CHUNK, RET_DK), F32)],
        compiler_params=_params(("parallel", "arbitrary")),
        name="retention",
    )(lg, proj3, proj3, proj3, cos_t, sin_t)


def _rotary_tables(L):
    half = RET_DK // 2
    lp = _round_up(L, RET_CHUNK)
    inv = ROPE_THETA ** (-jnp.arange(half, dtype=F32) / half)
    ang = jnp.arange(lp, dtype=F32)[:, None] * inv[None, :]
    cos, sin = jnp.cos(ang), jnp.sin(ang)
    return jnp.concatenate([cos, cos], axis=1), jnp.concatenate([-sin, sin], axis=1)


def _row_tiles(n, tile):
    return [(s, min(tile, n - s)) for s in range(0, n, tile)]


RADIX = 4


class _HyenaDims:
    def __init__(self, L):
        assert L % RADIX == 0
        self.L = L
        self.Q = L // RADIX
        self.F = self.Q + 1
        self.QP = _round_up(self.Q, BF16_SUBLANES)
        self.QK = _round_up(self.Q, MXU_DIM)
        self.FM = _round_up(self.F, BF16_SUBLANES)
        self.IK = _round_up(2 * self.FM, MXU_DIM)


def _dft_tables(dims):
    n = 2 * dims.L

    def trig(f, t, valid):
        ang = ((f * t) % n) * (2.0 * math.pi / n)
        return (np.where(valid, np.cos(ang), 0.0).astype(BF16), np.where(valid, np.sin(ang), 0.0).astype(BF16))

    f = np.arange(dims.FM, dtype=np.int64)[:, None]
    s = np.arange(dims.QK, dtype=np.int64)[None, :]
    ok = (f < dims.F) & (s < dims.Q)
    fwd = [trig(f, RADIX * s + r, ok) for r in range(RADIX)]

    s = np.arange(dims.QP, dtype=np.int64)[:, None]
    j = np.arange(dims.IK, dtype=np.int64)[None, :]
    f = j % dims.FM
    ok = (f < dims.F) & (s < dims.Q)
    inv = []
    for r in range(RADIX):
        cos, sin = trig(f, RADIX * s + r, ok)
        inv.append(np.where(j < dims.FM, cos, np.where(j < 2 * dims.FM, sin, np.zeros_like(cos))))
    return [t for pair in fwd for t in pair], inv


def _forward_dft(tables, parts, rows, dtype=F32):
    p = [jnp.dot(tables[2 * r][rows, :], parts[r][...], preferred_element_type=F32).astype(dtype)
         for r in range(RADIX)]
    a = [jnp.dot(tables[2 * r + 1][rows, :], parts[r][...], preferred_element_type=F32).astype(dtype)
         for r in range(RADIX)]
    ps02, pd02, ps13, pd13 = p[0] + p[2], p[0] - p[2], p[1] + p[3], p[1] - p[3]
    as02, ad02, as13, ad13 = a[0] + a[2], a[0] - a[2], a[1] + a[3], a[1] - a[3]
    return [(ps02 + ps13, as02 + as13),
            (pd02 + ad13, pd13 - ad02),
            (pd02 - ad13, pd13 + ad02),
            (ps02 - ps13, as13 - as02)]


def _class_weights(dims):
    f = lax.broadcasted_iota(jnp.int32, (dims.FM, 1), 0)
    n = 2.0 * dims.L
    edge = jnp.where(f == 0, 1.0, 2.0) / n
    return [jnp.where(f <= dims.Q, edge, 0.0),
            jnp.where(f < dims.Q, 2.0 / n, 0.0),
            jnp.where((f >= 1) & (f <= dims.Q), 2.0 / n, 0.0),
            jnp.where(f < dims.Q, edge, 0.0)]


def _spec_body(*refs, dims):
    tables = refs[:2 * RADIX]
    hs_ref, hd_ref = refs[2 * RADIX:2 * RADIX + 2]
    outs = refs[2 * RADIX + 2:4 * RADIX + 2]
    s_pads = refs[4 * RADIX + 2:5 * RADIX + 2]
    d_pads = refs[5 * RADIX + 2:]
    Q = dims.Q
    for pads, src in ((s_pads, hs_ref), (d_pads, hd_ref)):
        for r in range(RADIX):
            pads[r][Q:, :] = jnp.zeros((pads[r].shape[0] - Q, pads[r].shape[1]), BF16)
            pads[r][:Q, :] = src[r].astype(BF16)
    rows = slice(None)
    re = [c[0] for c in _forward_dft(tables, s_pads, rows)]
    nim = [c[1] for c in _forward_dft(tables, d_pads, rows)]
    for c, w in enumerate(_class_weights(dims)):
        outs[2 * c][...] = (w * re[c]).astype(outs[2 * c].dtype)
        outs[2 * c + 1][...] = (-w * nim[c]).astype(outs[2 * c + 1].dtype)


def _filter_spectrum(dims, fwd, hs, hd):
    width = hs.shape[2]
    cw = MXU_DIM
    resident = pl.BlockSpec((dims.FM, dims.QK), lambda j: (0, 0), pipeline_mode=pl.Buffered(1))
    return pl.pallas_call(
        functools.partial(_spec_body, dims=dims),
        grid=(width // cw,),
        in_specs=[resident] * (2 * RADIX) + [pl.BlockSpec((RADIX, dims.Q, cw), lambda j: (0, 0, j))] * 2,
        out_specs=[pl.BlockSpec((dims.FM, cw), lambda j: (0, j))] * (2 * RADIX),
        out_shape=[jax.ShapeDtypeStruct((dims.FM, width), BF16)] * (2 * RADIX),
        scratch_shapes=[pltpu.VMEM((dims.QK, cw), BF16)] * (2 * RADIX),
        compiler_params=_params(("arbitrary",)),
        name="hyena_filter_spectrum",
    )(*fwd, hs, hd)


STAGE_HEAD = 8


def _conv_body(*refs, dims, first):
    n_tab = 2 * RADIX
    tables = refs[:n_tab]
    inv_refs = refs[n_tab:n_tab + RADIX]
    u_ref, x_ref = refs[n_tab + RADIX:n_tab + RADIX + 2]
    at = n_tab + RADIX + 2
    n_taps = 4 if first else 2
    taps = refs[at:at + n_taps]
    at += n_taps
    skip_ref = refs[at]
    spec_refs = refs[at + 1:at + 1 + 2 * RADIX]
    o_ref = refs[at + 1 + 2 * RADIX]
    scratch = refs[at + 2 + 2 * RADIX:]
    u_store, scratch = scratch[:RADIX], scratch[RADIX:]
    x_parts, g_parts = scratch[:RADIX], scratch[RADIX:2 * RADIX]
    nat_ref, natx_ref = scratch[2 * RADIX:]
    L, Q, QP, QK, FM, IK = dims.L, dims.Q, dims.QP, dims.QK, dims.FM, dims.IK
    cw = x_parts[0].shape[1]

    lane_slabs = [(k, slice(k * LANES, (k + 1) * LANES)) for k in range(cw // LANES)]

    def short_conv_split(parts, ref, w_ref, b_ref, stage_ref, anchor=None):
        tail = stage_ref.shape[1] - STAGE_HEAD - L
        for k, lanes in lane_slabs:
            raw = ref[0, :, lanes].astype(F32)
            if anchor is not None:
                raw = raw + anchor[:, lanes]
            stage_ref[k, :STAGE_HEAD, :] = jnp.zeros((STAGE_HEAD, LANES), F32)
            stage_ref[k, STAGE_HEAD:STAGE_HEAD + L, :] = raw
            stage_ref[k, STAGE_HEAD + L:, :] = jnp.zeros((tail, LANES), F32)
            w = w_ref[:, lanes]
            bias = b_ref[:, lanes]
            for r in range(RADIX):
                taps3 = [stage_ref[k, pl.ds(STAGE_HEAD + r + d, Q, stride=RADIX), :] for d in (-1, 0, 1)]
                val = bias + w[0:1] * taps3[0] + w[1:2] * taps3[1] + w[2:3] * taps3[2]
                parts[r][:Q, lanes] = val.astype(BF16)
                if QP > Q:
                    parts[r][Q:QP, lanes] = jnp.zeros((QP - Q, LANES), BF16)

    def zero_tail(parts):
        if QK > QP:
            for r in range(RADIX):
                parts[r][QP:, :] = jnp.zeros((QK - QP, cw), BF16)

    def run(u_parts):
        for r in range(RADIX):
            if IK > 2 * FM:
                g_parts[r][2 * FM:, :] = jnp.zeros((IK - 2 * FM, cw), BF16)
        for lo, sz in _row_tiles(FM, MXU_DIM):
            rows = slice(lo, lo + sz)
            e1, e2 = [], []
            for c, (p, a) in enumerate(_forward_dft(tables, u_parts, rows, BF16)):
                k1, k2 = spec_refs[2 * c][rows, :], spec_refs[2 * c + 1][rows, :]
                e1.append(p * k1 + a * k2)
                e2.append(a * k1 - p * k2)
            for r, (gc, gs) in enumerate(_fold_classes(e1, e2)):
                g_parts[r][lo:lo + sz, :] = gc
                g_parts[r][FM + lo:FM + lo + sz, :] = gs

        anchor = g_parts[RADIX - 1][2 * FM - 1:2 * FM, :].astype(F32) * 0.0
        short_conv_split(x_parts, x_ref, taps[-2], taps[-1], natx_ref, anchor)
        skip = skip_ref[...]
        for lo, sz in _row_tiles(QP, MXU_DIM):
            rows = slice(lo, lo + sz)
            valid = min(sz, Q - lo)
            for r in range(RADIX):
                y = jnp.dot(inv_refs[r][rows, :], g_parts[r][...], preferred_element_type=F32)
                o = x_parts[r][rows, :].astype(F32) * (y + skip * u_parts[r][rows, :].astype(F32))
                if first:
                    o_ref[0, r, rows, :] = o.astype(o_ref.dtype)
                elif valid > 0:
                    for k, lanes in lane_slabs:
                        nat_ref[k, pl.ds(RADIX * lo + r, valid, stride=RADIX), :] = o[:valid, lanes]
        if not first:
            for k, lanes in lane_slabs:
                o_ref[0, :, lanes] = nat_ref[k, :L, :]

    zero_tail(u_store)
    if first:
        short_conv_split(u_store, u_ref, taps[0], taps[1], nat_ref)
    else:
        for r in range(RADIX):
            u_store[r][:QP, :] = u_ref[0, r]
    run(u_store)


def _fold_classes(e1, e2):
    a_p, a_m = e1[0] + e1[3], e1[0] - e1[3]
    b_p, b_m = e1[1] + e1[2], e1[1] - e1[2]
    c_p, c_m = e2[0] + e2[3], e2[0] - e2[3]
    d_p, d_m = e2[1] + e2[2], e2[2] - e2[1]
    return [(a_p + b_p, c_m + d_m),
            (a_m + d_p, c_p + b_m),
            (a_p - b_p, c_m - d_m),
            (a_m - d_p, c_p - b_m)]


def _long_conv(dims, fwd, inv, u_arr, u_col0, x_arr, x_col0, short_w, short_b, short_u_col0, short_x_col0,
               skip, spectrum, k_col0, width):
    B = u_arr.shape[0]
    L = dims.L
    cw = MXU_DIM
    ub, xb, kb = u_col0 // cw, x_col0 // cw, k_col0 // cw
    first = short_u_col0 is not None
    fwd_spec = pl.BlockSpec((dims.FM, dims.QK), lambda c, b: (0, 0), pipeline_mode=pl.Buffered(1))
    inv_spec = pl.BlockSpec((dims.QP, dims.IK), lambda c, b: (0, 0), pipeline_mode=pl.Buffered(1))

    def taps(col0):
        blk = col0 // cw
        return [pl.BlockSpec((HY_SHORT, cw), lambda c, b: (0, blk + c)),
                pl.BlockSpec((1, cw), lambda c, b: (0, blk + c))]

    if first:
        u_spec = pl.BlockSpec((1, L, cw), lambda c, b: (b, 0, ub + c))
        out_spec = pl.BlockSpec((1, RADIX, dims.QP, cw), lambda c, b: (b, 0, 0, c))
        out_shape = jax.ShapeDtypeStruct((B, RADIX, dims.QP, width), BF16)
    else:
        u_spec = pl.BlockSpec((1, RADIX, dims.QP, cw), lambda c, b: (b, 0, 0, ub + c))
        out_spec = pl.BlockSpec((1, L, cw), lambda c, b: (b, 0, c))
        out_shape = jax.ShapeDtypeStruct((B, L, width), F32)
    in_specs = ([fwd_spec] * (2 * RADIX) + [inv_spec] * RADIX
                + [u_spec, pl.BlockSpec((1, L, cw), lambda c, b: (b, 0, xb + c))])
    args = list(fwd) + list(inv) + [u_arr, x_arr]
    for col0 in ([short_u_col0] if first else []) + [short_x_col0]:
        in_specs += taps(col0)
        args += [short_w, short_b]
    in_specs += [pl.BlockSpec((1, cw), lambda c, b: (0, c))]
    in_specs += [pl.BlockSpec((dims.FM, cw), lambda c, b: (0, kb + c))] * (2 * RADIX)
    args += [skip.reshape(1, width)] + list(spectrum)
    stage = pltpu.VMEM((cw // LANES, RADIX * dims.QP + 2 * STAGE_HEAD, LANES), F32)
    return pl.pallas_call(
        functools.partial(_conv_body, dims=dims, first=first),
        grid=(width // cw, B),
        in_specs=in_specs,
        out_specs=out_spec,
        out_shape=out_shape,
        scratch_shapes=([pltpu.VMEM((dims.QK, cw), BF16)] * RADIX
                        + [pltpu.VMEM((dims.QP, cw), BF16)] * RADIX
                        + [pltpu.VMEM((dims.IK, cw), BF16)] * RADIX + [stage, stage]),
        compiler_params=_params(("parallel", "arbitrary")),
        name="hyena_long_conv",
    )(*args)


def _filter_body(z_ref, w1_ref, b1_ref, w2_ref, b2_ref, fr_ref, w3f_ref, w3b_ref, dec_ref, hs_ref, hd_ref,
                 h2_ref, nat_ref, *, dims):
    hp = lax.Precision.HIGHEST

    @pl.when(pl.program_id(0) == 0)
    def _():
        fr = fr_ref[...]
        h1 = jnp.sin(fr * (jnp.dot(z_ref[...], w1_ref[...], precision=hp, preferred_element_type=F32)
                           + b1_ref[...]))
        h2_ref[...] = jnp.sin(fr * (jnp.dot(h1, w2_ref[...], precision=hp, preferred_element_type=F32)
                                    + b2_ref[...]))

    h2 = h2_ref[...]
    dec = dec_ref[...]
    hf = jnp.dot(h2, w3f_ref[...], precision=hp, preferred_element_type=F32) * dec
    hb = jnp.dot(h2, w3b_ref[...], precision=hp, preferred_element_type=F32) * dec
    hb = jnp.where(lax.broadcasted_iota(jnp.int32, hb.shape, 0) == 0, 0.0, hb)
    scale = lax.rsqrt(jnp.sum(hf * hf + hb * hb, axis=0, keepdims=True) + 1e-6)
    for ref, val in ((hs_ref, (hf + hb) * scale), (hd_ref, (hf - hb) * scale)):
        for k in range(val.shape[1] // LANES):
            lanes = slice(k * LANES, (k + 1) * LANES)
            nat_ref[k] = val[:, lanes]
            for r in range(RADIX):
                ref[r, :, lanes] = nat_ref[k, pl.ds(r, dims.Q, stride=RADIX), :]


def _filter_constants(L, width):
    t = np.linspace(0.0, 1.0, L)
    w = (2.0 * math.pi / L) * np.arange(L)
    bands = np.linspace(1e-4, HY_BANDS - 1, HY_BANDS)
    fw = w[:, None] * bands[None, :]
    z = np.concatenate([t[:, None], np.cos(fw), -np.sin(fw)], axis=-1)
    z = np.pad(z, ((0, 0), (0, LANES - z.shape[1])))
    deltas = np.abs(np.linspace(HY_MIN_DECAY, HY_MAX_DECAY, width))
    return z.astype(np.float32), np.exp(-t[:, None] * deltas[None, :]).astype(np.float32)


def _hyena_filters(dims, consts, w1, b1, w2, b2, w3, freq):
    L = dims.L
    z, dec = consts
    hidden = w2.shape[0]
    width = w3.shape[1] // (2 * HY_ORDER)
    cw = MXU_DIM
    per_order = width // cw

    def lanes128(a):
        return jnp.pad(a.astype(F32), [(0, 0)] * (a.ndim - 1) + [(0, LANES - a.shape[-1])])

    w1p = jnp.pad(lanes128(w1), ((0, LANES - w1.shape[0]), (0, 0)))
    w2p = jnp.pad(lanes128(w2), ((0, LANES - hidden), (0, 0)))
    w3p = jnp.pad(w3.astype(F32), ((0, LANES - hidden), (0, 0)))
    row = lambda a: lanes128(a).reshape(1, LANES)
    full = lambda shape: pl.BlockSpec(shape, lambda j: (0, 0))
    fwd_col = lambda j: (0, (j // per_order) * 2 * per_order + j % per_order)
    bwd_col = lambda j: (0, (j // per_order) * 2 * per_order + per_order + j % per_order)
    out_spec = pl.BlockSpec((RADIX, dims.Q, cw), lambda j: (0, 0, j))
    return pl.pallas_call(
        functools.partial(_filter_body, dims=dims),
        grid=(HY_ORDER * per_order,),
        in_specs=[full((L, LANES)), full((LANES, LANES)), full((1, LANES)), full((LANES, LANES)),
                  full((1, LANES)), full((1, LANES)),
                  pl.BlockSpec((LANES, cw), fwd_col), pl.BlockSpec((LANES, cw), bwd_col),
                  pl.BlockSpec((L, cw), lambda j: (0, j % per_order))],
        out_specs=[out_spec, out_spec],
        out_shape=[jax.ShapeDtypeStruct((RADIX, dims.Q, HY_ORDER * width), F32)] * 2,
        scratch_shapes=[pltpu.VMEM((L, LANES), F32), pltpu.VMEM((cw // LANES, L, LANES), F32)],
        compiler_params=_params(("arbitrary",)),
        name="hyena_filters",
    )(z, w1p, row(b1), w2p, row(b2), row(freq), w3p, w3p, dec)


def _mix_body(ret_ref, g_ref, hy_ref, gr_ref, gh_ref, h_ref, wr_ref, wh_ref, wo_ref, o_ref):
    ret = None
    for hd in range(RET_HEADS):
        cols = slice(hd * RET_DV, (hd + 1) * RET_DV)
        o = ret_ref[:, cols].astype(F32)
        o = o * lax.rsqrt(jnp.mean(o * o, axis=-1, keepdims=True) + RMS_EPS)
        g = g_ref[:, cols].astype(F32)
        part = jnp.dot((g * jax.nn.sigmoid(g) * o).astype(BF16), wr_ref[cols, :], preferred_element_type=F32)
        ret = part if ret is None else ret + part
    hyo = jnp.dot(hy_ref[...].astype(BF16), wh_ref[...], preferred_element_type=F32)
    merged = (jax.nn.sigmoid(gr_ref[...].astype(F32)) * ret
              + jax.nn.sigmoid(gh_ref[...].astype(F32)) * hyo)
    o_ref[...] = h_ref[...] + jnp.dot(merged.astype(BF16), wo_ref[...], preferred_element_type=F32)


def _mix_out(ret2, hy2, proj2, g_col0, gate_col0, h2, w_ret_o, w_hy_o, w_out):
    T, D = h2.shape
    tm = _pick_tile(T, 384, BF16_SUBLANES)
    gb = gate_col0 // D
    v_w = ret2.shape[1]
    assert g_col0 % v_w == 0
    row = lambda w: pl.BlockSpec((tm, w), lambda i: (i, 0))
    full = lambda a: pl.BlockSpec(a.shape, lambda i: (0, 0))
    return pl.pallas_call(
        _mix_body,
        grid=(T // tm,),
        in_specs=[
            row(v_w), pl.BlockSpec((tm, v_w), lambda i: (i, g_col0 // v_w)), row(hy2.shape[1]),
            pl.BlockSpec((tm, D), lambda i: (i, gb)),
            pl.BlockSpec((tm, D), lambda i: (i, gb + 1)),
            row(D), full(w_ret_o), full(w_hy_o), full(w_out),
        ],
        out_specs=row(D),
        out_shape=jax.ShapeDtypeStruct((T, D), F32),
        compiler_params=_params(("parallel",)),
        name="merge_outproj",
    )(ret2, proj2, hy2, proj2, proj2, h2, w_ret_o, w_hy_o, w_out)


EXPERT_TILE = 512
META_COLS = 8
SC_WINDOW = 64
SC_WORKERS = 32
SC_CHUNK = 2 * SC_WINDOW * SC_WORKERS
SC_SCATTER_WINDOW = 128


def _pack_bf16_pairs(x):
    n = x.shape[1] // 2
    xb = x.astype(BF16).astype(F32)
    hi = lax.bitcast_convert_type(xb[:, :n], jnp.uint32)
    lo = lax.bitcast_convert_type(xb[:, n:], jnp.uint32)
    return lax.bitcast_convert_type(hi | (lo >> 16), jnp.int32)


def _unpack_bf16_pairs(w):
    u = lax.bitcast_convert_type(w, jnp.uint32)
    hi = lax.bitcast_convert_type(u & jnp.uint32(0xFFFF0000), F32)
    lo = lax.bitcast_convert_type(u << 16, F32)
    return hi, lo


def _route(lt):
    assert EXP_PER_GROUP == 8 and N_GROUPS <= 8
    tm = lt.shape[1]
    row = lax.broadcasted_iota(jnp.int32, (8, tm), 0)
    neg = -jnp.inf
    big = jnp.int32(1 << 20)
    gl = jnp.where(row < N_GROUPS, lt[N_EXPERTS:N_EXPERTS + 8, :], neg)
    gmax = jnp.max(gl, axis=0, keepdims=True)
    p_top = 1.0 / jnp.sum(jnp.exp(gl - gmax), axis=0, keepdims=True)
    g_idx = jnp.min(jnp.where(gl == gmax, row, big), axis=0, keepdims=True)
    el = lt[0:8, :]
    for g in range(1, N_GROUPS):
        el = jnp.where(g_idx == g, lt[8 * g:8 * g + 8, :], el)
    m1 = jnp.max(el, axis=0, keepdims=True)
    i1 = jnp.min(jnp.where(el == m1, row, big), axis=0, keepdims=True)
    el2 = jnp.where(row == i1, neg, el)
    m2 = jnp.max(el2, axis=0, keepdims=True)
    i2 = jnp.min(jnp.where(el2 == m2, row, big), axis=0, keepdims=True)
    r = jnp.exp(m2 - m1)
    base = g_idx * EXP_PER_GROUP
    return base + i1, base + i2, p_top / (1.0 + r), p_top * r / (1.0 + r)


def _route_body(h_ref, g_ref, whi_ref, wlo_ref, xpk_ref, meta_ref, meta_t_ref, cnt_ref, carry_ref):
    i = pl.program_id(0)

    @pl.when(i == 0)
    def _():
        carry_ref[...] = jnp.zeros_like(carry_ref)

    xn = _rms(h_ref[...], g_ref[...])
    xpk_ref[...] = _pack_bf16_pairs(xn)
    x_hi = xn.astype(BF16)
    x_lo = (xn - x_hi.astype(F32)).astype(BF16)
    logits = (jnp.dot(x_hi, whi_ref[...], preferred_element_type=F32)
              + jnp.dot(x_lo, whi_ref[...], preferred_element_type=F32)
              + jnp.dot(x_hi, wlo_ref[...], preferred_element_type=F32))
    e0, e1, w0, w1 = _route(logits.T)
    tm = logits.shape[0]
    row = lax.broadcasted_iota(jnp.int32, (LANES, tm), 0)
    onehot_t = jnp.where((row == e0) | (row == e1), 1.0, 0.0).astype(BF16)
    ri = lax.broadcasted_iota(jnp.int32, (tm, tm), 0)
    ci = lax.broadcasted_iota(jnp.int32, (tm, tm), 1)
    earlier = jnp.where(ci < ri, 1.0, 0.0).astype(BF16)
    prefix = carry_ref[...] + _dot_nt(earlier, onehot_t)
    prefix_t = prefix.T
    r0 = jnp.sum(jnp.where(row == e0, prefix_t, 0.0), axis=0, keepdims=True)
    r1 = jnp.sum(jnp.where(row == e1, prefix_t, 0.0), axis=0, keepdims=True)
    row8 = lax.broadcasted_iota(jnp.int32, (META_COLS, tm), 0)
    meta_t = jnp.zeros((META_COLS, tm), F32)
    for c, val in enumerate((e0.astype(F32), e1.astype(F32), r0, r1, w0, w1)):
        meta_t = jnp.where(row8 == c, val, meta_t)
    meta_t_ref[...] = meta_t
    padded = jnp.concatenate([meta_t, jnp.zeros((LANES - META_COLS, tm), F32)], axis=0)
    meta_ref[...] = padded.T[:, :META_COLS]
    counts = _dot_nt(jnp.ones((8, tm), BF16), onehot_t)
    carry_ref[...] += counts[0:1, :]
    cnt_ref[...] = carry_ref[...]


def _moe_route(h2, gain, w_router):
    T, D = h2.shape
    tm = _pick_tile(T, 1152, LANES)
    return pl.pallas_call(
        _route_body,
        grid=(T // tm,),
        in_specs=[
            pl.BlockSpec((tm, D), lambda i: (i, 0)),
            pl.BlockSpec((1, D), lambda i: (0, 0)),
            pl.BlockSpec((D, LANES), lambda i: (0, 0)),
            pl.BlockSpec((D, LANES), lambda i: (0, 0)),
        ],
        out_specs=[
            pl.BlockSpec((tm, D // 2), lambda i: (i, 0)),
            pl.BlockSpec((tm, META_COLS), lambda i: (i, 0)),
            pl.BlockSpec((META_COLS, tm), lambda i: (0, i)),
            pl.BlockSpec((1, LANES), lambda i: (0, 0)),
        ],
        out_shape=[
            jax.ShapeDtypeStruct((T, D // 2), jnp.int32),
            jax.ShapeDtypeStruct((T, META_COLS), F32),
            jax.ShapeDtypeStruct((META_COLS, T), F32),
            jax.ShapeDtypeStruct((1, LANES), F32),
        ],
        scratch_shapes=[pltpu.VMEM((1, LANES), F32)],
        compiler_params=_params(("arbitrary",)),
        name="moe_route",
    )(h2, gain.reshape(1, D), *w_router)


def _sc_gather(table, idx):
    n = idx.shape[0]
    width = table.shape[1]
    win = SC_WINDOW
    assert n % SC_CHUNK == 0
    per_worker = n // SC_WORKERS
    mesh = plsc.VectorSubcoreMesh(core_axis_name="c", subcore_axis_name="s")

    @functools.partial(
        pl.kernel, out_type=jax.ShapeDtypeStruct((n, width), table.dtype), mesh=mesh,
        scratch_types=[pltpu.VMEM((per_worker,), jnp.int32), pltpu.VMEM((2, win, width), table.dtype),
                       pltpu.SemaphoreType.DMA((2,)), pltpu.SemaphoreType.DMA((2,))],
        name="sc_row_gather")
    def gather(table_hbm, idx_hbm, out_hbm, idx_v, rows_v, gsem, osem):
        worker = lax.axis_index("s") * mesh.num_cores + lax.axis_index("c")
        base = worker * per_worker
        pltpu.sync_copy(idx_hbm.at[pl.ds(base, per_worker)], idx_v)

        @pl.loop(0, per_worker, step=2 * win)
        def _(off):
            fetch = [pltpu.async_copy(table_hbm.at[idx_v.at[pl.ds(off + s * win, win)]], rows_v.at[s], gsem.at[s])
                     for s in range(2)]
            store = []
            for s in range(2):
                fetch[s].wait()
                store.append(pltpu.async_copy(rows_v.at[s], out_hbm.at[pl.ds(base + off + s * win, win)],
                                              osem.at[s]))
            for s in range(2):
                store[s].wait()

    return gather(table, idx)


def _sc_dispatch(table, dest, n_out):
    n_rows, width = table.shape
    win = SC_SCATTER_WINDOW
    assert n_rows % win == 0
    n_win = n_rows // win
    per_worker = pl.cdiv(n_win, SC_WORKERS)
    idx = jnp.pad(dest, ((0, 0), (0, per_worker * SC_WORKERS * win - n_rows)))
    idx = idx.reshape(2, per_worker, SC_WORKERS, win).transpose(0, 2, 1, 3)
    mesh = plsc.VectorSubcoreMesh(core_axis_name="c", subcore_axis_name="s")

    @functools.partial(
        pl.kernel, out_type=jax.ShapeDtypeStruct((n_out, width), table.dtype), mesh=mesh,
        scratch_types=[pltpu.VMEM((per_worker, win), jnp.int32), pltpu.VMEM((per_worker, win), jnp.int32),
                       pltpu.VMEM((win, width), table.dtype),
                       pltpu.SemaphoreType.DMA, pltpu.SemaphoreType.DMA],
        name="sc_row_dispatch")
    def dispatch(table_hbm, idx_hbm, out_hbm, idx0_v, idx1_v, rows_v, sem0, sem1):
        worker = lax.axis_index("s") * mesh.num_cores + lax.axis_index("c")
        pltpu.sync_copy(idx_hbm.at[0, worker], idx0_v)
        pltpu.sync_copy(idx_hbm.at[1, worker], idx1_v)

        @pl.loop(0, per_worker)
        def _(j):
            window = j * SC_WORKERS + worker

            @pl.when(window < n_win)
            def _():
                pltpu.sync_copy(table_hbm.at[pl.ds(window * win, win)], rows_v)
                first = pltpu.async_copy(rows_v, out_hbm.at[idx0_v.at[j]], sem0)
                second = pltpu.async_copy(rows_v, out_hbm.at[idx1_v.at[j]], sem1)
                first.wait()
                second.wait()

    return dispatch(table, idx)


def _expert_body(te_ref, tv_ref, x_ref, wg_ref, wu_ref, wd_ref, y_ref, wg_s, wu_s, wd_s):
    t = pl.program_id(0)
    half = x_ref.shape[1]

    @pl.when((t == 0) | (te_ref[t] != te_ref[jnp.maximum(t - 1, 0)]))
    def _():
        wg_s[...] = wg_ref[0].astype(BF16)
        wu_s[...] = wu_ref[0].astype(BF16)
        wd_s[...] = wd_ref[0].astype(BF16)

    @pl.when(tv_ref[t] > 0)
    def _():
        row = lax.broadcasted_iota(jnp.int32, x_ref.shape, 0)
        hi, lo = _unpack_bf16_pairs(jnp.where(row < tv_ref[t], x_ref[...], 0))
        hi, lo = hi.astype(BF16), lo.astype(BF16)
        hg = jnp.dot(hi, wg_s[:half, :], preferred_element_type=F32)
        hg += jnp.dot(lo, wg_s[half:, :], preferred_element_type=F32)
        hu = jnp.dot(hi, wu_s[:half, :], preferred_element_type=F32)
        hu += jnp.dot(lo, wu_s[half:, :], preferred_element_type=F32)
        act = (hg * jax.nn.sigmoid(hg) * hu).astype(BF16)
        y_ref[...] = _pack_bf16_pairs(jnp.dot(act, wd_s[...], preferred_element_type=F32))

    @pl.when(tv_ref[t] == 0)
    def _():
        y_ref[...] = jnp.zeros_like(y_ref)


def _moe_experts(xs, n_sorted, tile_expert, tile_valid, w_gate, w_up, w_down):
    NP, half = n_sorted, xs.shape[1]
    _, D, FF = w_gate.shape
    tr = EXPERT_TILE
    grid_spec = pltpu.PrefetchScalarGridSpec(
        num_scalar_prefetch=2,
        grid=(NP // tr,),
        in_specs=[
            pl.BlockSpec((tr, half), lambda t, te, tv: (t, 0)),
            pl.BlockSpec((1, D, FF), lambda t, te, tv: (te[t], 0, 0)),
            pl.BlockSpec((1, D, FF), lambda t, te, tv: (te[t], 0, 0)),
            pl.BlockSpec((1, FF, D), lambda t, te, tv: (te[t], 0, 0)),
        ],
        out_specs=pl.BlockSpec((tr, half), lambda t, te, tv: (t, 0)),
        scratch_shapes=[pltpu.VMEM((D, FF), BF16), pltpu.VMEM((D, FF), BF16), pltpu.VMEM((FF, D), BF16)],
    )
    return pl.pallas_call(
        _expert_body,
        grid_spec=grid_spec,
        out_shape=jax.ShapeDtypeStruct((NP, half), jnp.int32),
        compiler_params=_params(("arbitrary",)),
        name="moe_experts",
    )(tile_expert, tile_valid, xs, w_gate, w_up, w_down)


def _moe_combine_into(o_ref, h, meta, y0, y1):
    half = y0.shape[1]
    w0 = meta[:, 4:5]
    w1 = meta[:, 5:6]
    hi0, lo0 = _unpack_bf16_pairs(y0)
    hi1, lo1 = _unpack_bf16_pairs(y1)
    o_ref[:, :half] = h[:, :half] + w0 * hi0 + w1 * hi1
    o_ref[:, half:] = h[:, half:] + w0 * lo0 + w1 * lo1


def _moe(h2, gain, w_router, layer, w_gate, w_up, w_down):
    T, D = h2.shape
    E = N_EXPERTS
    tr = EXPERT_TILE
    n_sorted = _round_up(2 * T + E * (tr - 1), tr)
    t_pad = _round_up(T, SC_CHUNK // 2)

    xpk, meta, meta_t, counts = _moe_route(h2, gain, w_router)

    cnt = counts[0, :E].astype(jnp.int32)
    padded = (cnt + tr - 1) // tr * tr
    ends = jnp.cumsum(padded)
    starts = ends - padded
    eid = meta_t[0:2].astype(jnp.int32)
    pos = meta_t[2:4].astype(jnp.int32)
    for e in range(E):
        pos = pos + jnp.where(eid == e, starts[e], 0)
    tile_start = jnp.arange(n_sorted // tr, dtype=jnp.int32) * tr
    tile_expert = jnp.minimum(jnp.sum(tile_start[:, None] >= ends[None, :], axis=1), E - 1).astype(jnp.int32)
    tile_valid = jnp.clip(cnt[tile_expert] - (tile_start - starts[tile_expert]), 0, tr).astype(jnp.int32)

    xs = _sc_dispatch(xpk, pos, n_sorted)
    ys = _moe_experts(xs, n_sorted, layer * E + tile_expert, tile_valid, w_gate, w_up, w_down)
    spare = jnp.arange(t_pad - T, dtype=jnp.int32)
    back = jnp.concatenate([pos, jnp.broadcast_to(spare[None], (2, t_pad - T))], axis=1)
    yg = _sc_gather(ys, back.reshape(-1)).reshape(2, t_pad, D // 2)
    return meta, yg


def _final_body(h_ref, meta_ref, y0_ref, y1_ref, g_ref, o_ref, hn_ref):
    _moe_combine_into(hn_ref, h_ref[0], meta_ref[0], y0_ref[0], y1_ref[0])
    o_ref[0] = _rms(hn_ref[N_META:, :], g_ref[...])


def _final_norm(h3, meta3, yg, gain, b0, nb):
    _, L, D = h3.shape
    return pl.pallas_call(
        _final_body,
        grid=(nb,),
        in_specs=[pl.BlockSpec((1, L, D), lambda b: (b0 + b, 0, 0)),
                  pl.BlockSpec((1, L, meta3.shape[2]), lambda b: (b0 + b, 0, 0)),
                  pl.BlockSpec((1, L, D // 2), lambda b: (0, b0 + b, 0)),
                  pl.BlockSpec((1, L, D // 2), lambda b: (1, b0 + b, 0)),
                  pl.BlockSpec((1, D), lambda b: (0, 0))],
        out_specs=pl.BlockSpec((1, L - N_META, D), lambda b: (b, 0, 0)),
        out_shape=jax.ShapeDtypeStruct((nb, L - N_META, D), F32),
        scratch_shapes=[pltpu.VMEM((L, D), F32)],
        compiler_params=_params(("parallel",)),
        name="combine_final_norm",
    )(h3, meta3, yg, yg, gain.reshape(1, D))


def kernel(x_prompt, x_sample, meta_tokens, norm_mix, w_in, ret_decay_fwd, ret_decay_bwd, hy_short_w, hy_short_b, hy_filt_w1, hy_filt_b1, hy_filt_w2, hy_filt_b2, hy_filt_w3, hy_sin_freq, hy_skip, w_ret_o, w_hy_o, w_out, norm_ffn, router_group, router_expert, moe_w_gate, moe_w_up, moe_w_down, norm_final):
    assert x_prompt.shape[1:] == x_sample.shape[1:]
    nbp, nbs = x_prompt.shape[0], x_sample.shape[0]
    B = nbp + nbs
    D = x_prompt.shape[2]
    L = N_META + x_prompt.shape[1]
    T = B * L
    depth = w_in.shape[0]
    q_w = RET_HEADS * RET_DK
    v_w = RET_HEADS * RET_DV
    hy_w = hy_skip.shape[2]
    hy_col0 = 2 * q_w + 2 * v_w
    gate_col0 = hy_col0 + 3 * hy_w
    assert D == q_w and w_in.shape[2] == gate_col0 + 2 * D

    x = jnp.concatenate([x_prompt, x_sample], axis=0)
    meta = jnp.broadcast_to(meta_tokens[None].astype(x.dtype), (B, N_META, D))
    h = jnp.concatenate([meta, x], axis=1).reshape(T, D)

    cos_t, sin_t = _rotary_tables(L)
    dims = _HyenaDims(L)
    dft_fwd, dft_inv = _dft_tables(dims)
    filt_consts = _filter_constants(L, hy_w)
    router = jnp.concatenate([router_expert, router_group], axis=2).astype(F32)
    router = jnp.pad(router, ((0, 0), (0, 0), (0, LANES - router.shape[2])))
    router_hi = router.astype(BF16)
    router_lo = (router - router_hi.astype(F32)).astype(BF16)
    assert moe_w_gate.shape[1] == N_EXPERTS
    w_gate_all = moe_w_gate.reshape((-1,) + moe_w_gate.shape[2:])
    w_up_all = moe_w_up.reshape((-1,) + moe_w_up.shape[2:])
    w_down_all = moe_w_down.reshape((-1,) + moe_w_down.shape[2:])

    moe = None
    for i in range(depth):
        proj, h = _inproj(h, norm_mix[i], w_in[i].astype(BF16), moe)
        proj3 = proj.reshape(B, L, -1)

        lg = jnp.stack([jax.nn.log_sigmoid(ret_decay_fwd[i].astype(F32)),
                        jax.nn.log_sigmoid(ret_decay_bwd[i].astype(F32))])
        ret = _retention(proj3, lg, cos_t, sin_t)

        hs, hd = _hyena_filters(dims, filt_consts, hy_filt_w1[i], hy_filt_b1[i], hy_filt_w2[i], hy_filt_b2[i],
                                hy_filt_w3[i], hy_sin_freq[i])
        spectrum = _filter_spectrum(dims, dft_fwd, hs, hd)
        sw = hy_short_w[i].astype(F32)
        sb = hy_short_b[i].astype(F32).reshape(1, -1)
        z = _long_conv(dims, dft_fwd, dft_inv, proj3, hy_col0, proj3, hy_col0 + hy_w, sw, sb, 0, hy_w,
                       hy_skip[i, 0].astype(F32), spectrum, 0, hy_w)
        z = _long_conv(dims, dft_fwd, dft_inv, z, 0, proj3, hy_col0 + 2 * hy_w, sw, sb, None, 2 * hy_w,
                       hy_skip[i, 1].astype(F32), spectrum, hy_w, hy_w)

        h = _mix_out(ret.reshape(T, v_w), z.reshape(T, hy_w), proj, 2 * q_w + v_w, gate_col0, h,
                     w_ret_o[i].astype(BF16), w_hy_o[i].astype(BF16), w_out[i].astype(BF16))
        moe = _moe(h, norm_ffn[i], (router_hi[i], router_lo[i]), i, w_gate_all, w_up_all, w_down_all)

    h3 = h.reshape(B, L, D)
    meta3 = moe[0].reshape(B, L, -1)
    return (_final_norm(h3, meta3, moe[1], norm_final, 0, nbp),
            _final_norm(h3, meta3, moe[1], norm_final, nbp, nbs))
```

```python
import functools
import math

import jax
import jax.numpy as jnp
import numpy as np
from jax import lax
from jax.experimental import pallas as pl
from jax.experimental.pallas import tpu as pltpu
from jax.experimental.pallas import tpu_sc as plsc

N_META = 16
RET_HEADS = 8
RET_DK = 128
RET_DV = 256
ROPE_THETA = 10000.0
HY_ORDER = 2
HY_SHORT = 3
HY_EMB = 33
HY_BANDS = (HY_EMB - 1) // 2
HY_DECAY_TARGET = 1e-2
HY_MIN_DECAY = math.log(HY_DECAY_TARGET) / 1.5
HY_MAX_DECAY = math.log(HY_DECAY_TARGET) / 0.3
N_GROUPS = 4
EXP_PER_GROUP = 8
N_EXPERTS = N_GROUPS * EXP_PER_GROUP
RMS_EPS = 1e-6

LANES = 128
BF16_SUBLANES = 16
MXU_DIM = 256
RET_CHUNK = 256
VMEM_LIMIT = 56 * 1024 * 1024

F32 = jnp.float32
BF16 = jnp.bfloat16


def _round_up(n, m):
    return (n + m - 1) // m * m


def _pick_tile(n, target, mult):
    best = None
    for t in range(mult, min(n, target) + 1, mult):
        if n % t == 0:
            best = t
    assert best is not None, (n, target, mult)
    return best


def _params(sem):
    return pltpu.CompilerParams(dimension_semantics=sem, vmem_limit_bytes=VMEM_LIMIT)


def _rms(x, gain):
    ms = jnp.mean(x * x, axis=-1, keepdims=True)
    return x * lax.rsqrt(ms + RMS_EPS) * gain


def _inproj_body(h_ref, g_ref, w_ref, o_ref, xn_ref):
    @pl.when(pl.program_id(1) == 0)
    def _():
        xn_ref[...] = _rms(h_ref[...], g_ref[...]).astype(BF16)

    o_ref[...] = jnp.dot(xn_ref[...], w_ref[...], preferred_element_type=F32).astype(o_ref.dtype)


def _inproj_moe_body(h_ref, meta_ref, y0_ref, y1_ref, g_ref, w_ref, o_ref, hn_ref, xn_ref):
    @pl.when(pl.program_id(1) == 0)
    def _():
        _moe_combine_into(hn_ref, h_ref[...], meta_ref[...], y0_ref[0], y1_ref[0])
        xn_ref[...] = _rms(hn_ref[...], g_ref[...]).astype(BF16)

    o_ref[...] = jnp.dot(xn_ref[...], w_ref[...], preferred_element_type=F32).astype(o_ref.dtype)


def _inproj(h2, gain, w, moe=None):
    T, D = h2.shape
    nc = w.shape[1]
    tm = _pick_tile(T, 1152 if moe is None else 576, BF16_SUBLANES)
    tn = _pick_tile(nc, 2816, MXU_DIM)
    common = dict(
        grid=(T // tm, nc // tn),
        scratch_shapes=[pltpu.VMEM((tm, D), BF16)],
        compiler_params=_params(("parallel", "arbitrary")),
    )
    h_spec = pl.BlockSpec((tm, D), lambda i, j: (i, 0))
    g_spec = pl.BlockSpec((1, D), lambda i, j: (0, 0))
    w_spec = pl.BlockSpec((D, tn), lambda i, j: (0, j))
    o_spec = pl.BlockSpec((tm, tn), lambda i, j: (i, j))
    o_shape = jax.ShapeDtypeStruct((T, nc), BF16)
    if moe is None:
        proj = pl.pallas_call(_inproj_body, in_specs=[h_spec, g_spec, w_spec], out_specs=o_spec,
                              out_shape=o_shape, name="norm_inproj", **common)(h2, gain.reshape(1, D), w)
        return proj, h2
    meta, yg = moe
    return pl.pallas_call(
        _inproj_moe_body,
        in_specs=[h_spec, pl.BlockSpec((tm, meta.shape[1]), lambda i, j: (i, 0)),
                  pl.BlockSpec((1, tm, D // 2), lambda i, j: (0, i, 0)),
                  pl.BlockSpec((1, tm, D // 2), lambda i, j: (1, i, 0)),
                  g_spec, w_spec],
        out_specs=[o_spec, h_spec],
        out_shape=[o_shape, jax.ShapeDtypeStruct((T, D), F32)],
        name="combine_norm_inproj", **common,
    )(h2, meta, yg, yg, gain.reshape(1, D), w)


def _dot_t(a, b):
    return lax.dot_general(a, b, (((0,), (0,)), ((), ())), preferred_element_type=F32)


def _dot_nt(a, b):
    return lax.dot_general(a, b, (((1,), (1,)), ((), ())), preferred_element_type=F32)


def _ret_body(lg_ref, q_ref, k_ref, v_ref, cos_ref, sin_ref, o_ref, rb_ref, kr_ref, *, seq_len):
    C = RET_CHUNK
    L = seq_len
    n_chunks = pl.cdiv(L, C)
    head = pl.program_id(1)
    lgf = lg_ref[0, head]
    lgb = lg_ref[1, head]

    def chunk(ref, n):
        lo, hi = n * C, min((n + 1) * C, L)
        x = ref[0, lo:hi, :]
        if hi - lo < C:
            x = jnp.concatenate([x, jnp.zeros((C - (hi - lo), x.shape[1]), x.dtype)], axis=0)
        return x

    def rotary(ref, n):
        x = chunk(ref, n).astype(F32)
        sl = slice(n * C, (n + 1) * C)
        return x * cos_ref[sl, :] + pltpu.roll(x, RET_DK // 2, 1) * sin_ref[sl, :]

    row = lax.broadcasted_iota(jnp.int32, (C, LANES), 0).astype(F32)
    ri = lax.broadcasted_iota(jnp.int32, (C, C), 0).astype(F32)
    ci = lax.broadcasted_iota(jnp.int32, (C, C), 1).astype(F32)
    decay = jnp.exp(jnp.where(ci <= ri, (ri - ci) * lgf, (ci - ri) * lgb))
    qf_dec = jnp.exp((row + 1.0) * lgf)
    qb_dec = jnp.exp((C - row) * lgb)
    kf_dec = jnp.exp((C - 1.0 - row) * lgf)
    kb_dec = jnp.exp(row * lgb)
    cf = jnp.exp(C * lgf)
    cb = jnp.exp(C * lgb)

    state = jnp.zeros((RET_DK, RET_DV), F32)
    for n in reversed(range(n_chunks)):
        rb_ref[n] = state.astype(BF16)
        if n > 0:
            kr_ref[n] = rotary(k_ref, n)
            kb = (kr_ref[n] * kb_dec).astype(BF16)
            state = cb * state + _dot_t(kb, chunk(v_ref, n))

    state = jnp.zeros((RET_DK, RET_DV), F32)
    for n in range(n_chunks):
        q = rotary(q_ref, n) * (RET_DK ** -0.5)
        k = kr_ref[n] if n > 0 else rotary(k_ref, n)
        v = chunk(v_ref, n)
        scores = _dot_nt(q.astype(BF16), k.astype(BF16)) * decay
        o = jnp.dot(scores.astype(BF16), v, preferred_element_type=F32)
        o += jnp.dot((q * qf_dec).astype(BF16), state.astype(BF16), preferred_element_type=F32)
        o += jnp.dot((q * qb_dec).astype(BF16), rb_ref[n], preferred_element_type=F32)
        if n + 1 < n_chunks:
            state = cf * state + _dot_t((k * kf_dec).astype(BF16), v)
        lo, hi = n * C, min((n + 1) * C, L)
        o_ref[0, lo:hi, :] = o[: hi - lo].astype(o_ref.dtype)


def _retention(proj3, lg, cos_t, sin_t):
    B, L, _ = proj3.shape
    n_chunks = pl.cdiv(L, RET_CHUNK)
    lp = n_chunks * RET_CHUNK
    k_blk = RET_HEADS
    v_blk = 2 * RET_HEADS * RET_DK // RET_DV
    return pl.pallas_call(
        functools.partial(_ret_body, seq_len=L),
        grid=(B, RET_HEADS),
        in_specs=[
            pl.BlockSpec(memory_space=pltpu.SMEM),
            pl.BlockSpec((1, L, RET_DK), lambda b, h: (b, 0, h)),
            pl.BlockSpec((1, L, RET_DK), lambda b, h: (b, 0, k_blk + h)),
            pl.BlockSpec((1, L, RET_DV), lambda b, h: (b, 0, v_blk + h)),
            pl.BlockSpec((lp, RET_DK), lambda b, h: (0, 0)),
            pl.BlockSpec((lp, RET_DK), lambda b, h: (0, 0)),
        ],
        out_specs=pl.BlockSpec((1, L, RET_DV), lambda b, h: (b, 0, h)),
        out_shape=jax.ShapeDtypeStruct((B, L, RET_HEADS * RET_DV), BF16),
        scratch_shapes=[pltpu.VMEM((n_chunks, RET_DK, RET_DV), BF16),
                        pltpu.VMEM((n_chunks, RET_CHUNK, RET_DK), F32)],
        compiler_params=_params(("parallel", "arbitrary")),
        name="retention",
    )(lg, proj3, proj3, proj3, cos_t, sin_t)


def _rotary_tables(L):
    half = RET_DK // 2
    lp = _round_up(L, RET_CHUNK)
    inv = ROPE_THETA ** (-jnp.arange(half, dtype=F32) / half)
    ang = jnp.arange(lp, dtype=F32)[:, None] * inv[None, :]
    cos, sin = jnp.cos(ang), jnp.sin(ang)
    return jnp.concatenate([cos, cos], axis=1), jnp.concatenate([-sin, sin], axis=1)


def _row_tiles(n, tile):
    return [(s, min(tile, n - s)) for s in range(0, n, tile)]


RADIX = 4


class _HyenaDims:
    def __init__(self, L):
        assert L % RADIX == 0
        self.L = L
        self.Q = L // RADIX
        self.F = self.Q + 1
        self.QP = _round_up(self.Q, BF16_SUBLANES)
        self.QK = _round_up(self.Q, MXU_DIM)
        self.FM = _round_up(self.F, BF16_SUBLANES)
        self.IK = _round_up(2 * self.FM, MXU_DIM)


def _dft_tables(dims):
    n = 2 * dims.L

    def trig(f, t, valid):
        ang = ((f * t) % n) * (2.0 * math.pi / n)
        return (np.where(valid, np.cos(ang), 0.0).astype(BF16), np.where(valid, np.sin(ang), 0.0).astype(BF16))

    f = np.arange(dims.FM, dtype=np.int64)[:, None]
    s = np.arange(dims.QK, dtype=np.int64)[None, :]
    ok = (f < dims.F) & (s < dims.Q)
    fwd = [trig(f, RADIX * s + r, ok) for r in range(RADIX)]

    s = np.arange(dims.QP, dtype=np.int64)[:, None]
    j = np.arange(dims.IK, dtype=np.int64)[None, :]
    f = j % dims.FM
    ok = (f < dims.F) & (s < dims.Q)
    inv = []
    for r in range(RADIX):
        cos, sin = trig(f, RADIX * s + r, ok)
        inv.append(np.where(j < dims.FM, cos, np.where(j < 2 * dims.FM, sin, np.zeros_like(cos))))
    return [t for pair in fwd for t in pair], inv


def _forward_dft(tables, parts, rows, dtype=F32):
    p = [jnp.dot(tables[2 * r][rows, :], parts[r][...], preferred_element_type=F32).astype(dtype)
         for r in range(RADIX)]
    a = [jnp.dot(tables[2 * r + 1][rows, :], parts[r][...], preferred_element_type=F32).astype(dtype)
         for r in range(RADIX)]
    ps02, pd02, ps13, pd13 = p[0] + p[2], p[0] - p[2], p[1] + p[3], p[1] - p[3]
    as02, ad02, as13, ad13 = a[0] + a[2], a[0] - a[2], a[1] + a[3], a[1] - a[3]
    return [(ps02 + ps13, as02 + as13),
            (pd02 + ad13, pd13 - ad02),
            (pd02 - ad13, pd13 + ad02),
            (ps02 - ps13, as13 - as02)]


def _class_weights(dims):
    f = lax.broadcasted_iota(jnp.int32, (dims.FM, 1), 0)
    n = 2.0 * dims.L
    edge = jnp.where(f == 0, 1.0, 2.0) / n
    return [jnp.where(f <= dims.Q, edge, 0.0),
            jnp.where(f < dims.Q, 2.0 / n, 0.0),
            jnp.where((f >= 1) & (f <= dims.Q), 2.0 / n, 0.0),
            jnp.where(f < dims.Q, edge, 0.0)]


def _spec_body(*refs, dims):
    tables = refs[:2 * RADIX]
    hs_ref, hd_ref = refs[2 * RADIX:2 * RADIX + 2]
    outs = refs[2 * RADIX + 2:4 * RADIX + 2]
    s_pads = refs[4 * RADIX + 2:5 * RADIX + 2]
    d_pads = refs[5 * RADIX + 2:]
    Q = dims.Q
    for pads, src in ((s_pads, hs_ref), (d_pads, hd_ref)):
        for r in range(RADIX):
            pads[r][Q:, :] = jnp.zeros((pads[r].shape[0] - Q, pads[r].shape[1]), BF16)
            pads[r][:Q, :] = src[r].astype(BF16)
    rows = slice(None)
    re = [c[0] for c in _forward_dft(tables, s_pads, rows)]
    nim = [c[1] for c in _forward_dft(tables, d_pads, rows)]
    for c, w in enumerate(_class_weights(dims)):
        outs[2 * c][...] = (w * re[c]).astype(outs[2 * c].dtype)
        outs[2 * c + 1][...] = (-w * nim[c]).astype(outs[2 * c + 1].dtype)


def _filter_spectrum(dims, fwd, hs, hd):
    width = hs.shape[2]
    cw = MXU_DIM
    resident = pl.BlockSpec((dims.FM, dims.QK), lambda j: (0, 0), pipeline_mode=pl.Buffered(1))
    return pl.pallas_call(
        functools.partial(_spec_body, dims=dims),
        grid=(width // cw,),
        in_specs=[resident] * (2 * RADIX) + [pl.BlockSpec((RADIX, dims.Q, cw), lambda j: (0, 0, j))] * 2,
        out_specs=[pl.BlockSpec((dims.FM, cw), lambda j: (0, j))] * (2 * RADIX),
        out_shape=[jax.ShapeDtypeStruct((dims.FM, width), BF16)] * (2 * RADIX),
        scratch_shapes=[pltpu.VMEM((dims.QK, cw), BF16)] * (2 * RADIX),
        compiler_params=_params(("arbitrary",)),
        name="hyena_filter_spectrum",
    )(*fwd, hs, hd)


STAGE_HEAD = 8


def _conv_body(*refs, dims, first):
    n_tab = 2 * RADIX
    tables = refs[:n_tab]
    inv_refs = refs[n_tab:n_tab + RADIX]
    u_ref, x_ref = refs[n_tab + RADIX:n_tab + RADIX + 2]
    at = n_tab + RADIX + 2
    n_taps = 4 if first else 2
    taps = refs[at:at + n_taps]
    at += n_taps
    skip_ref = refs[at]
    spec_refs = refs[at + 1:at + 1 + 2 * RADIX]
    o_ref = refs[at + 1 + 2 * RADIX]
    scratch = refs[at + 2 + 2 * RADIX:]
    u_store, scratch = scratch[:RADIX], scratch[RADIX:]
    x_parts, g_parts = scratch[:RADIX], scratch[RADIX:2 * RADIX]
    nat_ref, natx_ref = scratch[2 * RADIX:]
    L, Q, QP, QK, FM, IK = dims.L, dims.Q, dims.QP, dims.QK, dims.FM, dims.IK
    cw = x_parts[0].shape[1]

    lane_slabs = [(k, slice(k * LANES, (k + 1) * LANES)) for k in range(cw // LANES)]

    def short_conv_split(parts, ref, w_ref, b_ref, stage_ref, anchor=None):
        tail = stage_ref.shape[1] - STAGE_HEAD - L
        for k, lanes in lane_slabs:
            raw = ref[0, :, lanes].astype(F32)
            if anchor is not None:
                raw = raw + anchor[:, lanes]
            stage_ref[k, :STAGE_HEAD, :] = jnp.zeros((STAGE_HEAD, LANES), F32)
            stage_ref[k, STAGE_HEAD:STAGE_HEAD + L, :] = raw
            stage_ref[k, STAGE_HEAD + L:, :] = jnp.zeros((tail, LANES), F32)
            w = w_ref[:, lanes]
            bias = b_ref[:, lanes]
            for r in range(RADIX):
                taps3 = [stage_ref[k, pl.ds(STAGE_HEAD + r + d, Q, stride=RADIX), :] for d in (-1, 0, 1)]
                val = bias + w[0:1] * taps3[0] + w[1:2] * taps3[1] + w[2:3] * taps3[2]
                parts[r][:Q, lanes] = val.astype(BF16)
                if QP > Q:
                    parts[r][Q:QP, lanes] = jnp.zeros((QP - Q, LANES), BF16)

    def zero_tail(parts):
        if QK > QP:
            for r in range(RADIX):
                parts[r][QP:, :] = jnp.zeros((QK - QP, cw), BF16)

    def run(u_parts):
        for r in range(RADIX):
            if IK > 2 * FM:
                g_parts[r][2 * FM:, :] = jnp.zeros((IK - 2 * FM, cw), BF16)
        for lo, sz in _row_tiles(FM, MXU_DIM):
            rows = slice(lo, lo + sz)
            e1, e2 = [], []
            for c, (p, a) in enumerate(_forward_dft(tables, u_parts, rows, BF16)):
                k1, k2 = spec_refs[2 * c][rows, :], spec_refs[2 * c + 1][rows, :]
                e1.append(p * k1 + a * k2)
                e2.append(a * k1 - p * k2)
            for r, (gc, gs) in enumerate(_fold_classes(e1, e2)):
                g_parts[r][lo:lo + sz, :] = gc
                g_parts[r][FM + lo:FM + lo + sz, :] = gs

        anchor = g_parts[RADIX - 1][2 * FM - 1:2 * FM, :].astype(F32) * 0.0
        short_conv_split(x_parts, x_ref, taps[-2], taps[-1], natx_ref, anchor)
        skip = skip_ref[...]
        for lo, sz in _row_tiles(QP, MXU_DIM):
            rows = slice(lo, lo + sz)
            valid = min(sz, Q - lo)
            for r in range(RADIX):
                y = jnp.dot(inv_refs[r][rows, :], g_parts[r][...], preferred_element_type=F32)
                o = x_parts[r][rows, :].astype(F32) * (y + skip * u_parts[r][rows, :].astype(F32))
                if first:
                    o_ref[0, r, rows, :] = o.astype(o_ref.dtype)
                elif valid > 0:
                    for k, lanes in lane_slabs:
                        nat_ref[k, pl.ds(RADIX * lo + r, valid, stride=RADIX), :] = o[:valid, lanes]
        if not first:
            for k, lanes in lane_slabs:
                o_ref[0, :, lanes] = nat_ref[k, :L, :]

    zero_tail(u_store)
    if first:
        short_conv_split(u_store, u_ref, taps[0], taps[1], nat_ref)
    else:
        for r in range(RADIX):
            u_store[r][:QP, :] = u_ref[0, r]
    run(u_store)


def _fold_classes(e1, e2):
    a_p, a_m = e1[0] + e1[3], e1[0] - e1[3]
    b_p, b_m = e1[1] + e1[2], e1[1] - e1[2]
    c_p, c_m = e2[0] + e2[3], e2[0] - e2[3]
    d_p, d_m = e2[1] + e2[2], e2[2] - e2[1]
    return [(a_p + b_p, c_m + d_m),
            (a_m + d_p, c_p + b_m),
            (a_p - b_p, c_m - d_m),
            (a_m - d_p, c_p - b_m)]


def _long_conv(dims, fwd, inv, u_arr, u_col0, x_arr, x_col0, short_w, short_b, short_u_col0, short_x_col0,
               skip, spectrum, k_col0, width):
    B = u_arr.shape[0]
    L = dims.L
    cw = MXU_DIM
    ub, xb, kb = u_col0 // cw, x_col0 // cw, k_col0 // cw
    first = short_u_col0 is not None
    fwd_spec = pl.BlockSpec((dims.FM, dims.QK), lambda c, b: (0, 0), pipeline_mode=pl.Buffered(1))
    inv_spec = pl.BlockSpec((dims.QP, dims.IK), lambda c, b: (0, 0), pipeline_mode=pl.Buffered(1))

    def taps(col0):
        blk = col0 // cw
        return [pl.BlockSpec((HY_SHORT, cw), lambda c, b: (0, blk + c)),
                pl.BlockSpec((1, cw), lambda c, b: (0, blk + c))]

    if first:
        u_spec = pl.BlockSpec((1, L, cw), lambda c, b: (b, 0, ub + c))
        out_spec = pl.BlockSpec((1, RADIX, dims.QP, cw), lambda c, b: (b, 0, 0, c))
        out_shape = jax.ShapeDtypeStruct((B, RADIX, dims.QP, width), BF16)
    else:
        u_spec = pl.BlockSpec((1, RADIX, dims.QP, cw), lambda c, b: (b, 0, 0, ub + c))
        out_spec = pl.BlockSpec((1, L, cw), lambda c, b: (b, 0, c))
        out_shape = jax.ShapeDtypeStruct((B, L, width), F32)
    in_specs = ([fwd_spec] * (2 * RADIX) + [inv_spec] * RADIX
                + [u_spec, pl.BlockSpec((1, L, cw), lambda c, b: (b, 0, xb + c))])
    args = list(fwd) + list(inv) + [u_arr, x_arr]
    for col0 in ([short_u_col0] if first else []) + [short_x_col0]:
        in_specs += taps(col0)
        args += [short_w, short_b]
    in_specs += [pl.BlockSpec((1, cw), lambda c, b: (0, c))]
    in_specs += [pl.BlockSpec((dims.FM, cw), lambda c, b: (0, kb + c))] * (2 * RADIX)
    args += [skip.reshape(1, width)] + list(spectrum)
    stage = pltpu.VMEM((cw // LANES, RADIX * dims.QP + 2 * STAGE_HEAD, LANES), F32)
    return pl.pallas_call(
        functools.partial(_conv_body, dims=dims, first=first),
        grid=(width // cw, B),
        in_specs=in_specs,
        out_specs=out_spec,
        out_shape=out_shape,
        scratch_shapes=([pltpu.VMEM((dims.QK, cw), BF16)] * RADIX
                        + [pltpu.VMEM((dims.QP, cw), BF16)] * RADIX
                        + [pltpu.VMEM((dims.IK, cw), BF16)] * RADIX + [stage, stage]),
        compiler_params=_params(("parallel", "arbitrary")),
        name="hyena_long_conv",
    )(*args)


def _filter_body(z_ref, w1_ref, b1_ref, w2_ref, b2_ref, fr_ref, w3f_ref, w3b_ref, dec_ref, hs_ref, hd_ref,
                 h2_ref, nat_ref, *, dims):
    hp = lax.Precision.HIGHEST

    @pl.when(pl.program_id(0) == 0)
    def _():
        fr = fr_ref[...]
        h1 = jnp.sin(fr * (jnp.dot(z_ref[...], w1_ref[...], precision=hp, preferred_element_type=F32)
                           + b1_ref[...]))
        h2_ref[...] = jnp.sin(fr * (jnp.dot(h1, w2_ref[...], precision=hp, preferred_element_type=F32)
                                    + b2_ref[...]))

    h2 = h2_ref[...]
    dec = dec_ref[...]
    hf = jnp.dot(h2, w3f_ref[...], precision=hp, preferred_element_type=F32) * dec
    hb = jnp.dot(h2, w3b_ref[...], precision=hp, preferred_element_type=F32) * dec
    hb = jnp.where(lax.broadcasted_iota(jnp.int32, hb.shape, 0) == 0, 0.0, hb)
    scale = lax.rsqrt(jnp.sum(hf * hf + hb * hb, axis=0, keepdims=True) + 1e-6)
    for ref, val in ((hs_ref, (hf + hb) * scale), (hd_ref, (hf - hb) * scale)):
        for k in range(val.shape[1] // LANES):
            lanes = slice(k * LANES, (k + 1) * LANES)
            nat_ref[k] = val[:, lanes]
            for r in range(RADIX):
                ref[r, :, lanes] = nat_ref[k, pl.ds(r, dims.Q, stride=RADIX), :]


def _filter_constants(L, width):
    t = np.linspace(0.0, 1.0, L)
    w = (2.0 * math.pi / L) * np.arange(L)
    bands = np.linspace(1e-4, HY_BANDS - 1, HY_BANDS)
    fw = w[:, None] * bands[None, :]
    z = np.concatenate([t[:, None], np.cos(fw), -np.sin(fw)], axis=-1)
    z = np.pad(z, ((0, 0), (0, LANES - z.shape[1])))
    deltas = np.abs(np.linspace(HY_MIN_DECAY, HY_MAX_DECAY, width))
    return z.astype(np.float32), np.exp(-t[:, None] * deltas[None, :]).astype(np.float32)


def _hyena_filters(dims, consts, w1, b1, w2, b2, w3, freq):
    L = dims.L
    z, dec = consts
    hidden = w2.shape[0]
    width = w3.shape[1] // (2 * HY_ORDER)
    cw = MXU_DIM
    per_order = width // cw

    def lanes128(a):
        return jnp.pad(a.astype(F32), [(0, 0)] * (a.ndim - 1) + [(0, LANES - a.shape[-1])])

    w1p = jnp.pad(lanes128(w1), ((0, LANES - w1.shape[0]), (0, 0)))
    w2p = jnp.pad(lanes128(w2), ((0, LANES - hidden), (0, 0)))
    w3p = jnp.pad(w3.astype(F32), ((0, LANES - hidden), (0, 0)))
    row = lambda a: lanes128(a).reshape(1, LANES)
    full = lambda shape: pl.BlockSpec(shape, lambda j: (0, 0))
    fwd_col = lambda j: (0, (j // per_order) * 2 * per_order + j % per_order)
    bwd_col = lambda j: (0, (j // per_order) * 2 * per_order + per_order + j % per_order)
    out_spec = pl.BlockSpec((RADIX, dims.Q, cw), lambda j: (0, 0, j))
    return pl.pallas_call(
        functools.partial(_filter_body, dims=dims),
        grid=(HY_ORDER * per_order,),
        in_specs=[full((L, LANES)), full((LANES, LANES)), full((1, LANES)), full((LANES, LANES)),
                  full((1, LANES)), full((1, LANES)),
                  pl.BlockSpec((LANES, cw), fwd_col), pl.BlockSpec((LANES, cw), bwd_col),
                  pl.BlockSpec((L, cw), lambda j: (0, j % per_order))],
        out_specs=[out_spec, out_spec],
        out_shape=[jax.ShapeDtypeStruct((RADIX, dims.Q, HY_ORDER * width), F32)] * 2,
        scratch_shapes=[pltpu.VMEM((L, LANES), F32), pltpu.VMEM((cw // LANES, L, LANES), F32)],
        compiler_params=_params(("arbitrary",)),
        name="hyena_filters",
    )(z, w1p, row(b1), w2p, row(b2), row(freq), w3p, w3p, dec)


def _mix_body(ret_ref, g_ref, hy_ref, gr_ref, gh_ref, h_ref, wr_ref, wh_ref, wo_ref, o_ref):
    ret = None
    for hd in range(RET_HEADS):
        cols = slice(hd * RET_DV, (hd + 1) * RET_DV)
        o = ret_ref[:, cols].astype(F32)
        o = o * lax.rsqrt(jnp.mean(o * o, axis=-1, keepdims=True) + RMS_EPS)
        g = g_ref[:, cols].astype(F32)
        part = jnp.dot((g * jax.nn.sigmoid(g) * o).astype(BF16), wr_ref[cols, :], preferred_element_type=F32)
        ret = part if ret is None else ret + part
    hyo = jnp.dot(hy_ref[...].astype(BF16), wh_ref[...], preferred_element_type=F32)
    merged = (jax.nn.sigmoid(gr_ref[...].astype(F32)) * ret
              + jax.nn.sigmoid(gh_ref[...].astype(F32)) * hyo)
    o_ref[...] = h_ref[...] + jnp.dot(merged.astype(BF16), wo_ref[...], preferred_element_type=F32)


def _mix_out(ret2, hy2, proj2, g_col0, gate_col0, h2, w_ret_o, w_hy_o, w_out):
    T, D = h2.shape
    tm = _pick_tile(T, 384, BF16_SUBLANES)
    gb = gate_col0 // D
    v_w = ret2.shape[1]
    assert g_col0 % v_w == 0
    row = lambda w: pl.BlockSpec((tm, w), lambda i: (i, 0))
    full = lambda a: pl.BlockSpec(a.shape, lambda i: (0, 0))
    return pl.pallas_call(
        _mix_body,
        grid=(T // tm,),
        in_specs=[
            row(v_w), pl.BlockSpec((tm, v_w), lambda i: (i, g_col0 // v_w)), row(hy2.shape[1]),
            pl.BlockSpec((tm, D), lambda i: (i, gb)),
            pl.BlockSpec((tm, D), lambda i: (i, gb + 1)),
            row(D), full(w_ret_o), full(w_hy_o), full(w_out),
        ],
        out_specs=row(D),
        out_shape=jax.ShapeDtypeStruct((T, D), F32),
        compiler_params=_params(("parallel",)),
        name="merge_outproj",
    )(ret2, proj2, hy2, proj2, proj2, h2, w_ret_o, w_hy_o, w_out)


EXPERT_TILE = 512
META_COLS = 8
SC_WINDOW = 64
SC_WORKERS = 32
SC_CHUNK = 2 * SC_WINDOW * SC_WORKERS
SC_SCATTER_WINDOW = 128


def _pack_bf16_pairs(x):
    n = x.shape[1] // 2
    xb = x.astype(BF16).astype(F32)
    hi = lax.bitcast_convert_type(xb[:, :n], jnp.uint32)
    lo = lax.bitcast_convert_type(xb[:, n:], jnp.uint32)
    return lax.bitcast_convert_type(hi | (lo >> 16), jnp.int32)


def _unpack_bf16_pairs(w):
    u = lax.bitcast_convert_type(w, jnp.uint32)
    hi = lax.bitcast_convert_type(u & jnp.uint32(0xFFFF0000), F32)
    lo = lax.bitcast_convert_type(u << 16, F32)
    return hi, lo


def _route(lt):
    assert EXP_PER_GROUP == 8 and N_GROUPS <= 8
    tm = lt.shape[1]
    row = lax.broadcasted_iota(jnp.int32, (8, tm), 0)
    neg = -jnp.inf
    big = jnp.int32(1 << 20)
    gl = jnp.where(row < N_GROUPS, lt[N_EXPERTS:N_EXPERTS + 8, :], neg)
    gmax = jnp.max(gl, axis=0, keepdims=True)
    p_top = 1.0 / jnp.sum(jnp.exp(gl - gmax), axis=0, keepdims=True)
    g_idx = jnp.min(jnp.where(gl == gmax, row, big), axis=0, keepdims=True)
    el = lt[0:8, :]
    for g in range(1, N_GROUPS):
        el = jnp.where(g_idx == g, lt[8 * g:8 * g + 8, :], el)
    m1 = jnp.max(el, axis=0, keepdims=True)
    i1 = jnp.min(jnp.where(el == m1, row, big), axis=0, keepdims=True)
    el2 = jnp.where(row == i1, neg, el)
    m2 = jnp.max(el2, axis=0, keepdims=True)
    i2 = jnp.min(jnp.where(el2 == m2, row, big), axis=0, keepdims=True)
    r = jnp.exp(m2 - m1)
    base = g_idx * EXP_PER_GROUP
    return base + i1, base + i2, p_top / (1.0 + r), p_top * r / (1.0 + r)


def _route_body(h_ref, g_ref, whi_ref, wlo_ref, xpk_ref, meta_ref, meta_t_ref, cnt_ref, carry_ref):
    i = pl.program_id(0)

    @pl.when(i == 0)
    def _():
        carry_ref[...] = jnp.zeros_like(carry_ref)

    xn = _rms(h_ref[...], g_ref[...])
    xpk_ref[...] = _pack_bf16_pairs(xn)
    x_hi = xn.astype(BF16)
    x_lo = (xn - x_hi.astype(F32)).astype(BF16)
    logits = (jnp.dot(x_hi, whi_ref[...], preferred_element_type=F32)
              + jnp.dot(x_lo, whi_ref[...], preferred_element_type=F32)
              + jnp.dot(x_hi, wlo_ref[...], preferred_element_type=F32))
    e0, e1, w0, w1 = _route(logits.T)
    tm = logits.shape[0]
    row = lax.broadcasted_iota(jnp.int32, (LANES, tm), 0)
    onehot_t = jnp.where((row == e0) | (row == e1), 1.0, 0.0).astype(BF16)
    lane_tiles = tm // LANES
    sub = LANES * max(k for k in (1, 2, 3) if lane_tiles % k == 0)
    ri = lax.broadcasted_iota(jnp.int32, (sub, sub), 0)
    ci = lax.broadcasted_iota(jnp.int32, (sub, sub), 1)
    earlier = jnp.where(ci < ri, 1.0, 0.0).astype(BF16)
    ones = jnp.ones((8, sub), BF16)
    running = carry_ref[...]
    blocks = []
    for j in range(tm // sub):
        part = onehot_t[:, j * sub:(j + 1) * sub]
        blocks.append((running + _dot_nt(earlier, part)).T)
        running = running + _dot_nt(ones, part)[0:1, :]
    prefix_t = jnp.concatenate(blocks, axis=1)
    r0 = jnp.sum(jnp.where(row == e0, prefix_t, 0.0), axis=0, keepdims=True)
    r1 = jnp.sum(jnp.where(row == e1, prefix_t, 0.0), axis=0, keepdims=True)
    row8 = lax.broadcasted_iota(jnp.int32, (META_COLS, tm), 0)
    meta_t = jnp.zeros((META_COLS, tm), F32)
    for c, val in enumerate((e0.astype(F32), e1.astype(F32), r0, r1, w0, w1)):
        meta_t = jnp.where(row8 == c, val, meta_t)
    meta_t_ref[...] = meta_t
    padded = jnp.concatenate([meta_t, jnp.zeros((LANES - META_COLS, tm), F32)], axis=0)
    meta_ref[...] = padded.T[:, :META_COLS]
    carry_ref[...] = running
    cnt_ref[...] = running


def _moe_route(h2, gain, w_router):
    T, D = h2.shape
    tm = _pick_tile(T, 1152, LANES)
    return pl.pallas_call(
        _route_body,
        grid=(T // tm,),
        in_specs=[
            pl.BlockSpec((tm, D), lambda i: (i, 0)),
            pl.BlockSpec((1, D), lambda i: (0, 0)),
            pl.BlockSpec((D, LANES), lambda i: (0, 0)),
            pl.BlockSpec((D, LANES), lambda i: (0, 0)),
        ],
        out_specs=[
            pl.BlockSpec((tm, D // 2), lambda i: (i, 0)),
            pl.BlockSpec((tm, META_COLS), lambda i: (i, 0)),
            pl.BlockSpec((META_COLS, tm), lambda i: (0, i)),
            pl.BlockSpec((1, LANES), lambda i: (0, 0)),
        ],
        out_shape=[
            jax.ShapeDtypeStruct((T, D // 2), jnp.int32),
            jax.ShapeDtypeStruct((T, META_COLS), F32),
            jax.ShapeDtypeStruct((META_COLS, T), F32),
            jax.ShapeDtypeStruct((1, LANES), F32),
        ],
        scratch_shapes=[pltpu.VMEM((1, LANES), F32)],
        compiler_params=_params(("arbitrary",)),
        name="moe_route",
    )(h2, gain.reshape(1, D), *w_router)


def _sc_gather(table, idx):
    n = idx.shape[0]
    width = table.shape[1]
    win = SC_WINDOW
    assert n % SC_CHUNK == 0
    per_worker = n // SC_WORKERS
    mesh = plsc.VectorSubcoreMesh(core_axis_name="c", subcore_axis_name="s")

    @functools.partial(
        pl.kernel, out_type=jax.ShapeDtypeStruct((n, width), table.dtype), mesh=mesh,
        scratch_types=[pltpu.VMEM((per_worker,), jnp.int32), pltpu.VMEM((2, win, width), table.dtype),
                       pltpu.SemaphoreType.DMA((2,)), pltpu.SemaphoreType.DMA((2,))],
        name="sc_row_gather")
    def gather(table_hbm, idx_hbm, out_hbm, idx_v, rows_v, gsem, osem):
        worker = lax.axis_index("s") * mesh.num_cores + lax.axis_index("c")
        base = worker * per_worker
        pltpu.sync_copy(idx_hbm.at[pl.ds(base, per_worker)], idx_v)

        @pl.loop(0, per_worker, step=2 * win)
        def _(off):
            fetch = [pltpu.async_copy(table_hbm.at[idx_v.at[pl.ds(off + s * win, win)]], rows_v.at[s], gsem.at[s])
                     for s in range(2)]
            store = []
            for s in range(2):
                fetch[s].wait()
                store.append(pltpu.async_copy(rows_v.at[s], out_hbm.at[pl.ds(base + off + s * win, win)],
                                              osem.at[s]))
            for s in range(2):
                store[s].wait()

    return gather(table, idx)


def _sc_dispatch(table, dest, n_out):
    n_rows, width = table.shape
    win = SC_SCATTER_WINDOW
    assert n_rows % win == 0
    n_win = n_rows // win
    per_worker = pl.cdiv(n_win, SC_WORKERS)
    idx = jnp.pad(dest, ((0, 0), (0, per_worker * SC_WORKERS * win - n_rows)))
    idx = idx.reshape(2, per_worker, SC_WORKERS, win).transpose(0, 2, 1, 3)
    mesh = plsc.VectorSubcoreMesh(core_axis_name="c", subcore_axis_name="s")

    @functools.partial(
        pl.kernel, out_type=jax.ShapeDtypeStruct((n_out, width), table.dtype), mesh=mesh,
        scratch_types=[pltpu.VMEM((per_worker, win), jnp.int32), pltpu.VMEM((per_worker, win), jnp.int32),
                       pltpu.VMEM((win, width), table.dtype),
                       pltpu.SemaphoreType.DMA, pltpu.SemaphoreType.DMA],
        name="sc_row_dispatch")
    def dispatch(table_hbm, idx_hbm, out_hbm, idx0_v, idx1_v, rows_v, sem0, sem1):
        worker = lax.axis_index("s") * mesh.num_cores + lax.axis_index("c")
        pltpu.sync_copy(idx_hbm.at[0, worker], idx0_v)
        pltpu.sync_copy(idx_hbm.at[1, worker], idx1_v)

        @pl.loop(0, per_worker)
        def _(j):
            window = j * SC_WORKERS + worker

            @pl.when(window < n_win)
            def _():
                pltpu.sync_copy(table_hbm.at[pl.ds(window * win, win)], rows_v)
                first = pltpu.async_copy(rows_v, out_hbm.at[idx0_v.at[j]], sem0)
                second = pltpu.async_copy(rows_v, out_hbm.at[idx1_v.at[j]], sem1)
                first.wait()
                second.wait()

    return dispatch(table, idx)


def _expert_body(te_ref, tv_ref, x_ref, wg_ref, wu_ref, wd_ref, y_ref, wg_s, wu_s, wd_s):
    t = pl.program_id(0)
    half = x_ref.shape[1]

    @pl.when((t == 0) | (te_ref[t] != te_ref[jnp.maximum(t - 1, 0)]))
    def _():
        wg_s[...] = wg_ref[0].astype(BF16)
        wu_s[...] = wu_ref[0].astype(BF16)
        wd_s[...] = wd_ref[0].astype(BF16)

    @pl.when(tv_ref[t] > 0)
    def _():
        row = lax.broadcasted_iota(jnp.int32, x_ref.shape, 0)
        hi, lo = _unpack_bf16_pairs(jnp.where(row < tv_ref[t], x_ref[...], 0))
        hi, lo = hi.astype(BF16), lo.astype(BF16)
        hg = jnp.dot(hi, wg_s[:half, :], preferred_element_type=F32)
        hg += jnp.dot(lo, wg_s[half:, :], preferred_element_type=F32)
        hu = jnp.dot(hi, wu_s[:half, :], preferred_element_type=F32)
        hu += jnp.dot(lo, wu_s[half:, :], preferred_element_type=F32)
        act = (hg * jax.nn.sigmoid(hg) * hu).astype(BF16)
        y_ref[...] = _pack_bf16_pairs(jnp.dot(act, wd_s[...], preferred_element_type=F32))

    @pl.when(tv_ref[t] == 0)
    def _():
        y_ref[...] = jnp.zeros_like(y_ref)


def _moe_experts(xs, n_sorted, tile_expert, tile_valid, w_gate, w_up, w_down):
    NP, half = n_sorted, xs.shape[1]
    _, D, FF = w_gate.shape
    tr = EXPERT_TILE
    grid_spec = pltpu.PrefetchScalarGridSpec(
        num_scalar_prefetch=2,
        grid=(NP // tr,),
        in_specs=[
            pl.BlockSpec((tr, half), lambda t, te, tv: (t, 0)),
            pl.BlockSpec((1, D, FF), lambda t, te, tv: (te[t], 0, 0)),
            pl.BlockSpec((1, D, FF), lambda t, te, tv: (te[t], 0, 0)),
            pl.BlockSpec((1, FF, D), lambda t, te, tv: (te[t], 0, 0)),
        ],
        out_specs=pl.BlockSpec((tr, half), lambda t, te, tv: (t, 0)),
        scratch_shapes=[pltpu.VMEM((D, FF), BF16), pltpu.VMEM((D, FF), BF16), pltpu.VMEM((FF, D), BF16)],
    )
    return pl.pallas_call(
        _expert_body,
        grid_spec=grid_spec,
        out_shape=jax.ShapeDtypeStruct((NP, half), jnp.int32),
        compiler_params=_params(("arbitrary",)),
        name="moe_experts",
    )(tile_expert, tile_valid, xs, w_gate, w_up, w_down)


def _moe_combine_into(o_ref, h, meta, y0, y1):
    half = y0.shape[1]
    w0 = meta[:, 4:5]
    w1 = meta[:, 5:6]
    hi0, lo0 = _unpack_bf16_pairs(y0)
    hi1, lo1 = _unpack_bf16_pairs(y1)
    o_ref[:, :half] = h[:, :half] + w0 * hi0 + w1 * hi1
    o_ref[:, half:] = h[:, half:] + w0 * lo0 + w1 * lo1


def _moe(h2, gain, w_router, layer, w_gate, w_up, w_down):
    T, D = h2.shape
    E = N_EXPERTS
    tr = EXPERT_TILE
    n_sorted = _round_up(2 * T + E * (tr - 1), tr)
    t_pad = _round_up(T, SC_CHUNK // 2)

    xpk, meta, meta_t, counts = _moe_route(h2, gain, w_router)

    cnt = counts[0, :E].astype(jnp.int32)
    padded = (cnt + tr - 1) // tr * tr
    ends = jnp.cumsum(padded)
    starts = ends - padded
    eid = meta_t[0:2].astype(jnp.int32)
    pos = meta_t[2:4].astype(jnp.int32)
    for e in range(E):
        pos = pos + jnp.where(eid == e, starts[e], 0)
    tile_start = jnp.arange(n_sorted // tr, dtype=jnp.int32) * tr
    tile_expert = jnp.minimum(jnp.sum(tile_start[:, None] >= ends[None, :], axis=1), E - 1).astype(jnp.int32)
    tile_valid = jnp.clip(cnt[tile_expert] - (tile_start - starts[tile_expert]), 0, tr).astype(jnp.int32)

    xs = _sc_dispatch(xpk, pos, n_sorted)
    ys = _moe_experts(xs, n_sorted, layer * E + tile_expert, tile_valid, w_gate, w_up, w_down)
    spare = jnp.arange(t_pad - T, dtype=jnp.int32)
    back = jnp.concatenate([pos, jnp.broadcast_to(spare[None], (2, t_pad - T))], axis=1)
    yg = _sc_gather(ys, back.reshape(-1)).reshape(2, t_pad, D // 2)
    return meta, yg


def _final_body(h_ref, meta_ref, y0_ref, y1_ref, g_ref, o_ref, hn_ref):
    _moe_combine_into(hn_ref, h_ref[0], meta_ref[0], y0_ref[0], y1_ref[0])
    o_ref[0] = _rms(hn_ref[N_META:, :], g_ref[...])


def _final_norm(h3, meta3, yg, gain, b0, nb):
    _, L, D = h3.shape
    return pl.pallas_call(
        _final_body,
        grid=(nb,),
        in_specs=[pl.BlockSpec((1, L, D), lambda b: (b0 + b, 0, 0)),
                  pl.BlockSpec((1, L, meta3.shape[2]), lambda b: (b0 + b, 0, 0)),
                  pl.BlockSpec((1, L, D // 2), lambda b: (0, b0 + b, 0)),
                  pl.BlockSpec((1, L, D // 2), lambda b: (1, b0 + b, 0)),
                  pl.BlockSpec((1, D), lambda b: (0, 0))],
        out_specs=pl.BlockSpec((1, L - N_META, D), lambda b: (b, 0, 0)),
        out_shape=jax.ShapeDtypeStruct((nb, L - N_META, D), F32),
        scratch_shapes=[pltpu.VMEM((L, D), F32)],
        compiler_params=_params(("parallel",)),
        name="combine_final_norm",
    )(h3, meta3, yg, yg, gain.reshape(1, D))


def kernel(x_prompt, x_sample, meta_tokens, norm_mix, w_in, ret_decay_fwd, ret_decay_bwd, hy_short_w, hy_short_b, hy_filt_w1, hy_filt_b1, hy_filt_w2, hy_filt_b2, hy_filt_w3, hy_sin_freq, hy_skip, w_ret_o, w_hy_o, w_out, norm_ffn, router_group, router_expert, moe_w_gate, moe_w_up, moe_w_down, norm_final):
    assert x_prompt.shape[1:] == x_sample.shape[1:]
    nbp, nbs = x_prompt.shape[0], x_sample.shape[0]
    B = nbp + nbs
    D = x_prompt.shape[2]
    L = N_META + x_prompt.shape[1]
    T = B * L
    depth = w_in.shape[0]
    q_w = RET_HEADS * RET_DK
    v_w = RET_HEADS * RET_DV
    hy_w = hy_skip.shape[2]
    hy_col0 = 2 * q_w + 2 * v_w
    gate_col0 = hy_col0 + 3 * hy_w
    assert D == q_w and w_in.shape[2] == gate_col0 + 2 * D

    x = jnp.concatenate([x_prompt, x_sample], axis=0)
    meta = jnp.broadcast_to(meta_tokens[None].astype(x.dtype), (B, N_META, D))
    h = jnp.concatenate([meta, x], axis=1).reshape(T, D)

    cos_t, sin_t = _rotary_tables(L)
    dims = _HyenaDims(L)
    dft_fwd, dft_inv = _dft_tables(dims)
    filt_consts = _filter_constants(L, hy_w)
    router = jnp.concatenate([router_expert, router_group], axis=2).astype(F32)
    router = jnp.pad(router, ((0, 0), (0, 0), (0, LANES - router.shape[2])))
    router_hi = router.astype(BF16)
    router_lo = (router - router_hi.astype(F32)).astype(BF16)
    assert moe_w_gate.shape[1] == N_EXPERTS
    w_gate_all = moe_w_gate.reshape((-1,) + moe_w_gate.shape[2:])
    w_up_all = moe_w_up.reshape((-1,) + moe_w_up.shape[2:])
    w_down_all = moe_w_down.reshape((-1,) + moe_w_down.shape[2:])

    moe = None
    for i in range(depth):
        proj, h = _inproj(h, norm_mix[i], w_in[i].astype(BF16), moe)
        proj3 = proj.reshape(B, L, -1)

        lg = jnp.stack([jax.nn.log_sigmoid(ret_decay_fwd[i].astype(F32)),
                        jax.nn.log_sigmoid(ret_decay_bwd[i].astype(F32))])
        ret = _retention(proj3, lg, cos_t, sin_t)

        hs, hd = _hyena_filters(dims, filt_consts, hy_filt_w1[i], hy_filt_b1[i], hy_filt_w2[i], hy_filt_b2[i],
                                hy_filt_w3[i], hy_sin_freq[i])
        spectrum = _filter_spectrum(dims, dft_fwd, hs, hd)
        sw = hy_short_w[i].astype(F32)
        sb = hy_short_b[i].astype(F32).reshape(1, -1)
        z = _long_conv(dims, dft_fwd, dft_inv, proj3, hy_col0, proj3, hy_col0 + hy_w, sw, sb, 0, hy_w,
                       hy_skip[i, 0].astype(F32), spectrum, 0, hy_w)
        z = _long_conv(dims, dft_fwd, dft_inv, z, 0, proj3, hy_col0 + 2 * hy_w, sw, sb, None, 2 * hy_w,
                       hy_skip[i, 1].astype(F32), spectrum, hy_w, hy_w)

        h = _mix_out(ret.reshape(T, v_w), z.reshape(T, hy_w), proj, 2 * q_w + v_w, gate_col0, h,
                     w_ret_o[i].astype(BF16), w_hy_o[i].astype(BF16), w_out[i].astype(BF16))
        moe = _moe(h, norm_ffn[i], (router_hi[i], router_lo[i]), i, w_gate_all, w_up_all, w_down_all)

    h3 = h.reshape(B, L, D)
    meta3 = moe[0].reshape(B, L, -1)
    return (_final_norm(h3, meta3, moe[1], norm_final, 0, nbp),
            _final_norm(h3, meta3, moe[1], norm_final, nbp, nbs))
```

```python
import functools
import math

import jax
import jax.numpy as jnp
import numpy as np
from jax import lax
from jax.experimental import pallas as pl
from jax.experimental.pallas import tpu as pltpu
from jax.experimental.pallas import tpu_sc as plsc

N_META = 16
RET_HEADS = 8
RET_DK = 128
RET_DV = 256
ROPE_THETA = 10000.0
HY_ORDER = 2
HY_SHORT = 3
HY_EMB = 33
HY_BANDS = (HY_EMB - 1) // 2
HY_DECAY_TARGET = 1e-2
HY_MIN_DECAY = math.log(HY_DECAY_TARGET) / 1.5
HY_MAX_DECAY = math.log(HY_DECAY_TARGET) / 0.3
N_GROUPS = 4
EXP_PER_GROUP = 8
N_EXPERTS = N_GROUPS * EXP_PER_GROUP
RMS_EPS = 1e-6

LANES = 128
BF16_SUBLANES = 16
MXU_DIM = 256
RET_CHUNK = 256
VMEM_LIMIT = 56 * 1024 * 1024

F32 = jnp.float32
BF16 = jnp.bfloat16


def _round_up(n, m):
    return (n + m - 1) // m * m


def _pick_tile(n, target, mult):
    best = None
    for t in range(mult, min(n, target) + 1, mult):
        if n % t == 0:
            best = t
    assert best is not None, (n, target, mult)
    return best


def _params(sem):
    return pltpu.CompilerParams(dimension_semantics=sem, vmem_limit_bytes=VMEM_LIMIT)


def _rms(x, gain):
    ms = jnp.mean(x * x, axis=-1, keepdims=True)
    return x * lax.rsqrt(ms + RMS_EPS) * gain


def _inproj_body(h_ref, g_ref, w_ref, o_ref, xn_ref):
    @pl.when(pl.program_id(1) == 0)
    def _():
        xn_ref[...] = _rms(h_ref[...], g_ref[...]).astype(BF16)

    o_ref[...] = jnp.dot(xn_ref[...], w_ref[...], preferred_element_type=F32).astype(o_ref.dtype)


def _proj_body(xn_ref, w_ref, o_ref):
    o_ref[...] = jnp.dot(xn_ref[...], w_ref[...], preferred_element_type=F32).astype(o_ref.dtype)


def _combine_norm_body(h_ref, meta_ref, y0_ref, y1_ref, g_ref, hn_ref, xn_ref):
    _moe_combine_into(hn_ref, h_ref[...], meta_ref[...], y0_ref[0], y1_ref[0])
    xn_ref[...] = _rms(hn_ref[...], g_ref[...]).astype(BF16)


def _inproj(h2, gain, w, moe=None):
    T, D = h2.shape
    nc = w.shape[1]
    tm = _pick_tile(T, 1152, BF16_SUBLANES)
    tn = _pick_tile(nc, 2816, MXU_DIM)
    grid = (T // tm, nc // tn)
    h_spec = pl.BlockSpec((tm, D), lambda i, j: (i, 0))
    w_spec = pl.BlockSpec((D, tn), lambda i, j: (0, j))
    o_spec = pl.BlockSpec((tm, tn), lambda i, j: (i, j))
    o_shape = jax.ShapeDtypeStruct((T, nc), BF16)
    if moe is None:
        proj = pl.pallas_call(
            _inproj_body, grid=grid,
            in_specs=[h_spec, pl.BlockSpec((1, D), lambda i, j: (0, 0)), w_spec],
            out_specs=o_spec, out_shape=o_shape,
            scratch_shapes=[pltpu.VMEM((tm, D), BF16)],
            compiler_params=_params(("parallel", "arbitrary")),
            name="norm_inproj")(h2, gain.reshape(1, D), w)
        return proj, h2
    meta, yg = moe
    row = pl.BlockSpec((tm, D), lambda i: (i, 0))
    hn, xn = pl.pallas_call(
        _combine_norm_body, grid=(T // tm,),
        in_specs=[row, pl.BlockSpec((tm, meta.shape[1]), lambda i: (i, 0)),
                  pl.BlockSpec((1, tm, D // 2), lambda i: (0, i, 0)),
                  pl.BlockSpec((1, tm, D // 2), lambda i: (1, i, 0)),
                  pl.BlockSpec((1, D), lambda i: (0, 0))],
        out_specs=[row, row],
        out_shape=[jax.ShapeDtypeStruct((T, D), F32), jax.ShapeDtypeStruct((T, D), BF16)],
        compiler_params=_params(("parallel",)),
        name="combine_norm")(h2, meta, yg, yg, gain.reshape(1, D))
    proj = pl.pallas_call(
        _proj_body, grid=grid, in_specs=[h_spec, w_spec], out_specs=o_spec, out_shape=o_shape,
        compiler_params=_params(("parallel", "parallel")),
        name="inproj")(xn, w)
    return proj, hn


def _dot_t(a, b):
    return lax.dot_general(a, b, (((0,), (0,)), ((), ())), preferred_element_type=F32)


def _dot_nt(a, b):
    return lax.dot_general(a, b, (((1,), (1,)), ((), ())), preferred_element_type=F32)


def _ret_body(lg_ref, q_ref, k_ref, v_ref, cos_ref, sin_ref, o_ref, rb_ref, kr_ref, *, seq_len):
    C = RET_CHUNK
    L = seq_len
    n_chunks = pl.cdiv(L, C)
    head = pl.program_id(1)
    lgf = lg_ref[0, head]
    lgb = lg_ref[1, head]

    def chunk(ref, n):
        lo, hi = n * C, min((n + 1) * C, L)
        x = ref[0, lo:hi, :]
        if hi - lo < C:
            x = jnp.concatenate([x, jnp.zeros((C - (hi - lo), x.shape[1]), x.dtype)], axis=0)
        return x

    def rotary(ref, n):
        x = chunk(ref, n).astype(F32)
        sl = slice(n * C, (n + 1) * C)
        return x * cos_ref[sl, :] + pltpu.roll(x, RET_DK // 2, 1) * sin_ref[sl, :]

    row = lax.broadcasted_iota(jnp.int32, (C, LANES), 0).astype(F32)
    ri = lax.broadcasted_iota(jnp.int32, (C, C), 0).astype(F32)
    ci = lax.broadcasted_iota(jnp.int32, (C, C), 1).astype(F32)
    decay = jnp.exp(jnp.where(ci <= ri, (ri - ci) * lgf, (ci - ri) * lgb))
    qf_dec = jnp.exp((row + 1.0) * lgf)
    qb_dec = jnp.exp((C - row) * lgb)
    kf_dec = jnp.exp((C - 1.0 - row) * lgf)
    kb_dec = jnp.exp(row * lgb)
    cf = jnp.exp(C * lgf)
    cb = jnp.exp(C * lgb)

    state = jnp.zeros((RET_DK, RET_DV), F32)
    for n in reversed(range(n_chunks)):
        rb_ref[n] = state.astype(BF16)
        if n > 0:
            kr_ref[n] = rotary(k_ref, n)
            kb = (kr_ref[n] * kb_dec).astype(BF16)
            state = cb * state + _dot_t(kb, chunk(v_ref, n))

    state = jnp.zeros((RET_DK, RET_DV), F32)
    for n in range(n_chunks):
        q = rotary(q_ref, n) * (RET_DK ** -0.5)
        k = kr_ref[n] if n > 0 else rotary(k_ref, n)
        v = chunk(v_ref, n)
        scores = _dot_nt(q.astype(BF16), k.astype(BF16)) * decay
        o = jnp.dot(scores.astype(BF16), v, preferred_element_type=F32)
        o += jnp.dot((q * qf_dec).astype(BF16), state.astype(BF16), preferred_element_type=F32)
        o += jnp.dot((q * qb_dec).astype(BF16), rb_ref[n], preferred_element_type=F32)
        if n + 1 < n_chunks:
            state = cf * state + _dot_t((k * kf_dec).astype(BF16), v)
        lo, hi = n * C, min((n + 1) * C, L)
        o_ref[0, lo:hi, :] = o[: hi - lo].astype(o_ref.dtype)


def _retention(proj3, lg, cos_t, sin_t):
    B, L, _ = proj3.shape
    n_chunks = pl.cdiv(L, RET_CHUNK)
    lp = n_chunks * RET_CHUNK
    k_blk = RET_HEADS
    v_blk = 2 * RET_HEADS * RET_DK // RET_DV
    return pl.pallas_call(
        functools.partial(_ret_body, seq_len=L),
        grid=(B, RET_HEADS),
        in_specs=[
            pl.BlockSpec(memory_space=pltpu.SMEM),
            pl.BlockSpec((1, L, RET_DK), lambda b, h: (b, 0, h)),
            pl.BlockSpec((1, L, RET_DK), lambda b, h: (b, 0, k_blk + h)),
            pl.BlockSpec((1, L, RET_DV), lambda b, h: (b, 0, v_blk + h)),
            pl.BlockSpec((lp, RET_DK), lambda b, h: (0, 0)),
            pl.BlockSpec((lp, RET_DK), lambda b, h: (0, 0)),
        ],
        out_specs=pl.BlockSpec((1, L, RET_DV), lambda b, h: (b, 0, h)),
        out_shape=jax.ShapeDtypeStruct((B, L, RET_HEADS * RET_DV), BF16),
        scratch_shapes=[pltpu.VMEM((n_chunks, RET_DK, RET_DV), BF16),
                        pltpu.VMEM((n_chunks, RET_CHUNK, RET_DK), F32)],
        compiler_params=_params(("parallel", "arbitrary")),
        name="retention",
    )(lg, proj3, proj3, proj3, cos_t, sin_t)


def _rotary_tables(L):
    half = RET_DK // 2
    lp = _round_up(L, RET_CHUNK)
    inv = ROPE_THETA ** (-jnp.arange(half, dtype=F32) / half)
    ang = jnp.arange(lp, dtype=F32)[:, None] * inv[None, :]
    cos, sin = jnp.cos(ang), jnp.sin(ang)
    return jnp.concatenate([cos, cos], axis=1), jnp.concatenate([-sin, sin], axis=1)


def _row_tiles(n, tile):
    return [(s, min(tile, n - s)) for s in range(0, n, tile)]


RADIX = 4


class _HyenaDims:
    def __init__(self, L):
        assert L % RADIX == 0
        self.L = L
        self.Q = L // RADIX
        self.F = self.Q + 1
        self.QP = _round_up(self.Q, BF16_SUBLANES)
        self.QK = _round_up(self.Q, MXU_DIM)
        self.FM = _round_up(self.F, BF16_SUBLANES)
        self.IK = _round_up(2 * self.FM, MXU_DIM)


def _dft_tables(dims):
    n = 2 * dims.L

    def trig(f, t, valid):
        ang = ((f * t) % n) * (2.0 * math.pi / n)
        return (np.where(valid, np.cos(ang), 0.0).astype(BF16), np.where(valid, np.sin(ang), 0.0).astype(BF16))

    f = np.arange(dims.FM, dtype=np.int64)[:, None]
    s = np.arange(dims.QK, dtype=np.int64)[None, :]
    ok = (f < dims.F) & (s < dims.Q)
    fwd = [trig(f, RADIX * s + r, ok) for r in range(RADIX)]

    s = np.arange(dims.QP, dtype=np.int64)[:, None]
    j = np.arange(dims.IK, dtype=np.int64)[None, :]
    f = j % dims.FM
    ok = (f < dims.F) & (s < dims.Q)
    inv = []
    for r in range(RADIX):
        cos, sin = trig(f, RADIX * s + r, ok)
        inv.append(np.where(j < dims.FM, cos, np.where(j < 2 * dims.FM, sin, np.zeros_like(cos))))
    return [t for pair in fwd for t in pair], inv


def _forward_dft(tables, parts, rows, dtype=F32):
    p = [jnp.dot(tables[2 * r][rows, :], parts[r][...], preferred_element_type=F32).astype(dtype)
         for r in range(RADIX)]
    a = [jnp.dot(tables[2 * r + 1][rows, :], parts[r][...], preferred_element_type=F32).astype(dtype)
         for r in range(RADIX)]
    ps02, pd02, ps13, pd13 = p[0] + p[2], p[0] - p[2], p[1] + p[3], p[1] - p[3]
    as02, ad02, as13, ad13 = a[0] + a[2], a[0] - a[2], a[1] + a[3], a[1] - a[3]
    return [(ps02 + ps13, as02 + as13),
            (pd02 + ad13, pd13 - ad02),
            (pd02 - ad13, pd13 + ad02),
            (ps02 - ps13, as13 - as02)]


def _class_weights(dims):
    f = lax.broadcasted_iota(jnp.int32, (dims.FM, 1), 0)
    n = 2.0 * dims.L
    edge = jnp.where(f == 0, 1.0, 2.0) / n
    return [jnp.where(f <= dims.Q, edge, 0.0),
            jnp.where(f < dims.Q, 2.0 / n, 0.0),
            jnp.where((f >= 1) & (f <= dims.Q), 2.0 / n, 0.0),
            jnp.where(f < dims.Q, edge, 0.0)]


def _spec_body(*refs, dims):
    tables = refs[:2 * RADIX]
    hs_ref, hd_ref = refs[2 * RADIX:2 * RADIX + 2]
    outs = refs[2 * RADIX + 2:4 * RADIX + 2]
    s_pads = refs[4 * RADIX + 2:5 * RADIX + 2]
    d_pads = refs[5 * RADIX + 2:]
    Q = dims.Q
    for pads, src in ((s_pads, hs_ref), (d_pads, hd_ref)):
        for r in range(RADIX):
            pads[r][Q:, :] = jnp.zeros((pads[r].shape[0] - Q, pads[r].shape[1]), BF16)
            pads[r][:Q, :] = src[r].astype(BF16)
    rows = slice(None)
    re = [c[0] for c in _forward_dft(tables, s_pads, rows)]
    nim = [c[1] for c in _forward_dft(tables, d_pads, rows)]
    for c, w in enumerate(_class_weights(dims)):
        outs[2 * c][...] = (w * re[c]).astype(outs[2 * c].dtype)
        outs[2 * c + 1][...] = (-w * nim[c]).astype(outs[2 * c + 1].dtype)


def _filter_spectrum(dims, fwd, hs, hd):
    width = hs.shape[2]
    cw = MXU_DIM
    resident = pl.BlockSpec((dims.FM, dims.QK), lambda j: (0, 0), pipeline_mode=pl.Buffered(1))
    return pl.pallas_call(
        functools.partial(_spec_body, dims=dims),
        grid=(width // cw,),
        in_specs=[resident] * (2 * RADIX) + [pl.BlockSpec((RADIX, dims.Q, cw), lambda j: (0, 0, j))] * 2,
        out_specs=[pl.BlockSpec((dims.FM, cw), lambda j: (0, j))] * (2 * RADIX),
        out_shape=[jax.ShapeDtypeStruct((dims.FM, width), BF16)] * (2 * RADIX),
        scratch_shapes=[pltpu.VMEM((dims.QK, cw), BF16)] * (2 * RADIX),
        compiler_params=_params(("arbitrary",)),
        name="hyena_filter_spectrum",
    )(*fwd, hs, hd)


STAGE_HEAD = 8


def _conv_body(*refs, dims, first):
    n_tab = 2 * RADIX
    tables = refs[:n_tab]
    inv_refs = refs[n_tab:n_tab + RADIX]
    u_ref, x_ref = refs[n_tab + RADIX:n_tab + RADIX + 2]
    at = n_tab + RADIX + 2
    n_taps = 4 if first else 2
    taps = refs[at:at + n_taps]
    at += n_taps
    skip_ref = refs[at]
    spec_refs = refs[at + 1:at + 1 + 2 * RADIX]
    o_ref = refs[at + 1 + 2 * RADIX]
    scratch = refs[at + 2 + 2 * RADIX:]
    u_store, scratch = scratch[:RADIX], scratch[RADIX:]
    x_parts, g_parts = scratch[:RADIX], scratch[RADIX:2 * RADIX]
    nat_ref, natx_ref = scratch[2 * RADIX:]
    L, Q, QP, QK, FM, IK = dims.L, dims.Q, dims.QP, dims.QK, dims.FM, dims.IK
    cw = x_parts[0].shape[1]

    lane_slabs = [(k, slice(k * LANES, (k + 1) * LANES)) for k in range(cw // LANES)]

    def short_conv_split(parts, ref, w_ref, b_ref, stage_ref, anchor=None):
        tail = stage_ref.shape[1] - STAGE_HEAD - L
        for k, lanes in lane_slabs:
            raw = ref[0, :, lanes].astype(F32)
            if anchor is not None:
                raw = raw + anchor[:, lanes]
            stage_ref[k, :STAGE_HEAD, :] = jnp.zeros((STAGE_HEAD, LANES), F32)
            stage_ref[k, STAGE_HEAD:STAGE_HEAD + L, :] = raw
            stage_ref[k, STAGE_HEAD + L:, :] = jnp.zeros((tail, LANES), F32)
            w = w_ref[:, lanes]
            bias = b_ref[:, lanes]
            for r in range(RADIX):
                taps3 = [stage_ref[k, pl.ds(STAGE_HEAD + r + d, Q, stride=RADIX), :] for d in (-1, 0, 1)]
                val = bias + w[0:1] * taps3[0] + w[1:2] * taps3[1] + w[2:3] * taps3[2]
                parts[r][:Q, lanes] = val.astype(BF16)
                if QP > Q:
                    parts[r][Q:QP, lanes] = jnp.zeros((QP - Q, LANES), BF16)

    def zero_tail(parts):
        if QK > QP:
            for r in range(RADIX):
                parts[r][QP:, :] = jnp.zeros((QK - QP, cw), BF16)

    def run(u_parts):
        for r in range(RADIX):
            if IK > 2 * FM:
                g_parts[r][2 * FM:, :] = jnp.zeros((IK - 2 * FM, cw), BF16)
        for lo, sz in _row_tiles(FM, MXU_DIM):
            rows = slice(lo, lo + sz)
            e1, e2 = [], []
            for c, (p, a) in enumerate(_forward_dft(tables, u_parts, rows, BF16)):
                k1, k2 = spec_refs[2 * c][rows, :], spec_refs[2 * c + 1][rows, :]
                e1.append(p * k1 + a * k2)
                e2.append(a * k1 - p * k2)
            for r, (gc, gs) in enumerate(_fold_classes(e1, e2)):
                g_parts[r][lo:lo + sz, :] = gc
                g_parts[r][FM + lo:FM + lo + sz, :] = gs

        anchor = g_parts[RADIX - 1][2 * FM - 1:2 * FM, :].astype(F32) * 0.0
        short_conv_split(x_parts, x_ref, taps[-2], taps[-1], natx_ref, anchor)
        skip = skip_ref[...]
        for lo, sz in _row_tiles(QP, MXU_DIM):
            rows = slice(lo, lo + sz)
            valid = min(sz, Q - lo)
            for r in range(RADIX):
                y = jnp.dot(inv_refs[r][rows, :], g_parts[r][...], preferred_element_type=F32)
                o = x_parts[r][rows, :].astype(F32) * (y + skip * u_parts[r][rows, :].astype(F32))
                if first:
                    o_ref[0, r, rows, :] = o.astype(o_ref.dtype)
                elif valid > 0:
                    for k, lanes in lane_slabs:
                        nat_ref[k, pl.ds(RADIX * lo + r, valid, stride=RADIX), :] = o[:valid, lanes]
        if not first:
            for k, lanes in lane_slabs:
                o_ref[0, :, lanes] = nat_ref[k, :L, :]

    zero_tail(u_store)
    if first:
        short_conv_split(u_store, u_ref, taps[0], taps[1], nat_ref)
    else:
        for r in range(RADIX):
            u_store[r][:QP, :] = u_ref[0, r]
    run(u_store)


def _fold_classes(e1, e2):
    a_p, a_m = e1[0] + e1[3], e1[0] - e1[3]
    b_p, b_m = e1[1] + e1[2], e1[1] - e1[2]
    c_p, c_m = e2[0] + e2[3], e2[0] - e2[3]
    d_p, d_m = e2[1] + e2[2], e2[2] - e2[1]
    return [(a_p + b_p, c_m + d_m),
            (a_m + d_p, c_p + b_m),
            (a_p - b_p, c_m - d_m),
            (a_m - d_p, c_p - b_m)]


def _long_conv(dims, fwd, inv, u_arr, u_col0, x_arr, x_col0, short_w, short_b, short_u_col0, short_x_col0,
               skip, spectrum, k_col0, width):
    B = u_arr.shape[0]
    L = dims.L
    cw = MXU_DIM
    ub, xb, kb = u_col0 // cw, x_col0 // cw, k_col0 // cw
    first = short_u_col0 is not None
    fwd_spec = pl.BlockSpec((dims.FM, dims.QK), lambda c, b: (0, 0), pipeline_mode=pl.Buffered(1))
    inv_spec = pl.BlockSpec((dims.QP, dims.IK), lambda c, b: (0, 0), pipeline_mode=pl.Buffered(1))

    def taps(col0):
        blk = col0 // cw
        return [pl.BlockSpec((HY_SHORT, cw), lambda c, b: (0, blk + c)),
                pl.BlockSpec((1, cw), lambda c, b: (0, blk + c))]

    if first:
        u_spec = pl.BlockSpec((1, L, cw), lambda c, b: (b, 0, ub + c))
        out_spec = pl.BlockSpec((1, RADIX, dims.QP, cw), lambda c, b: (b, 0, 0, c))
        out_shape = jax.ShapeDtypeStruct((B, RADIX, dims.QP, width), BF16)
    else:
        u_spec = pl.BlockSpec((1, RADIX, dims.QP, cw), lambda c, b: (b, 0, 0, ub + c))
        out_spec = pl.BlockSpec((1, L, cw), lambda c, b: (b, 0, c))
        out_shape = jax.ShapeDtypeStruct((B, L, width), F32)
    in_specs = ([fwd_spec] * (2 * RADIX) + [inv_spec] * RADIX
                + [u_spec, pl.BlockSpec((1, L, cw), lambda c, b: (b, 0, xb + c))])
    args = list(fwd) + list(inv) + [u_arr, x_arr]
    for col0 in ([short_u_col0] if first else []) + [short_x_col0]:
        in_specs += taps(col0)
        args += [short_w, short_b]
    in_specs += [pl.BlockSpec((1, cw), lambda c, b: (0, c))]
    in_specs += [pl.BlockSpec((dims.FM, cw), lambda c, b: (0, kb + c))] * (2 * RADIX)
    args += [skip.reshape(1, width)] + list(spectrum)
    stage = pltpu.VMEM((cw // LANES, RADIX * dims.QP + 2 * STAGE_HEAD, LANES), F32)
    return pl.pallas_call(
        functools.partial(_conv_body, dims=dims, first=first),
        grid=(width // cw, B),
        in_specs=in_specs,
        out_specs=out_spec,
        out_shape=out_shape,
        scratch_shapes=([pltpu.VMEM((dims.QK, cw), BF16)] * RADIX
                        + [pltpu.VMEM((dims.QP, cw), BF16)] * RADIX
                        + [pltpu.VMEM((dims.IK, cw), BF16)] * RADIX + [stage, stage]),
        compiler_params=_params(("parallel", "arbitrary")),
        name="hyena_long_conv",
    )(*args)


def _filter_body(z_ref, w1_ref, b1_ref, w2_ref, b2_ref, fr_ref, w3f_ref, w3b_ref, dec_ref, hs_ref, hd_ref,
                 h2_ref, nat_ref, *, dims):
    hp = lax.Precision.HIGHEST

    @pl.when(pl.program_id(0) == 0)
    def _():
        fr = fr_ref[...]
        h1 = jnp.sin(fr * (jnp.dot(z_ref[...], w1_ref[...], precision=hp, preferred_element_type=F32)
                           + b1_ref[...]))
        h2_ref[...] = jnp.sin(fr * (jnp.dot(h1, w2_ref[...], precision=hp, preferred_element_type=F32)
                                    + b2_ref[...]))

    h2 = h2_ref[...]
    dec = dec_ref[...]
    hf = jnp.dot(h2, w3f_ref[...], precision=hp, preferred_element_type=F32) * dec
    hb = jnp.dot(h2, w3b_ref[...], precision=hp, preferred_element_type=F32) * dec
    hb = jnp.where(lax.broadcasted_iota(jnp.int32, hb.shape, 0) == 0, 0.0, hb)
    scale = lax.rsqrt(jnp.sum(hf * hf + hb * hb, axis=0, keepdims=True) + 1e-6)
    for ref, val in ((hs_ref, (hf + hb) * scale), (hd_ref, (hf - hb) * scale)):
        for k in range(val.shape[1] // LANES):
            lanes = slice(k * LANES, (k + 1) * LANES)
            nat_ref[k] = val[:, lanes]
            for r in range(RADIX):
                ref[r, :, lanes] = nat_ref[k, pl.ds(r, dims.Q, stride=RADIX), :]


def _filter_constants(L, width):
    t = np.linspace(0.0, 1.0, L)
    w = (2.0 * math.pi / L) * np.arange(L)
    bands = np.linspace(1e-4, HY_BANDS - 1, HY_BANDS)
    fw = w[:, None] * bands[None, :]
    z = np.concatenate([t[:, None], np.cos(fw), -np.sin(fw)], axis=-1)
    z = np.pad(z, ((0, 0), (0, LANES - z.shape[1])))
    deltas = np.abs(np.linspace(HY_MIN_DECAY, HY_MAX_DECAY, width))
    return z.astype(np.float32), np.exp(-t[:, None] * deltas[None, :]).astype(np.float32)


def _hyena_filters(dims, consts, w1, b1, w2, b2, w3, freq):
    L = dims.L
    z, dec = consts
    hidden = w2.shape[0]
    width = w3.shape[1] // (2 * HY_ORDER)
    cw = MXU_DIM
    per_order = width // cw

    def lanes128(a):
        return jnp.pad(a.astype(F32), [(0, 0)] * (a.ndim - 1) + [(0, LANES - a.shape[-1])])

    w1p = jnp.pad(lanes128(w1), ((0, LANES - w1.shape[0]), (0, 0)))
    w2p = jnp.pad(lanes128(w2), ((0, LANES - hidden), (0, 0)))
    w3p = jnp.pad(w3.astype(F32), ((0, LANES - hidden), (0, 0)))
    row = lambda a: lanes128(a).reshape(1, LANES)
    full = lambda shape: pl.BlockSpec(shape, lambda j: (0, 0))
    fwd_col = lambda j: (0, (j // per_order) * 2 * per_order + j % per_order)
    bwd_col = lambda j: (0, (j // per_order) * 2 * per_order + per_order + j % per_order)
    out_spec = pl.BlockSpec((RADIX, dims.Q, cw), lambda j: (0, 0, j))
    return pl.pallas_call(
        functools.partial(_filter_body, dims=dims),
        grid=(HY_ORDER * per_order,),
        in_specs=[full((L, LANES)), full((LANES, LANES)), full((1, LANES)), full((LANES, LANES)),
                  full((1, LANES)), full((1, LANES)),
                  pl.BlockSpec((LANES, cw), fwd_col), pl.BlockSpec((LANES, cw), bwd_col),
                  pl.BlockSpec((L, cw), lambda j: (0, j % per_order))],
        out_specs=[out_spec, out_spec],
        out_shape=[jax.ShapeDtypeStruct((RADIX, dims.Q, HY_ORDER * width), F32)] * 2,
        scratch_shapes=[pltpu.VMEM((L, LANES), F32), pltpu.VMEM((cw // LANES, L, LANES), F32)],
        compiler_params=_params(("arbitrary",)),
        name="hyena_filters",
    )(z, w1p, row(b1), w2p, row(b2), row(freq), w3p, w3p, dec)


def _mix_body(ret_ref, g_ref, hy_ref, gr_ref, gh_ref, h_ref, wr_ref, wh_ref, wo_ref, o_ref):
    ret = None
    for hd in range(RET_HEADS):
        cols = slice(hd * RET_DV, (hd + 1) * RET_DV)
        o = ret_ref[:, cols].astype(F32)
        o = o * lax.rsqrt(jnp.mean(o * o, axis=-1, keepdims=True) + RMS_EPS)
        g = g_ref[:, cols].astype(F32)
        part = jnp.dot((g * jax.nn.sigmoid(g) * o).astype(BF16), wr_ref[cols, :], preferred_element_type=F32)
        ret = part if ret is None else ret + part
    hyo = jnp.dot(hy_ref[...].astype(BF16), wh_ref[...], preferred_element_type=F32)
    merged = (jax.nn.sigmoid(gr_ref[...].astype(F32)) * ret
              + jax.nn.sigmoid(gh_ref[...].astype(F32)) * hyo)
    o_ref[...] = h_ref[...] + jnp.dot(merged.astype(BF16), wo_ref[...], preferred_element_type=F32)


def _mix_out(ret2, hy2, proj2, g_col0, gate_col0, h2, w_ret_o, w_hy_o, w_out):
    T, D = h2.shape
    tm = _pick_tile(T, 384, BF16_SUBLANES)
    gb = gate_col0 // D
    v_w = ret2.shape[1]
    assert g_col0 % v_w == 0
    row = lambda w: pl.BlockSpec((tm, w), lambda i: (i, 0))
    full = lambda a: pl.BlockSpec(a.shape, lambda i: (0, 0))
    return pl.pallas_call(
        _mix_body,
        grid=(T // tm,),
        in_specs=[
            row(v_w), pl.BlockSpec((tm, v_w), lambda i: (i, g_col0 // v_w)), row(hy2.shape[1]),
            pl.BlockSpec((tm, D), lambda i: (i, gb)),
            pl.BlockSpec((tm, D), lambda i: (i, gb + 1)),
            row(D), full(w_ret_o), full(w_hy_o), full(w_out),
        ],
        out_specs=row(D),
        out_shape=jax.ShapeDtypeStruct((T, D), F32),
        compiler_params=_params(("parallel",)),
        name="merge_outproj",
    )(ret2, proj2, hy2, proj2, proj2, h2, w_ret_o, w_hy_o, w_out)


EXPERT_TILE = 512
META_COLS = 8
SC_WINDOW = 64
SC_WORKERS = 32
SC_CHUNK = 2 * SC_WINDOW * SC_WORKERS
SC_SCATTER_WINDOW = 128


def _pack_bf16_pairs(x):
    n = x.shape[1] // 2
    xb = x.astype(BF16).astype(F32)
    hi = lax.bitcast_convert_type(xb[:, :n], jnp.uint32)
    lo = lax.bitcast_convert_type(xb[:, n:], jnp.uint32)
    return lax.bitcast_convert_type(hi | (lo >> 16), jnp.int32)


def _unpack_bf16_pairs(w):
    u = lax.bitcast_convert_type(w, jnp.uint32)
    hi = lax.bitcast_convert_type(u & jnp.uint32(0xFFFF0000), F32)
    lo = lax.bitcast_convert_type(u << 16, F32)
    return hi, lo


def _route(lt):
    assert EXP_PER_GROUP == 8 and N_GROUPS <= 8
    tm = lt.shape[1]
    row = lax.broadcasted_iota(jnp.int32, (8, tm), 0)
    neg = -jnp.inf
    big = jnp.int32(1 << 20)
    gl = jnp.where(row < N_GROUPS, lt[N_EXPERTS:N_EXPERTS + 8, :], neg)
    gmax = jnp.max(gl, axis=0, keepdims=True)
    p_top = 1.0 / jnp.sum(jnp.exp(gl - gmax), axis=0, keepdims=True)
    g_idx = jnp.min(jnp.where(gl == gmax, row, big), axis=0, keepdims=True)
    el = lt[0:8, :]
    for g in range(1, N_GROUPS):
        el = jnp.where(g_idx == g, lt[8 * g:8 * g + 8, :], el)
    m1 = jnp.max(el, axis=0, keepdims=True)
    i1 = jnp.min(jnp.where(el == m1, row, big), axis=0, keepdims=True)
    el2 = jnp.where(row == i1, neg, el)
    m2 = jnp.max(el2, axis=0, keepdims=True)
    i2 = jnp.min(jnp.where(el2 == m2, row, big), axis=0, keepdims=True)
    r = jnp.exp(m2 - m1)
    base = g_idx * EXP_PER_GROUP
    return base + i1, base + i2, p_top / (1.0 + r), p_top * r / (1.0 + r)


def _route_body(h_ref, g_ref, whi_ref, wlo_ref, xpk_ref, meta_ref, meta_t_ref, cnt_ref, carry_ref):
    i = pl.program_id(0)

    @pl.when(i == 0)
    def _():
        carry_ref[...] = jnp.zeros_like(carry_ref)

    xn = _rms(h_ref[...], g_ref[...])
    xpk_ref[...] = _pack_bf16_pairs(xn)
    x_hi = xn.astype(BF16)
    x_lo = (xn - x_hi.astype(F32)).astype(BF16)
    logits = (jnp.dot(x_hi, whi_ref[...], preferred_element_type=F32)
              + jnp.dot(x_lo, whi_ref[...], preferred_element_type=F32)
              + jnp.dot(x_hi, wlo_ref[...], preferred_element_type=F32))
    e0, e1, w0, w1 = _route(logits.T)
    tm = logits.shape[0]
    row = lax.broadcasted_iota(jnp.int32, (LANES, tm), 0)
    onehot_t = jnp.where((row == e0) | (row == e1), 1.0, 0.0).astype(BF16)
    lane_tiles = tm // LANES
    sub = LANES * max(k for k in (1, 2, 3) if lane_tiles % k == 0)
    ri = lax.broadcasted_iota(jnp.int32, (sub, sub), 0)
    ci = lax.broadcasted_iota(jnp.int32, (sub, sub), 1)
    earlier = jnp.where(ci < ri, 1.0, 0.0).astype(BF16)
    ones = jnp.ones((8, sub), BF16)
    running = carry_ref[...]
    blocks = []
    for j in range(tm // sub):
        part = onehot_t[:, j * sub:(j + 1) * sub]
        blocks.append((running + _dot_nt(earlier, part)).T)
        running = running + _dot_nt(ones, part)[0:1, :]
    prefix_t = jnp.concatenate(blocks, axis=1)
    r0 = jnp.sum(jnp.where(row == e0, prefix_t, 0.0), axis=0, keepdims=True)
    r1 = jnp.sum(jnp.where(row == e1, prefix_t, 0.0), axis=0, keepdims=True)
    row8 = lax.broadcasted_iota(jnp.int32, (META_COLS, tm), 0)
    meta_t = jnp.zeros((META_COLS, tm), F32)
    for c, val in enumerate((e0.astype(F32), e1.astype(F32), r0, r1, w0, w1)):
        meta_t = jnp.where(row8 == c, val, meta_t)
    meta_t_ref[...] = meta_t
    padded = jnp.concatenate([meta_t, jnp.zeros((LANES - META_COLS, tm), F32)], axis=0)
    meta_ref[...] = padded.T[:, :META_COLS]
    carry_ref[...] = running
    cnt_ref[...] = running


def _moe_route(h2, gain, w_router):
    T, D = h2.shape
    tm = _pick_tile(T, 1152, LANES)
    return pl.pallas_call(
        _route_body,
        grid=(T // tm,),
        in_specs=[
            pl.BlockSpec((tm, D), lambda i: (i, 0)),
            pl.BlockSpec((1, D), lambda i: (0, 0)),
            pl.BlockSpec((D, LANES), lambda i: (0, 0)),
            pl.BlockSpec((D, LANES), lambda i: (0, 0)),
        ],
        out_specs=[
            pl.BlockSpec((tm, D // 2), lambda i: (i, 0)),
            pl.BlockSpec((tm, META_COLS), lambda i: (i, 0)),
            pl.BlockSpec((META_COLS, tm), lambda i: (0, i)),
            pl.BlockSpec((1, LANES), lambda i: (0, 0)),
        ],
        out_shape=[
            jax.ShapeDtypeStruct((T, D // 2), jnp.int32),
            jax.ShapeDtypeStruct((T, META_COLS), F32),
            jax.ShapeDtypeStruct((META_COLS, T), F32),
            jax.ShapeDtypeStruct((1, LANES), F32),
        ],
        scratch_shapes=[pltpu.VMEM((1, LANES), F32)],
        compiler_params=_params(("arbitrary",)),
        name="moe_route",
    )(h2, gain.reshape(1, D), *w_router)


def _sc_gather(table, idx):
    n = idx.shape[0]
    width = table.shape[1]
    win = SC_WINDOW
    assert n % SC_CHUNK == 0
    per_worker = n // SC_WORKERS
    mesh = plsc.VectorSubcoreMesh(core_axis_name="c", subcore_axis_name="s")

    @functools.partial(
        pl.kernel, out_type=jax.ShapeDtypeStruct((n, width), table.dtype), mesh=mesh,
        scratch_types=[pltpu.VMEM((per_worker,), jnp.int32), pltpu.VMEM((2, win, width), table.dtype),
                       pltpu.SemaphoreType.DMA((2,)), pltpu.SemaphoreType.DMA((2,))],
        name="sc_row_gather")
    def gather(table_hbm, idx_hbm, out_hbm, idx_v, rows_v, gsem, osem):
        worker = lax.axis_index("s") * mesh.num_cores + lax.axis_index("c")
        base = worker * per_worker
        pltpu.sync_copy(idx_hbm.at[pl.ds(base, per_worker)], idx_v)

        @pl.loop(0, per_worker, step=2 * win)
        def _(off):
            fetch = [pltpu.async_copy(table_hbm.at[idx_v.at[pl.ds(off + s * win, win)]], rows_v.at[s], gsem.at[s])
                     for s in range(2)]
            store = []
            for s in range(2):
                fetch[s].wait()
                store.append(pltpu.async_copy(rows_v.at[s], out_hbm.at[pl.ds(base + off + s * win, win)],
                                              osem.at[s]))
            for s in range(2):
                store[s].wait()

    return gather(table, idx)


def _sc_dispatch(table, dest, n_out):
    n_rows, width = table.shape
    win = SC_SCATTER_WINDOW
    assert n_rows % win == 0
    n_win = n_rows // win
    per_worker = pl.cdiv(n_win, SC_WORKERS)
    idx = jnp.pad(dest, ((0, 0), (0, per_worker * SC_WORKERS * win - n_rows)))
    idx = idx.reshape(2, per_worker, SC_WORKERS, win).transpose(0, 2, 1, 3)
    mesh = plsc.VectorSubcoreMesh(core_axis_name="c", subcore_axis_name="s")

    @functools.partial(
        pl.kernel, out_type=jax.ShapeDtypeStruct((n_out, width), table.dtype), mesh=mesh,
        scratch_types=[pltpu.VMEM((per_worker, win), jnp.int32), pltpu.VMEM((per_worker, win), jnp.int32),
                       pltpu.VMEM((win, width), table.dtype),
                       pltpu.SemaphoreType.DMA, pltpu.SemaphoreType.DMA],
        name="sc_row_dispatch")
    def dispatch(table_hbm, idx_hbm, out_hbm, idx0_v, idx1_v, rows_v, sem0, sem1):
        worker = lax.axis_index("s") * mesh.num_cores + lax.axis_index("c")
        pltpu.sync_copy(idx_hbm.at[0, worker], idx0_v)
        pltpu.sync_copy(idx_hbm.at[1, worker], idx1_v)

        @pl.loop(0, per_worker)
        def _(j):
            window = j * SC_WORKERS + worker

            @pl.when(window < n_win)
            def _():
                pltpu.sync_copy(table_hbm.at[pl.ds(window * win, win)], rows_v)
                first = pltpu.async_copy(rows_v, out_hbm.at[idx0_v.at[j]], sem0)
                second = pltpu.async_copy(rows_v, out_hbm.at[idx1_v.at[j]], sem1)
                first.wait()
                second.wait()

    return dispatch(table, idx)


def _expert_body(te_ref, tv_ref, x_ref, wg_ref, wu_ref, wd_ref, y_ref, wg_s, wu_s, wd_s):
    t = pl.program_id(0)
    half = x_ref.shape[1]

    @pl.when((t == 0) | (te_ref[t] != te_ref[jnp.maximum(t - 1, 0)]))
    def _():
        wg_s[...] = wg_ref[0].astype(BF16)
        wu_s[...] = wu_ref[0].astype(BF16)
        wd_s[...] = wd_ref[0].astype(BF16)

    @pl.when(tv_ref[t] > 0)
    def _():
        row = lax.broadcasted_iota(jnp.int32, x_ref.shape, 0)
        hi, lo = _unpack_bf16_pairs(jnp.where(row < tv_ref[t], x_ref[...], 0))
        hi, lo = hi.astype(BF16), lo.astype(BF16)
        hg = jnp.dot(hi, wg_s[:half, :], preferred_element_type=F32)
        hg += jnp.dot(lo, wg_s[half:, :], preferred_element_type=F32)
        hu = jnp.dot(hi, wu_s[:half, :], preferred_element_type=F32)
        hu += jnp.dot(lo, wu_s[half:, :], preferred_element_type=F32)
        act = (hg * jax.nn.sigmoid(hg) * hu).astype(BF16)
        y_ref[...] = _pack_bf16_pairs(jnp.dot(act, wd_s[...], preferred_element_type=F32))

    @pl.when(tv_ref[t] == 0)
    def _():
        y_ref[...] = jnp.zeros_like(y_ref)


def _moe_experts(xs, n_sorted, tile_expert, tile_valid, w_gate, w_up, w_down):
    NP, half = n_sorted, xs.shape[1]
    _, D, FF = w_gate.shape
    tr = EXPERT_TILE
    grid_spec = pltpu.PrefetchScalarGridSpec(
        num_scalar_prefetch=2,
        grid=(NP // tr,),
        in_specs=[
            pl.BlockSpec((tr, half), lambda t, te, tv: (t, 0)),
            pl.BlockSpec((1, D, FF), lambda t, te, tv: (te[t], 0, 0)),
            pl.BlockSpec((1, D, FF), lambda t, te, tv: (te[t], 0, 0)),
            pl.BlockSpec((1, FF, D), lambda t, te, tv: (te[t], 0, 0)),
        ],
        out_specs=pl.BlockSpec((tr, half), lambda t, te, tv: (t, 0)),
        scratch_shapes=[pltpu.VMEM((D, FF), BF16), pltpu.VMEM((D, FF), BF16), pltpu.VMEM((FF, D), BF16)],
    )
    return pl.pallas_call(
        _expert_body,
        grid_spec=grid_spec,
        out_shape=jax.ShapeDtypeStruct((NP, half), jnp.int32),
        compiler_params=_params(("arbitrary",)),
        name="moe_experts",
    )(tile_expert, tile_valid, xs, w_gate, w_up, w_down)


def _moe_combine_into(o_ref, h, meta, y0, y1):
    half = y0.shape[1]
    w0 = meta[:, 4:5]
    w1 = meta[:, 5:6]
    hi0, lo0 = _unpack_bf16_pairs(y0)
    hi1, lo1 = _unpack_bf16_pairs(y1)
    o_ref[:, :half] = h[:, :half] + w0 * hi0 + w1 * hi1
    o_ref[:, half:] = h[:, half:] + w0 * lo0 + w1 * lo1


def _moe(h2, gain, w_router, layer, w_gate, w_up, w_down):
    T, D = h2.shape
    E = N_EXPERTS
    tr = EXPERT_TILE
    n_sorted = _round_up(2 * T + E * (tr - 1), tr)
    t_pad = _round_up(T, SC_CHUNK // 2)

    xpk, meta, meta_t, counts = _moe_route(h2, gain, w_router)

    cnt = counts[0, :E].astype(jnp.int32)
    padded = (cnt + tr - 1) // tr * tr
    ends = jnp.cumsum(padded)
    starts = ends - padded
    eid = meta_t[0:2].astype(jnp.int32)
    pos = meta_t[2:4].astype(jnp.int32)
    for e in range(E):
        pos = pos + jnp.where(eid == e, starts[e], 0)
    tile_start = jnp.arange(n_sorted // tr, dtype=jnp.int32) * tr
    tile_expert = jnp.minimum(jnp.sum(tile_start[:, None] >= ends[None, :], axis=1), E - 1).astype(jnp.int32)
    tile_valid = jnp.clip(cnt[tile_expert] - (tile_start - starts[tile_expert]), 0, tr).astype(jnp.int32)

    xs = _sc_dispatch(xpk, pos, n_sorted)
    ys = _moe_experts(xs, n_sorted, layer * E + tile_expert, tile_valid, w_gate, w_up, w_down)
    spare = jnp.arange(t_pad - T, dtype=jnp.int32)
    back = jnp.concatenate([pos, jnp.broadcast_to(spare[None], (2, t_pad - T))], axis=1)
    yg = _sc_gather(ys, back.reshape(-1)).reshape(2, t_pad, D // 2)
    return meta, yg


def _final_body(h_ref, meta_ref, y0_ref, y1_ref, g_ref, o_ref, hn_ref):
    _moe_combine_into(hn_ref, h_ref[0], meta_ref[0], y0_ref[0], y1_ref[0])
    o_ref[0] = _rms(hn_ref[N_META:, :], g_ref[...])


def _final_norm(h3, meta3, yg, gain, b0, nb):
    _, L, D = h3.shape
    return pl.pallas_call(
        _final_body,
        grid=(nb,),
        in_specs=[pl.BlockSpec((1, L, D), lambda b: (b0 + b, 0, 0)),
                  pl.BlockSpec((1, L, meta3.shape[2]), lambda b: (b0 + b, 0, 0)),
                  pl.BlockSpec((1, L, D // 2), lambda b: (0, b0 + b, 0)),
                  pl.BlockSpec((1, L, D // 2), lambda b: (1, b0 + b, 0)),
                  pl.BlockSpec((1, D), lambda b: (0, 0))],
        out_specs=pl.BlockSpec((1, L - N_META, D), lambda b: (b, 0, 0)),
        out_shape=jax.ShapeDtypeStruct((nb, L - N_META, D), F32),
        scratch_shapes=[pltpu.VMEM((L, D), F32)],
        compiler_params=_params(("parallel",)),
        name="combine_final_norm",
    )(h3, meta3, yg, yg, gain.reshape(1, D))


def kernel(x_prompt, x_sample, meta_tokens, norm_mix, w_in, ret_decay_fwd, ret_decay_bwd, hy_short_w, hy_short_b, hy_filt_w1, hy_filt_b1, hy_filt_w2, hy_filt_b2, hy_filt_w3, hy_sin_freq, hy_skip, w_ret_o, w_hy_o, w_out, norm_ffn, router_group, router_expert, moe_w_gate, moe_w_up, moe_w_down, norm_final):
    assert x_prompt.shape[1:] == x_sample.shape[1:]
    nbp, nbs = x_prompt.shape[0], x_sample.shape[0]
    B = nbp + nbs
    D = x_prompt.shape[2]
    L = N_META + x_prompt.shape[1]
    T = B * L
    depth = w_in.shape[0]
    q_w = RET_HEADS * RET_DK
    v_w = RET_HEADS * RET_DV
    hy_w = hy_skip.shape[2]
    hy_col0 = 2 * q_w + 2 * v_w
    gate_col0 = hy_col0 + 3 * hy_w
    assert D == q_w and w_in.shape[2] == gate_col0 + 2 * D

    x = jnp.concatenate([x_prompt, x_sample], axis=0)
    meta = jnp.broadcast_to(meta_tokens[None].astype(x.dtype), (B, N_META, D))
    h = jnp.concatenate([meta, x], axis=1).reshape(T, D)

    cos_t, sin_t = _rotary_tables(L)
    dims = _HyenaDims(L)
    dft_fwd, dft_inv = _dft_tables(dims)
    filt_consts = _filter_constants(L, hy_w)
    router = jnp.concatenate([router_expert, router_group], axis=2).astype(F32)
    router = jnp.pad(router, ((0, 0), (0, 0), (0, LANES - router.shape[2])))
    router_hi = router.astype(BF16)
    router_lo = (router - router_hi.astype(F32)).astype(BF16)
    assert moe_w_gate.shape[1] == N_EXPERTS
    w_gate_all = moe_w_gate.reshape((-1,) + moe_w_gate.shape[2:])
    w_up_all = moe_w_up.reshape((-1,) + moe_w_up.shape[2:])
    w_down_all = moe_w_down.reshape((-1,) + moe_w_down.shape[2:])

    moe = None
    for i in range(depth):
        proj, h = _inproj(h, norm_mix[i], w_in[i].astype(BF16), moe)
        proj3 = proj.reshape(B, L, -1)

        lg = jnp.stack([jax.nn.log_sigmoid(ret_decay_fwd[i].astype(F32)),
                        jax.nn.log_sigmoid(ret_decay_bwd[i].astype(F32))])
        ret = _retention(proj3, lg, cos_t, sin_t)

        hs, hd = _hyena_filters(dims, filt_consts, hy_filt_w1[i], hy_filt_b1[i], hy_filt_w2[i], hy_filt_b2[i],
                                hy_filt_w3[i], hy_sin_freq[i])
        spectrum = _filter_spectrum(dims, dft_fwd, hs, hd)
        sw = hy_short_w[i].astype(F32)
        sb = hy_short_b[i].astype(F32).reshape(1, -1)
        z = _long_conv(dims, dft_fwd, dft_inv, proj3, hy_col0, proj3, hy_col0 + hy_w, sw, sb, 0, hy_w,
                       hy_skip[i, 0].astype(F32), spectrum, 0, hy_w)
        z = _long_conv(dims, dft_fwd, dft_inv, z, 0, proj3, hy_col0 + 2 * hy_w, sw, sb, None, 2 * hy_w,
                       hy_skip[i, 1].astype(F32), spectrum, hy_w, hy_w)

        h = _mix_out(ret.reshape(T, v_w), z.reshape(T, hy_w), proj, 2 * q_w + v_w, gate_col0, h,
                     w_ret_o[i].astype(BF16), w_hy_o[i].astype(BF16), w_out[i].astype(BF16))
        moe = _moe(h, norm_ffn[i], (router_hi[i], router_lo[i]), i, w_gate_all, w_up_all, w_down_all)

    h3 = h.reshape(B, L, D)
    meta3 = moe[0].reshape(B, L, -1)
    return (_final_norm(h3, meta3, moe[1], norm_final, 0, nbp),
            _final_norm(h3, meta3, moe[1], norm_final, nbp, nbs))
```

```python
import functools
import math

import jax
import jax.numpy as jnp
import numpy as np
from jax import lax
from jax.experimental import pallas as pl
from jax.experimental.pallas import tpu as pltpu
from jax.experimental.pallas import tpu_sc as plsc

N_META = 16
RET_HEADS = 8
RET_DK = 128
RET_DV = 256
ROPE_THETA = 10000.0
HY_ORDER = 2
HY_SHORT = 3
HY_EMB = 33
HY_BANDS = (HY_EMB - 1) // 2
HY_DECAY_TARGET = 1e-2
HY_MIN_DECAY = math.log(HY_DECAY_TARGET) / 1.5
HY_MAX_DECAY = math.log(HY_DECAY_TARGET) / 0.3
N_GROUPS = 4
EXP_PER_GROUP = 8
N_EXPERTS = N_GROUPS * EXP_PER_GROUP
RMS_EPS = 1e-6

LANES = 128
BF16_SUBLANES = 16
MXU_DIM = 256
RET_CHUNK = 256
VMEM_LIMIT = 56 * 1024 * 1024

F32 = jnp.float32
BF16 = jnp.bfloat16


def _round_up(n, m):
    return (n + m - 1) // m * m


def _pick_tile(n, target, mult):
    best = None
    for t in range(mult, min(n, target) + 1, mult):
        if n % t == 0:
            best = t
    assert best is not None, (n, target, mult)
    return best


def _params(sem):
    return pltpu.CompilerParams(dimension_semantics=sem, vmem_limit_bytes=VMEM_LIMIT)


def _rms(x, gain):
    ms = jnp.mean(x * x, axis=-1, keepdims=True)
    return x * lax.rsqrt(ms + RMS_EPS) * gain


def _inproj_body(h_ref, g_ref, w_ref, o_ref, xn_ref):
    @pl.when(pl.program_id(1) == 0)
    def _():
        xn_ref[...] = _rms(h_ref[...], g_ref[...]).astype(BF16)

    o_ref[...] = jnp.dot(xn_ref[...], w_ref[...], preferred_element_type=F32).astype(o_ref.dtype)


def _proj_body(xn_ref, w_ref, o_ref):
    o_ref[...] = jnp.dot(xn_ref[...], w_ref[...], preferred_element_type=F32).astype(o_ref.dtype)


def _combine_norm_body(h_ref, meta_ref, y0_ref, y1_ref, g_ref, hn_ref, xn_ref):
    _moe_combine_into(hn_ref, h_ref[...], meta_ref[...], y0_ref[0], y1_ref[0])
    xn_ref[...] = _rms(hn_ref[...], g_ref[...]).astype(BF16)


def _inproj(h2, gain, w, moe=None):
    T, D = h2.shape
    nc = w.shape[1]
    tm = _pick_tile(T, 1152, BF16_SUBLANES)
    tn = _pick_tile(nc, 2816, MXU_DIM)
    grid = (T // tm, nc // tn)
    h_spec = pl.BlockSpec((tm, D), lambda i, j: (i, 0))
    w_spec = pl.BlockSpec((D, tn), lambda i, j: (0, j))
    o_spec = pl.BlockSpec((tm, tn), lambda i, j: (i, j))
    o_shape = jax.ShapeDtypeStruct((T, nc), BF16)
    if moe is None:
        proj = pl.pallas_call(
            _inproj_body, grid=grid,
            in_specs=[h_spec, pl.BlockSpec((1, D), lambda i, j: (0, 0)), w_spec],
            out_specs=o_spec, out_shape=o_shape,
            scratch_shapes=[pltpu.VMEM((tm, D), BF16)],
            compiler_params=_params(("parallel", "arbitrary")),
            name="norm_inproj")(h2, gain.reshape(1, D), w)
        return proj, h2
    meta, yg = moe
    row = pl.BlockSpec((tm, D), lambda i: (i, 0))
    hn, xn = pl.pallas_call(
        _combine_norm_body, grid=(T // tm,),
        in_specs=[row, pl.BlockSpec((tm, meta.shape[1]), lambda i: (i, 0)),
                  pl.BlockSpec((1, tm, D // 2), lambda i: (0, i, 0)),
                  pl.BlockSpec((1, tm, D // 2), lambda i: (1, i, 0)),
                  pl.BlockSpec((1, D), lambda i: (0, 0))],
        out_specs=[row, row],
        out_shape=[jax.ShapeDtypeStruct((T, D), F32), jax.ShapeDtypeStruct((T, D), BF16)],
        compiler_params=_params(("parallel",)),
        name="combine_norm")(h2, meta, yg, yg, gain.reshape(1, D))
    proj = pl.pallas_call(
        _proj_body, grid=grid, in_specs=[h_spec, w_spec], out_specs=o_spec, out_shape=o_shape,
        compiler_params=_params(("parallel", "parallel")),
        name="inproj")(xn, w)
    return proj, hn


def _dot_t(a, b):
    return lax.dot_general(a, b, (((0,), (0,)), ((), ())), preferred_element_type=F32)


def _dot_nt(a, b):
    return lax.dot_general(a, b, (((1,), (1,)), ((), ())), preferred_element_type=F32)


def _ret_body(lg_ref, q_ref, k_ref, v_ref, cos_ref, sin_ref, o_ref, rb_ref, kr_ref, *, seq_len):
    C = RET_CHUNK
    L = seq_len
    n_chunks = pl.cdiv(L, C)
    head = pl.program_id(1)
    lgf = lg_ref[0, head]
    lgb = lg_ref[1, head]

    def chunk(ref, n):
        lo, hi = n * C, min((n + 1) * C, L)
        x = ref[0, lo:hi, :]
        if hi - lo < C:
            x = jnp.concatenate([x, jnp.zeros((C - (hi - lo), x.shape[1]), x.dtype)], axis=0)
        return x

    def rotary(ref, n):
        x = chunk(ref, n).astype(F32)
        sl = slice(n * C, (n + 1) * C)
        return x * cos_ref[sl, :] + pltpu.roll(x, RET_DK // 2, 1) * sin_ref[sl, :]

    row = lax.broadcasted_iota(jnp.int32, (C, LANES), 0).astype(F32)
    ri = lax.broadcasted_iota(jnp.int32, (C, C), 0).astype(F32)
    ci = lax.broadcasted_iota(jnp.int32, (C, C), 1).astype(F32)
    decay = jnp.exp(jnp.where(ci <= ri, (ri - ci) * lgf, (ci - ri) * lgb))
    qf_dec = jnp.exp((row + 1.0) * lgf)
    qb_dec = jnp.exp((C - row) * lgb)
    kf_dec = jnp.exp((C - 1.0 - row) * lgf)
    kb_dec = jnp.exp(row * lgb)
    cf = jnp.exp(C * lgf)
    cb = jnp.exp(C * lgb)

    state = jnp.zeros((RET_DK, RET_DV), F32)
    for n in reversed(range(n_chunks)):
        rb_ref[n] = state.astype(BF16)
        if n > 0:
            kr_ref[n] = rotary(k_ref, n)
            kb = (kr_ref[n] * kb_dec).astype(BF16)
            state = cb * state + _dot_t(kb, chunk(v_ref, n))

    state = jnp.zeros((RET_DK, RET_DV), F32)
    for n in range(n_chunks):
        q = rotary(q_ref, n) * (RET_DK ** -0.5)
        k = kr_ref[n] if n > 0 else rotary(k_ref, n)
        v = chunk(v_ref, n)
        scores = _dot_nt(q.astype(BF16), k.astype(BF16)) * decay
        o = jnp.dot(scores.astype(BF16), v, preferred_element_type=F32)
        o += jnp.dot((q * qf_dec).astype(BF16), state.astype(BF16), preferred_element_type=F32)
        o += jnp.dot((q * qb_dec).astype(BF16), rb_ref[n], preferred_element_type=F32)
        if n + 1 < n_chunks:
            state = cf * state + _dot_t((k * kf_dec).astype(BF16), v)
        lo, hi = n * C, min((n + 1) * C, L)
        o_ref[0, lo:hi, :] = o[: hi - lo].astype(o_ref.dtype)


def _retention(proj3, lg, cos_t, sin_t):
    B, L, _ = proj3.shape
    n_chunks = pl.cdiv(L, RET_CHUNK)
    lp = n_chunks * RET_CHUNK
    k_blk = RET_HEADS
    v_blk = 2 * RET_HEADS * RET_DK // RET_DV
    return pl.pallas_call(
        functools.partial(_ret_body, seq_len=L),
        grid=(B, RET_HEADS),
        in_specs=[
            pl.BlockSpec(memory_space=pltpu.SMEM),
            pl.BlockSpec((1, L, RET_DK), lambda b, h: (b, 0, h)),
            pl.BlockSpec((1, L, RET_DK), lambda b, h: (b, 0, k_blk + h)),
            pl.BlockSpec((1, L, RET_DV), lambda b, h: (b, 0, v_blk + h)),
            pl.BlockSpec((lp, RET_DK), lambda b, h: (0, 0)),
            pl.BlockSpec((lp, RET_DK), lambda b, h: (0, 0)),
        ],
        out_specs=pl.BlockSpec((1, L, RET_DV), lambda b, h: (b, 0, h)),
        out_shape=jax.ShapeDtypeStruct((B, L, RET_HEADS * RET_DV), BF16),
        scratch_shapes=[pltpu.VMEM((n_chunks, RET_DK, RET_DV), BF16),
                        pltpu.VMEM((n_chunks, RET_CHUNK, RET_DK), F32)],
        compiler_params=_params(("parallel", "arbitrary")),
        name="retention",
    )(lg, proj3, proj3, proj3, cos_t, sin_t)


def _rotary_tables(L):
    half = RET_DK // 2
    lp = _round_up(L, RET_CHUNK)
    inv = ROPE_THETA ** (-jnp.arange(half, dtype=F32) / half)
    ang = jnp.arange(lp, dtype=F32)[:, None] * inv[None, :]
    cos, sin = jnp.cos(ang), jnp.sin(ang)
    return jnp.concatenate([cos, cos], axis=1), jnp.concatenate([-sin, sin], axis=1)


def _row_tiles(n, tile):
    return [(s, min(tile, n - s)) for s in range(0, n, tile)]


RADIX = 4


class _HyenaDims:
    def __init__(self, L):
        assert L % RADIX == 0
        self.L = L
        self.Q = L // RADIX
        self.F = self.Q + 1
        self.QP = _round_up(self.Q, BF16_SUBLANES)
        self.QK = _round_up(self.Q, MXU_DIM)
        self.FM = _round_up(self.F, BF16_SUBLANES)
        self.IK = _round_up(2 * self.FM, MXU_DIM)


def _dft_tables(dims):
    n = 2 * dims.L

    def trig(f, t, valid):
        ang = ((f * t) % n) * (2.0 * math.pi / n)
        return (np.where(valid, np.cos(ang), 0.0).astype(BF16), np.where(valid, np.sin(ang), 0.0).astype(BF16))

    f = np.arange(dims.FM, dtype=np.int64)[:, None]
    s = np.arange(dims.QK, dtype=np.int64)[None, :]
    ok = (f < dims.F) & (s < dims.Q)
    fwd = [trig(f, RADIX * s + r, ok) for r in range(RADIX)]

    s = np.arange(dims.QP, dtype=np.int64)[:, None]
    j = np.arange(dims.IK, dtype=np.int64)[None, :]
    f = j % dims.FM
    ok = (f < dims.F) & (s < dims.Q)
    inv = []
    for r in range(RADIX):
        cos, sin = trig(f, RADIX * s + r, ok)
        inv.append(np.where(j < dims.FM, cos, np.where(j < 2 * dims.FM, sin, np.zeros_like(cos))))
    return [t for pair in fwd for t in pair], inv


def _forward_dft(tables, parts, rows, dtype=F32):
    p = [jnp.dot(tables[2 * r][rows, :], parts[r][...], preferred_element_type=F32).astype(dtype)
         for r in range(RADIX)]
    a = [jnp.dot(tables[2 * r + 1][rows, :], parts[r][...], preferred_element_type=F32).astype(dtype)
         for r in range(RADIX)]
    ps02, pd02, ps13, pd13 = p[0] + p[2], p[0] - p[2], p[1] + p[3], p[1] - p[3]
    as02, ad02, as13, ad13 = a[0] + a[2], a[0] - a[2], a[1] + a[3], a[1] - a[3]
    return [(ps02 + ps13, as02 + as13),
            (pd02 + ad13, pd13 - ad02),
            (pd02 - ad13, pd13 + ad02),
            (ps02 - ps13, as13 - as02)]


def _class_weights(dims):
    f = lax.broadcasted_iota(jnp.int32, (dims.FM, 1), 0)
    n = 2.0 * dims.L
    edge = jnp.where(f == 0, 1.0, 2.0) / n
    return [jnp.where(f <= dims.Q, edge, 0.0),
            jnp.where(f < dims.Q, 2.0 / n, 0.0),
            jnp.where((f >= 1) & (f <= dims.Q), 2.0 / n, 0.0),
            jnp.where(f < dims.Q, edge, 0.0)]


def _spec_body(*refs, dims):
    tables = refs[:2 * RADIX]
    hs_ref, hd_ref = refs[2 * RADIX:2 * RADIX + 2]
    outs = refs[2 * RADIX + 2:4 * RADIX + 2]
    s_pads = refs[4 * RADIX + 2:5 * RADIX + 2]
    d_pads = refs[5 * RADIX + 2:]
    Q = dims.Q
    for pads, src in ((s_pads, hs_ref), (d_pads, hd_ref)):
        for r in range(RADIX):
            pads[r][Q:, :] = jnp.zeros((pads[r].shape[0] - Q, pads[r].shape[1]), BF16)
            pads[r][:Q, :] = src[r].astype(BF16)
    rows = slice(None)
    re = [c[0] for c in _forward_dft(tables, s_pads, rows)]
    nim = [c[1] for c in _forward_dft(tables, d_pads, rows)]
    for c, w in enumerate(_class_weights(dims)):
        outs[2 * c][...] = (w * re[c]).astype(outs[2 * c].dtype)
        outs[2 * c + 1][...] = (-w * nim[c]).astype(outs[2 * c + 1].dtype)


def _filter_spectrum(dims, fwd, hs, hd):
    width = hs.shape[2]
    cw = MXU_DIM
    resident = pl.BlockSpec((dims.FM, dims.QK), lambda j: (0, 0), pipeline_mode=pl.Buffered(1))
    return pl.pallas_call(
        functools.partial(_spec_body, dims=dims),
        grid=(width // cw,),
        in_specs=[resident] * (2 * RADIX) + [pl.BlockSpec((RADIX, dims.Q, cw), lambda j: (0, 0, j))] * 2,
        out_specs=[pl.BlockSpec((dims.FM, cw), lambda j: (0, j))] * (2 * RADIX),
        out_shape=[jax.ShapeDtypeStruct((dims.FM, width), BF16)] * (2 * RADIX),
        scratch_shapes=[pltpu.VMEM((dims.QK, cw), BF16)] * (2 * RADIX),
        compiler_params=_params(("arbitrary",)),
        name="hyena_filter_spectrum",
    )(*fwd, hs, hd)


STAGE_HEAD = 8


def _conv_body(*refs, dims, first):
    n_tab = 2 * RADIX
    tables = refs[:n_tab]
    inv_refs = refs[n_tab:n_tab + RADIX]
    u_ref, x_ref = refs[n_tab + RADIX:n_tab + RADIX + 2]
    at = n_tab + RADIX + 2
    n_taps = 4 if first else 2
    taps = refs[at:at + n_taps]
    at += n_taps
    skip_ref = refs[at]
    spec_refs = refs[at + 1:at + 1 + 2 * RADIX]
    o_ref = refs[at + 1 + 2 * RADIX]
    scratch = refs[at + 2 + 2 * RADIX:]
    u_store, scratch = scratch[:RADIX], scratch[RADIX:]
    x_parts, g_parts = scratch[:RADIX], scratch[RADIX:2 * RADIX]
    nat_ref, natx_ref = scratch[2 * RADIX:]
    L, Q, QP, QK, FM, IK = dims.L, dims.Q, dims.QP, dims.QK, dims.FM, dims.IK
    cw = x_parts[0].shape[1]

    lane_slabs = [(k, slice(k * LANES, (k + 1) * LANES)) for k in range(cw // LANES)]

    def short_conv_split(parts, ref, w_ref, b_ref, stage_ref, anchor=None):
        tail = stage_ref.shape[1] - STAGE_HEAD - L
        for k, lanes in lane_slabs:
            raw = ref[0, :, lanes].astype(F32)
            if anchor is not None:
                raw = raw + anchor[:, lanes]
            stage_ref[k, :STAGE_HEAD, :] = jnp.zeros((STAGE_HEAD, LANES), F32)
            stage_ref[k, STAGE_HEAD:STAGE_HEAD + L, :] = raw
            stage_ref[k, STAGE_HEAD + L:, :] = jnp.zeros((tail, LANES), F32)
            w = w_ref[:, lanes]
            bias = b_ref[:, lanes]
            for r in range(RADIX):
                taps3 = [stage_ref[k, pl.ds(STAGE_HEAD + r + d, Q, stride=RADIX), :] for d in (-1, 0, 1)]
                val = bias + w[0:1] * taps3[0] + w[1:2] * taps3[1] + w[2:3] * taps3[2]
                parts[r][:Q, lanes] = val.astype(BF16)
                if QP > Q:
                    parts[r][Q:QP, lanes] = jnp.zeros((QP - Q, LANES), BF16)

    def zero_tail(parts):
        if QK > QP:
            for r in range(RADIX):
                parts[r][QP:, :] = jnp.zeros((QK - QP, cw), BF16)

    def run(u_parts):
        for r in range(RADIX):
            if IK > 2 * FM:
                g_parts[r][2 * FM:, :] = jnp.zeros((IK - 2 * FM, cw), BF16)
        for lo, sz in _row_tiles(FM, MXU_DIM):
            rows = slice(lo, lo + sz)
            e1, e2 = [], []
            for c, (p, a) in enumerate(_forward_dft(tables, u_parts, rows, BF16)):
                k1, k2 = spec_refs[2 * c][rows, :], spec_refs[2 * c + 1][rows, :]
                e1.append(p * k1 + a * k2)
                e2.append(a * k1 - p * k2)
            for r, (gc, gs) in enumerate(_fold_classes(e1, e2)):
                g_parts[r][lo:lo + sz, :] = gc
                g_parts[r][FM + lo:FM + lo + sz, :] = gs

        anchor = g_parts[RADIX - 1][2 * FM - 1:2 * FM, :].astype(F32) * 0.0
        short_conv_split(x_parts, x_ref, taps[-2], taps[-1], natx_ref, anchor)
        skip = skip_ref[...]
        for lo, sz in _row_tiles(QP, MXU_DIM):
            rows = slice(lo, lo + sz)
            valid = min(sz, Q - lo)
            for r in range(RADIX):
                y = jnp.dot(inv_refs[r][rows, :], g_parts[r][...], preferred_element_type=F32)
                o = x_parts[r][rows, :].astype(F32) * (y + skip * u_parts[r][rows, :].astype(F32))
                if first:
                    o_ref[0, r, rows, :] = o.astype(o_ref.dtype)
                elif valid > 0:
                    for k, lanes in lane_slabs:
                        nat_ref[k, pl.ds(RADIX * lo + r, valid, stride=RADIX), :] = o[:valid, lanes]
        if not first:
            for k, lanes in lane_slabs:
                o_ref[0, :, lanes] = nat_ref[k, :L, :]

    zero_tail(u_store)
    if first:
        short_conv_split(u_store, u_ref, taps[0], taps[1], nat_ref)
    else:
        for r in range(RADIX):
            u_store[r][:QP, :] = u_ref[0, r]
    run(u_store)


def _fold_classes(e1, e2):
    a_p, a_m = e1[0] + e1[3], e1[0] - e1[3]
    b_p, b_m = e1[1] + e1[2], e1[1] - e1[2]
    c_p, c_m = e2[0] + e2[3], e2[0] - e2[3]
    d_p, d_m = e2[1] + e2[2], e2[2] - e2[1]
    return [(a_p + b_p, c_m + d_m),
            (a_m + d_p, c_p + b_m),
            (a_p - b_p, c_m - d_m),
            (a_m - d_p, c_p - b_m)]


def _long_conv(dims, fwd, inv, u_arr, u_col0, x_arr, x_col0, short_w, short_b, short_u_col0, short_x_col0,
               skip, spectrum, k_col0, width):
    B = u_arr.shape[0]
    L = dims.L
    cw = MXU_DIM
    ub, xb, kb = u_col0 // cw, x_col0 // cw, k_col0 // cw
    first = short_u_col0 is not None
    fwd_spec = pl.BlockSpec((dims.FM, dims.QK), lambda c, b: (0, 0), pipeline_mode=pl.Buffered(1))
    inv_spec = pl.BlockSpec((dims.QP, dims.IK), lambda c, b: (0, 0), pipeline_mode=pl.Buffered(1))

    def taps(col0):
        blk = col0 // cw
        return [pl.BlockSpec((HY_SHORT, cw), lambda c, b: (0, blk + c)),
                pl.BlockSpec((1, cw), lambda c, b: (0, blk + c))]

    if first:
        u_spec = pl.BlockSpec((1, L, cw), lambda c, b: (b, 0, ub + c))
        out_spec = pl.BlockSpec((1, RADIX, dims.QP, cw), lambda c, b: (b, 0, 0, c))
        out_shape = jax.ShapeDtypeStruct((B, RADIX, dims.QP, width), BF16)
    else:
        u_spec = pl.BlockSpec((1, RADIX, dims.QP, cw), lambda c, b: (b, 0, 0, ub + c))
        out_spec = pl.BlockSpec((1, L, cw), lambda c, b: (b, 0, c))
        out_shape = jax.ShapeDtypeStruct((B, L, width), F32)
    in_specs = ([fwd_spec] * (2 * RADIX) + [inv_spec] * RADIX
                + [u_spec, pl.BlockSpec((1, L, cw), lambda c, b: (b, 0, xb + c))])
    args = list(fwd) + list(inv) + [u_arr, x_arr]
    for col0 in ([short_u_col0] if first else []) + [short_x_col0]:
        in_specs += taps(col0)
        args += [short_w, short_b]
    in_specs += [pl.BlockSpec((1, cw), lambda c, b: (0, c))]
    in_specs += [pl.BlockSpec((dims.FM, cw), lambda c, b: (0, kb + c))] * (2 * RADIX)
    args += [skip.reshape(1, width)] + list(spectrum)
    stage = pltpu.VMEM((cw // LANES, RADIX * dims.QP + 2 * STAGE_HEAD, LANES), F32)
    return pl.pallas_call(
        functools.partial(_conv_body, dims=dims, first=first),
        grid=(width // cw, B),
        in_specs=in_specs,
        out_specs=out_spec,
        out_shape=out_shape,
        scratch_shapes=([pltpu.VMEM((dims.QK, cw), BF16)] * RADIX
                        + [pltpu.VMEM((dims.QP, cw), BF16)] * RADIX
                        + [pltpu.VMEM((dims.IK, cw), BF16)] * RADIX + [stage, stage]),
        compiler_params=_params(("parallel", "arbitrary")),
        name="hyena_long_conv",
    )(*args)


def _filter_body(z_ref, w1_ref, b1_ref, w2_ref, b2_ref, fr_ref, w3f_ref, w3b_ref, dec_ref, hs_ref, hd_ref,
                 h2_ref, nat_ref, *, dims):
    hp = lax.Precision.HIGHEST

    @pl.when(pl.program_id(0) == 0)
    def _():
        fr = fr_ref[...]
        h1 = jnp.sin(fr * (jnp.dot(z_ref[...], w1_ref[...], precision=hp, preferred_element_type=F32)
                           + b1_ref[...]))
        h2_ref[...] = jnp.sin(fr * (jnp.dot(h1, w2_ref[...], precision=hp, preferred_element_type=F32)
                                    + b2_ref[...]))

    h2 = h2_ref[...]
    dec = dec_ref[...]
    hf = jnp.dot(h2, w3f_ref[...], precision=hp, preferred_element_type=F32) * dec
    hb = jnp.dot(h2, w3b_ref[...], precision=hp, preferred_element_type=F32) * dec
    hb = jnp.where(lax.broadcasted_iota(jnp.int32, hb.shape, 0) == 0, 0.0, hb)
    scale = lax.rsqrt(jnp.sum(hf * hf + hb * hb, axis=0, keepdims=True) + 1e-6)
    for ref, val in ((hs_ref, (hf + hb) * scale), (hd_ref, (hf - hb) * scale)):
        for k in range(val.shape[1] // LANES):
            lanes = slice(k * LANES, (k + 1) * LANES)
            nat_ref[k] = val[:, lanes]
            for r in range(RADIX):
                ref[r, :, lanes] = nat_ref[k, pl.ds(r, dims.Q, stride=RADIX), :]


def _filter_constants(L, width):
    t = np.linspace(0.0, 1.0, L)
    w = (2.0 * math.pi / L) * np.arange(L)
    bands = np.linspace(1e-4, HY_BANDS - 1, HY_BANDS)
    fw = w[:, None] * bands[None, :]
    z = np.concatenate([t[:, None], np.cos(fw), -np.sin(fw)], axis=-1)
    z = np.pad(z, ((0, 0), (0, LANES - z.shape[1])))
    deltas = np.abs(np.linspace(HY_MIN_DECAY, HY_MAX_DECAY, width))
    return z.astype(np.float32), np.exp(-t[:, None] * deltas[None, :]).astype(np.float32)


def _hyena_filters(dims, consts, w1, b1, w2, b2, w3, freq):
    L = dims.L
    z, dec = consts
    hidden = w2.shape[0]
    width = w3.shape[1] // (2 * HY_ORDER)
    cw = MXU_DIM
    per_order = width // cw

    def lanes128(a):
        return jnp.pad(a.astype(F32), [(0, 0)] * (a.ndim - 1) + [(0, LANES - a.shape[-1])])

    w1p = jnp.pad(lanes128(w1), ((0, LANES - w1.shape[0]), (0, 0)))
    w2p = jnp.pad(lanes128(w2), ((0, LANES - hidden), (0, 0)))
    w3p = jnp.pad(w3.astype(F32), ((0, LANES - hidden), (0, 0)))
    row = lambda a: lanes128(a).reshape(1, LANES)
    full = lambda shape: pl.BlockSpec(shape, lambda j: (0, 0))
    fwd_col = lambda j: (0, (j // per_order) * 2 * per_order + j % per_order)
    bwd_col = lambda j: (0, (j // per_order) * 2 * per_order + per_order + j % per_order)
    out_spec = pl.BlockSpec((RADIX, dims.Q, cw), lambda j: (0, 0, j))
    return pl.pallas_call(
        functools.partial(_filter_body, dims=dims),
        grid=(HY_ORDER * per_order,),
        in_specs=[full((L, LANES)), full((LANES, LANES)), full((1, LANES)), full((LANES, LANES)),
                  full((1, LANES)), full((1, LANES)),
                  pl.BlockSpec((LANES, cw), fwd_col), pl.BlockSpec((LANES, cw), bwd_col),
                  pl.BlockSpec((L, cw), lambda j: (0, j % per_order))],
        out_specs=[out_spec, out_spec],
        out_shape=[jax.ShapeDtypeStruct((RADIX, dims.Q, HY_ORDER * width), F32)] * 2,
        scratch_shapes=[pltpu.VMEM((L, LANES), F32), pltpu.VMEM((cw // LANES, L, LANES), F32)],
        compiler_params=_params(("arbitrary",)),
        name="hyena_filters",
    )(z, w1p, row(b1), w2p, row(b2), row(freq), w3p, w3p, dec)


def _mix_body(ret_ref, g_ref, hy_ref, gr_ref, gh_ref, h_ref, wr_ref, wh_ref, wo_ref, o_ref):
    ret = None
    for hd in range(RET_HEADS):
        cols = slice(hd * RET_DV, (hd + 1) * RET_DV)
        o = ret_ref[:, cols].astype(F32)
        o = o * lax.rsqrt(jnp.mean(o * o, axis=-1, keepdims=True) + RMS_EPS)
        g = g_ref[:, cols].astype(F32)
        part = jnp.dot((g * jax.nn.sigmoid(g) * o).astype(BF16), wr_ref[cols, :], preferred_element_type=F32)
        ret = part if ret is None else ret + part
    hyo = jnp.dot(hy_ref[...].astype(BF16), wh_ref[...], preferred_element_type=F32)
    merged = (jax.nn.sigmoid(gr_ref[...].astype(F32)) * ret
              + jax.nn.sigmoid(gh_ref[...].astype(F32)) * hyo)
    o_ref[...] = h_ref[...] + jnp.dot(merged.astype(BF16), wo_ref[...], preferred_element_type=F32)


def _mix_out(ret2, hy2, proj2, g_col0, gate_col0, h2, w_ret_o, w_hy_o, w_out):
    T, D = h2.shape
    tm = _pick_tile(T, 576, BF16_SUBLANES)
    gb = gate_col0 // D
    v_w = ret2.shape[1]
    assert g_col0 % v_w == 0
    row = lambda w: pl.BlockSpec((tm, w), lambda i: (i, 0))
    full = lambda a: pl.BlockSpec(a.shape, lambda i: (0, 0))
    return pl.pallas_call(
        _mix_body,
        grid=(T // tm,),
        in_specs=[
            row(v_w), pl.BlockSpec((tm, v_w), lambda i: (i, g_col0 // v_w)), row(hy2.shape[1]),
            pl.BlockSpec((tm, D), lambda i: (i, gb)),
            pl.BlockSpec((tm, D), lambda i: (i, gb + 1)),
            row(D), full(w_ret_o), full(w_hy_o), full(w_out),
        ],
        out_specs=row(D),
        out_shape=jax.ShapeDtypeStruct((T, D), F32),
        compiler_params=_params(("parallel",)),
        name="merge_outproj",
    )(ret2, proj2, hy2, proj2, proj2, h2, w_ret_o, w_hy_o, w_out)


EXPERT_TILE = 1024
META_COLS = 8
SC_WINDOW = 64
SC_WORKERS = 32
SC_CHUNK = 2 * SC_WINDOW * SC_WORKERS
SC_SCATTER_WINDOW = 128


def _pack_bf16_pairs(x):
    n = x.shape[1] // 2
    xb = x.astype(BF16).astype(F32)
    hi = lax.bitcast_convert_type(xb[:, :n], jnp.uint32)
    lo = lax.bitcast_convert_type(xb[:, n:], jnp.uint32)
    return lax.bitcast_convert_type(hi | (lo >> 16), jnp.int32)


def _unpack_bf16_pairs(w):
    u = lax.bitcast_convert_type(w, jnp.uint32)
    hi = lax.bitcast_convert_type(u & jnp.uint32(0xFFFF0000), F32)
    lo = lax.bitcast_convert_type(u << 16, F32)
    return hi, lo


def _route(lt):
    assert EXP_PER_GROUP == 8 and N_GROUPS <= 8
    tm = lt.shape[1]
    row = lax.broadcasted_iota(jnp.int32, (8, tm), 0)
    neg = -jnp.inf
    big = jnp.int32(1 << 20)
    gl = jnp.where(row < N_GROUPS, lt[N_EXPERTS:N_EXPERTS + 8, :], neg)
    gmax = jnp.max(gl, axis=0, keepdims=True)
    p_top = 1.0 / jnp.sum(jnp.exp(gl - gmax), axis=0, keepdims=True)
    g_idx = jnp.min(jnp.where(gl == gmax, row, big), axis=0, keepdims=True)
    el = lt[0:8, :]
    for g in range(1, N_GROUPS):
        el = jnp.where(g_idx == g, lt[8 * g:8 * g + 8, :], el)
    m1 = jnp.max(el, axis=0, keepdims=True)
    i1 = jnp.min(jnp.where(el == m1, row, big), axis=0, keepdims=True)
    el2 = jnp.where(row == i1, neg, el)
    m2 = jnp.max(el2, axis=0, keepdims=True)
    i2 = jnp.min(jnp.where(el2 == m2, row, big), axis=0, keepdims=True)
    r = jnp.exp(m2 - m1)
    base = g_idx * EXP_PER_GROUP
    return base + i1, base + i2, p_top / (1.0 + r), p_top * r / (1.0 + r)


def _route_body(h_ref, g_ref, whi_ref, wlo_ref, xpk_ref, meta_ref, meta_t_ref, cnt_ref, carry_ref):
    i = pl.program_id(0)

    @pl.when(i == 0)
    def _():
        carry_ref[...] = jnp.zeros_like(carry_ref)

    xn = _rms(h_ref[...], g_ref[...])
    xpk_ref[...] = _pack_bf16_pairs(xn)
    x_hi = xn.astype(BF16)
    x_lo = (xn - x_hi.astype(F32)).astype(BF16)
    logits = (jnp.dot(x_hi, whi_ref[...], preferred_element_type=F32)
              + jnp.dot(x_lo, whi_ref[...], preferred_element_type=F32)
              + jnp.dot(x_hi, wlo_ref[...], preferred_element_type=F32))
    e0, e1, w0, w1 = _route(logits.T)
    tm = logits.shape[0]
    row = lax.broadcasted_iota(jnp.int32, (LANES, tm), 0)
    onehot_t = jnp.where((row == e0) | (row == e1), 1.0, 0.0).astype(BF16)
    lane_tiles = tm // LANES
    sub = LANES * max(k for k in (1, 2, 3) if lane_tiles % k == 0)
    ri = lax.broadcasted_iota(jnp.int32, (sub, sub), 0)
    ci = lax.broadcasted_iota(jnp.int32, (sub, sub), 1)
    earlier = jnp.where(ci < ri, 1.0, 0.0).astype(BF16)
    ones = jnp.ones((8, sub), BF16)
    running = carry_ref[...]
    blocks = []
    for j in range(tm // sub):
        part = onehot_t[:, j * sub:(j + 1) * sub]
        blocks.append((running + _dot_nt(earlier, part)).T)
        running = running + _dot_nt(ones, part)[0:1, :]
    prefix_t = jnp.concatenate(blocks, axis=1)
    r0 = jnp.sum(jnp.where(row == e0, prefix_t, 0.0), axis=0, keepdims=True)
    r1 = jnp.sum(jnp.where(row == e1, prefix_t, 0.0), axis=0, keepdims=True)
    row8 = lax.broadcasted_iota(jnp.int32, (META_COLS, tm), 0)
    meta_t = jnp.zeros((META_COLS, tm), F32)
    for c, val in enumerate((e0.astype(F32), e1.astype(F32), r0, r1, w0, w1)):
        meta_t = jnp.where(row8 == c, val, meta_t)
    meta_t_ref[...] = meta_t
    padded = jnp.concatenate([meta_t, jnp.zeros((LANES - META_COLS, tm), F32)], axis=0)
    meta_ref[...] = padded.T[:, :META_COLS]
    carry_ref[...] = running
    cnt_ref[...] = running


def _moe_route(h2, gain, w_router):
    T, D = h2.shape
    tm = _pick_tile(T, 1152, LANES)
    return pl.pallas_call(
        _route_body,
        grid=(T // tm,),
        in_specs=[
            pl.BlockSpec((tm, D), lambda i: (i, 0)),
            pl.BlockSpec((1, D), lambda i: (0, 0)),
            pl.BlockSpec((D, LANES), lambda i: (0, 0)),
            pl.BlockSpec((D, LANES), lambda i: (0, 0)),
        ],
        out_specs=[
            pl.BlockSpec((tm, D // 2), lambda i: (i, 0)),
            pl.BlockSpec((tm, META_COLS), lambda i: (i, 0)),
            pl.BlockSpec((META_COLS, tm), lambda i: (0, i)),
            pl.BlockSpec((1, LANES), lambda i: (0, 0)),
        ],
        out_shape=[
            jax.ShapeDtypeStruct((T, D // 2), jnp.int32),
            jax.ShapeDtypeStruct((T, META_COLS), F32),
            jax.ShapeDtypeStruct((META_COLS, T), F32),
            jax.ShapeDtypeStruct((1, LANES), F32),
        ],
        scratch_shapes=[pltpu.VMEM((1, LANES), F32)],
        compiler_params=_params(("arbitrary",)),
        name="moe_route",
    )(h2, gain.reshape(1, D), *w_router)


def _sc_gather(table, idx):
    n = idx.shape[0]
    width = table.shape[1]
    win = SC_WINDOW
    assert n % SC_CHUNK == 0
    per_worker = n // SC_WORKERS
    mesh = plsc.VectorSubcoreMesh(core_axis_name="c", subcore_axis_name="s")

    @functools.partial(
        pl.kernel, out_type=jax.ShapeDtypeStruct((n, width), table.dtype), mesh=mesh,
        scratch_types=[pltpu.VMEM((per_worker,), jnp.int32), pltpu.VMEM((2, win, width), table.dtype),
                       pltpu.SemaphoreType.DMA((2,)), pltpu.SemaphoreType.DMA((2,))],
        name="sc_row_gather")
    def gather(table_hbm, idx_hbm, out_hbm, idx_v, rows_v, gsem, osem):
        worker = lax.axis_index("s") * mesh.num_cores + lax.axis_index("c")
        base = worker * per_worker
        pltpu.sync_copy(idx_hbm.at[pl.ds(base, per_worker)], idx_v)

        @pl.loop(0, per_worker, step=2 * win)
        def _(off):
            fetch = [pltpu.async_copy(table_hbm.at[idx_v.at[pl.ds(off + s * win, win)]], rows_v.at[s], gsem.at[s])
                     for s in range(2)]
            store = []
            for s in range(2):
                fetch[s].wait()
                store.append(pltpu.async_copy(rows_v.at[s], out_hbm.at[pl.ds(base + off + s * win, win)],
                                              osem.at[s]))
            for s in range(2):
                store[s].wait()

    return gather(table, idx)


def _sc_dispatch(table, dest, n_out):
    n_rows, width = table.shape
    win = SC_SCATTER_WINDOW
    assert n_rows % win == 0
    n_win = n_rows // win
    per_worker = pl.cdiv(n_win, SC_WORKERS)
    idx = jnp.pad(dest, ((0, 0), (0, per_worker * SC_WORKERS * win - n_rows)))
    idx = idx.reshape(2, per_worker, SC_WORKERS, win).transpose(0, 2, 1, 3)
    mesh = plsc.VectorSubcoreMesh(core_axis_name="c", subcore_axis_name="s")

    @functools.partial(
        pl.kernel, out_type=jax.ShapeDtypeStruct((n_out, width), table.dtype), mesh=mesh,
        scratch_types=[pltpu.VMEM((per_worker, win), jnp.int32), pltpu.VMEM((per_worker, win), jnp.int32),
                       pltpu.VMEM((win, width), table.dtype),
                       pltpu.SemaphoreType.DMA, pltpu.SemaphoreType.DMA],
        name="sc_row_dispatch")
    def dispatch(table_hbm, idx_hbm, out_hbm, idx0_v, idx1_v, rows_v, sem0, sem1):
        worker = lax.axis_index("s") * mesh.num_cores + lax.axis_index("c")
        pltpu.sync_copy(idx_hbm.at[0, worker], idx0_v)
        pltpu.sync_copy(idx_hbm.at[1, worker], idx1_v)

        @pl.loop(0, per_worker)
        def _(j):
            window = j * SC_WORKERS + worker

            @pl.when(window < n_win)
            def _():
                pltpu.sync_copy(table_hbm.at[pl.ds(window * win, win)], rows_v)
                first = pltpu.async_copy(rows_v, out_hbm.at[idx0_v.at[j]], sem0)
                second = pltpu.async_copy(rows_v, out_hbm.at[idx1_v.at[j]], sem1)
                first.wait()
                second.wait()

    return dispatch(table, idx)


def _expert_body(te_ref, tv_ref, x_ref, wg_ref, wu_ref, wd_ref, y_ref, wg_s, wu_s, wd_s):
    t = pl.program_id(0)
    half = x_ref.shape[1]

    @pl.when((t == 0) | (te_ref[t] != te_ref[jnp.maximum(t - 1, 0)]))
    def _():
        wg_s[...] = wg_ref[0].astype(BF16)
        wu_s[...] = wu_ref[0].astype(BF16)
        wd_s[...] = wd_ref[0].astype(BF16)

    @pl.when(tv_ref[t] > 0)
    def _():
        row = lax.broadcasted_iota(jnp.int32, x_ref.shape, 0)
        hi, lo = _unpack_bf16_pairs(jnp.where(row < tv_ref[t], x_ref[...], 0))
        hi, lo = hi.astype(BF16), lo.astype(BF16)
        hg = jnp.dot(hi, wg_s[:half, :], preferred_element_type=F32)
        hg += jnp.dot(lo, wg_s[half:, :], preferred_element_type=F32)
        hu = jnp.dot(hi, wu_s[:half, :], preferred_element_type=F32)
        hu += jnp.dot(lo, wu_s[half:, :], preferred_element_type=F32)
        act = (hg * jax.nn.sigmoid(hg) * hu).astype(BF16)
        y_ref[...] = _pack_bf16_pairs(jnp.dot(act, wd_s[...], preferred_element_type=F32))

    @pl.when(tv_ref[t] == 0)
    def _():
        y_ref[...] = jnp.zeros_like(y_ref)


def _moe_experts(xs, n_sorted, tile_expert, tile_valid, w_gate, w_up, w_down):
    NP, half = n_sorted, xs.shape[1]
    _, D, FF = w_gate.shape
    tr = EXPERT_TILE
    grid_spec = pltpu.PrefetchScalarGridSpec(
        num_scalar_prefetch=2,
        grid=(NP // tr,),
        in_specs=[
            pl.BlockSpec((tr, half), lambda t, te, tv: (t, 0)),
            pl.BlockSpec((1, D, FF), lambda t, te, tv: (te[t], 0, 0)),
            pl.BlockSpec((1, D, FF), lambda t, te, tv: (te[t], 0, 0)),
            pl.BlockSpec((1, FF, D), lambda t, te, tv: (te[t], 0, 0)),
        ],
        out_specs=pl.BlockSpec((tr, half), lambda t, te, tv: (t, 0)),
        scratch_shapes=[pltpu.VMEM((D, FF), BF16), pltpu.VMEM((D, FF), BF16), pltpu.VMEM((FF, D), BF16)],
    )
    return pl.pallas_call(
        _expert_body,
        grid_spec=grid_spec,
        out_shape=jax.ShapeDtypeStruct((NP, half), jnp.int32),
        compiler_params=_params(("arbitrary",)),
        name="moe_experts",
    )(tile_expert, tile_valid, xs, w_gate, w_up, w_down)


def _moe_combine_into(o_ref, h, meta, y0, y1):
    half = y0.shape[1]
    w0 = meta[:, 4:5]
    w1 = meta[:, 5:6]
    hi0, lo0 = _unpack_bf16_pairs(y0)
    hi1, lo1 = _unpack_bf16_pairs(y1)
    o_ref[:, :half] = h[:, :half] + w0 * hi0 + w1 * hi1
    o_ref[:, half:] = h[:, half:] + w0 * lo0 + w1 * lo1


def _moe(h2, gain, w_router, layer, w_gate, w_up, w_down):
    T, D = h2.shape
    E = N_EXPERTS
    tr = EXPERT_TILE
    n_sorted = _round_up(2 * T + E * (tr - 1), tr)
    t_pad = _round_up(T, SC_CHUNK // 2)

    xpk, meta, meta_t, counts = _moe_route(h2, gain, w_router)

    cnt = counts[0, :E].astype(jnp.int32)
    padded = (cnt + tr - 1) // tr * tr
    ends = jnp.cumsum(padded)
    starts = ends - padded
    eid = meta_t[0:2].astype(jnp.int32)
    pos = meta_t[2:4].astype(jnp.int32)
    for e in range(E):
        pos = pos + jnp.where(eid == e, starts[e], 0)
    tile_start = jnp.arange(n_sorted // tr, dtype=jnp.int32) * tr
    tile_expert = jnp.minimum(jnp.sum(tile_start[:, None] >= ends[None, :], axis=1), E - 1).astype(jnp.int32)
    tile_valid = jnp.clip(cnt[tile_expert] - (tile_start - starts[tile_expert]), 0, tr).astype(jnp.int32)

    xs = _sc_dispatch(xpk, pos, n_sorted)
    ys = _moe_experts(xs, n_sorted, layer * E + tile_expert, tile_valid, w_gate, w_up, w_down)
    spare = jnp.arange(t_pad - T, dtype=jnp.int32)
    back = jnp.concatenate([pos, jnp.broadcast_to(spare[None], (2, t_pad - T))], axis=1)
    yg = _sc_gather(ys, back.reshape(-1)).reshape(2, t_pad, D // 2)
    return meta, yg


def _final_body(h_ref, meta_ref, y0_ref, y1_ref, g_ref, o_ref, hn_ref):
    _moe_combine_into(hn_ref, h_ref[0], meta_ref[0], y0_ref[0], y1_ref[0])
    o_ref[0] = _rms(hn_ref[N_META:, :], g_ref[...])


def _final_norm(h3, meta3, yg, gain, b0, nb):
    _, L, D = h3.shape
    return pl.pallas_call(
        _final_body,
        grid=(nb,),
        in_specs=[pl.BlockSpec((1, L, D), lambda b: (b0 + b, 0, 0)),
                  pl.BlockSpec((1, L, meta3.shape[2]), lambda b: (b0 + b, 0, 0)),
                  pl.BlockSpec((1, L, D // 2), lambda b: (0, b0 + b, 0)),
                  pl.BlockSpec((1, L, D // 2), lambda b: (1, b0 + b, 0)),
                  pl.BlockSpec((1, D), lambda b: (0, 0))],
        out_specs=pl.BlockSpec((1, L - N_META, D), lambda b: (b, 0, 0)),
        out_shape=jax.ShapeDtypeStruct((nb, L - N_META, D), F32),
        scratch_shapes=[pltpu.VMEM((L, D), F32)],
        compiler_params=_params(("parallel",)),
        name="combine_final_norm",
    )(h3, meta3, yg, yg, gain.reshape(1, D))


def kernel(x_prompt, x_sample, meta_tokens, norm_mix, w_in, ret_decay_fwd, ret_decay_bwd, hy_short_w, hy_short_b, hy_filt_w1, hy_filt_b1, hy_filt_w2, hy_filt_b2, hy_filt_w3, hy_sin_freq, hy_skip, w_ret_o, w_hy_o, w_out, norm_ffn, router_group, router_expert, moe_w_gate, moe_w_up, moe_w_down, norm_final):
    assert x_prompt.shape[1:] == x_sample.shape[1:]
    nbp, nbs = x_prompt.shape[0], x_sample.shape[0]
    B = nbp + nbs
    D = x_prompt.shape[2]
    L = N_META + x_prompt.shape[1]
    T = B * L
    depth = w_in.shape[0]
    q_w = RET_HEADS * RET_DK
    v_w = RET_HEADS * RET_DV
    hy_w = hy_skip.shape[2]
    hy_col0 = 2 * q_w + 2 * v_w
    gate_col0 = hy_col0 + 3 * hy_w
    assert D == q_w and w_in.shape[2] == gate_col0 + 2 * D

    x = jnp.concatenate([x_prompt, x_sample], axis=0)
    meta = jnp.broadcast_to(meta_tokens[None].astype(x.dtype), (B, N_META, D))
    h = jnp.concatenate([meta, x], axis=1).reshape(T, D)

    cos_t, sin_t = _rotary_tables(L)
    dims = _HyenaDims(L)
    dft_fwd, dft_inv = _dft_tables(dims)
    filt_consts = _filter_constants(L, hy_w)
    router = jnp.concatenate([router_expert, router_group], axis=2).astype(F32)
    router = jnp.pad(router, ((0, 0), (0, 0), (0, LANES - router.shape[2])))
    router_hi = router.astype(BF16)
    router_lo = (router - router_hi.astype(F32)).astype(BF16)
    assert moe_w_gate.shape[1] == N_EXPERTS
    w_gate_all = moe_w_gate.reshape((-1,) + moe_w_gate.shape[2:])
    w_up_all = moe_w_up.reshape((-1,) + moe_w_up.shape[2:])
    w_down_all = moe_w_down.reshape((-1,) + moe_w_down.shape[2:])

    moe = None
    for i in range(depth):
        proj, h = _inproj(h, norm_mix[i], w_in[i].astype(BF16), moe)
        proj3 = proj.reshape(B, L, -1)

        lg = jnp.stack([jax.nn.log_sigmoid(ret_decay_fwd[i].astype(F32)),
                        jax.nn.log_sigmoid(ret_decay_bwd[i].astype(F32))])
        ret = _retention(proj3, lg, cos_t, sin_t)

        hs, hd = _hyena_filters(dims, filt_consts, hy_filt_w1[i], hy_filt_b1[i], hy_filt_w2[i], hy_filt_b2[i],
                                hy_filt_w3[i], hy_sin_freq[i])
        spectrum = _filter_spectrum(dims, dft_fwd, hs, hd)
        sw = hy_short_w[i].astype(F32)
        sb = hy_short_b[i].astype(F32).reshape(1, -1)
        z = _long_conv(dims, dft_fwd, dft_inv, proj3, hy_col0, proj3, hy_col0 + hy_w, sw, sb, 0, hy_w,
                       hy_skip[i, 0].astype(F32), spectrum, 0, hy_w)
        z = _long_conv(dims, dft_fwd, dft_inv, z, 0, proj3, hy_col0 + 2 * hy_w, sw, sb, None, 2 * hy_w,
                       hy_skip[i, 1].astype(F32), spectrum, hy_w, hy_w)

        h = _mix_out(ret.reshape(T, v_w), z.reshape(T, hy_w), proj, 2 * q_w + v_w, gate_col0, h,
                     w_ret_o[i].astype(BF16), w_hy_o[i].astype(BF16), w_out[i].astype(BF16))
        moe = _moe(h, norm_ffn[i], (router_hi[i], router_lo[i]), i, w_gate_all, w_up_all, w_down_all)

    h3 = h.reshape(B, L, D)
    meta3 = moe[0].reshape(B, L, -1)
    return (_final_norm(h3, meta3, moe[1], norm_final, 0, nbp),
            _final_norm(h3, meta3, moe[1], norm_final, nbp, nbs))
```

```python
import functools
import math

import jax
import jax.numpy as jnp
import numpy as np
from jax import lax
from jax.experimental import pallas as pl
from jax.experimental.pallas import tpu as pltpu
from jax.experimental.pallas import tpu_sc as plsc

N_META = 16
RET_HEADS = 8
RET_DK = 128
RET_DV = 256
ROPE_THETA = 10000.0
HY_ORDER = 2
HY_SHORT = 3
HY_EMB = 33
HY_BANDS = (HY_EMB - 1) // 2
HY_DECAY_TARGET = 1e-2
HY_MIN_DECAY = math.log(HY_DECAY_TARGET) / 1.5
HY_MAX_DECAY = math.log(HY_DECAY_TARGET) / 0.3
N_GROUPS = 4
EXP_PER_GROUP = 8
N_EXPERTS = N_GROUPS * EXP_PER_GROUP
RMS_EPS = 1e-6

LANES = 128
BF16_SUBLANES = 16
MXU_DIM = 256
RET_CHUNK = 256
VMEM_LIMIT = 56 * 1024 * 1024

F32 = jnp.float32
BF16 = jnp.bfloat16


def _round_up(n, m):
    return (n + m - 1) // m * m


def _pick_tile(n, target, mult):
    best = None
    for t in range(mult, min(n, target) + 1, mult):
        if n % t == 0:
            best = t
    assert best is not None, (n, target, mult)
    return best


def _params(sem):
    return pltpu.CompilerParams(dimension_semantics=sem, vmem_limit_bytes=VMEM_LIMIT)


def _rms(x, gain):
    ms = jnp.mean(x * x, axis=-1, keepdims=True)
    return x * lax.rsqrt(ms + RMS_EPS) * gain


def _inproj_body(h_ref, g_ref, w_ref, o_ref, xn_ref):
    @pl.when(pl.program_id(1) == 0)
    def _():
        xn_ref[...] = _rms(h_ref[...], g_ref[...]).astype(BF16)

    o_ref[...] = jnp.dot(xn_ref[...], w_ref[...], preferred_element_type=F32).astype(o_ref.dtype)


def _proj_body(xn_ref, w_ref, o_ref):
    o_ref[...] = jnp.dot(xn_ref[...], w_ref[...], preferred_element_type=F32).astype(o_ref.dtype)


def _combine_norm_body(h_ref, meta_ref, y0_ref, y1_ref, g_ref, hn_ref, xn_ref):
    _moe_combine_into(hn_ref, h_ref[...], meta_ref[...], y0_ref[0], y1_ref[0])
    xn_ref[...] = _rms(hn_ref[...], g_ref[...]).astype(BF16)


def _inproj(h2, gain, w, moe=None):
    T, D = h2.shape
    nc = w.shape[1]
    tm = _pick_tile(T, 1408, BF16_SUBLANES)
    tn = _pick_tile(nc, 2816, MXU_DIM)
    grid = (T // tm, nc // tn)
    h_spec = pl.BlockSpec((tm, D), lambda i, j: (i, 0))
    w_spec = pl.BlockSpec((D, tn), lambda i, j: (0, j))
    o_spec = pl.BlockSpec((tm, tn), lambda i, j: (i, j))
    o_shape = jax.ShapeDtypeStruct((T, nc), BF16)
    if moe is None:
        proj = pl.pallas_call(
            _inproj_body, grid=grid,
            in_specs=[h_spec, pl.BlockSpec((1, D), lambda i, j: (0, 0)), w_spec],
            out_specs=o_spec, out_shape=o_shape,
            scratch_shapes=[pltpu.VMEM((tm, D), BF16)],
            compiler_params=_params(("parallel", "arbitrary")),
            name="norm_inproj")(h2, gain.reshape(1, D), w)
        return proj, h2
    meta, yg = moe
    row = pl.BlockSpec((tm, D), lambda i: (i, 0))
    hn, xn = pl.pallas_call(
        _combine_norm_body, grid=(T // tm,),
        in_specs=[row, pl.BlockSpec((tm, meta.shape[1]), lambda i: (i, 0)),
                  pl.BlockSpec((1, tm, D // 2), lambda i: (0, i, 0)),
                  pl.BlockSpec((1, tm, D // 2), lambda i: (1, i, 0)),
                  pl.BlockSpec((1, D), lambda i: (0, 0))],
        out_specs=[row, row],
        out_shape=[jax.ShapeDtypeStruct((T, D), F32), jax.ShapeDtypeStruct((T, D), BF16)],
        compiler_params=_params(("parallel",)),
        name="combine_norm")(h2, meta, yg, yg, gain.reshape(1, D))
    proj = pl.pallas_call(
        _proj_body, grid=grid, in_specs=[h_spec, w_spec], out_specs=o_spec, out_shape=o_shape,
        compiler_params=_params(("parallel", "parallel")),
        name="inproj")(xn, w)
    return proj, hn


def _dot_t(a, b):
    return lax.dot_general(a, b, (((0,), (0,)), ((), ())), preferred_element_type=F32)


def _dot_nt(a, b):
    return lax.dot_general(a, b, (((1,), (1,)), ((), ())), preferred_element_type=F32)


def _ret_body(lg_ref, q_ref, k_ref, v_ref, cos_ref, sin_ref, o_ref, rb_ref, kr_ref, *, seq_len):
    C = RET_CHUNK
    L = seq_len
    n_chunks = pl.cdiv(L, C)
    head = pl.program_id(1)
    lgf = lg_ref[0, head]
    lgb = lg_ref[1, head]

    def chunk(ref, n):
        lo, hi = n * C, min((n + 1) * C, L)
        x = ref[0, lo:hi, :]
        if hi - lo < C:
            x = jnp.concatenate([x, jnp.zeros((C - (hi - lo), x.shape[1]), x.dtype)], axis=0)
        return x

    def rotary(ref, n):
        x = chunk(ref, n).astype(F32)
        sl = slice(n * C, (n + 1) * C)
        return x * cos_ref[sl, :] + pltpu.roll(x, RET_DK // 2, 1) * sin_ref[sl, :]

    row = lax.broadcasted_iota(jnp.int32, (C, LANES), 0).astype(F32)
    ri = lax.broadcasted_iota(jnp.int32, (C, C), 0).astype(F32)
    ci = lax.broadcasted_iota(jnp.int32, (C, C), 1).astype(F32)
    decay = jnp.exp(jnp.where(ci <= ri, (ri - ci) * lgf, (ci - ri) * lgb))
    qf_dec = jnp.exp((row + 1.0) * lgf)
    qb_dec = jnp.exp((C - row) * lgb)
    kf_dec = jnp.exp((C - 1.0 - row) * lgf)
    kb_dec = jnp.exp(row * lgb)
    cf = jnp.exp(C * lgf)
    cb = jnp.exp(C * lgb)

    state = jnp.zeros((RET_DK, RET_DV), F32)
    for n in reversed(range(n_chunks)):
        rb_ref[n] = state.astype(BF16)
        if n > 0:
            kr_ref[n] = rotary(k_ref, n)
            kb = (kr_ref[n] * kb_dec).astype(BF16)
            state = cb * state + _dot_t(kb, chunk(v_ref, n))

    state = jnp.zeros((RET_DK, RET_DV), F32)
    for n in range(n_chunks):
        q = rotary(q_ref, n) * (RET_DK ** -0.5)
        k = kr_ref[n] if n > 0 else rotary(k_ref, n)
        v = chunk(v_ref, n)
        scores = _dot_nt(q.astype(BF16), k.astype(BF16)) * decay
        o = jnp.dot(scores.astype(BF16), v, preferred_element_type=F32)
        o += jnp.dot((q * qf_dec).astype(BF16), state.astype(BF16), preferred_element_type=F32)
        o += jnp.dot((q * qb_dec).astype(BF16), rb_ref[n], preferred_element_type=F32)
        if n + 1 < n_chunks:
            state = cf * state + _dot_t((k * kf_dec).astype(BF16), v)
        lo, hi = n * C, min((n + 1) * C, L)
        o_ref[0, lo:hi, :] = o[: hi - lo].astype(o_ref.dtype)


def _retention(proj3, lg, cos_t, sin_t):
    B, L, _ = proj3.shape
    n_chunks = pl.cdiv(L, RET_CHUNK)
    lp = n_chunks * RET_CHUNK
    k_blk = RET_HEADS
    v_blk = 2 * RET_HEADS * RET_DK // RET_DV
    return pl.pallas_call(
        functools.partial(_ret_body, seq_len=L),
        grid=(B, RET_HEADS),
        in_specs=[
            pl.BlockSpec(memory_space=pltpu.SMEM),
            pl.BlockSpec((1, L, RET_DK), lambda b, h: (b, 0, h)),
            pl.BlockSpec((1, L, RET_DK), lambda b, h: (b, 0, k_blk + h)),
            pl.BlockSpec((1, L, RET_DV), lambda b, h: (b, 0, v_blk + h)),
            pl.BlockSpec((lp, RET_DK), lambda b, h: (0, 0)),
            pl.BlockSpec((lp, RET_DK), lambda b, h: (0, 0)),
        ],
        out_specs=pl.BlockSpec((1, L, RET_DV), lambda b, h: (b, 0, h)),
        out_shape=jax.ShapeDtypeStruct((B, L, RET_HEADS * RET_DV), BF16),
        scratch_shapes=[pltpu.VMEM((n_chunks, RET_DK, RET_DV), BF16),
                        pltpu.VMEM((n_chunks, RET_CHUNK, RET_DK), F32)],
        compiler_params=_params(("parallel", "arbitrary")),
        name="retention",
    )(lg, proj3, proj3, proj3, cos_t, sin_t)


def _rotary_tables(L):
    half = RET_DK // 2
    lp = _round_up(L, RET_CHUNK)
    inv = ROPE_THETA ** (-jnp.arange(half, dtype=F32) / half)
    ang = jnp.arange(lp, dtype=F32)[:, None] * inv[None, :]
    cos, sin = jnp.cos(ang), jnp.sin(ang)
    return jnp.concatenate([cos, cos], axis=1), jnp.concatenate([-sin, sin], axis=1)


def _row_tiles(n, tile):
    return [(s, min(tile, n - s)) for s in range(0, n, tile)]


RADIX = 4


class _HyenaDims:
    def __init__(self, L):
        assert L % RADIX == 0
        self.L = L
        self.Q = L // RADIX
        self.F = self.Q + 1
        self.QP = _round_up(self.Q, BF16_SUBLANES)
        self.QK = _round_up(self.Q, MXU_DIM)
        self.FM = _round_up(self.F, BF16_SUBLANES)
        self.IK = _round_up(2 * self.FM, MXU_DIM)


def _dft_tables(dims):
    n = 2 * dims.L

    def trig(f, t, valid):
        ang = ((f * t) % n) * (2.0 * math.pi / n)
        return (np.where(valid, np.cos(ang), 0.0).astype(BF16), np.where(valid, np.sin(ang), 0.0).astype(BF16))

    f = np.arange(dims.FM, dtype=np.int64)[:, None]
    s = np.arange(dims.QK, dtype=np.int64)[None, :]
    ok = (f < dims.F) & (s < dims.Q)
    fwd = [trig(f, RADIX * s + r, ok) for r in range(RADIX)]

    s = np.arange(dims.QP, dtype=np.int64)[:, None]
    j = np.arange(dims.IK, dtype=np.int64)[None, :]
    f = j % dims.FM
    ok = (f < dims.F) & (s < dims.Q)
    inv = []
    for r in range(RADIX):
        cos, sin = trig(f, RADIX * s + r, ok)
        inv.append(np.where(j < dims.FM, cos, np.where(j < 2 * dims.FM, sin, np.zeros_like(cos))))
    return [t for pair in fwd for t in pair], inv


def _forward_dft(tables, parts, rows, dtype=F32):
    p = [jnp.dot(tables[2 * r][rows, :], parts[r][...], preferred_element_type=F32).astype(dtype)
         for r in range(RADIX)]
    a = [jnp.dot(tables[2 * r + 1][rows, :], parts[r][...], preferred_element_type=F32).astype(dtype)
         for r in range(RADIX)]
    ps02, pd02, ps13, pd13 = p[0] + p[2], p[0] - p[2], p[1] + p[3], p[1] - p[3]
    as02, ad02, as13, ad13 = a[0] + a[2], a[0] - a[2], a[1] + a[3], a[1] - a[3]
    return [(ps02 + ps13, as02 + as13),
            (pd02 + ad13, pd13 - ad02),
            (pd02 - ad13, pd13 + ad02),
            (ps02 - ps13, as13 - as02)]


def _class_weights(dims):
    f = lax.broadcasted_iota(jnp.int32, (dims.FM, 1), 0)
    n = 2.0 * dims.L
    edge = jnp.where(f == 0, 1.0, 2.0) / n
    return [jnp.where(f <= dims.Q, edge, 0.0),
            jnp.where(f < dims.Q, 2.0 / n, 0.0),
            jnp.where((f >= 1) & (f <= dims.Q), 2.0 / n, 0.0),
            jnp.where(f < dims.Q, edge, 0.0)]


def _spec_body(*refs, dims):
    tables = refs[:2 * RADIX]
    hs_ref, hd_ref = refs[2 * RADIX:2 * RADIX + 2]
    outs = refs[2 * RADIX + 2:4 * RADIX + 2]
    s_pads = refs[4 * RADIX + 2:5 * RADIX + 2]
    d_pads = refs[5 * RADIX + 2:]
    Q = dims.Q
    for pads, src in ((s_pads, hs_ref), (d_pads, hd_ref)):
        for r in range(RADIX):
            pads[r][Q:, :] = jnp.zeros((pads[r].shape[0] - Q, pads[r].shape[1]), BF16)
            pads[r][:Q, :] = src[r].astype(BF16)
    rows = slice(None)
    re = [c[0] for c in _forward_dft(tables, s_pads, rows)]
    nim = [c[1] for c in _forward_dft(tables, d_pads, rows)]
    for c, w in enumerate(_class_weights(dims)):
        outs[2 * c][...] = (w * re[c]).astype(outs[2 * c].dtype)
        outs[2 * c + 1][...] = (-w * nim[c]).astype(outs[2 * c + 1].dtype)


def _filter_spectrum(dims, fwd, hs, hd):
    width = hs.shape[2]
    cw = MXU_DIM
    resident = pl.BlockSpec((dims.FM, dims.QK), lambda j: (0, 0), pipeline_mode=pl.Buffered(1))
    return pl.pallas_call(
        functools.partial(_spec_body, dims=dims),
        grid=(width // cw,),
        in_specs=[resident] * (2 * RADIX) + [pl.BlockSpec((RADIX, dims.Q, cw), lambda j: (0, 0, j))] * 2,
        out_specs=[pl.BlockSpec((dims.FM, cw), lambda j: (0, j))] * (2 * RADIX),
        out_shape=[jax.ShapeDtypeStruct((dims.FM, width), BF16)] * (2 * RADIX),
        scratch_shapes=[pltpu.VMEM((dims.QK, cw), BF16)] * (2 * RADIX),
        compiler_params=_params(("arbitrary",)),
        name="hyena_filter_spectrum",
    )(*fwd, hs, hd)


STAGE_HEAD = 8


def _conv_body(*refs, dims, first):
    n_tab = 2 * RADIX
    tables = refs[:n_tab]
    inv_refs = refs[n_tab:n_tab + RADIX]
    u_ref, x_ref = refs[n_tab + RADIX:n_tab + RADIX + 2]
    at = n_tab + RADIX + 2
    n_taps = 4 if first else 2
    taps = refs[at:at + n_taps]
    at += n_taps
    skip_ref = refs[at]
    spec_refs = refs[at + 1:at + 1 + 2 * RADIX]
    o_ref = refs[at + 1 + 2 * RADIX]
    scratch = refs[at + 2 + 2 * RADIX:]
    u_store, scratch = scratch[:RADIX], scratch[RADIX:]
    x_parts, g_parts = scratch[:RADIX], scratch[RADIX:2 * RADIX]
    nat_ref, natx_ref = scratch[2 * RADIX:]
    L, Q, QP, QK, FM, IK = dims.L, dims.Q, dims.QP, dims.QK, dims.FM, dims.IK
    cw = x_parts[0].shape[1]

    lane_slabs = [(k, slice(k * LANES, (k + 1) * LANES)) for k in range(cw // LANES)]

    def short_conv_split(parts, ref, w_ref, b_ref, stage_ref, anchor=None):
        tail = stage_ref.shape[1] - STAGE_HEAD - L
        for k, lanes in lane_slabs:
            raw = ref[0, :, lanes].astype(F32)
            if anchor is not None:
                raw = raw + anchor[:, lanes]
            stage_ref[k, :STAGE_HEAD, :] = jnp.zeros((STAGE_HEAD, LANES), F32)
            stage_ref[k, STAGE_HEAD:STAGE_HEAD + L, :] = raw
            stage_ref[k, STAGE_HEAD + L:, :] = jnp.zeros((tail, LANES), F32)
            w = w_ref[:, lanes]
            bias = b_ref[:, lanes]
            for r in range(RADIX):
                taps3 = [stage_ref[k, pl.ds(STAGE_HEAD + r + d, Q, stride=RADIX), :] for d in (-1, 0, 1)]
                val = bias + w[0:1] * taps3[0] + w[1:2] * taps3[1] + w[2:3] * taps3[2]
                parts[r][:Q, lanes] = val.astype(BF16)
                if QP > Q:
                    parts[r][Q:QP, lanes] = jnp.zeros((QP - Q, LANES), BF16)

    def zero_tail(parts):
        if QK > QP:
            for r in range(RADIX):
                parts[r][QP:, :] = jnp.zeros((QK - QP, cw), BF16)

    def run(u_parts):
        for r in range(RADIX):
            if IK > 2 * FM:
                g_parts[r][2 * FM:, :] = jnp.zeros((IK - 2 * FM, cw), BF16)
        for lo, sz in _row_tiles(FM, MXU_DIM):
            rows = slice(lo, lo + sz)
            e1, e2 = [], []
            for c, (p, a) in enumerate(_forward_dft(tables, u_parts, rows, BF16)):
                k1, k2 = spec_refs[2 * c][rows, :], spec_refs[2 * c + 1][rows, :]
                e1.append(p * k1 + a * k2)
                e2.append(a * k1 - p * k2)
            for r, (gc, gs) in enumerate(_fold_classes(e1, e2)):
                g_parts[r][lo:lo + sz, :] = gc
                g_parts[r][FM + lo:FM + lo + sz, :] = gs

        anchor = g_parts[RADIX - 1][2 * FM - 1:2 * FM, :].astype(F32) * 0.0
        short_conv_split(x_parts, x_ref, taps[-2], taps[-1], natx_ref, anchor)
        skip = skip_ref[...]
        for lo, sz in _row_tiles(QP, MXU_DIM):
            rows = slice(lo, lo + sz)
            valid = min(sz, Q - lo)
            for r in range(RADIX):
                y = jnp.dot(inv_refs[r][rows, :], g_parts[r][...], preferred_element_type=F32)
                o = x_parts[r][rows, :].astype(F32) * (y + skip * u_parts[r][rows, :].astype(F32))
                if first:
                    o_ref[0, r, rows, :] = o.astype(o_ref.dtype)
                elif valid > 0:
                    for k, lanes in lane_slabs:
                        nat_ref[k, pl.ds(RADIX * lo + r, valid, stride=RADIX), :] = o[:valid, lanes]
        if not first:
            for k, lanes in lane_slabs:
                o_ref[0, :, lanes] = nat_ref[k, :L, :]

    zero_tail(u_store)
    if first:
        short_conv_split(u_store, u_ref, taps[0], taps[1], nat_ref)
    else:
        for r in range(RADIX):
            u_store[r][:QP, :] = u_ref[0, r]
    run(u_store)


def _fold_classes(e1, e2):
    a_p, a_m = e1[0] + e1[3], e1[0] - e1[3]
    b_p, b_m = e1[1] + e1[2], e1[1] - e1[2]
    c_p, c_m = e2[0] + e2[3], e2[0] - e2[3]
    d_p, d_m = e2[1] + e2[2], e2[2] - e2[1]
    return [(a_p + b_p, c_m + d_m),
            (a_m + d_p, c_p + b_m),
            (a_p - b_p, c_m - d_m),
            (a_m - d_p, c_p - b_m)]


def _long_conv(dims, fwd, inv, u_arr, u_col0, x_arr, x_col0, short_w, short_b, short_u_col0, short_x_col0,
               skip, spectrum, k_col0, width):
    B = u_arr.shape[0]
    L = dims.L
    cw = MXU_DIM
    ub, xb, kb = u_col0 // cw, x_col0 // cw, k_col0 // cw
    first = short_u_col0 is not None
    fwd_spec = pl.BlockSpec((dims.FM, dims.QK), lambda c, b: (0, 0), pipeline_mode=pl.Buffered(1))
    inv_spec = pl.BlockSpec((dims.QP, dims.IK), lambda c, b: (0, 0), pipeline_mode=pl.Buffered(1))

    def taps(col0):
        blk = col0 // cw
        return [pl.BlockSpec((HY_SHORT, cw), lambda c, b: (0, blk + c)),
                pl.BlockSpec((1, cw), lambda c, b: (0, blk + c))]

    if first:
        u_spec = pl.BlockSpec((1, L, cw), lambda c, b: (b, 0, ub + c))
        out_spec = pl.BlockSpec((1, RADIX, dims.QP, cw), lambda c, b: (b, 0, 0, c))
        out_shape = jax.ShapeDtypeStruct((B, RADIX, dims.QP, width), BF16)
    else:
        u_spec = pl.BlockSpec((1, RADIX, dims.QP, cw), lambda c, b: (b, 0, 0, ub + c))
        out_spec = pl.BlockSpec((1, L, cw), lambda c, b: (b, 0, c))
        out_shape = jax.ShapeDtypeStruct((B, L, width), F32)
    in_specs = ([fwd_spec] * (2 * RADIX) + [inv_spec] * RADIX
                + [u_spec, pl.BlockSpec((1, L, cw), lambda c, b: (b, 0, xb + c))])
    args = list(fwd) + list(inv) + [u_arr, x_arr]
    for col0 in ([short_u_col0] if first else []) + [short_x_col0]:
        in_specs += taps(col0)
        args += [short_w, short_b]
    in_specs += [pl.BlockSpec((1, cw), lambda c, b: (0, c))]
    in_specs += [pl.BlockSpec((dims.FM, cw), lambda c, b: (0, kb + c))] * (2 * RADIX)
    args += [skip.reshape(1, width)] + list(spectrum)
    stage = pltpu.VMEM((cw // LANES, RADIX * dims.QP + 2 * STAGE_HEAD, LANES), F32)
    return pl.pallas_call(
        functools.partial(_conv_body, dims=dims, first=first),
        grid=(width // cw, B),
        in_specs=in_specs,
        out_specs=out_spec,
        out_shape=out_shape,
        scratch_shapes=([pltpu.VMEM((dims.QK, cw), BF16)] * RADIX
                        + [pltpu.VMEM((dims.QP, cw), BF16)] * RADIX
                        + [pltpu.VMEM((dims.IK, cw), BF16)] * RADIX + [stage, stage]),
        compiler_params=_params(("parallel", "arbitrary")),
        name="hyena_long_conv",
    )(*args)


def _filter_body(z_ref, w1_ref, b1_ref, w2_ref, b2_ref, fr_ref, w3f_ref, w3b_ref, dec_ref, hs_ref, hd_ref,
                 h2_ref, nat_ref, *, dims):
    hp = lax.Precision.HIGHEST

    @pl.when(pl.program_id(0) == 0)
    def _():
        fr = fr_ref[...]
        h1 = jnp.sin(fr * (jnp.dot(z_ref[...], w1_ref[...], precision=hp, preferred_element_type=F32)
                           + b1_ref[...]))
        h2_ref[...] = jnp.sin(fr * (jnp.dot(h1, w2_ref[...], precision=hp, preferred_element_type=F32)
                                    + b2_ref[...]))

    h2 = h2_ref[...]
    dec = dec_ref[...]
    hf = jnp.dot(h2, w3f_ref[...], precision=hp, preferred_element_type=F32) * dec
    hb = jnp.dot(h2, w3b_ref[...], precision=hp, preferred_element_type=F32) * dec
    hb = jnp.where(lax.broadcasted_iota(jnp.int32, hb.shape, 0) == 0, 0.0, hb)
    scale = lax.rsqrt(jnp.sum(hf * hf + hb * hb, axis=0, keepdims=True) + 1e-6)
    for ref, val in ((hs_ref, (hf + hb) * scale), (hd_ref, (hf - hb) * scale)):
        for k in range(val.shape[1] // LANES):
            lanes = slice(k * LANES, (k + 1) * LANES)
            nat_ref[k] = val[:, lanes]
            for r in range(RADIX):
                ref[r, :, lanes] = nat_ref[k, pl.ds(r, dims.Q, stride=RADIX), :]


def _filter_constants(L, width):
    t = np.linspace(0.0, 1.0, L)
    w = (2.0 * math.pi / L) * np.arange(L)
    bands = np.linspace(1e-4, HY_BANDS - 1, HY_BANDS)
    fw = w[:, None] * bands[None, :]
    z = np.concatenate([t[:, None], np.cos(fw), -np.sin(fw)], axis=-1)
    z = np.pad(z, ((0, 0), (0, LANES - z.shape[1])))
    deltas = np.abs(np.linspace(HY_MIN_DECAY, HY_MAX_DECAY, width))
    return z.astype(np.float32), np.exp(-t[:, None] * deltas[None, :]).astype(np.float32)


def _hyena_filters(dims, consts, w1, b1, w2, b2, w3, freq):
    L = dims.L
    z, dec = consts
    hidden = w2.shape[0]
    width = w3.shape[1] // (2 * HY_ORDER)
    cw = MXU_DIM
    per_order = width // cw

    def lanes128(a):
        return jnp.pad(a.astype(F32), [(0, 0)] * (a.ndim - 1) + [(0, LANES - a.shape[-1])])

    w1p = jnp.pad(lanes128(w1), ((0, LANES - w1.shape[0]), (0, 0)))
    w2p = jnp.pad(lanes128(w2), ((0, LANES - hidden), (0, 0)))
    w3p = jnp.pad(w3.astype(F32), ((0, LANES - hidden), (0, 0)))
    row = lambda a: lanes128(a).reshape(1, LANES)
    full = lambda shape: pl.BlockSpec(shape, lambda j: (0, 0))
    fwd_col = lambda j: (0, (j // per_order) * 2 * per_order + j % per_order)
    bwd_col = lambda j: (0, (j // per_order) * 2 * per_order + per_order + j % per_order)
    out_spec = pl.BlockSpec((RADIX, dims.Q, cw), lambda j: (0, 0, j))
    return pl.pallas_call(
        functools.partial(_filter_body, dims=dims),
        grid=(HY_ORDER * per_order,),
        in_specs=[full((L, LANES)), full((LANES, LANES)), full((1, LANES)), full((LANES, LANES)),
                  full((1, LANES)), full((1, LANES)),
                  pl.BlockSpec((LANES, cw), fwd_col), pl.BlockSpec((LANES, cw), bwd_col),
                  pl.BlockSpec((L, cw), lambda j: (0, j % per_order))],
        out_specs=[out_spec, out_spec],
        out_shape=[jax.ShapeDtypeStruct((RADIX, dims.Q, HY_ORDER * width), F32)] * 2,
        scratch_shapes=[pltpu.VMEM((L, LANES), F32), pltpu.VMEM((cw // LANES, L, LANES), F32)],
        compiler_params=_params(("arbitrary",)),
        name="hyena_filters",
    )(z, w1p, row(b1), w2p, row(b2), row(freq), w3p, w3p, dec)


def _mix_body(ret_ref, g_ref, hy_ref, gr_ref, gh_ref, h_ref, wr_ref, wh_ref, wo_ref, o_ref):
    ret = None
    for hd in range(RET_HEADS):
        cols = slice(hd * RET_DV, (hd + 1) * RET_DV)
        o = ret_ref[:, cols].astype(F32)
        o = o * lax.rsqrt(jnp.mean(o * o, axis=-1, keepdims=True) + RMS_EPS)
        g = g_ref[:, cols].astype(F32)
        part = jnp.dot((g * jax.nn.sigmoid(g) * o).astype(BF16), wr_ref[cols, :], preferred_element_type=F32)
        ret = part if ret is None else ret + part
    hyo = jnp.dot(hy_ref[...].astype(BF16), wh_ref[...], preferred_element_type=F32)
    merged = (jax.nn.sigmoid(gr_ref[...].astype(F32)) * ret
              + jax.nn.sigmoid(gh_ref[...].astype(F32)) * hyo)
    o_ref[...] = h_ref[...] + jnp.dot(merged.astype(BF16), wo_ref[...], preferred_element_type=F32)


def _mix_out(ret2, hy2, proj2, g_col0, gate_col0, h2, w_ret_o, w_hy_o, w_out):
    T, D = h2.shape
    tm = _pick_tile(T, 576, BF16_SUBLANES)
    gb = gate_col0 // D
    v_w = ret2.shape[1]
    assert g_col0 % v_w == 0
    row = lambda w: pl.BlockSpec((tm, w), lambda i: (i, 0))
    full = lambda a: pl.BlockSpec(a.shape, lambda i: (0, 0))
    return pl.pallas_call(
        _mix_body,
        grid=(T // tm,),
        in_specs=[
            row(v_w), pl.BlockSpec((tm, v_w), lambda i: (i, g_col0 // v_w)), row(hy2.shape[1]),
            pl.BlockSpec((tm, D), lambda i: (i, gb)),
            pl.BlockSpec((tm, D), lambda i: (i, gb + 1)),
            row(D), full(w_ret_o), full(w_hy_o), full(w_out),
        ],
        out_specs=row(D),
        out_shape=jax.ShapeDtypeStruct((T, D), F32),
        compiler_params=_params(("parallel",)),
        name="merge_outproj",
    )(ret2, proj2, hy2, proj2, proj2, h2, w_ret_o, w_hy_o, w_out)


EXPERT_TILE = 1024
META_COLS = 8
SC_WINDOW = 64
SC_WORKERS = 32
SC_CHUNK = 2 * SC_WINDOW * SC_WORKERS
SC_SCATTER_WINDOW = 128


def _pack_bf16_pairs(x):
    n = x.shape[1] // 2
    xb = x.astype(BF16).astype(F32)
    hi = lax.bitcast_convert_type(xb[:, :n], jnp.uint32)
    lo = lax.bitcast_convert_type(xb[:, n:], jnp.uint32)
    return lax.bitcast_convert_type(hi | (lo >> 16), jnp.int32)


def _unpack_bf16_pairs(w):
    u = lax.bitcast_convert_type(w, jnp.uint32)
    hi = lax.bitcast_convert_type(u & jnp.uint32(0xFFFF0000), F32)
    lo = lax.bitcast_convert_type(u << 16, F32)
    return hi, lo


def _route(lt):
    assert EXP_PER_GROUP == 8 and N_GROUPS <= 8
    tm = lt.shape[1]
    row = lax.broadcasted_iota(jnp.int32, (8, tm), 0)
    neg = -jnp.inf
    big = jnp.int32(1 << 20)
    gl = jnp.where(row < N_GROUPS, lt[N_EXPERTS:N_EXPERTS + 8, :], neg)
    gmax = jnp.max(gl, axis=0, keepdims=True)
    p_top = 1.0 / jnp.sum(jnp.exp(gl - gmax), axis=0, keepdims=True)
    g_idx = jnp.min(jnp.where(gl == gmax, row, big), axis=0, keepdims=True)
    el = lt[0:8, :]
    for g in range(1, N_GROUPS):
        el = jnp.where(g_idx == g, lt[8 * g:8 * g + 8, :], el)
    m1 = jnp.max(el, axis=0, keepdims=True)
    i1 = jnp.min(jnp.where(el == m1, row, big), axis=0, keepdims=True)
    el2 = jnp.where(row == i1, neg, el)
    m2 = jnp.max(el2, axis=0, keepdims=True)
    i2 = jnp.min(jnp.where(el2 == m2, row, big), axis=0, keepdims=True)
    r = jnp.exp(m2 - m1)
    base = g_idx * EXP_PER_GROUP
    return base + i1, base + i2, p_top / (1.0 + r), p_top * r / (1.0 + r)


def _route_body(h_ref, g_ref, whi_ref, wlo_ref, xpk_ref, meta_ref, meta_t_ref, cnt_ref, carry_ref):
    i = pl.program_id(0)

    @pl.when(i == 0)
    def _():
        carry_ref[...] = jnp.zeros_like(carry_ref)

    xn = _rms(h_ref[...], g_ref[...])
    xpk_ref[...] = _pack_bf16_pairs(xn)
    x_hi = xn.astype(BF16)
    x_lo = (xn - x_hi.astype(F32)).astype(BF16)
    logits = (jnp.dot(x_hi, whi_ref[...], preferred_element_type=F32)
              + jnp.dot(x_lo, whi_ref[...], preferred_element_type=F32)
              + jnp.dot(x_hi, wlo_ref[...], preferred_element_type=F32))
    e0, e1, w0, w1 = _route(logits.T)
    tm = logits.shape[0]
    row = lax.broadcasted_iota(jnp.int32, (LANES, tm), 0)
    onehot_t = jnp.where((row == e0) | (row == e1), 1.0, 0.0).astype(BF16)
    lane_tiles = tm // LANES
    sub = LANES * max(k for k in (1, 2, 3) if lane_tiles % k == 0)
    ri = lax.broadcasted_iota(jnp.int32, (sub, sub), 0)
    ci = lax.broadcasted_iota(jnp.int32, (sub, sub), 1)
    earlier = jnp.where(ci < ri, 1.0, 0.0).astype(BF16)
    ones = jnp.ones((8, sub), BF16)
    running = carry_ref[...]
    blocks = []
    for j in range(tm // sub):
        part = onehot_t[:, j * sub:(j + 1) * sub]
        blocks.append((running + _dot_nt(earlier, part)).T)
        running = running + _dot_nt(ones, part)[0:1, :]
    prefix_t = jnp.concatenate(blocks, axis=1)
    r0 = jnp.sum(jnp.where(row == e0, prefix_t, 0.0), axis=0, keepdims=True)
    r1 = jnp.sum(jnp.where(row == e1, prefix_t, 0.0), axis=0, keepdims=True)
    row8 = lax.broadcasted_iota(jnp.int32, (META_COLS, tm), 0)
    meta_t = jnp.zeros((META_COLS, tm), F32)
    for c, val in enumerate((e0.astype(F32), e1.astype(F32), r0, r1, w0, w1)):
        meta_t = jnp.where(row8 == c, val, meta_t)
    meta_t_ref[...] = meta_t
    padded = jnp.concatenate([meta_t, jnp.zeros((LANES - META_COLS, tm), F32)], axis=0)
    meta_ref[...] = padded.T[:, :META_COLS]
    carry_ref[...] = running
    cnt_ref[...] = running


def _moe_route(h2, gain, w_router):
    T, D = h2.shape
    tm = _pick_tile(T, 1152, LANES)
    return pl.pallas_call(
        _route_body,
        grid=(T // tm,),
        in_specs=[
            pl.BlockSpec((tm, D), lambda i: (i, 0)),
            pl.BlockSpec((1, D), lambda i: (0, 0)),
            pl.BlockSpec((D, LANES), lambda i: (0, 0)),
            pl.BlockSpec((D, LANES), lambda i: (0, 0)),
        ],
        out_specs=[
            pl.BlockSpec((tm, D // 2), lambda i: (i, 0)),
            pl.BlockSpec((tm, META_COLS), lambda i: (i, 0)),
            pl.BlockSpec((META_COLS, tm), lambda i: (0, i)),
            pl.BlockSpec((1, LANES), lambda i: (0, 0)),
        ],
        out_shape=[
            jax.ShapeDtypeStruct((T, D // 2), jnp.int32),
            jax.ShapeDtypeStruct((T, META_COLS), F32),
            jax.ShapeDtypeStruct((META_COLS, T), F32),
            jax.ShapeDtypeStruct((1, LANES), F32),
        ],
        scratch_shapes=[pltpu.VMEM((1, LANES), F32)],
        compiler_params=_params(("arbitrary",)),
        name="moe_route",
    )(h2, gain.reshape(1, D), *w_router)


def _sc_gather(table, idx):
    n = idx.shape[0]
    width = table.shape[1]
    win = SC_WINDOW
    assert n % SC_CHUNK == 0
    per_worker = n // SC_WORKERS
    mesh = plsc.VectorSubcoreMesh(core_axis_name="c", subcore_axis_name="s")

    @functools.partial(
        pl.kernel, out_type=jax.ShapeDtypeStruct((n, width), table.dtype), mesh=mesh,
        scratch_types=[pltpu.VMEM((per_worker,), jnp.int32), pltpu.VMEM((2, win, width), table.dtype),
                       pltpu.SemaphoreType.DMA((2,)), pltpu.SemaphoreType.DMA((2,))],
        name="sc_row_gather")
    def gather(table_hbm, idx_hbm, out_hbm, idx_v, rows_v, gsem, osem):
        worker = lax.axis_index("s") * mesh.num_cores + lax.axis_index("c")
        base = worker * per_worker
        pltpu.sync_copy(idx_hbm.at[pl.ds(base, per_worker)], idx_v)

        @pl.loop(0, per_worker, step=2 * win)
        def _(off):
            fetch = [pltpu.async_copy(table_hbm.at[idx_v.at[pl.ds(off + s * win, win)]], rows_v.at[s], gsem.at[s])
                     for s in range(2)]
            store = []
            for s in range(2):
                fetch[s].wait()
                store.append(pltpu.async_copy(rows_v.at[s], out_hbm.at[pl.ds(base + off + s * win, win)],
                                              osem.at[s]))
            for s in range(2):
                store[s].wait()

    return gather(table, idx)


def _sc_dispatch(table, dest, n_out):
    n_rows, width = table.shape
    win = SC_SCATTER_WINDOW
    assert n_rows % win == 0
    n_win = n_rows // win
    per_worker = pl.cdiv(n_win, SC_WORKERS)
    idx = jnp.pad(dest, ((0, 0), (0, per_worker * SC_WORKERS * win - n_rows)))
    idx = idx.reshape(2, per_worker, SC_WORKERS, win).transpose(0, 2, 1, 3)
    mesh = plsc.VectorSubcoreMesh(core_axis_name="c", subcore_axis_name="s")

    @functools.partial(
        pl.kernel, out_type=jax.ShapeDtypeStruct((n_out, width), table.dtype), mesh=mesh,
        scratch_types=[pltpu.VMEM((per_worker, win), jnp.int32), pltpu.VMEM((per_worker, win), jnp.int32),
                       pltpu.VMEM((win, width), table.dtype),
                       pltpu.SemaphoreType.DMA, pltpu.SemaphoreType.DMA],
        name="sc_row_dispatch")
    def dispatch(table_hbm, idx_hbm, out_hbm, idx0_v, idx1_v, rows_v, sem0, sem1):
        worker = lax.axis_index("s") * mesh.num_cores + lax.axis_index("c")
        pltpu.sync_copy(idx_hbm.at[0, worker], idx0_v)
        pltpu.sync_copy(idx_hbm.at[1, worker], idx1_v)

        @pl.loop(0, per_worker)
        def _(j):
            window = j * SC_WORKERS + worker

            @pl.when(window < n_win)
            def _():
                pltpu.sync_copy(table_hbm.at[pl.ds(window * win, win)], rows_v)
                first = pltpu.async_copy(rows_v, out_hbm.at[idx0_v.at[j]], sem0)
                second = pltpu.async_copy(rows_v, out_hbm.at[idx1_v.at[j]], sem1)
                first.wait()
                second.wait()

    return dispatch(table, idx)


def _expert_body(te_ref, tv_ref, x_ref, wg_ref, wu_ref, wd_ref, y_ref, wg_s, wu_s, wd_s):
    t = pl.program_id(0)
    half = x_ref.shape[1]

    @pl.when((t == 0) | (te_ref[t] != te_ref[jnp.maximum(t - 1, 0)]))
    def _():
        wg_s[...] = wg_ref[0].astype(BF16)
        wu_s[...] = wu_ref[0].astype(BF16)
        wd_s[...] = wd_ref[0].astype(BF16)

    @pl.when(tv_ref[t] > 0)
    def _():
        row = lax.broadcasted_iota(jnp.int32, x_ref.shape, 0)
        hi, lo = _unpack_bf16_pairs(jnp.where(row < tv_ref[t], x_ref[...], 0))
        hi, lo = hi.astype(BF16), lo.astype(BF16)
        hg = jnp.dot(hi, wg_s[:half, :], preferred_element_type=F32)
        hg += jnp.dot(lo, wg_s[half:, :], preferred_element_type=F32)
        hu = jnp.dot(hi, wu_s[:half, :], preferred_element_type=F32)
        hu += jnp.dot(lo, wu_s[half:, :], preferred_element_type=F32)
        act = (hg * jax.nn.sigmoid(hg) * hu).astype(BF16)
        y_ref[...] = _pack_bf16_pairs(jnp.dot(act, wd_s[...], preferred_element_type=F32))

    @pl.when(tv_ref[t] == 0)
    def _():
        y_ref[...] = jnp.zeros_like(y_ref)


def _moe_experts(xs, n_sorted, tile_expert, tile_valid, w_gate, w_up, w_down):
    NP, half = n_sorted, xs.shape[1]
    _, D, FF = w_gate.shape
    tr = EXPERT_TILE
    grid_spec = pltpu.PrefetchScalarGridSpec(
        num_scalar_prefetch=2,
        grid=(NP // tr,),
        in_specs=[
            pl.BlockSpec((tr, half), lambda t, te, tv: (t, 0)),
            pl.BlockSpec((1, D, FF), lambda t, te, tv: (te[t], 0, 0)),
            pl.BlockSpec((1, D, FF), lambda t, te, tv: (te[t], 0, 0)),
            pl.BlockSpec((1, FF, D), lambda t, te, tv: (te[t], 0, 0)),
        ],
        out_specs=pl.BlockSpec((tr, half), lambda t, te, tv: (t, 0)),
        scratch_shapes=[pltpu.VMEM((D, FF), BF16), pltpu.VMEM((D, FF), BF16), pltpu.VMEM((FF, D), BF16)],
    )
    return pl.pallas_call(
        _expert_body,
        grid_spec=grid_spec,
        out_shape=jax.ShapeDtypeStruct((NP, half), jnp.int32),
        compiler_params=_params(("arbitrary",)),
        name="moe_experts",
    )(tile_expert, tile_valid, xs, w_gate, w_up, w_down)


def _moe_combine_into(o_ref, h, meta, y0, y1):
    half = y0.shape[1]
    w0 = meta[:, 4:5]
    w1 = meta[:, 5:6]
    hi0, lo0 = _unpack_bf16_pairs(y0)
    hi1, lo1 = _unpack_bf16_pairs(y1)
    o_ref[:, :half] = h[:, :half] + w0 * hi0 + w1 * hi1
    o_ref[:, half:] = h[:, half:] + w0 * lo0 + w1 * lo1


def _moe(h2, gain, w_router, layer, w_gate, w_up, w_down):
    T, D = h2.shape
    E = N_EXPERTS
    tr = EXPERT_TILE
    n_sorted = _round_up(2 * T + E * (tr - 1), tr)
    t_pad = _round_up(T, SC_CHUNK // 2)

    xpk, meta, meta_t, counts = _moe_route(h2, gain, w_router)

    cnt = counts[0, :E].astype(jnp.int32)
    padded = (cnt + tr - 1) // tr * tr
    ends = jnp.cumsum(padded)
    starts = ends - padded
    eid = meta_t[0:2].astype(jnp.int32)
    pos = meta_t[2:4].astype(jnp.int32)
    for e in range(E):
        pos = pos + jnp.where(eid == e, starts[e], 0)
    tile_start = jnp.arange(n_sorted // tr, dtype=jnp.int32) * tr
    tile_expert = jnp.minimum(jnp.sum(tile_start[:, None] >= ends[None, :], axis=1), E - 1).astype(jnp.int32)
    tile_valid = jnp.clip(cnt[tile_expert] - (tile_start - starts[tile_expert]), 0, tr).astype(jnp.int32)

    xs = _sc_dispatch(xpk, pos, n_sorted)
    ys = _moe_experts(xs, n_sorted, layer * E + tile_expert, tile_valid, w_gate, w_up, w_down)
    spare = jnp.arange(t_pad - T, dtype=jnp.int32)
    back = jnp.concatenate([pos, jnp.broadcast_to(spare[None], (2, t_pad - T))], axis=1)
    yg = _sc_gather(ys, back.reshape(-1)).reshape(2, t_pad, D // 2)
    return meta, yg


def _final_body(h_ref, meta_ref, y0_ref, y1_ref, g_ref, o_ref, hn_ref):
    _moe_combine_into(hn_ref, h_ref[0], meta_ref[0], y0_ref[0], y1_ref[0])
    o_ref[0] = _rms(hn_ref[N_META:, :], g_ref[...])


def _final_norm(h3, meta3, yg, gain, b0, nb):
    _, L, D = h3.shape
    return pl.pallas_call(
        _final_body,
        grid=(nb,),
        in_specs=[pl.BlockSpec((1, L, D), lambda b: (b0 + b, 0, 0)),
                  pl.BlockSpec((1, L, meta3.shape[2]), lambda b: (b0 + b, 0, 0)),
                  pl.BlockSpec((1, L, D // 2), lambda b: (0, b0 + b, 0)),
                  pl.BlockSpec((1, L, D // 2), lambda b: (1, b0 + b, 0)),
                  pl.BlockSpec((1, D), lambda b: (0, 0))],
        out_specs=pl.BlockSpec((1, L - N_META, D), lambda b: (b, 0, 0)),
        out_shape=jax.ShapeDtypeStruct((nb, L - N_META, D), F32),
        scratch_shapes=[pltpu.VMEM((L, D), F32)],
        compiler_params=_params(("parallel",)),
        name="combine_final_norm",
    )(h3, meta3, yg, yg, gain.reshape(1, D))


def kernel(x_prompt, x_sample, meta_tokens, norm_mix, w_in, ret_decay_fwd, ret_decay_bwd, hy_short_w, hy_short_b, hy_filt_w1, hy_filt_b1, hy_filt_w2, hy_filt_b2, hy_filt_w3, hy_sin_freq, hy_skip, w_ret_o, w_hy_o, w_out, norm_ffn, router_group, router_expert, moe_w_gate, moe_w_up, moe_w_down, norm_final):
    assert x_prompt.shape[1:] == x_sample.shape[1:]
    nbp, nbs = x_prompt.shape[0], x_sample.shape[0]
    B = nbp + nbs
    D = x_prompt.shape[2]
    L = N_META + x_prompt.shape[1]
    T = B * L
    depth = w_in.shape[0]
    q_w = RET_HEADS * RET_DK
    v_w = RET_HEADS * RET_DV
    hy_w = hy_skip.shape[2]
    hy_col0 = 2 * q_w + 2 * v_w
    gate_col0 = hy_col0 + 3 * hy_w
    assert D == q_w and w_in.shape[2] == gate_col0 + 2 * D

    x = jnp.concatenate([x_prompt, x_sample], axis=0)
    meta = jnp.broadcast_to(meta_tokens[None].astype(x.dtype), (B, N_META, D))
    h = jnp.concatenate([meta, x], axis=1).reshape(T, D)

    cos_t, sin_t = _rotary_tables(L)
    dims = _HyenaDims(L)
    dft_fwd, dft_inv = _dft_tables(dims)
    filt_consts = _filter_constants(L, hy_w)
    router = jnp.concatenate([router_expert, router_group], axis=2).astype(F32)
    router = jnp.pad(router, ((0, 0), (0, 0), (0, LANES - router.shape[2])))
    router_hi = router.astype(BF16)
    router_lo = (router - router_hi.astype(F32)).astype(BF16)
    assert moe_w_gate.shape[1] == N_EXPERTS
    w_gate_all = moe_w_gate.reshape((-1,) + moe_w_gate.shape[2:])
    w_up_all = moe_w_up.reshape((-1,) + moe_w_up.shape[2:])
    w_down_all = moe_w_down.reshape((-1,) + moe_w_down.shape[2:])

    moe = None
    for i in range(depth):
        proj, h = _inproj(h, norm_mix[i], w_in[i].astype(BF16), moe)
        proj3 = proj.reshape(B, L, -1)

        lg = jnp.stack([jax.nn.log_sigmoid(ret_decay_fwd[i].astype(F32)),
                        jax.nn.log_sigmoid(ret_decay_bwd[i].astype(F32))])
        ret = _retention(proj3, lg, cos_t, sin_t)

        hs, hd = _hyena_filters(dims, filt_consts, hy_filt_w1[i], hy_filt_b1[i], hy_filt_w2[i], hy_filt_b2[i],
                                hy_filt_w3[i], hy_sin_freq[i])
        spectrum = _filter_spectrum(dims, dft_fwd, hs, hd)
        sw = hy_short_w[i].astype(F32)
        sb = hy_short_b[i].astype(F32).reshape(1, -1)
        z = _long_conv(dims, dft_fwd, dft_inv, proj3, hy_col0, proj3, hy_col0 + hy_w, sw, sb, 0, hy_w,
                       hy_skip[i, 0].astype(F32), spectrum, 0, hy_w)
        z = _long_conv(dims, dft_fwd, dft_inv, z, 0, proj3, hy_col0 + 2 * hy_w, sw, sb, None, 2 * hy_w,
                       hy_skip[i, 1].astype(F32), spectrum, hy_w, hy_w)

        h = _mix_out(ret.reshape(T, v_w), z.reshape(T, hy_w), proj, 2 * q_w + v_w, gate_col0, h,
                     w_ret_o[i].astype(BF16), w_hy_o[i].astype(BF16), w_out[i].astype(BF16))
        moe = _moe(h, norm_ffn[i], (router_hi[i], router_lo[i]), i, w_gate_all, w_up_all, w_down_all)

    h3 = h.reshape(B, L, D)
    meta3 = moe[0].reshape(B, L, -1)
    return (_final_norm(h3, meta3, moe[1], norm_final, 0, nbp),
            _final_norm(h3, meta3, moe[1], norm_final, nbp, nbs))
```

```python
import functools
import math

import jax
import jax.numpy as jnp
import numpy as np
from jax import lax
from jax.experimental import pallas as pl
from jax.experimental.pallas import tpu as pltpu
from jax.experimental.pallas import tpu_sc as plsc

N_META = 16
RET_HEADS = 8
RET_DK = 128
RET_DV = 256
ROPE_THETA = 10000.0
HY_ORDER = 2
HY_SHORT = 3
HY_EMB = 33
HY_BANDS = (HY_EMB - 1) // 2
HY_DECAY_TARGET = 1e-2
HY_MIN_DECAY = math.log(HY_DECAY_TARGET) / 1.5
HY_MAX_DECAY = math.log(HY_DECAY_TARGET) / 0.3
N_GROUPS = 4
EXP_PER_GROUP = 8
N_EXPERTS = N_GROUPS * EXP_PER_GROUP
RMS_EPS = 1e-6

LANES = 128
BF16_SUBLANES = 16
MXU_DIM = 256
RET_CHUNK = 256
VMEM_LIMIT = 56 * 1024 * 1024

F32 = jnp.float32
BF16 = jnp.bfloat16


def _round_up(n, m):
    return (n + m - 1) // m * m


def _pick_tile(n, target, mult):
    best = None
    for t in range(mult, min(n, target) + 1, mult):
        if n % t == 0:
            best = t
    assert best is not None, (n, target, mult)
    return best


def _params(sem):
    return pltpu.CompilerParams(dimension_semantics=sem, vmem_limit_bytes=VMEM_LIMIT)


def _rms(x, gain):
    ms = jnp.mean(x * x, axis=-1, keepdims=True)
    return x * lax.rsqrt(ms + RMS_EPS) * gain


def _inproj_body(h_ref, g_ref, w_ref, o_ref, xn_ref):
    @pl.when(pl.program_id(1) == 0)
    def _():
        xn_ref[...] = _rms(h_ref[...], g_ref[...]).astype(BF16)

    o_ref[...] = jnp.dot(xn_ref[...], w_ref[...], preferred_element_type=F32).astype(o_ref.dtype)


def _proj_body(xn_ref, w_ref, o_ref):
    o_ref[...] = jnp.dot(xn_ref[...], w_ref[...], preferred_element_type=F32).astype(o_ref.dtype)


def _combine_norm_body(h_ref, meta_ref, y0_ref, y1_ref, g_ref, hn_ref, xn_ref):
    _moe_combine_into(hn_ref, h_ref[...], meta_ref[...], y0_ref[0], y1_ref[0])
    xn_ref[...] = _rms(hn_ref[...], g_ref[...]).astype(BF16)


def _inproj(h2, gain, w, moe=None):
    T, D = h2.shape
    nc = w.shape[1]
    tm = _pick_tile(T, 1408, BF16_SUBLANES)
    tn = _pick_tile(nc, 2816, MXU_DIM)
    grid = (T // tm, nc // tn)
    h_spec = pl.BlockSpec((tm, D), lambda i, j: (i, 0))
    w_spec = pl.BlockSpec((D, tn), lambda i, j: (0, j))
    o_spec = pl.BlockSpec((tm, tn), lambda i, j: (i, j))
    o_shape = jax.ShapeDtypeStruct((T, nc), BF16)
    if moe is None:
        proj = pl.pallas_call(
            _inproj_body, grid=grid,
            in_specs=[h_spec, pl.BlockSpec((1, D), lambda i, j: (0, 0)), w_spec],
            out_specs=o_spec, out_shape=o_shape,
            scratch_shapes=[pltpu.VMEM((tm, D), BF16)],
            compiler_params=_params(("parallel", "arbitrary")),
            name="norm_inproj")(h2, gain.reshape(1, D), w)
        return proj, h2
    meta, yg = moe
    row = pl.BlockSpec((tm, D), lambda i: (i, 0))
    hn, xn = pl.pallas_call(
        _combine_norm_body, grid=(T // tm,),
        in_specs=[row, pl.BlockSpec((tm, meta.shape[1]), lambda i: (i, 0)),
                  pl.BlockSpec((1, tm, D // 2), lambda i: (0, i, 0)),
                  pl.BlockSpec((1, tm, D // 2), lambda i: (1, i, 0)),
                  pl.BlockSpec((1, D), lambda i: (0, 0))],
        out_specs=[row, row],
        out_shape=[jax.ShapeDtypeStruct((T, D), F32), jax.ShapeDtypeStruct((T, D), BF16)],
        compiler_params=_params(("parallel",)),
        name="combine_norm")(h2, meta, yg, yg, gain.reshape(1, D))
    proj = pl.pallas_call(
        _proj_body, grid=grid, in_specs=[h_spec, w_spec], out_specs=o_spec, out_shape=o_shape,
        compiler_params=_params(("parallel", "parallel")),
        name="inproj")(xn, w)
    return proj, hn


def _dot_t(a, b):
    return lax.dot_general(a, b, (((0,), (0,)), ((), ())), preferred_element_type=F32)


def _dot_nt(a, b):
    return lax.dot_general(a, b, (((1,), (1,)), ((), ())), preferred_element_type=F32)


def _ret_body(lg_ref, q_ref, k_ref, v_ref, cos_ref, sin_ref, o_ref, rb_ref, kr_ref, *, seq_len):
    C = RET_CHUNK
    L = seq_len
    n_chunks = pl.cdiv(L, C)
    head = pl.program_id(1)
    lgf = lg_ref[0, head]
    lgb = lg_ref[1, head]

    def chunk(ref, n):
        lo, hi = n * C, min((n + 1) * C, L)
        x = ref[0, lo:hi, :]
        if hi - lo < C:
            x = jnp.concatenate([x, jnp.zeros((C - (hi - lo), x.shape[1]), x.dtype)], axis=0)
        return x

    def rotary(ref, n):
        x = chunk(ref, n).astype(F32)
        sl = slice(n * C, (n + 1) * C)
        return x * cos_ref[sl, :] + pltpu.roll(x, RET_DK // 2, 1) * sin_ref[sl, :]

    row = lax.broadcasted_iota(jnp.int32, (C, LANES), 0).astype(F32)
    ri = lax.broadcasted_iota(jnp.int32, (C, C), 0).astype(F32)
    ci = lax.broadcasted_iota(jnp.int32, (C, C), 1).astype(F32)
    decay = jnp.exp(jnp.where(ci <= ri, (ri - ci) * lgf, (ci - ri) * lgb))
    qf_dec = jnp.exp((row + 1.0) * lgf)
    qb_dec = jnp.exp((C - row) * lgb)
    kf_dec = jnp.exp((C - 1.0 - row) * lgf)
    kb_dec = jnp.exp(row * lgb)
    cf = jnp.exp(C * lgf)
    cb = jnp.exp(C * lgb)

    state = jnp.zeros((RET_DK, RET_DV), F32)
    for n in reversed(range(n_chunks)):
        rb_ref[n] = state.astype(BF16)
        if n > 0:
            kr_ref[n] = rotary(k_ref, n)
            kb = (kr_ref[n] * kb_dec).astype(BF16)
            state = cb * state + _dot_t(kb, chunk(v_ref, n))

    state = jnp.zeros((RET_DK, RET_DV), F32)
    for n in range(n_chunks):
        q = rotary(q_ref, n) * (RET_DK ** -0.5)
        k = kr_ref[n] if n > 0 else rotary(k_ref, n)
        v = chunk(v_ref, n)
        scores = _dot_nt(q.astype(BF16), k.astype(BF16)) * decay
        o = jnp.dot(scores.astype(BF16), v, preferred_element_type=F32)
        o += jnp.dot((q * qf_dec).astype(BF16), state.astype(BF16), preferred_element_type=F32)
        o += jnp.dot((q * qb_dec).astype(BF16), rb_ref[n], preferred_element_type=F32)
        if n + 1 < n_chunks:
            state = cf * state + _dot_t((k * kf_dec).astype(BF16), v)
        lo, hi = n * C, min((n + 1) * C, L)
        o_ref[0, lo:hi, :] = o[: hi - lo].astype(o_ref.dtype)


def _retention(proj3, lg, cos_t, sin_t):
    B, L, _ = proj3.shape
    n_chunks = pl.cdiv(L, RET_CHUNK)
    lp = n_chunks * RET_CHUNK
    k_blk = RET_HEADS
    v_blk = 2 * RET_HEADS * RET_DK // RET_DV
    return pl.pallas_call(
        functools.partial(_ret_body, seq_len=L),
        grid=(B, RET_HEADS),
        in_specs=[
            pl.BlockSpec(memory_space=pltpu.SMEM),
            pl.BlockSpec((1, L, RET_DK), lambda b, h: (b, 0, h)),
            pl.BlockSpec((1, L, RET_DK), lambda b, h: (b, 0, k_blk + h)),
            pl.BlockSpec((1, L, RET_DV), lambda b, h: (b, 0, v_blk + h)),
            pl.BlockSpec((lp, RET_DK), lambda b, h: (0, 0)),
            pl.BlockSpec((lp, RET_DK), lambda b, h: (0, 0)),
        ],
        out_specs=pl.BlockSpec((1, L, RET_DV), lambda b, h: (b, 0, h)),
        out_shape=jax.ShapeDtypeStruct((B, L, RET_HEADS * RET_DV), BF16),
        scratch_shapes=[pltpu.VMEM((n_chunks, RET_DK, RET_DV), BF16),
                        pltpu.VMEM((n_chunks, RET_CHUNK, RET_DK), F32)],
        compiler_params=_params(("parallel", "arbitrary")),
        name="retention",
    )(lg, proj3, proj3, proj3, cos_t, sin_t)


def _rotary_tables(L):
    half = RET_DK // 2
    lp = _round_up(L, RET_CHUNK)
    inv = ROPE_THETA ** (-jnp.arange(half, dtype=F32) / half)
    ang = jnp.arange(lp, dtype=F32)[:, None] * inv[None, :]
    cos, sin = jnp.cos(ang), jnp.sin(ang)
    return jnp.concatenate([cos, cos], axis=1), jnp.concatenate([-sin, sin], axis=1)


def _row_tiles(n, tile):
    return [(s, min(tile, n - s)) for s in range(0, n, tile)]


RADIX = 4


class _HyenaDims:
    def __init__(self, L):
        assert L % RADIX == 0
        self.L = L
        self.Q = L // RADIX
        self.F = self.Q + 1
        self.QP = _round_up(self.Q, BF16_SUBLANES)
        self.QK = _round_up(self.Q, MXU_DIM)
        self.FM = _round_up(self.F, BF16_SUBLANES)
        self.IK = _round_up(2 * self.FM, MXU_DIM)


def _dft_tables(dims):
    n = 2 * dims.L

    def trig(f, t, valid):
        ang = ((f * t) % n) * (2.0 * math.pi / n)
        return (np.where(valid, np.cos(ang), 0.0).astype(BF16), np.where(valid, np.sin(ang), 0.0).astype(BF16))

    f = np.arange(dims.FM, dtype=np.int64)[:, None]
    s = np.arange(dims.QK, dtype=np.int64)[None, :]
    ok = (f < dims.F) & (s < dims.Q)
    fwd = [trig(f, RADIX * s + r, ok) for r in range(RADIX)]

    s = np.arange(dims.QP, dtype=np.int64)[:, None]
    j = np.arange(dims.IK, dtype=np.int64)[None, :]
    f = j % dims.FM
    ok = (f < dims.F) & (s < dims.Q)
    inv = []
    for r in range(RADIX):
        cos, sin = trig(f, RADIX * s + r, ok)
        inv.append(np.where(j < dims.FM, cos, np.where(j < 2 * dims.FM, sin, np.zeros_like(cos))))
    return [t for pair in fwd for t in pair], inv


def _forward_dft(tables, parts, rows, dtype=F32):
    p = [jnp.dot(tables[2 * r][rows, :], parts[r][...], preferred_element_type=F32).astype(dtype)
         for r in range(RADIX)]
    a = [jnp.dot(tables[2 * r + 1][rows, :], parts[r][...], preferred_element_type=F32).astype(dtype)
         for r in range(RADIX)]
    ps02, pd02, ps13, pd13 = p[0] + p[2], p[0] - p[2], p[1] + p[3], p[1] - p[3]
    as02, ad02, as13, ad13 = a[0] + a[2], a[0] - a[2], a[1] + a[3], a[1] - a[3]
    return [(ps02 + ps13, as02 + as13),
            (pd02 + ad13, pd13 - ad02),
            (pd02 - ad13, pd13 + ad02),
            (ps02 - ps13, as13 - as02)]


def _class_weights(dims):
    f = lax.broadcasted_iota(jnp.int32, (dims.FM, 1), 0)
    n = 2.0 * dims.L
    edge = jnp.where(f == 0, 1.0, 2.0) / n
    return [jnp.where(f <= dims.Q, edge, 0.0),
            jnp.where(f < dims.Q, 2.0 / n, 0.0),
            jnp.where((f >= 1) & (f <= dims.Q), 2.0 / n, 0.0),
            jnp.where(f < dims.Q, edge, 0.0)]


def _spec_body(*refs, dims):
    tables = refs[:2 * RADIX]
    hs_ref, hd_ref = refs[2 * RADIX:2 * RADIX + 2]
    outs = refs[2 * RADIX + 2:4 * RADIX + 2]
    s_pads = refs[4 * RADIX + 2:5 * RADIX + 2]
    d_pads = refs[5 * RADIX + 2:]
    Q = dims.Q
    for pads, src in ((s_pads, hs_ref), (d_pads, hd_ref)):
        for r in range(RADIX):
            pads[r][Q:, :] = jnp.zeros((pads[r].shape[0] - Q, pads[r].shape[1]), BF16)
            pads[r][:Q, :] = src[r].astype(BF16)
    rows = slice(None)
    re = [c[0] for c in _forward_dft(tables, s_pads, rows)]
    nim = [c[1] for c in _forward_dft(tables, d_pads, rows)]
    for c, w in enumerate(_class_weights(dims)):
        outs[2 * c][...] = (w * re[c]).astype(outs[2 * c].dtype)
        outs[2 * c + 1][...] = (-w * nim[c]).astype(outs[2 * c + 1].dtype)


def _filter_spectrum(dims, fwd, hs, hd):
    width = hs.shape[2]
    cw = MXU_DIM
    resident = pl.BlockSpec((dims.FM, dims.QK), lambda j: (0, 0), pipeline_mode=pl.Buffered(1))
    return pl.pallas_call(
        functools.partial(_spec_body, dims=dims),
        grid=(width // cw,),
        in_specs=[resident] * (2 * RADIX) + [pl.BlockSpec((RADIX, dims.Q, cw), lambda j: (0, 0, j))] * 2,
        out_specs=[pl.BlockSpec((dims.FM, cw), lambda j: (0, j))] * (2 * RADIX),
        out_shape=[jax.ShapeDtypeStruct((dims.FM, width), BF16)] * (2 * RADIX),
        scratch_shapes=[pltpu.VMEM((dims.QK, cw), BF16)] * (2 * RADIX),
        compiler_params=_params(("arbitrary",)),
        name="hyena_filter_spectrum",
    )(*fwd, hs, hd)


STAGE_HEAD = 8


def _conv_body(*refs, dims, first):
    n_tab = 2 * RADIX
    tables = refs[:n_tab]
    inv_refs = refs[n_tab:n_tab + RADIX]
    u_ref, x_ref = refs[n_tab + RADIX:n_tab + RADIX + 2]
    at = n_tab + RADIX + 2
    n_taps = 4 if first else 2
    taps = refs[at:at + n_taps]
    at += n_taps
    skip_ref = refs[at]
    spec_refs = refs[at + 1:at + 1 + 2 * RADIX]
    o_ref = refs[at + 1 + 2 * RADIX]
    scratch = refs[at + 2 + 2 * RADIX:]
    u_store, scratch = scratch[:RADIX], scratch[RADIX:]
    x_parts, g_parts = scratch[:RADIX], scratch[RADIX:2 * RADIX]
    nat_ref, natx_ref = scratch[2 * RADIX:]
    L, Q, QP, QK, FM, IK = dims.L, dims.Q, dims.QP, dims.QK, dims.FM, dims.IK
    cw = x_parts[0].shape[1]

    lane_slabs = [(k, slice(k * LANES, (k + 1) * LANES)) for k in range(cw // LANES)]

    def short_conv_split(parts, ref, w_ref, b_ref, stage_ref, anchor=None):
        tail = stage_ref.shape[1] - STAGE_HEAD - L
        for k, lanes in lane_slabs:
            raw = ref[0, :, lanes].astype(F32)
            if anchor is not None:
                raw = raw + anchor[:, lanes]
            stage_ref[k, :STAGE_HEAD, :] = jnp.zeros((STAGE_HEAD, LANES), F32)
            stage_ref[k, STAGE_HEAD:STAGE_HEAD + L, :] = raw
            stage_ref[k, STAGE_HEAD + L:, :] = jnp.zeros((tail, LANES), F32)
            w = w_ref[:, lanes]
            bias = b_ref[:, lanes]
            for r in range(RADIX):
                taps3 = [stage_ref[k, pl.ds(STAGE_HEAD + r + d, Q, stride=RADIX), :] for d in (-1, 0, 1)]
                val = bias + w[0:1] * taps3[0] + w[1:2] * taps3[1] + w[2:3] * taps3[2]
                parts[r][:Q, lanes] = val.astype(BF16)
                if QP > Q:
                    parts[r][Q:QP, lanes] = jnp.zeros((QP - Q, LANES), BF16)

    def zero_tail(parts):
        if QK > QP:
            for r in range(RADIX):
                parts[r][QP:, :] = jnp.zeros((QK - QP, cw), BF16)

    def run(u_parts):
        for r in range(RADIX):
            if IK > 2 * FM:
                g_parts[r][2 * FM:, :] = jnp.zeros((IK - 2 * FM, cw), BF16)
        for lo, sz in _row_tiles(FM, MXU_DIM):
            rows = slice(lo, lo + sz)
            e1, e2 = [], []
            for c, (p, a) in enumerate(_forward_dft(tables, u_parts, rows, BF16)):
                k1, k2 = spec_refs[2 * c][rows, :], spec_refs[2 * c + 1][rows, :]
                e1.append(p * k1 + a * k2)
                e2.append(a * k1 - p * k2)
            for r, (gc, gs) in enumerate(_fold_classes(e1, e2)):
                g_parts[r][lo:lo + sz, :] = gc
                g_parts[r][FM + lo:FM + lo + sz, :] = gs

        anchor = g_parts[RADIX - 1][2 * FM - 1:2 * FM, :].astype(F32) * 0.0
        short_conv_split(x_parts, x_ref, taps[-2], taps[-1], natx_ref, anchor)
        skip = skip_ref[...]
        for lo, sz in _row_tiles(QP, MXU_DIM):
            rows = slice(lo, lo + sz)
            valid = min(sz, Q - lo)
            for r in range(RADIX):
                y = jnp.dot(inv_refs[r][rows, :], g_parts[r][...], preferred_element_type=F32)
                o = x_parts[r][rows, :].astype(F32) * (y + skip * u_parts[r][rows, :].astype(F32))
                if first:
                    o_ref[0, r, rows, :] = o.astype(o_ref.dtype)
                elif valid > 0:
                    for k, lanes in lane_slabs:
                        nat_ref[k, pl.ds(RADIX * lo + r, valid, stride=RADIX), :] = o[:valid, lanes]
        if not first:
            for k, lanes in lane_slabs:
                o_ref[0, :, lanes] = nat_ref[k, :L, :]

    zero_tail(u_store)
    if first:
        short_conv_split(u_store, u_ref, taps[0], taps[1], nat_ref)
    else:
        for r in range(RADIX):
            u_store[r][:QP, :] = u_ref[0, r]
    run(u_store)


def _fold_classes(e1, e2):
    a_p, a_m = e1[0] + e1[3], e1[0] - e1[3]
    b_p, b_m = e1[1] + e1[2], e1[1] - e1[2]
    c_p, c_m = e2[0] + e2[3], e2[0] - e2[3]
    d_p, d_m = e2[1] + e2[2], e2[2] - e2[1]
    return [(a_p + b_p, c_m + d_m),
            (a_m + d_p, c_p + b_m),
            (a_p - b_p, c_m - d_m),
            (a_m - d_p, c_p - b_m)]


def _long_conv(dims, fwd, inv, u_arr, u_col0, x_arr, x_col0, short_w, short_b, short_u_col0, short_x_col0,
               skip, spectrum, k_col0, width):
    B = u_arr.shape[0]
    L = dims.L
    cw = 2 * MXU_DIM
    ub, xb, kb = u_col0 // cw, x_col0 // cw, k_col0 // cw
    first = short_u_col0 is not None
    fwd_spec = pl.BlockSpec((dims.FM, dims.QK), lambda c, b: (0, 0), pipeline_mode=pl.Buffered(1))
    inv_spec = pl.BlockSpec((dims.QP, dims.IK), lambda c, b: (0, 0), pipeline_mode=pl.Buffered(1))

    def taps(col0):
        blk = col0 // cw
        return [pl.BlockSpec((HY_SHORT, cw), lambda c, b: (0, blk + c)),
                pl.BlockSpec((1, cw), lambda c, b: (0, blk + c))]

    if first:
        u_spec = pl.BlockSpec((1, L, cw), lambda c, b: (b, 0, ub + c))
        out_spec = pl.BlockSpec((1, RADIX, dims.QP, cw), lambda c, b: (b, 0, 0, c))
        out_shape = jax.ShapeDtypeStruct((B, RADIX, dims.QP, width), BF16)
    else:
        u_spec = pl.BlockSpec((1, RADIX, dims.QP, cw), lambda c, b: (b, 0, 0, ub + c))
        out_spec = pl.BlockSpec((1, L, cw), lambda c, b: (b, 0, c))
        out_shape = jax.ShapeDtypeStruct((B, L, width), F32)
    in_specs = ([fwd_spec] * (2 * RADIX) + [inv_spec] * RADIX
                + [u_spec, pl.BlockSpec((1, L, cw), lambda c, b: (b, 0, xb + c))])
    args = list(fwd) + list(inv) + [u_arr, x_arr]
    for col0 in ([short_u_col0] if first else []) + [short_x_col0]:
        in_specs += taps(col0)
        args += [short_w, short_b]
    in_specs += [pl.BlockSpec((1, cw), lambda c, b: (0, c))]
    in_specs += [pl.BlockSpec((dims.FM, cw), lambda c, b: (0, kb + c))] * (2 * RADIX)
    args += [skip.reshape(1, width)] + list(spectrum)
    stage = pltpu.VMEM((cw // LANES, RADIX * dims.QP + 2 * STAGE_HEAD, LANES), F32)
    return pl.pallas_call(
        functools.partial(_conv_body, dims=dims, first=first),
        grid=(width // cw, B),
        in_specs=in_specs,
        out_specs=out_spec,
        out_shape=out_shape,
        scratch_shapes=([pltpu.VMEM((dims.QK, cw), BF16)] * RADIX
                        + [pltpu.VMEM((dims.QP, cw), BF16)] * RADIX
                        + [pltpu.VMEM((dims.IK, cw), BF16)] * RADIX + [stage, stage]),
        compiler_params=_params(("parallel", "arbitrary")),
        name="hyena_long_conv",
    )(*args)


def _filter_body(z_ref, w1_ref, b1_ref, w2_ref, b2_ref, fr_ref, w3f_ref, w3b_ref, dec_ref, hs_ref, hd_ref,
                 h2_ref, nat_ref, *, dims):
    hp = lax.Precision.HIGHEST

    @pl.when(pl.program_id(0) == 0)
    def _():
        fr = fr_ref[...]
        h1 = jnp.sin(fr * (jnp.dot(z_ref[...], w1_ref[...], precision=hp, preferred_element_type=F32)
                           + b1_ref[...]))
        h2_ref[...] = jnp.sin(fr * (jnp.dot(h1, w2_ref[...], precision=hp, preferred_element_type=F32)
                                    + b2_ref[...]))

    h2 = h2_ref[...]
    dec = dec_ref[...]
    hf = jnp.dot(h2, w3f_ref[...], precision=hp, preferred_element_type=F32) * dec
    hb = jnp.dot(h2, w3b_ref[...], precision=hp, preferred_element_type=F32) * dec
    hb = jnp.where(lax.broadcasted_iota(jnp.int32, hb.shape, 0) == 0, 0.0, hb)
    scale = lax.rsqrt(jnp.sum(hf * hf + hb * hb, axis=0, keepdims=True) + 1e-6)
    for ref, val in ((hs_ref, (hf + hb) * scale), (hd_ref, (hf - hb) * scale)):
        for k in range(val.shape[1] // LANES):
            lanes = slice(k * LANES, (k + 1) * LANES)
            nat_ref[k] = val[:, lanes]
            for r in range(RADIX):
                ref[r, :, lanes] = nat_ref[k, pl.ds(r, dims.Q, stride=RADIX), :]


def _filter_constants(L, width):
    t = np.linspace(0.0, 1.0, L)
    w = (2.0 * math.pi / L) * np.arange(L)
    bands = np.linspace(1e-4, HY_BANDS - 1, HY_BANDS)
    fw = w[:, None] * bands[None, :]
    z = np.concatenate([t[:, None], np.cos(fw), -np.sin(fw)], axis=-1)
    z = np.pad(z, ((0, 0), (0, LANES - z.shape[1])))
    deltas = np.abs(np.linspace(HY_MIN_DECAY, HY_MAX_DECAY, width))
    return z.astype(np.float32), np.exp(-t[:, None] * deltas[None, :]).astype(np.float32)


def _hyena_filters(dims, consts, w1, b1, w2, b2, w3, freq):
    L = dims.L
    z, dec = consts
    hidden = w2.shape[0]
    width = w3.shape[1] // (2 * HY_ORDER)
    cw = MXU_DIM
    per_order = width // cw

    def lanes128(a):
        return jnp.pad(a.astype(F32), [(0, 0)] * (a.ndim - 1) + [(0, LANES - a.shape[-1])])

    w1p = jnp.pad(lanes128(w1), ((0, LANES - w1.shape[0]), (0, 0)))
    w2p = jnp.pad(lanes128(w2), ((0, LANES - hidden), (0, 0)))
    w3p = jnp.pad(w3.astype(F32), ((0, LANES - hidden), (0, 0)))
    row = lambda a: lanes128(a).reshape(1, LANES)
    full = lambda shape: pl.BlockSpec(shape, lambda j: (0, 0))
    fwd_col = lambda j: (0, (j // per_order) * 2 * per_order + j % per_order)
    bwd_col = lambda j: (0, (j // per_order) * 2 * per_order + per_order + j % per_order)
    out_spec = pl.BlockSpec((RADIX, dims.Q, cw), lambda j: (0, 0, j))
    return pl.pallas_call(
        functools.partial(_filter_body, dims=dims),
        grid=(HY_ORDER * per_order,),
        in_specs=[full((L, LANES)), full((LANES, LANES)), full((1, LANES)), full((LANES, LANES)),
                  full((1, LANES)), full((1, LANES)),
                  pl.BlockSpec((LANES, cw), fwd_col), pl.BlockSpec((LANES, cw), bwd_col),
                  pl.BlockSpec((L, cw), lambda j: (0, j % per_order))],
        out_specs=[out_spec, out_spec],
        out_shape=[jax.ShapeDtypeStruct((RADIX, dims.Q, HY_ORDER * width), F32)] * 2,
        scratch_shapes=[pltpu.VMEM((L, LANES), F32), pltpu.VMEM((cw // LANES, L, LANES), F32)],
        compiler_params=_params(("arbitrary",)),
        name="hyena_filters",
    )(z, w1p, row(b1), w2p, row(b2), row(freq), w3p, w3p, dec)


def _mix_body(ret_ref, g_ref, hy_ref, gr_ref, gh_ref, h_ref, wr_ref, wh_ref, wo_ref, o_ref):
    ret = None
    for hd in range(RET_HEADS):
        cols = slice(hd * RET_DV, (hd + 1) * RET_DV)
        o = ret_ref[:, cols].astype(F32)
        o = o * lax.rsqrt(jnp.mean(o * o, axis=-1, keepdims=True) + RMS_EPS)
        g = g_ref[:, cols].astype(F32)
        part = jnp.dot((g * jax.nn.sigmoid(g) * o).astype(BF16), wr_ref[cols, :], preferred_element_type=F32)
        ret = part if ret is None else ret + part
    hyo = jnp.dot(hy_ref[...].astype(BF16), wh_ref[...], preferred_element_type=F32)
    merged = (jax.nn.sigmoid(gr_ref[...].astype(F32)) * ret
              + jax.nn.sigmoid(gh_ref[...].astype(F32)) * hyo)
    o_ref[...] = h_ref[...] + jnp.dot(merged.astype(BF16), wo_ref[...], preferred_element_type=F32)


def _mix_out(ret2, hy2, proj2, g_col0, gate_col0, h2, w_ret_o, w_hy_o, w_out):
    T, D = h2.shape
    tm = _pick_tile(T, 576, BF16_SUBLANES)
    gb = gate_col0 // D
    v_w = ret2.shape[1]
    assert g_col0 % v_w == 0
    row = lambda w: pl.BlockSpec((tm, w), lambda i: (i, 0))
    full = lambda a: pl.BlockSpec(a.shape, lambda i: (0, 0))
    return pl.pallas_call(
        _mix_body,
        grid=(T // tm,),
        in_specs=[
            row(v_w), pl.BlockSpec((tm, v_w), lambda i: (i, g_col0 // v_w)), row(hy2.shape[1]),
            pl.BlockSpec((tm, D), lambda i: (i, gb)),
            pl.BlockSpec((tm, D), lambda i: (i, gb + 1)),
            row(D), full(w_ret_o), full(w_hy_o), full(w_out),
        ],
        out_specs=row(D),
        out_shape=jax.ShapeDtypeStruct((T, D), F32),
        compiler_params=_params(("parallel",)),
        name="merge_outproj",
    )(ret2, proj2, hy2, proj2, proj2, h2, w_ret_o, w_hy_o, w_out)


EXPERT_TILE = 1024
META_COLS = 8
SC_WINDOW = 64
SC_WORKERS = 32
SC_CHUNK = 2 * SC_WINDOW * SC_WORKERS
SC_SCATTER_WINDOW = 128


def _pack_bf16_pairs(x):
    n = x.shape[1] // 2
    xb = x.astype(BF16).astype(F32)
    hi = lax.bitcast_convert_type(xb[:, :n], jnp.uint32)
    lo = lax.bitcast_convert_type(xb[:, n:], jnp.uint32)
    return lax.bitcast_convert_type(hi | (lo >> 16), jnp.int32)


def _unpack_bf16_pairs(w):
    u = lax.bitcast_convert_type(w, jnp.uint32)
    hi = lax.bitcast_convert_type(u & jnp.uint32(0xFFFF0000), F32)
    lo = lax.bitcast_convert_type(u << 16, F32)
    return hi, lo


def _route(lt):
    assert EXP_PER_GROUP == 8 and N_GROUPS <= 8
    tm = lt.shape[1]
    row = lax.broadcasted_iota(jnp.int32, (8, tm), 0)
    neg = -jnp.inf
    big = jnp.int32(1 << 20)
    gl = jnp.where(row < N_GROUPS, lt[N_EXPERTS:N_EXPERTS + 8, :], neg)
    gmax = jnp.max(gl, axis=0, keepdims=True)
    p_top = 1.0 / jnp.sum(jnp.exp(gl - gmax), axis=0, keepdims=True)
    g_idx = jnp.min(jnp.where(gl == gmax, row, big), axis=0, keepdims=True)
    el = lt[0:8, :]
    for g in range(1, N_GROUPS):
        el = jnp.where(g_idx == g, lt[8 * g:8 * g + 8, :], el)
    m1 = jnp.max(el, axis=0, keepdims=True)
    i1 = jnp.min(jnp.where(el == m1, row, big), axis=0, keepdims=True)
    el2 = jnp.where(row == i1, neg, el)
    m2 = jnp.max(el2, axis=0, keepdims=True)
    i2 = jnp.min(jnp.where(el2 == m2, row, big), axis=0, keepdims=True)
    r = jnp.exp(m2 - m1)
    base = g_idx * EXP_PER_GROUP
    return base + i1, base + i2, p_top / (1.0 + r), p_top * r / (1.0 + r)


def _route_body(h_ref, g_ref, whi_ref, wlo_ref, xpk_ref, meta_ref, meta_t_ref, cnt_ref, carry_ref):
    i = pl.program_id(0)

    @pl.when(i == 0)
    def _():
        carry_ref[...] = jnp.zeros_like(carry_ref)

    xn = _rms(h_ref[...], g_ref[...])
    xpk_ref[...] = _pack_bf16_pairs(xn)
    x_hi = xn.astype(BF16)
    x_lo = (xn - x_hi.astype(F32)).astype(BF16)
    logits = (jnp.dot(x_hi, whi_ref[...], preferred_element_type=F32)
              + jnp.dot(x_lo, whi_ref[...], preferred_element_type=F32)
              + jnp.dot(x_hi, wlo_ref[...], preferred_element_type=F32))
    e0, e1, w0, w1 = _route(logits.T)
    tm = logits.shape[0]
    row = lax.broadcasted_iota(jnp.int32, (LANES, tm), 0)
    onehot_t = jnp.where((row == e0) | (row == e1), 1.0, 0.0).astype(BF16)
    lane_tiles = tm // LANES
    sub = LANES * max(k for k in (1, 2, 3) if lane_tiles % k == 0)
    ri = lax.broadcasted_iota(jnp.int32, (sub, sub), 0)
    ci = lax.broadcasted_iota(jnp.int32, (sub, sub), 1)
    earlier = jnp.where(ci < ri, 1.0, 0.0).astype(BF16)
    ones = jnp.ones((8, sub), BF16)
    running = carry_ref[...]
    blocks = []
    for j in range(tm // sub):
        part = onehot_t[:, j * sub:(j + 1) * sub]
        blocks.append((running + _dot_nt(earlier, part)).T)
        running = running + _dot_nt(ones, part)[0:1, :]
    prefix_t = jnp.concatenate(blocks, axis=1)
    r0 = jnp.sum(jnp.where(row == e0, prefix_t, 0.0), axis=0, keepdims=True)
    r1 = jnp.sum(jnp.where(row == e1, prefix_t, 0.0), axis=0, keepdims=True)
    row8 = lax.broadcasted_iota(jnp.int32, (META_COLS, tm), 0)
    meta_t = jnp.zeros((META_COLS, tm), F32)
    for c, val in enumerate((e0.astype(F32), e1.astype(F32), r0, r1, w0, w1)):
        meta_t = jnp.where(row8 == c, val, meta_t)
    meta_t_ref[...] = meta_t
    padded = jnp.concatenate([meta_t, jnp.zeros((LANES - META_COLS, tm), F32)], axis=0)
    meta_ref[...] = padded.T[:, :META_COLS]
    carry_ref[...] = running
    cnt_ref[...] = running


def _moe_route(h2, gain, w_router):
    T, D = h2.shape
    tm = _pick_tile(T, 1152, LANES)
    return pl.pallas_call(
        _route_body,
        grid=(T // tm,),
        in_specs=[
            pl.BlockSpec((tm, D), lambda i: (i, 0)),
            pl.BlockSpec((1, D), lambda i: (0, 0)),
            pl.BlockSpec((D, LANES), lambda i: (0, 0)),
            pl.BlockSpec((D, LANES), lambda i: (0, 0)),
        ],
        out_specs=[
            pl.BlockSpec((tm, D // 2), lambda i: (i, 0)),
            pl.BlockSpec((tm, META_COLS), lambda i: (i, 0)),
            pl.BlockSpec((META_COLS, tm), lambda i: (0, i)),
            pl.BlockSpec((1, LANES), lambda i: (0, 0)),
        ],
        out_shape=[
            jax.ShapeDtypeStruct((T, D // 2), jnp.int32),
            jax.ShapeDtypeStruct((T, META_COLS), F32),
            jax.ShapeDtypeStruct((META_COLS, T), F32),
            jax.ShapeDtypeStruct((1, LANES), F32),
        ],
        scratch_shapes=[pltpu.VMEM((1, LANES), F32)],
        compiler_params=_params(("arbitrary",)),
        name="moe_route",
    )(h2, gain.reshape(1, D), *w_router)


def _sc_gather(table, idx):
    n = idx.shape[0]
    width = table.shape[1]
    win = SC_WINDOW
    assert n % SC_CHUNK == 0
    per_worker = n // SC_WORKERS
    mesh = plsc.VectorSubcoreMesh(core_axis_name="c", subcore_axis_name="s")

    @functools.partial(
        pl.kernel, out_type=jax.ShapeDtypeStruct((n, width), table.dtype), mesh=mesh,
        scratch_types=[pltpu.VMEM((per_worker,), jnp.int32), pltpu.VMEM((2, win, width), table.dtype),
                       pltpu.SemaphoreType.DMA((2,)), pltpu.SemaphoreType.DMA((2,))],
        name="sc_row_gather")
    def gather(table_hbm, idx_hbm, out_hbm, idx_v, rows_v, gsem, osem):
        worker = lax.axis_index("s") * mesh.num_cores + lax.axis_index("c")
        base = worker * per_worker
        pltpu.sync_copy(idx_hbm.at[pl.ds(base, per_worker)], idx_v)

        @pl.loop(0, per_worker, step=2 * win)
        def _(off):
            fetch = [pltpu.async_copy(table_hbm.at[idx_v.at[pl.ds(off + s * win, win)]], rows_v.at[s], gsem.at[s])
                     for s in range(2)]
            store = []
            for s in range(2):
                fetch[s].wait()
                store.append(pltpu.async_copy(rows_v.at[s], out_hbm.at[pl.ds(base + off + s * win, win)],
                                              osem.at[s]))
            for s in range(2):
                store[s].wait()

    return gather(table, idx)


def _sc_dispatch(table, dest, n_out):
    n_rows, width = table.shape
    win = SC_SCATTER_WINDOW
    assert n_rows % win == 0
    n_win = n_rows // win
    per_worker = pl.cdiv(n_win, SC_WORKERS)
    idx = jnp.pad(dest, ((0, 0), (0, per_worker * SC_WORKERS * win - n_rows)))
    idx = idx.reshape(2, per_worker, SC_WORKERS, win).transpose(0, 2, 1, 3)
    mesh = plsc.VectorSubcoreMesh(core_axis_name="c", subcore_axis_name="s")

    @functools.partial(
        pl.kernel, out_type=jax.ShapeDtypeStruct((n_out, width), table.dtype), mesh=mesh,
        scratch_types=[pltpu.VMEM((per_worker, win), jnp.int32), pltpu.VMEM((per_worker, win), jnp.int32),
                       pltpu.VMEM((win, width), table.dtype),
                       pltpu.SemaphoreType.DMA, pltpu.SemaphoreType.DMA],
        name="sc_row_dispatch")
    def dispatch(table_hbm, idx_hbm, out_hbm, idx0_v, idx1_v, rows_v, sem0, sem1):
        worker = lax.axis_index("s") * mesh.num_cores + lax.axis_index("c")
        pltpu.sync_copy(idx_hbm.at[0, worker], idx0_v)
        pltpu.sync_copy(idx_hbm.at[1, worker], idx1_v)

        @pl.loop(0, per_worker)
        def _(j):
            window = j * SC_WORKERS + worker

            @pl.when(window < n_win)
            def _():
                pltpu.sync_copy(table_hbm.at[pl.ds(window * win, win)], rows_v)
                first = pltpu.async_copy(rows_v, out_hbm.at[idx0_v.at[j]], sem0)
                second = pltpu.async_copy(rows_v, out_hbm.at[idx1_v.at[j]], sem1)
                first.wait()
                second.wait()

    return dispatch(table, idx)


def _expert_body(te_ref, tv_ref, x_ref, wg_ref, wu_ref, wd_ref, y_ref, wg_s, wu_s, wd_s):
    t = pl.program_id(0)
    half = x_ref.shape[1]

    @pl.when((t == 0) | (te_ref[t] != te_ref[jnp.maximum(t - 1, 0)]))
    def _():
        wg_s[...] = wg_ref[0].astype(BF16)
        wu_s[...] = wu_ref[0].astype(BF16)
        wd_s[...] = wd_ref[0].astype(BF16)

    @pl.when(tv_ref[t] > 0)
    def _():
        row = lax.broadcasted_iota(jnp.int32, x_ref.shape, 0)
        hi, lo = _unpack_bf16_pairs(jnp.where(row < tv_ref[t], x_ref[...], 0))
        hi, lo = hi.astype(BF16), lo.astype(BF16)
        hg = jnp.dot(hi, wg_s[:half, :], preferred_element_type=F32)
        hg += jnp.dot(lo, wg_s[half:, :], preferred_element_type=F32)
        hu = jnp.dot(hi, wu_s[:half, :], preferred_element_type=F32)
        hu += jnp.dot(lo, wu_s[half:, :], preferred_element_type=F32)
        act = (hg * jax.nn.sigmoid(hg) * hu).astype(BF16)
        y_ref[...] = _pack_bf16_pairs(jnp.dot(act, wd_s[...], preferred_element_type=F32))

    @pl.when(tv_ref[t] == 0)
    def _():
        y_ref[...] = jnp.zeros_like(y_ref)


def _moe_experts(xs, n_sorted, tile_expert, tile_valid, w_gate, w_up, w_down):
    NP, half = n_sorted, xs.shape[1]
    _, D, FF = w_gate.shape
    tr = EXPERT_TILE
    grid_spec = pltpu.PrefetchScalarGridSpec(
        num_scalar_prefetch=2,
        grid=(NP // tr,),
        in_specs=[
            pl.BlockSpec((tr, half), lambda t, te, tv: (t, 0)),
            pl.BlockSpec((1, D, FF), lambda t, te, tv: (te[t], 0, 0)),
            pl.BlockSpec((1, D, FF), lambda t, te, tv: (te[t], 0, 0)),
            pl.BlockSpec((1, FF, D), lambda t, te, tv: (te[t], 0, 0)),
        ],
        out_specs=pl.BlockSpec((tr, half), lambda t, te, tv: (t, 0)),
        scratch_shapes=[pltpu.VMEM((D, FF), BF16), pltpu.VMEM((D, FF), BF16), pltpu.VMEM((FF, D), BF16)],
    )
    return pl.pallas_call(
        _expert_body,
        grid_spec=grid_spec,
        out_shape=jax.ShapeDtypeStruct((NP, half), jnp.int32),
        compiler_params=_params(("arbitrary",)),
        name="moe_experts",
    )(tile_expert, tile_valid, xs, w_gate, w_up, w_down)


def _moe_combine_into(o_ref, h, meta, y0, y1):
    half = y0.shape[1]
    w0 = meta[:, 4:5]
    w1 = meta[:, 5:6]
    hi0, lo0 = _unpack_bf16_pairs(y0)
    hi1, lo1 = _unpack_bf16_pairs(y1)
    o_ref[:, :half] = h[:, :half] + w0 * hi0 + w1 * hi1
    o_ref[:, half:] = h[:, half:] + w0 * lo0 + w1 * lo1


def _moe(h2, gain, w_router, layer, w_gate, w_up, w_down):
    T, D = h2.shape
    E = N_EXPERTS
    tr = EXPERT_TILE
    n_sorted = _round_up(2 * T + E * (tr - 1), tr)
    t_pad = _round_up(T, SC_CHUNK // 2)

    xpk, meta, meta_t, counts = _moe_route(h2, gain, w_router)

    cnt = counts[0, :E].astype(jnp.int32)
    padded = (cnt + tr - 1) // tr * tr
    ends = jnp.cumsum(padded)
    starts = ends - padded
    eid = meta_t[0:2].astype(jnp.int32)
    pos = meta_t[2:4].astype(jnp.int32)
    for e in range(E):
        pos = pos + jnp.where(eid == e, starts[e], 0)
    tile_start = jnp.arange(n_sorted // tr, dtype=jnp.int32) * tr
    tile_expert = jnp.minimum(jnp.sum(tile_start[:, None] >= ends[None, :], axis=1), E - 1).astype(jnp.int32)
    tile_valid = jnp.clip(cnt[tile_expert] - (tile_start - starts[tile_expert]), 0, tr).astype(jnp.int32)

    xs = _sc_dispatch(xpk, pos, n_sorted)
    ys = _moe_experts(xs, n_sorted, layer * E + tile_expert, tile_valid, w_gate, w_up, w_down)
    spare = jnp.arange(t_pad - T, dtype=jnp.int32)
    back = jnp.concatenate([pos, jnp.broadcast_to(spare[None], (2, t_pad - T))], axis=1)
    yg = _sc_gather(ys, back.reshape(-1)).reshape(2, t_pad, D // 2)
    return meta, yg


def _final_body(h_ref, meta_ref, y0_ref, y1_ref, g_ref, o_ref, hn_ref):
    _moe_combine_into(hn_ref, h_ref[0], meta_ref[0], y0_ref[0], y1_ref[0])
    o_ref[0] = _rms(hn_ref[N_META:, :], g_ref[...])


def _final_norm(h3, meta3, yg, gain, b0, nb):
    _, L, D = h3.shape
    return pl.pallas_call(
        _final_body,
        grid=(nb,),
        in_specs=[pl.BlockSpec((1, L, D), lambda b: (b0 + b, 0, 0)),
                  pl.BlockSpec((1, L, meta3.shape[2]), lambda b: (b0 + b, 0, 0)),
                  pl.BlockSpec((1, L, D // 2), lambda b: (0, b0 + b, 0)),
                  pl.BlockSpec((1, L, D // 2), lambda b: (1, b0 + b, 0)),
                  pl.BlockSpec((1, D), lambda b: (0, 0))],
        out_specs=pl.BlockSpec((1, L - N_META, D), lambda b: (b, 0, 0)),
        out_shape=jax.ShapeDtypeStruct((nb, L - N_META, D), F32),
        scratch_shapes=[pltpu.VMEM((L, D), F32)],
        compiler_params=_params(("parallel",)),
        name="combine_final_norm",
    )(h3, meta3, yg, yg, gain.reshape(1, D))


def kernel(x_prompt, x_sample, meta_tokens, norm_mix, w_in, ret_decay_fwd, ret_decay_bwd, hy_short_w, hy_short_b, hy_filt_w1, hy_filt_b1, hy_filt_w2, hy_filt_b2, hy_filt_w3, hy_sin_freq, hy_skip, w_ret_o, w_hy_o, w_out, norm_ffn, router_group, router_expert, moe_w_gate, moe_w_up, moe_w_down, norm_final):
    assert x_prompt.shape[1:] == x_sample.shape[1:]
    nbp, nbs = x_prompt.shape[0], x_sample.shape[0]
    B = nbp + nbs
    D = x_prompt.shape[2]
    L = N_META + x_prompt.shape[1]
    T = B * L
    depth = w_in.shape[0]
    q_w = RET_HEADS * RET_DK
    v_w = RET_HEADS * RET_DV
    hy_w = hy_skip.shape[2]
    hy_col0 = 2 * q_w + 2 * v_w
    gate_col0 = hy_col0 + 3 * hy_w
    assert D == q_w and w_in.shape[2] == gate_col0 + 2 * D

    x = jnp.concatenate([x_prompt, x_sample], axis=0)
    meta = jnp.broadcast_to(meta_tokens[None].astype(x.dtype), (B, N_META, D))
    h = jnp.concatenate([meta, x], axis=1).reshape(T, D)

    cos_t, sin_t = _rotary_tables(L)
    dims = _HyenaDims(L)
    dft_fwd, dft_inv = _dft_tables(dims)
    filt_consts = _filter_constants(L, hy_w)
    router = jnp.concatenate([router_expert, router_group], axis=2).astype(F32)
    router = jnp.pad(router, ((0, 0), (0, 0), (0, LANES - router.shape[2])))
    router_hi = router.astype(BF16)
    router_lo = (router - router_hi.astype(F32)).astype(BF16)
    assert moe_w_gate.shape[1] == N_EXPERTS
    w_gate_all = moe_w_gate.reshape((-1,) + moe_w_gate.shape[2:])
    w_up_all = moe_w_up.reshape((-1,) + moe_w_up.shape[2:])
    w_down_all = moe_w_down.reshape((-1,) + moe_w_down.shape[2:])

    moe = None
    for i in range(depth):
        proj, h = _inproj(h, norm_mix[i], w_in[i].astype(BF16), moe)
        proj3 = proj.reshape(B, L, -1)

        lg = jnp.stack([jax.nn.log_sigmoid(ret_decay_fwd[i].astype(F32)),
                        jax.nn.log_sigmoid(ret_decay_bwd[i].astype(F32))])
        ret = _retention(proj3, lg, cos_t, sin_t)

        hs, hd = _hyena_filters(dims, filt_consts, hy_filt_w1[i], hy_filt_b1[i], hy_filt_w2[i], hy_filt_b2[i],
                                hy_filt_w3[i], hy_sin_freq[i])
        spectrum = _filter_spectrum(dims, dft_fwd, hs, hd)
        sw = hy_short_w[i].astype(F32)
        sb = hy_short_b[i].astype(F32).reshape(1, -1)
        z = _long_conv(dims, dft_fwd, dft_inv, proj3, hy_col0, proj3, hy_col0 + hy_w, sw, sb, 0, hy_w,
                       hy_skip[i, 0].astype(F32), spectrum, 0, hy_w)
        z = _long_conv(dims, dft_fwd, dft_inv, z, 0, proj3, hy_col0 + 2 * hy_w, sw, sb, None, 2 * hy_w,
                       hy_skip[i, 1].astype(F32), spectrum, hy_w, hy_w)

        h = _mix_out(ret.reshape(T, v_w), z.reshape(T, hy_w), proj, 2 * q_w + v_w, gate_col0, h,
                     w_ret_o[i].astype(BF16), w_hy_o[i].astype(BF16), w_out[i].astype(BF16))
        moe = _moe(h, norm_ffn[i], (router_hi[i], router_lo[i]), i, w_gate_all, w_up_all, w_down_all)

    h3 = h.reshape(B, L, D)
    meta3 = moe[0].reshape(B, L, -1)
    return (_final_norm(h3, meta3, moe[1], norm_final, 0, nbp),
            _final_norm(h3, meta3, moe[1], norm_final, nbp, nbs))
```
